```python
import math
import jax, jax.numpy as jnp
from jax import lax
import numpy as np

D_MODEL = 1024
BATCH = 1
SEQ = 16384
DEPTH = 1
DEC_BATCH = 128
DEC_SEQ = 1
PAST_LEN = 16384
PAGE_SIZE = 128

HEAD_DIM_A = 64
N_HEADS_A = D_MODEL // 128
N_KV_A = N_HEADS_A // 4
GROUP_A = N_HEADS_A // N_KV_A
WIDTH_A = N_HEADS_A * HEAD_DIM_A
WINDOW = 128
BLOCK_Q = 128
ROT_DIM = HEAD_DIM_A // 4
ROPE_THETA = 500000.0
HEAD_DIM_B = 128
N_HEADS_B = D_MODEL // 256
WIDTH_B = N_HEADS_B * HEAD_DIM_B
CHUNK_B = 64
EPS = 1e-6
PROJ_SIZES = [WIDTH_A, N_KV_A * HEAD_DIM_A, N_KV_A * HEAD_DIM_A, WIDTH_A,
              WIDTH_B, WIDTH_B, WIDTH_B, WIDTH_B]
D_IN = sum(PROJ_SIZES)
SPLIT_POINTS = [int(s) for s in np.cumsum(PROJ_SIZES)[:-1]]

kernel_name = 'hybrid_swa_sink_hgrn2_gated_merge_adaln_step'


def rms_norm(x, g):
    xf = x.astype(jnp.float32)
    y = xf * lax.rsqrt(jnp.mean(xf * xf, axis=-1, keepdims=True) + EPS)
    return (y * g.astype(jnp.float32)).astype(x.dtype)


def rope_partial(x, pos0):
    T = x.shape[1]
    pos = (pos0 + jnp.arange(T)).astype(jnp.float32)
    inv = jnp.power(ROPE_THETA, -jnp.arange(0, ROT_DIM, 2, dtype=jnp.float32) / ROT_DIM)
    ang = pos[:, None] * inv[None, :]
    cos = jnp.cos(ang)[None, :, None, :]
    sin = jnp.sin(ang)[None, :, None, :]
    xr = x[..., :ROT_DIM].astype(jnp.float32)
    x1, x2 = xr[..., :ROT_DIM // 2], xr[..., ROT_DIM // 2:]
    rot = jnp.concatenate([x1 * cos - x2 * sin, x2 * cos + x1 * sin], axis=-1).astype(x.dtype)
    return jnp.concatenate([rot, x[..., ROT_DIM:]], axis=-1)


def window_attention(q, k_all, v_all, pos0, sinks):
    B, T, H, d = q.shape
    blk = math.gcd(T, BLOCK_Q)
    nb = T // blk
    span = WINDOW + blk
    idx = (jnp.arange(nb) * blk)[:, None] + jnp.arange(span)[None, :]
    kb = k_all[:, idx]
    vb = v_all[:, idx]
    qb = q.reshape(B, nb, blk, N_KV_A, GROUP_A, d)
    s = jnp.einsum('bnikgd,bnjkd->bnkgij', qb, kb).astype(jnp.float32) * (d ** -0.5)
    tq = pos0 + jnp.arange(T).reshape(nb, blk)
    tk = pos0 - WINDOW + idx
    rel = tq[:, :, None] - tk[:, None, :]
    valid = (rel >= 0) & (rel < WINDOW) & (tk[:, None, :] >= 0)
    s = jnp.where(valid[None, :, None, None], s, -jnp.inf)
    sink = sinks.astype(jnp.float32).reshape(N_KV_A, GROUP_A)[None, None, :, :, None, None]
    m = jnp.maximum(jnp.max(s, axis=-1, keepdims=True), sink)
    p = jnp.exp(s - m)
    p = p / (jnp.sum(p, axis=-1, keepdims=True) + jnp.exp(sink - m))
    o = jnp.einsum('bnkgij,bnjkd->bnikgd', p.astype(v_all.dtype), vb)
    return o.reshape(B, T, H * d)


def hgrn2_chunked(q, log_f, k, v, s0):
    B, T, H, DK = q.shape
    C = math.gcd(T, CHUNK_B)
    n = T // C
    to_chunks = lambda a: a.reshape(B, n, C, H, a.shape[-1]).swapaxes(0, 1)
    xs = (to_chunks(q), to_chunks(log_f), to_chunks(k), to_chunks(v))
    causal = jnp.tril(jnp.ones((C, C), dtype=bool))

    def step(S, inp):
        qc, gc, kc, vc = inp
        L = jnp.cumsum(gc, axis=1)
        o_inter = jnp.einsum('bchk,bhkv->bchv', qc * jnp.exp(L), S)
        diff = L[:, :, None] - L[:, None, :]
        decay = jnp.exp(jnp.where(causal[None, :, :, None, None], diff, -jnp.inf))
        A = jnp.einsum('bihk,bjhk,bijhk->bhij', qc, kc, decay)
        o_intra = jnp.einsum('bhij,bjhv->bihv', A, vc)
        Lc = L[:, -1]
        S_new = jnp.exp(Lc)[..., None] * S + jnp.einsum(
            'bjhk,bjhv->bhkv', kc * jnp.exp(Lc[:, None] - L), vc)
        return S_new, o_inter + o_intra

    s_fin, o = lax.scan(step, s0, xs)
    o = o.swapaxes(0, 1).reshape(B, T, H, v.shape[-1])
    return o, s_fin


def mixer_layer(x, c, k_buf, v_buf, s0, pos0, lb, w_ada, b_ada, norm_g, w_in,
                q_norm_g, k_norm_g, sinks, o_norm_g, w_merge, b_merge,
                w_proj_a, w_proj_b, w_out):
    B, T, _ = x.shape
    mod = c @ w_ada + b_ada
    shift, scale, gate = jnp.split(mod[:, None, :], 3, axis=-1)
    h = rms_norm(x, norm_g) * (1 + scale) + shift
    proj = h @ w_in
    qa, ka, va, za, qb, fb, ib, zb = jnp.split(proj, SPLIT_POINTS, axis=-1)

    qa = rope_partial(rms_norm(qa.reshape(B, T, N_HEADS_A, HEAD_DIM_A), q_norm_g), pos0)
    ka = rope_partial(rms_norm(ka.reshape(B, T, N_KV_A, HEAD_DIM_A), k_norm_g), pos0)
    va = va.reshape(B, T, N_KV_A, HEAD_DIM_A)
    k_all = jnp.concatenate([k_buf.astype(ka.dtype), ka], axis=1)
    v_all = jnp.concatenate([v_buf.astype(va.dtype), va], axis=1)
    ya = window_attention(qa, k_all, v_all, pos0, sinks) * jax.nn.silu(za)

    zf = fb.reshape(B, T, N_HEADS_B, HEAD_DIM_B).astype(jnp.float32)
    lbh = lb.reshape(N_HEADS_B, HEAD_DIM_B)
    log_f = jnp.log(lbh + (1 - lbh) * jax.nn.sigmoid(zf))
    k_in = (1 - lbh) * jax.nn.sigmoid(-zf)
    o, s_new = hgrn2_chunked(qb.reshape(B, T, N_HEADS_B, HEAD_DIM_B).astype(jnp.float32), log_f, k_in,
                             ib.reshape(B, T, N_HEADS_B, HEAD_DIM_B).astype(jnp.float32),
                             s0.astype(jnp.float32))
    yb = rms_norm(o, o_norm_g).astype(x.dtype).reshape(B, T, WIDTH_B) * jax.nn.silu(zb)

    g_a, g_b = jnp.split(jax.nn.sigmoid(h @ w_merge + b_merge), 2, axis=-1)
    y = g_a * (ya @ w_proj_a) + g_b * (yb @ w_proj_b)
    out = x + gate * (y @ w_out)
    return out, k_all[:, -WINDOW:], v_all[:, -WINDOW:], s_new.astype(x.dtype)


def setup_inputs(seed: int = 0) -> dict:
    key = jax.random.key(seed)
    ks = jax.random.split(key, 24)
    f32 = jnp.float32
    nrm = lambda k, shape, s: jax.random.normal(k, shape, f32) * s
    D = D_MODEL
    return {
        'x_prompt': nrm(ks[0], (BATCH, SEQ, D), 1.0),
        'x_sample': nrm(ks[1], (DEC_BATCH, DEC_SEQ, D), 1.0),
        'cache_win_k': nrm(ks[2], (DEPTH, DEC_BATCH, WINDOW, N_KV_A, HEAD_DIM_A), 1.0),
        'cache_win_v': nrm(ks[3], (DEPTH, DEC_BATCH, WINDOW, N_KV_A, HEAD_DIM_A), 1.0),
        'state_hgrn': nrm(ks[4], (DEPTH, DEC_BATCH, N_HEADS_B, HEAD_DIM_B, HEAD_DIM_B), 0.3),
        'c_prompt': nrm(ks[5], (BATCH, D), 1.0),
        'c_sample': nrm(ks[6], (DEC_BATCH, D), 1.0),
        'w_ada': nrm(ks[7], (DEPTH, D, 3 * D), 0.5 * D ** -0.5),
        'b_ada': nrm(ks[8], (DEPTH, 3 * D), 0.02),
        'norm_g': 1.0 + nrm(ks[9], (DEPTH, D), 0.05),
        'w_in': nrm(ks[10], (DEPTH, D, D_IN), D ** -0.5),
        'q_norm_g': 1.0 + nrm(ks[11], (DEPTH, HEAD_DIM_A), 0.05),
        'k_norm_g': 1.0 + nrm(ks[12], (DEPTH, HEAD_DIM_A), 0.05),
        'sinks': nrm(ks[13], (DEPTH, N_HEADS_A), 0.5),
        'lb_logits': nrm(ks[14], (DEPTH + 1, WIDTH_B), 0.5),
        'o_norm_g': 1.0 + nrm(ks[15], (DEPTH, HEAD_DIM_B), 0.05),
        'w_merge': nrm(ks[16], (DEPTH, D, 2 * D), D ** -0.5),
        'b_merge': nrm(ks[17], (DEPTH, 2 * D), 0.1),
        'w_proj_a': nrm(ks[18], (DEPTH, WIDTH_A, D), WIDTH_A ** -0.5),
        'w_proj_b': nrm(ks[19], (DEPTH, WIDTH_B, D), WIDTH_B ** -0.5),
        'w_out': nrm(ks[20], (DEPTH, D, D), D ** -0.5),
    }


def reference(x_prompt, x_sample, cache_win_k, cache_win_v, state_hgrn, c_prompt, c_sample,
              w_ada, b_ada, norm_g, w_in, q_norm_g, k_norm_g, sinks, lb_logits, o_norm_g,
              w_merge, b_merge, w_proj_a, w_proj_b, w_out):
    lb_all = jnp.cumsum(jax.nn.softmax(lb_logits.astype(jnp.float32), axis=0), axis=0)
    xp, xs = x_prompt, x_sample
    bp = x_prompt.shape[0]
    kp_l, vp_l, sp_l, ks_l, vs_l, ss_l = [], [], [], [], [], []
    for l in range(DEPTH):
        params = (w_ada[l], b_ada[l], norm_g[l], w_in[l], q_norm_g[l], k_norm_g[l], sinks[l],
                  o_norm_g[l], w_merge[l], b_merge[l], w_proj_a[l], w_proj_b[l], w_out[l])
        zero_buf = jnp.zeros((bp, WINDOW, N_KV_A, HEAD_DIM_A), xp.dtype)
        zero_state = jnp.zeros((bp, N_HEADS_B, HEAD_DIM_B, HEAD_DIM_B), jnp.float32)
        xp, kp, vp, sp = mixer_layer(xp, c_prompt, zero_buf, zero_buf, zero_state, 0, lb_all[l], *params)
        xs, kq, vq, sq = mixer_layer(xs, c_sample, cache_win_k[l], cache_win_v[l], state_hgrn[l],
                                     PAST_LEN, lb_all[l], *params)
        kp_l.append(kp); vp_l.append(vp); sp_l.append(sp)
        ks_l.append(kq); vs_l.append(vq); ss_l.append(sq)
    return (xp, xs, jnp.stack(kp_l), jnp.stack(vp_l), jnp.stack(sp_l),
            jnp.stack(ks_l), jnp.stack(vs_l), jnp.stack(ss_l))
```

```python
import functools

import numpy as np
import jax
import jax.numpy as jnp
from jax import lax
from jax.experimental import pallas as pl
from jax.experimental.pallas import tpu as pltpu

D_MODEL = 1024
HEAD_DIM_A = 64
N_HEADS_A = 8
N_KV_A = 2
WIDTH_A = N_HEADS_A * HEAD_DIM_A
KV_WIDTH = N_KV_A * HEAD_DIM_A
WINDOW = 128
ROT_DIM = HEAD_DIM_A // 4
ROPE_THETA = 500000.0
HEAD_DIM_B = 128
N_HEADS_B = 4
WIDTH_B = N_HEADS_B * HEAD_DIM_B
EPS = 1e-6
OFF_QA = 0
OFF_KA = OFF_QA + WIDTH_A
OFF_VA = OFF_KA + KV_WIDTH
OFF_ZA = OFF_VA + KV_WIDTH
OFF_QB = OFF_ZA + WIDTH_A
OFF_FB = OFF_QB + WIDTH_B
OFF_IB = OFF_FB + WIDTH_B
OFF_ZB = OFF_IB + WIDTH_B
D_IN = OFF_ZB + WIDTH_B

LANES = 128
BLK = 128
SUB = 8
LEVELS = (64, 32, 16, 8)
NEG = -1e30
VMEM_LIMIT = 56 * 1024 * 1024

BF = jnp.bfloat16
F32 = jnp.float32


def _dot(a, b):
    return jnp.dot(a, b, preferred_element_type=F32)


def _dot_nt(a, b):
    return lax.dot_general(a, b, (((1,), (1,)), ((), ())), preferred_element_type=F32)


def _dot_tn(a, b):
    return lax.dot_general(a, b, (((0,), (0,)), ((), ())), preferred_element_type=F32)


def _split_dot(a_f32, b_bf):
    hi = a_f32.astype(BF)
    lo = (a_f32 - hi.astype(F32)).astype(BF)
    return _dot(hi, b_bf) + _dot(lo, b_bf)


def _sigmoid(x):
    return 1.0 / (1.0 + jnp.exp(-x))


def _silu(x):
    return x * _sigmoid(x)


def _lower_bound(lb_logits):
    l0 = lb_logits[0:1, :]
    l1 = lb_logits[1:2, :]
    m = jnp.maximum(l0, l1)
    e0 = jnp.exp(l0 - m)
    e1 = jnp.exp(l1 - m)
    return e0 / (e0 + e1)


def _rope(x, c, s_up, s_dn):
    return x * c + pltpu.roll(x, LANES - ROT_DIM // 2, 1) * s_up + pltpu.roll(x, ROT_DIM // 2, 1) * s_dn


def _head_norm_scale(x, seg_mean_bf):
    ms = _dot((x * x).astype(BF), seg_mean_bf)
    return lax.rsqrt(ms + EPS)


def _ada_kernel(c_ref, w_ref, b_ref, o_ref):
    o_ref[...] = jnp.dot(c_ref[...], w_ref[...], preferred_element_type=F32,
                         precision=lax.Precision.HIGHEST) + b_ref[...]


def _ada_call(c_all, w_ada, b_ada):
    m = c_all.shape[0]
    n = w_ada.shape[1]
    tn = 512
    return pl.pallas_call(
        _ada_kernel,
        grid=(n // tn,),
        in_specs=[pl.BlockSpec((m, D_MODEL), lambda j: (0, 0)),
                  pl.BlockSpec((D_MODEL, tn), lambda j: (0, j)),
                  pl.BlockSpec((1, tn), lambda j: (0, j))],
        out_specs=pl.BlockSpec((m, tn), lambda j: (0, j)),
        out_shape=jax.ShapeDtypeStruct((m, n), F32),
        name="ada",
    )(c_all, w_ada, b_ada)


def _attn_block(q_blk, kcat, kcat_sw, vcat, vcat_sw, bias, sink_a, sink_b):
    lane = lax.broadcasted_iota(jnp.int32, (BLK, LANES), 1)
    lo = lane < HEAD_DIM_A
    chunks = [q_blk[:, c * LANES:(c + 1) * LANES] for c in range(4)]
    zero = jnp.zeros((BLK, LANES), F32)
    q_lo = [jnp.where(lo, c, zero).astype(BF) for c in chunks]
    q_hi = [jnp.where(lo, zero, c).astype(BF) for c in chunks]
    qa = jnp.concatenate([q_lo[0], q_lo[1], q_hi[2], q_hi[3]], axis=0)
    qb = jnp.concatenate([q_hi[0], q_hi[1], q_lo[2], q_lo[3]], axis=0)

    def soft(qs, kc, vc, sink):
        s = _dot_nt(qs, kc) + bias
        m = jnp.maximum(jnp.max(s, axis=-1, keepdims=True), sink)
        p = jnp.exp(s - m)
        den = jnp.sum(p, axis=-1, keepdims=True) + jnp.exp(sink - m)
        o = _dot(p.astype(BF), vc)
        return o * (1.0 / den)

    oa = soft(qa, kcat, vcat, sink_a)
    ob = soft(qb, kcat_sw, vcat_sw, sink_b)
    r = lambda o, i: o[i * BLK:(i + 1) * BLK, :]
    return jnp.concatenate([
        jnp.where(lo, r(oa, 0), r(ob, 0)),
        jnp.where(lo, r(oa, 1), r(ob, 1)),
        jnp.where(lo, r(ob, 2), r(oa, 2)),
        jnp.where(lo, r(ob, 3), r(oa, 3)),
    ], axis=1)


def _hgrn_block(qb, fb, ib, lb, one_m_lb, tri_bf, lvl_mask_ref, st_ref):
    sig = _sigmoid(fb)
    kin = one_m_lb * (1.0 - sig)
    f = lb + one_m_lb * sig
    logf = jnp.log(f)
    cum = _split_dot_left(tri_bf, logf)
    q_dec = (qb * jnp.exp(cum)).astype(BF)
    last = cum[BLK - 1:BLK, :]
    k_dec = (kin * jnp.exp(last - cum)).astype(BF)
    v_bf = ib.astype(BF)

    row = lax.broadcasted_iota(jnp.int32, (BLK, 1), 0)
    lvl_ops = []
    for b in LEVELS:
        pieces = []
        for r0 in range(0, BLK, 2 * b):
            piv = cum[r0 + b - 1:r0 + b, :]
            pieces.append(piv - cum[r0:r0 + b, :])
            pieces.append(cum[r0 + b:r0 + 2 * b, :] - piv)
        w = jnp.exp(jnp.concatenate(pieces, axis=0))
        second = (row & b) != 0
        lvl_ops.append((jnp.where(second, qb, kin) * w).astype(BF))

    n8 = BLK // SUB
    q3 = qb.reshape(n8, SUB, WIDTH_B)
    k3 = kin.reshape(n8, SUB, WIDTH_B)
    f3 = f.reshape(n8, SUB, WIDTH_B)
    v3 = ib.reshape(n8, SUB, WIDTH_B)
    subl = lax.broadcasted_iota(jnp.int32, (n8, SUB, 1), 1)

    def head(x, h):
        return x[..., h * LANES:(h + 1) * LANES]

    g = q3 * k3
    acc = [jnp.sum(head(g, h), axis=-1, keepdims=True) * head(v3, h) for h in range(N_HEADS_B)]
    dec = jnp.ones_like(f3)
    kd = k3
    vd = v3
    for d in range(1, SUB):
        dec = f3 * pltpu.roll(dec, 1, 1)
        kd = pltpu.roll(kd, 1, 1)
        vd = pltpu.roll(vd, 1, 1)
        g = q3 * kd * dec
        ok = subl >= d
        for h in range(N_HEADS_B):
            a = jnp.where(ok, jnp.sum(head(g, h), axis=-1, keepdims=True), 0.0)
            acc[h] = acc[h] + a * head(vd, h)

    outs = []
    for h in range(N_HEADS_B):
        sl = slice(h * LANES, (h + 1) * LANES)
        st = st_ref[h]
        o = _dot_nt(q_dec[:, sl], st.astype(BF))
        amat = jnp.zeros((BLK, BLK), F32)
        for li in range(len(LEVELS)):
            p = lvl_ops[li][:, sl]
            amat = amat + lvl_mask_ref[li] * _dot_nt(p, p)
        o = o + _dot(amat.astype(BF), v_bf[:, sl]) + acc[h].reshape(BLK, LANES)
        outs.append(o)
        st_ref[h] = st * jnp.exp(last[:, sl]) + _dot_tn(v_bf[:, sl], k_dec[:, sl])
    return jnp.concatenate(outs, axis=1)


def _split_dot_left(a_bf, b_f32):
    hi = b_f32.astype(BF)
    lo = (b_f32 - hi.astype(F32)).astype(BF)
    return _dot(a_bf, hi) + _dot(a_bf, lo)


def _branch_b_out(o, zb, og):
    outs = []
    for h in range(N_HEADS_B):
        sl = slice(h * LANES, (h + 1) * LANES)
        oh = o[:, sl]
        ms = jnp.mean(oh * oh, axis=-1, keepdims=True)
        outs.append(oh * lax.rsqrt(ms + EPS))
    return jnp.concatenate(outs, axis=1) * og * _silu(zb)


def _prompt_kernel(sinks_ref,
                   x_ref, shift_ref, scale_ref, gate_ref, ng_ref,
                   w_in_ref, w_mg_ref, b_mg_ref, w_pa_ref, w_pb_ref, w_out_ref,
                   qg_ref, kg_ref, og_ref, lbl_ref,
                   rc_ref, ru_ref, rd_ref,
                   seg_ref, tri_ref, bias_ref, lmask_ref,
                   y_ref, kwin_ref, vwin_ref, state_ref,
                   st_ref, kbuf_ref, ksw_ref, vbuf_ref, vsw_ref,
                   *, tile, n_tiles):
    i = pl.program_id(0)
    nblk = tile // BLK

    @pl.when(i == 0)
    def _():
        st_ref[...] = jnp.zeros_like(st_ref)
        for r in (kbuf_ref, ksw_ref, vbuf_ref, vsw_ref):
            r[...] = jnp.zeros_like(r)

    x = x_ref[...]
    ms = jnp.mean(x * x, axis=-1, keepdims=True)
    h = x * lax.rsqrt(ms + EPS) * (ng_ref[...] * (1.0 + scale_ref[...])) + shift_ref[...]
    h_bf = h.astype(BF)

    proj = _dot(h_bf, w_in_ref[...])
    gates = _sigmoid(_dot(h_bf, w_mg_ref[...]) + b_mg_ref[...])

    seg = seg_ref[...]
    rc, ru, rd = rc_ref[...], ru_ref[...], rd_ref[...]
    qa = proj[:, OFF_QA:OFF_QA + WIDTH_A]
    qa = qa * _head_norm_scale(qa, seg) * (qg_ref[...] * (HEAD_DIM_A ** -0.5))
    qa = jnp.concatenate(
        [_rope(qa[:, c * LANES:(c + 1) * LANES], rc, ru, rd) for c in range(4)], axis=1)
    ka = proj[:, OFF_KA:OFF_KA + KV_WIDTH]
    ka = ka * _head_norm_scale(ka, seg[:KV_WIDTH, :KV_WIDTH]) * kg_ref[...]
    ka = _rope(ka, rc, ru, rd)
    va = proj[:, OFF_VA:OFF_VA + KV_WIDTH]
    ka_sw = pltpu.roll(ka, HEAD_DIM_A, 1)
    va_sw = pltpu.roll(va, HEAD_DIM_A, 1)

    lb = _lower_bound(lbl_ref[...])
    one_m_lb = 1.0 - lb
    tri = tri_ref[...]

    rows4 = lax.broadcasted_iota(jnp.int32, (4 * BLK, 1), 0) // BLK
    def sink_col(heads):
        col = jnp.zeros((4 * BLK, 1), F32)
        for n, hd in enumerate(heads):
            col = jnp.where(rows4 == n, sinks_ref[hd], col)
        return col
    sink_a = sink_col((0, 2, 5, 7))
    sink_b = sink_col((1, 3, 4, 6))

    ya_parts, ob_parts = [], []
    for blk in range(nblk):
        rs = slice(blk * BLK, (blk + 1) * BLK)
        cur = slice(BLK, 2 * BLK)
        kbuf_ref[cur, :] = ka[rs].astype(BF)
        ksw_ref[cur, :] = ka_sw[rs].astype(BF)
        vbuf_ref[cur, :] = va[rs].astype(BF)
        vsw_ref[cur, :] = va_sw[rs].astype(BF)
        first = jnp.logical_and(i == 0, blk == 0)
        bias = bias_ref[jnp.where(first, 0, 1)]
        ya_parts.append(_attn_block(qa[rs], kbuf_ref[...], ksw_ref[...], vbuf_ref[...],
                                    vsw_ref[...], bias, sink_a, sink_b))
        for r in (kbuf_ref, ksw_ref, vbuf_ref, vsw_ref):
            r[0:BLK, :] = r[cur, :]
        ob_parts.append(_hgrn_block(proj[rs, OFF_QB:OFF_QB + WIDTH_B],
                                    proj[rs, OFF_FB:OFF_FB + WIDTH_B],
                                    proj[rs, OFF_IB:OFF_IB + WIDTH_B],
                                    lb, one_m_lb, tri, lmask_ref, st_ref))

    ya = jnp.concatenate(ya_parts, axis=0) * _silu(proj[:, OFF_ZA:OFF_ZA + WIDTH_A])
    yb = _branch_b_out(jnp.concatenate(ob_parts, axis=0),
                       proj[:, OFF_ZB:OFF_ZB + WIDTH_B], og_ref[...])
    y = (gates[:, :D_MODEL] * _dot(ya.astype(BF), w_pa_ref[...])
         + gates[:, D_MODEL:] * _dot(yb.astype(BF), w_pb_ref[...]))
    y_ref[...] = x + gate_ref[...] * _dot(y.astype(BF), w_out_ref[...])

    @pl.when(i == n_tiles - 1)
    def _():
        kwin_ref[...] = ka[tile - WINDOW:, :]
        vwin_ref[...] = va[tile - WINDOW:, :]
        for hd in range(N_HEADS_B):
            state_ref[hd] = st_ref[hd].T


def _const_spec(shape):
    nd = len(shape)
    return pl.BlockSpec(shape, lambda i, *_: (0,) * nd, pipeline_mode=pl.Buffered(1))


def _prompt_call(x, shift, scale, gate, consts, w, tile):
    t = x.shape[0]
    n_tiles = t // tile
    row = lambda n: _const_spec((1, n))
    in_specs = [
        pl.BlockSpec((tile, D_MODEL), lambda i, *_: (i, 0)),
        row(D_MODEL), row(D_MODEL), row(D_MODEL), row(D_MODEL),
        _const_spec((D_MODEL, D_IN)), _const_spec((D_MODEL, 2 * D_MODEL)), row(2 * D_MODEL),
        _const_spec((WIDTH_A, D_MODEL)), _const_spec((WIDTH_B, D_MODEL)),
        _const_spec((D_MODEL, D_MODEL)),
        row(WIDTH_A), row(KV_WIDTH), row(WIDTH_B), _const_spec((2, WIDTH_B)),
        pl.BlockSpec((tile, LANES), lambda i, *_: (i, 0)),
        pl.BlockSpec((tile, LANES), lambda i, *_: (i, 0)),
        pl.BlockSpec((tile, LANES), lambda i, *_: (i, 0)),
        _const_spec((WIDTH_A, WIDTH_A)), _const_spec((BLK, BLK)),
        _const_spec((2, 4 * BLK, 2 * BLK)), _const_spec((len(LEVELS), BLK, BLK)),
    ]
    out_specs = [
        pl.BlockSpec((tile, D_MODEL), lambda i, *_: (i, 0)),
        pl.BlockSpec((WINDOW, KV_WIDTH), lambda i, *_: (0, 0)),
        pl.BlockSpec((WINDOW, KV_WIDTH), lambda i, *_: (0, 0)),
        pl.BlockSpec((N_HEADS_B, HEAD_DIM_B, HEAD_DIM_B), lambda i, *_: (0, 0, 0)),
    ]
    out_shape = [
        jax.ShapeDtypeStruct((t, D_MODEL), F32),
        jax.ShapeDtypeStruct((WINDOW, KV_WIDTH), F32),
        jax.ShapeDtypeStruct((WINDOW, KV_WIDTH), F32),
        jax.ShapeDtypeStruct((N_HEADS_B, HEAD_DIM_B, HEAD_DIM_B), F32),
    ]
    scratch = [
        pltpu.VMEM((N_HEADS_B, HEAD_DIM_B, HEAD_DIM_B), F32),
        pltpu.VMEM((2 * BLK, KV_WIDTH), BF), pltpu.VMEM((2 * BLK, KV_WIDTH), BF),
        pltpu.VMEM((2 * BLK, KV_WIDTH), BF), pltpu.VMEM((2 * BLK, KV_WIDTH), BF),
    ]
    return pl.pallas_call(
        functools.partial(_prompt_kernel, tile=tile, n_tiles=n_tiles),
        grid_spec=pltpu.PrefetchScalarGridSpec(
            num_scalar_prefetch=1, grid=(n_tiles,),
            in_specs=in_specs, out_specs=out_specs, scratch_shapes=scratch),
        out_shape=out_shape,
        compiler_params=pltpu.CompilerParams(
            dimension_semantics=("arbitrary",), vmem_limit_bytes=VMEM_LIMIT),
        name="prompt_layer",
    )(w["sinks"], x, shift, scale, gate, w["norm_g"],
      w["w_in"], w["w_merge"], w["b_merge"], w["w_proj_a"], w["w_proj_b"], w["w_out"],
      w["q_g"], w["k_g"], w["o_g"], w["lb_logits"],
      consts["rope_c"], consts["rope_u"], consts["rope_d"],
      consts["seg"], consts["tri"], consts["bias"], consts["lmask"])


def _dec_in_kernel(x_ref, shift_ref, scale_ref, ng_ref, w_in_ref, w_mg_ref, b_mg_ref,
                   qg_ref, kg_ref, lbl_ref, rc_ref, ru_ref, rd_ref, seg_ref, perm_ref,
                   qm_ref, kn_ref, vn_ref, za_ref, qb_ref, kin_ref, ib_ref, zb_ref, g_ref):
    x = x_ref[...]
    ms = jnp.mean(x * x, axis=-1, keepdims=True)
    h = x * lax.rsqrt(ms + EPS) * (ng_ref[...] * (1.0 + scale_ref[...])) + shift_ref[...]
    h_bf = h.astype(BF)
    proj = _dot(h_bf, w_in_ref[...])
    g_ref[...] = _sigmoid(_dot(h_bf, w_mg_ref[...]) + b_mg_ref[...])

    seg = seg_ref[...]
    rc, ru, rd = rc_ref[...], ru_ref[...], rd_ref[...]
    qa = proj[:, OFF_QA:OFF_QA + WIDTH_A]
    qa = qa * _head_norm_scale(qa, seg) * (qg_ref[...] * (HEAD_DIM_A ** -0.5))
    qa = jnp.concatenate(
        [_rope(qa[:, c * LANES:(c + 1) * LANES], rc, ru, rd) for c in range(4)], axis=1)
    qm_ref[...] = _dot(qa.astype(BF), perm_ref[...])
    ka = proj[:, OFF_KA:OFF_KA + KV_WIDTH]
    ka = ka * _head_norm_scale(ka, seg[:KV_WIDTH, :KV_WIDTH]) * kg_ref[...]
    kn_ref[...] = _rope(ka, rc, ru, rd)
    vn_ref[...] = proj[:, OFF_VA:OFF_VA + KV_WIDTH]
    za_ref[...] = proj[:, OFF_ZA:OFF_ZA + WIDTH_A]
    qb_ref[...] = proj[:, OFF_QB:OFF_QB + WIDTH_B]
    lb = _lower_bound(lbl_ref[...])
    kin_ref[...] = (1.0 - lb) * (1.0 - _sigmoid(proj[:, OFF_FB:OFF_FB + WIDTH_B]))
    ib_ref[...] = proj[:, OFF_IB:OFF_IB + WIDTH_B]
    zb_ref[...] = proj[:, OFF_ZB:OFF_ZB + WIDTH_B]


def _dec_in_call(x, shift, scale, consts, w):
    b = x.shape[0]
    f = lambda n: jax.ShapeDtypeStruct((b, n), F32)
    return pl.pallas_call(
        _dec_in_kernel,
        out_shape=[f(N_HEADS_A * LANES), f(KV_WIDTH), f(KV_WIDTH), f(WIDTH_A),
                   f(WIDTH_B), f(WIDTH_B), f(WIDTH_B), f(WIDTH_B), f(2 * D_MODEL)],
        compiler_params=pltpu.CompilerParams(vmem_limit_bytes=VMEM_LIMIT),
        name="decode_in",
    )(x, shift, scale, w["norm_g"], w["w_in"], w["w_merge"], w["b_merge"],
      w["q_g"], w["k_g"], w["lb_logits"],
      consts["rope_c1"], consts["rope_u1"], consts["rope_d1"], consts["seg"], consts["perm"])


def _dec_mix_kernel(qm_ref, kn_ref, vn_ref, qb_ref, kin_ref, ib_ref, sink_ref,
                    ck_ref, cv_ref, st_ref,
                    att_ref, ob_ref, nk_ref, nv_ref, nst_ref, *, bt):
    key = lax.broadcasted_iota(jnp.int32, (N_HEADS_A, WINDOW), 1)
    sink = sink_ref[...]
    for j in range(bt):
        qm = qm_ref[j]
        kc = ck_ref[j]
        vc = cv_ref[j]
        kn = kn_ref[j:j + 1, :]
        vn = vn_ref[j:j + 1, :]
        s = _dot_nt(qm.astype(BF), kc.astype(BF))
        s = jnp.where(key == 0, NEG, s)
        s_new = jnp.sum(qm * kn, axis=-1, keepdims=True)
        m = jnp.maximum(jnp.maximum(jnp.max(s, axis=-1, keepdims=True), s_new), sink)
        p = jnp.exp(s - m)
        p_new = jnp.exp(s_new - m)
        den = jnp.sum(p, axis=-1, keepdims=True) + p_new + jnp.exp(sink - m)
        o = (_dot(p.astype(BF), vc.astype(BF)) + p_new * vn) * (1.0 / den)
        att_ref[j] = o
        nk_ref[j] = jnp.concatenate([kc[1:, :], kn], axis=0)
        nv_ref[j] = jnp.concatenate([vc[1:, :], vn], axis=0)

    for hd in range(N_HEADS_B):
        sl = slice(hd * LANES, (hd + 1) * LANES)
        kin_t = kin_ref[:, sl].T
        q_t = qb_ref[:, sl].T
        for j in range(bt):
            kcol = kin_t[:, j:j + 1]
            qcol = q_t[:, j:j + 1]
            st = st_ref[j, hd]
            new = st - kcol * (st - ib_ref[j:j + 1, sl])
            nst_ref[j, hd] = new
            ob_ref[j:j + 1, sl] = jnp.sum(qcol * new, axis=0, keepdims=True)


def _dec_mix_call(qm, kn, vn, qb, kin, ib, sink_col, cache_k, cache_v, state, bt):
    b = kn.shape[0]
    rows = lambda n: pl.BlockSpec((bt, n), lambda i: (i, 0))
    cache_spec = pl.BlockSpec((bt, WINDOW, KV_WIDTH), lambda i: (i, 0, 0))
    st_spec = pl.BlockSpec((bt, N_HEADS_B, HEAD_DIM_B, HEAD_DIM_B), lambda i: (i, 0, 0, 0))
    qm_spec = pl.BlockSpec((bt, N_HEADS_A, LANES), lambda i: (i, 0, 0))
    return pl.pallas_call(
        functools.partial(_dec_mix_kernel, bt=bt),
        grid=(b // bt,),
        in_specs=[qm_spec, rows(KV_WIDTH), rows(KV_WIDTH), rows(WIDTH_B), rows(WIDTH_B),
                  rows(WIDTH_B), pl.BlockSpec((N_HEADS_A, 1), lambda i: (0, 0)),
                  cache_spec, cache_spec, st_spec],
        out_specs=[qm_spec, rows(WIDTH_B), cache_spec, cache_spec, st_spec],
        out_shape=[jax.ShapeDtypeStruct((b, N_HEADS_A, LANES), F32),
                   jax.ShapeDtypeStruct((b, WIDTH_B), F32),
                   jax.ShapeDtypeStruct(cache_k.shape, F32),
                   jax.ShapeDtypeStruct(cache_v.shape, F32),
                   jax.ShapeDtypeStruct(state.shape, F32)],
        compiler_params=pltpu.CompilerParams(
            dimension_semantics=("arbitrary",), vmem_limit_bytes=VMEM_LIMIT),
        name="decode_mix",
    )(qm, kn, vn, qb, kin, ib, sink_col, cache_k, cache_v, state)


def _dec_out_kernel(x_ref, gate_ref, att_ref, za_ref, ob_ref, zb_ref, g_ref, og_ref,
                    permt_ref, w_pa_ref, w_pb_ref, w_out_ref, y_ref):
    att = _split_dot(att_ref[...], permt_ref[...])
    ya = att * _silu(za_ref[...])
    yb = _branch_b_out(ob_ref[...], zb_ref[...], og_ref[...])
    g = g_ref[...]
    y = (g[:, :D_MODEL] * _dot(ya.astype(BF), w_pa_ref[...])
         + g[:, D_MODEL:] * _dot(yb.astype(BF), w_pb_ref[...]))
    y_ref[...] = x_ref[...] + gate_ref[...] * _dot(y.astype(BF), w_out_ref[...])


def _dec_out_call(x, gate, att, za, ob, zb, g, consts, w):
    return pl.pallas_call(
        _dec_out_kernel,
        out_shape=jax.ShapeDtypeStruct(x.shape, F32),
        compiler_params=pltpu.CompilerParams(vmem_limit_bytes=VMEM_LIMIT),
        name="decode_out",
    )(x, gate, att, za, ob, zb, g, w["o_g"], consts["permt"],
      w["w_proj_a"], w["w_proj_b"], w["w_out"])


def _rope_tables(pos0, t):
    pos = (pos0 + jnp.arange(t)).astype(F32)
    inv = jnp.power(ROPE_THETA, -jnp.arange(0, ROT_DIM, 2, dtype=F32) / ROT_DIM)
    ang = pos[:, None] * inv[None, :]
    cos, sin = jnp.cos(ang), jnp.sin(ang)
    half = ROT_DIM // 2
    pad = HEAD_DIM_A - ROT_DIM
    ones = jnp.ones((t, pad), F32)
    zeros = jnp.zeros((t, pad), F32)
    zh = jnp.zeros((t, half), F32)
    c = jnp.concatenate([cos, cos, ones], axis=1)
    up = jnp.concatenate([-sin, zh, zeros], axis=1)
    dn = jnp.concatenate([zh, sin, zeros], axis=1)
    rep = lambda a: jnp.concatenate([a, a], axis=1)
    return rep(c), rep(up), rep(dn)


def _static_consts():
    seg = np.kron(np.eye(N_HEADS_A), np.full((HEAD_DIM_A, HEAD_DIM_A), 1.0 / HEAD_DIM_A))
    tri = np.tril(np.ones((BLK, BLK)))
    r = np.arange(4 * BLK)[:, None] % BLK
    c = np.arange(2 * BLK)[None, :]
    ok_prev = (c < BLK) & (c > r)
    ok_cur = (c >= BLK) & (c - BLK <= r)
    bias = np.stack([np.where(ok_cur, 0.0, NEG), np.where(ok_prev | ok_cur, 0.0, NEG)])
    i = np.arange(BLK)[:, None]
    j = np.arange(BLK)[None, :]
    lmask = np.stack([((i // (2 * b)) == (j // (2 * b))) & ((i & b) != 0) & ((j & b) == 0)
                      for b in LEVELS]).astype(np.float32)
    perm = np.zeros((WIDTH_A, N_HEADS_A * LANES), np.float32)
    for hd in range(N_HEADS_A):
        kvh = hd // (N_HEADS_A // N_KV_A)
        for d in range(HEAD_DIM_A):
            perm[hd * HEAD_DIM_A + d, hd * LANES + kvh * HEAD_DIM_A + d] = 1.0
    return {
        "seg": jnp.asarray(seg, BF), "tri": jnp.asarray(tri, BF),
        "bias": jnp.asarray(bias, F32), "lmask": jnp.asarray(lmask, F32),
        "perm": jnp.asarray(perm, BF), "permt": jnp.asarray(perm.T, BF),
    }


def kernel(x_prompt, x_sample, cache_win_k, cache_win_v, state_hgrn, c_prompt, c_sample,
           w_ada, b_ada, norm_g, w_in, q_norm_g, k_norm_g, sinks, lb_logits, o_norm_g,
           w_merge, b_merge, w_proj_a, w_proj_b, w_out):
    depth = w_in.shape[0]
    assert depth == 1 and x_prompt.shape[0] == 1 and x_sample.shape[1] == 1
    t = x_prompt.shape[1]
    nb = x_sample.shape[0]
    past_len = t
    tile = 256
    bt = 8

    consts = _static_consts()
    consts["rope_c"], consts["rope_u"], consts["rope_d"] = _rope_tables(0, t)
    consts["rope_c1"], consts["rope_u1"], consts["rope_d1"] = _rope_tables(past_len, 1)

    w = {
        "sinks": sinks[0], "norm_g": norm_g[0][None, :],
        "w_in": w_in[0].astype(BF), "w_merge": w_merge[0].astype(BF),
        "b_merge": b_merge[0][None, :],
        "w_proj_a": w_proj_a[0].astype(BF), "w_proj_b": w_proj_b[0].astype(BF),
        "w_out": w_out[0].astype(BF),
        "q_g": jnp.tile(q_norm_g[0], N_HEADS_A)[None, :],
        "k_g": jnp.tile(k_norm_g[0], N_KV_A)[None, :],
        "o_g": jnp.tile(o_norm_g[0], N_HEADS_B)[None, :],
        "lb_logits": lb_logits,
    }

    n_c = 1 + nb
    pad = (-n_c) % SUB
    c_all = jnp.concatenate([c_prompt, c_sample, jnp.zeros((pad, D_MODEL), F32)], axis=0)
    mod = _ada_call(c_all, w_ada[0], b_ada[0][None, :])
    shift, scale, gate = (mod[:, k * D_MODEL:(k + 1) * D_MODEL] for k in range(3))

    y_p, kwin, vwin, st_p = _prompt_call(
        x_prompt[0], shift[0:1], scale[0:1], gate[0:1], consts, w, tile)

    xs = x_sample[:, 0, :]
    qm, kn, vn, za, qb, kin, ib, zb, g = _dec_in_call(
        xs, shift[1:n_c], scale[1:n_c], consts, w)
    att, ob, nk, nv, nst = _dec_mix_call(
        qm.reshape(nb, N_HEADS_A, LANES), kn, vn, qb, kin, ib, sinks[0][:, None],
        cache_win_k[0].reshape(nb, WINDOW, KV_WIDTH), cache_win_v[0].reshape(nb, WINDOW, KV_WIDTH),
        state_hgrn[0], bt)
    y_s = _dec_out_call(xs, gate[1:n_c], att.reshape(nb, N_HEADS_A * LANES), za, ob, zb, g,
                        consts, w)

    kv_shape = (1, 1, WINDOW, N_KV_A, HEAD_DIM_A)
    kv_shape_s = (1, nb, WINDOW, N_KV_A, HEAD_DIM_A)
    return (y_p[None], y_s[:, None, :],
            kwin.reshape(kv_shape), vwin.reshape(kv_shape), st_p[None, None],
            nk.reshape(kv_shape_s), nv.reshape(kv_shape_s), nst[None])
```

```python
import functools

import numpy as np
import jax
import jax.numpy as jnp
from jax import lax
from jax.experimental import pallas as pl
from jax.experimental.pallas import tpu as pltpu

D_MODEL = 1024
HEAD_DIM_A = 64
N_HEADS_A = 8
N_KV_A = 2
WIDTH_A = N_HEADS_A * HEAD_DIM_A
KV_WIDTH = N_KV_A * HEAD_DIM_A
WINDOW = 128
ROT_DIM = HEAD_DIM_A // 4
ROPE_THETA = 500000.0
HEAD_DIM_B = 128
N_HEADS_B = 4
WIDTH_B = N_HEADS_B * HEAD_DIM_B
EPS = 1e-6
OFF_QA = 0
OFF_KA = OFF_QA + WIDTH_A
OFF_VA = OFF_KA + KV_WIDTH
OFF_ZA = OFF_VA + KV_WIDTH
OFF_QB = OFF_ZA + WIDTH_A
OFF_FB = OFF_QB + WIDTH_B
OFF_IB = OFF_FB + WIDTH_B
OFF_ZB = OFF_IB + WIDTH_B
D_IN = OFF_ZB + WIDTH_B

LANES = 128
BLK = 128
SUB = 8
LEVELS = (64, 32, 16, 8)
NEG = -1e30
VMEM_LIMIT = 56 * 1024 * 1024

BF = jnp.bfloat16
F32 = jnp.float32


def _dot(a, b):
    return jnp.dot(a, b, preferred_element_type=F32)


def _dot_nt(a, b):
    return lax.dot_general(a, b, (((1,), (1,)), ((), ())), preferred_element_type=F32)


def _dot_tn(a, b):
    return lax.dot_general(a, b, (((0,), (0,)), ((), ())), preferred_element_type=F32)


def _split_dot(a_f32, b_bf):
    hi = a_f32.astype(BF)
    lo = (a_f32 - hi.astype(F32)).astype(BF)
    return _dot(hi, b_bf) + _dot(lo, b_bf)


def _sigmoid(x):
    return 1.0 / (1.0 + jnp.exp(-x))


def _silu(x):
    return x * _sigmoid(x)


def _lower_bound(lb_logits):
    l0 = lb_logits[0:1, :]
    l1 = lb_logits[1:2, :]
    m = jnp.maximum(l0, l1)
    e0 = jnp.exp(l0 - m)
    e1 = jnp.exp(l1 - m)
    return e0 / (e0 + e1)


def _rope_tables(cb, sb, cl, sl, sgn_up, sgn_dn):
    c = cb * cl - sb * sl
    s = sb * cl + cb * sl
    return c, s * sgn_up, s * sgn_dn


def _rope(x, c, s_up, s_dn):
    return x * c + pltpu.roll(x, LANES - ROT_DIM // 2, 1) * s_up + pltpu.roll(x, ROT_DIM // 2, 1) * s_dn


def _head_norm_scale(x, seg_mean_bf):
    ms = _dot((x * x).astype(BF), seg_mean_bf)
    return lax.rsqrt(ms + EPS)


def _ada_kernel(cp_ref, cs_ref, w_ref, b_ref, op_ref, os_ref):
    w = w_ref[...]
    b = b_ref[...]
    hp = lax.Precision.HIGHEST
    op_ref[...] = jnp.dot(cp_ref[...], w, preferred_element_type=F32, precision=hp) + b
    os_ref[...] = jnp.dot(cs_ref[...], w, preferred_element_type=F32, precision=hp) + b


def _ada_call(c_p, c_s, w_ada, b_ada):
    mp, ms = c_p.shape[0], c_s.shape[0]
    n = w_ada.shape[1]
    tn = 512
    return pl.pallas_call(
        _ada_kernel,
        grid=(n // tn,),
        in_specs=[pl.BlockSpec((mp, D_MODEL), lambda j: (0, 0)),
                  pl.BlockSpec((ms, D_MODEL), lambda j: (0, 0)),
                  pl.BlockSpec((D_MODEL, tn), lambda j: (0, j)),
                  pl.BlockSpec((1, tn), lambda j: (0, j))],
        out_specs=[pl.BlockSpec((mp, tn), lambda j: (0, j)),
                   pl.BlockSpec((ms, tn), lambda j: (0, j))],
        out_shape=[jax.ShapeDtypeStruct((mp, n), F32), jax.ShapeDtypeStruct((ms, n), F32)],
        name="ada",
    )(c_p, c_s, w_ada, b_ada)


def _attn_block(q_blk, kcat, kcat_sw, vcat, vcat_sw, bias, sink_a, sink_b):
    lane = lax.broadcasted_iota(jnp.int32, (BLK, LANES), 1)
    lo = lane < HEAD_DIM_A
    chunks = [q_blk[:, c * LANES:(c + 1) * LANES] for c in range(4)]
    zero = jnp.zeros((BLK, LANES), F32)
    q_lo = [jnp.where(lo, c, zero).astype(BF) for c in chunks]
    q_hi = [jnp.where(lo, zero, c).astype(BF) for c in chunks]
    qa = jnp.concatenate([q_lo[0], q_lo[1], q_hi[2], q_hi[3]], axis=0)
    qb = jnp.concatenate([q_hi[0], q_hi[1], q_lo[2], q_lo[3]], axis=0)

    def soft(qs, kc, vc, sink):
        s = _dot_nt(qs, kc) + bias
        m = jnp.maximum(jnp.max(s, axis=-1, keepdims=True), sink)
        p = jnp.exp(s - m)
        den = jnp.sum(p, axis=-1, keepdims=True) + jnp.exp(sink - m)
        o = _dot(p.astype(BF), vc)
        return o * (1.0 / den)

    oa = soft(qa, kcat, vcat, sink_a)
    ob = soft(qb, kcat_sw, vcat_sw, sink_b)
    r = lambda o, i: o[i * BLK:(i + 1) * BLK, :]
    return jnp.concatenate([
        jnp.where(lo, r(oa, 0), r(ob, 0)),
        jnp.where(lo, r(oa, 1), r(ob, 1)),
        jnp.where(lo, r(ob, 2), r(oa, 2)),
        jnp.where(lo, r(ob, 3), r(oa, 3)),
    ], axis=1)


def _hgrn_block(qb, fb, ib, lb, one_m_lb, tri_bf, lvl_mask_ref, st_ref):
    sig = _sigmoid(fb)
    kin = one_m_lb * (1.0 - sig)
    f = lb + one_m_lb * sig
    logf = jnp.log(f)
    cum = _split_dot_left(tri_bf, logf)
    q_dec = (qb * jnp.exp(cum)).astype(BF)
    last = cum[BLK - 1:BLK, :]
    k_dec = (kin * jnp.exp(last - cum)).astype(BF)
    v_bf = ib.astype(BF)

    row = lax.broadcasted_iota(jnp.int32, (BLK, 1), 0)
    lvl_ops = []
    for b in LEVELS:
        pieces = []
        for r0 in range(0, BLK, 2 * b):
            piv = cum[r0 + b - 1:r0 + b, :]
            pieces.append(piv - cum[r0:r0 + b, :])
            pieces.append(cum[r0 + b:r0 + 2 * b, :] - piv)
        w = jnp.exp(jnp.concatenate(pieces, axis=0))
        second = (row & b) != 0
        lvl_ops.append((jnp.where(second, qb, kin) * w).astype(BF))

    n8 = BLK // SUB
    q3 = qb.reshape(n8, SUB, WIDTH_B)
    k3 = kin.reshape(n8, SUB, WIDTH_B)
    f3 = f.reshape(n8, SUB, WIDTH_B)
    v3 = ib.reshape(n8, SUB, WIDTH_B)
    subl = lax.broadcasted_iota(jnp.int32, (n8, SUB, 1), 1)

    def head(x, h):
        return x[..., h * LANES:(h + 1) * LANES]

    g = q3 * k3
    acc = [jnp.sum(head(g, h), axis=-1, keepdims=True) * head(v3, h) for h in range(N_HEADS_B)]
    dec = jnp.ones_like(f3)
    kd = k3
    vd = v3
    for d in range(1, SUB):
        dec = f3 * pltpu.roll(dec, 1, 1)
        kd = pltpu.roll(kd, 1, 1)
        vd = pltpu.roll(vd, 1, 1)
        g = q3 * kd * dec
        ok = subl >= d
        for h in range(N_HEADS_B):
            a = jnp.where(ok, jnp.sum(head(g, h), axis=-1, keepdims=True), 0.0)
            acc[h] = acc[h] + a * head(vd, h)

    outs = []
    for h in range(N_HEADS_B):
        sl = slice(h * LANES, (h + 1) * LANES)
        st = st_ref[h]
        o = _dot_nt(q_dec[:, sl], st.astype(BF))
        amat = jnp.zeros((BLK, BLK), F32)
        for li in range(len(LEVELS)):
            p = lvl_ops[li][:, sl]
            amat = amat + lvl_mask_ref[li] * _dot_nt(p, p)
        o = o + _dot(amat.astype(BF), v_bf[:, sl]) + acc[h].reshape(BLK, LANES)
        outs.append(o)
        st_ref[h] = st * jnp.exp(last[:, sl]) + _dot_tn(v_bf[:, sl], k_dec[:, sl])
    return jnp.concatenate(outs, axis=1)


def _split_dot_left(a_bf, b_f32):
    hi = b_f32.astype(BF)
    lo = (b_f32 - hi.astype(F32)).astype(BF)
    return _dot(a_bf, hi) + _dot(a_bf, lo)


def _branch_b_out(o, zb, og):
    outs = []
    for h in range(N_HEADS_B):
        sl = slice(h * LANES, (h + 1) * LANES)
        oh = o[:, sl]
        ms = jnp.mean(oh * oh, axis=-1, keepdims=True)
        outs.append(oh * lax.rsqrt(ms + EPS))
    return jnp.concatenate(outs, axis=1) * og * _silu(zb)


def _prompt_kernel(sinks_ref,
                   x_ref, shift_ref, scale_ref, gate_ref, ng_ref,
                   w_in_ref, w_mg_ref, b_mg_ref, w_pa_ref, w_pb_ref, w_out_ref,
                   qg_ref, kg_ref, og_ref, lbl_ref,
                   cb_ref, sb_ref, cl_ref, sl_ref, sgn_ref,
                   seg_ref, tri_ref, bias_ref, lmask_ref,
                   y_ref, kwin_ref, vwin_ref, state_ref,
                   st_ref, kbuf_ref, ksw_ref, vbuf_ref, vsw_ref,
                   *, tile, n_tiles):
    i = pl.program_id(0)
    nblk = tile // BLK

    @pl.when(i == 0)
    def _():
        st_ref[...] = jnp.zeros_like(st_ref)
        for r in (kbuf_ref, ksw_ref, vbuf_ref, vsw_ref):
            r[...] = jnp.zeros_like(r)

    x = x_ref[...]
    ms = jnp.mean(x * x, axis=-1, keepdims=True)
    h = x * lax.rsqrt(ms + EPS) * (ng_ref[...] * (1.0 + scale_ref[...])) + shift_ref[...]
    h_bf = h.astype(BF)

    proj = _dot(h_bf, w_in_ref[...])
    gates = _sigmoid(_dot(h_bf, w_mg_ref[...]) + b_mg_ref[...])

    seg = seg_ref[...]
    rc, ru, rd = _rope_tables(cb_ref[pl.ds(i, 1), :], sb_ref[pl.ds(i, 1), :],
                              cl_ref[...], sl_ref[...], sgn_ref[0:1, :], sgn_ref[1:2, :])
    qa = proj[:, OFF_QA:OFF_QA + WIDTH_A]
    qa = qa * _head_norm_scale(qa, seg) * (qg_ref[...] * (HEAD_DIM_A ** -0.5))
    qa = jnp.concatenate(
        [_rope(qa[:, c * LANES:(c + 1) * LANES], rc, ru, rd) for c in range(4)], axis=1)
    ka = proj[:, OFF_KA:OFF_KA + KV_WIDTH]
    ka = ka * _head_norm_scale(ka, seg[:KV_WIDTH, :KV_WIDTH]) * kg_ref[...]
    ka = _rope(ka, rc, ru, rd)
    va = proj[:, OFF_VA:OFF_VA + KV_WIDTH]
    ka_sw = pltpu.roll(ka, HEAD_DIM_A, 1)
    va_sw = pltpu.roll(va, HEAD_DIM_A, 1)

    lb = _lower_bound(lbl_ref[...])
    one_m_lb = 1.0 - lb
    tri = tri_ref[...]

    rows4 = lax.broadcasted_iota(jnp.int32, (4 * BLK, 1), 0) // BLK
    def sink_col(heads):
        col = jnp.zeros((4 * BLK, 1), F32)
        for n, hd in enumerate(heads):
            col = jnp.where(rows4 == n, sinks_ref[hd], col)
        return col
    sink_a = sink_col((0, 2, 5, 7))
    sink_b = sink_col((1, 3, 4, 6))

    ya_parts, ob_parts = [], []
    for blk in range(nblk):
        rs = slice(blk * BLK, (blk + 1) * BLK)
        cur = slice(BLK, 2 * BLK)
        kbuf_ref[cur, :] = ka[rs].astype(BF)
        ksw_ref[cur, :] = ka_sw[rs].astype(BF)
        vbuf_ref[cur, :] = va[rs].astype(BF)
        vsw_ref[cur, :] = va_sw[rs].astype(BF)
        first = jnp.logical_and(i == 0, blk == 0)
        bias = bias_ref[jnp.where(first, 0, 1)]
        ya_parts.append(_attn_block(qa[rs], kbuf_ref[...], ksw_ref[...], vbuf_ref[...],
                                    vsw_ref[...], bias, sink_a, sink_b))
        for r in (kbuf_ref, ksw_ref, vbuf_ref, vsw_ref):
            r[0:BLK, :] = r[cur, :]
        ob_parts.append(_hgrn_block(proj[rs, OFF_QB:OFF_QB + WIDTH_B],
                                    proj[rs, OFF_FB:OFF_FB + WIDTH_B],
                                    proj[rs, OFF_IB:OFF_IB + WIDTH_B],
                                    lb, one_m_lb, tri, lmask_ref, st_ref))

    ya = jnp.concatenate(ya_parts, axis=0) * _silu(proj[:, OFF_ZA:OFF_ZA + WIDTH_A])
    yb = _branch_b_out(jnp.concatenate(ob_parts, axis=0),
                       proj[:, OFF_ZB:OFF_ZB + WIDTH_B], og_ref[...])
    y = (gates[:, :D_MODEL] * _dot(ya.astype(BF), w_pa_ref[...])
         + gates[:, D_MODEL:] * _dot(yb.astype(BF), w_pb_ref[...]))
    y_ref[...] = x + gate_ref[...] * _dot(y.astype(BF), w_out_ref[...])

    @pl.when(i == n_tiles - 1)
    def _():
        kwin_ref[...] = ka[tile - WINDOW:, :]
        vwin_ref[...] = va[tile - WINDOW:, :]
        for hd in range(N_HEADS_B):
            state_ref[hd] = st_ref[hd].T


def _const_spec(shape):
    nd = len(shape)
    return pl.BlockSpec(shape, lambda i, *_: (0,) * nd, pipeline_mode=pl.Buffered(1))


def _prompt_call(x, shift, scale, gate, consts, w, tile):
    t = x.shape[0]
    n_tiles = t // tile
    row = lambda n: _const_spec((1, n))
    in_specs = [
        pl.BlockSpec((tile, D_MODEL), lambda i, *_: (i, 0)),
        row(D_MODEL), row(D_MODEL), row(D_MODEL), row(D_MODEL),
        _const_spec((D_MODEL, D_IN)), _const_spec((D_MODEL, 2 * D_MODEL)), row(2 * D_MODEL),
        _const_spec((WIDTH_A, D_MODEL)), _const_spec((WIDTH_B, D_MODEL)),
        _const_spec((D_MODEL, D_MODEL)),
        row(WIDTH_A), row(KV_WIDTH), row(WIDTH_B), _const_spec((2, WIDTH_B)),
        _const_spec((n_tiles, LANES)), _const_spec((n_tiles, LANES)),
        _const_spec((tile, LANES)), _const_spec((tile, LANES)), _const_spec((2, LANES)),
        _const_spec((WIDTH_A, WIDTH_A)), _const_spec((BLK, BLK)),
        _const_spec((2, 4 * BLK, 2 * BLK)), _const_spec((len(LEVELS), BLK, BLK)),
    ]
    out_specs = [
        pl.BlockSpec((tile, D_MODEL), lambda i, *_: (i, 0)),
        pl.BlockSpec((WINDOW, KV_WIDTH), lambda i, *_: (0, 0)),
        pl.BlockSpec((WINDOW, KV_WIDTH), lambda i, *_: (0, 0)),
        pl.BlockSpec((N_HEADS_B, HEAD_DIM_B, HEAD_DIM_B), lambda i, *_: (0, 0, 0)),
    ]
    out_shape = [
        jax.ShapeDtypeStruct((t, D_MODEL), F32),
        jax.ShapeDtypeStruct((WINDOW, KV_WIDTH), F32),
        jax.ShapeDtypeStruct((WINDOW, KV_WIDTH), F32),
        jax.ShapeDtypeStruct((N_HEADS_B, HEAD_DIM_B, HEAD_DIM_B), F32),
    ]
    scratch = [
        pltpu.VMEM((N_HEADS_B, HEAD_DIM_B, HEAD_DIM_B), F32),
        pltpu.VMEM((2 * BLK, KV_WIDTH), BF), pltpu.VMEM((2 * BLK, KV_WIDTH), BF),
        pltpu.VMEM((2 * BLK, KV_WIDTH), BF), pltpu.VMEM((2 * BLK, KV_WIDTH), BF),
    ]
    return pl.pallas_call(
        functools.partial(_prompt_kernel, tile=tile, n_tiles=n_tiles),
        grid_spec=pltpu.PrefetchScalarGridSpec(
            num_scalar_prefetch=1, grid=(n_tiles,),
            in_specs=in_specs, out_specs=out_specs, scratch_shapes=scratch),
        out_shape=out_shape,
        compiler_params=pltpu.CompilerParams(
            dimension_semantics=("arbitrary",), vmem_limit_bytes=VMEM_LIMIT),
        name="prompt_layer",
    )(w["sinks"], x, shift, scale, gate, w["norm_g"],
      w["w_in"], w["w_merge"], w["b_merge"], w["w_proj_a"], w["w_proj_b"], w["w_out"],
      w["q_g"], w["k_g"], w["o_g"], w["lb_logits"],
      consts["rope_cb"], consts["rope_sb"], consts["rope_cl"], consts["rope_sl"],
      consts["rope_sgn"],
      consts["seg"], consts["tri"], consts["bias"], consts["lmask"])


def _dec_in_kernel(x_ref, shift_ref, scale_ref, ng_ref, w_in_ref, w_mg_ref, b_mg_ref,
                   qg_ref, kg_ref, lbl_ref, rc_ref, ru_ref, rd_ref, seg_ref, perm_ref,
                   qm_ref, kn_ref, vn_ref, za_ref, qb_ref, kin_ref, ib_ref, zb_ref, g_ref):
    x = x_ref[...]
    ms = jnp.mean(x * x, axis=-1, keepdims=True)
    h = x * lax.rsqrt(ms + EPS) * (ng_ref[...] * (1.0 + scale_ref[...])) + shift_ref[...]
    h_bf = h.astype(BF)
    proj = _dot(h_bf, w_in_ref[...])
    g_ref[...] = _sigmoid(_dot(h_bf, w_mg_ref[...]) + b_mg_ref[...])

    seg = seg_ref[...]
    rc, ru, rd = rc_ref[...], ru_ref[...], rd_ref[...]
    qa = proj[:, OFF_QA:OFF_QA + WIDTH_A]
    qa = qa * _head_norm_scale(qa, seg) * (qg_ref[...] * (HEAD_DIM_A ** -0.5))
    qa = jnp.concatenate(
        [_rope(qa[:, c * LANES:(c + 1) * LANES], rc, ru, rd) for c in range(4)], axis=1)
    qm_ref[...] = _dot(qa.astype(BF), perm_ref[...])
    ka = proj[:, OFF_KA:OFF_KA + KV_WIDTH]
    ka = ka * _head_norm_scale(ka, seg[:KV_WIDTH, :KV_WIDTH]) * kg_ref[...]
    kn_ref[...] = _rope(ka, rc, ru, rd)
    vn_ref[...] = proj[:, OFF_VA:OFF_VA + KV_WIDTH]
    za_ref[...] = proj[:, OFF_ZA:OFF_ZA + WIDTH_A]
    qb_ref[...] = proj[:, OFF_QB:OFF_QB + WIDTH_B]
    lb = _lower_bound(lbl_ref[...])
    kin_ref[...] = (1.0 - lb) * (1.0 - _sigmoid(proj[:, OFF_FB:OFF_FB + WIDTH_B]))
    ib_ref[...] = proj[:, OFF_IB:OFF_IB + WIDTH_B]
    zb_ref[...] = proj[:, OFF_ZB:OFF_ZB + WIDTH_B]


def _dec_in_call(x, shift, scale, consts, w):
    b = x.shape[0]
    f = lambda n: jax.ShapeDtypeStruct((b, n), F32)
    return pl.pallas_call(
        _dec_in_kernel,
        out_shape=[f(N_HEADS_A * LANES), f(KV_WIDTH), f(KV_WIDTH), f(WIDTH_A),
                   f(WIDTH_B), f(WIDTH_B), f(WIDTH_B), f(WIDTH_B), f(2 * D_MODEL)],
        compiler_params=pltpu.CompilerParams(vmem_limit_bytes=VMEM_LIMIT),
        name="decode_in",
    )(x, shift, scale, w["norm_g"], w["w_in"], w["w_merge"], w["b_merge"],
      w["q_g"], w["k_g"], w["lb_logits"],
      consts["rope_c1"], consts["rope_u1"], consts["rope_d1"], consts["seg"], consts["perm"])


def _dec_mix_kernel(qm_ref, kn_ref, vn_ref, qb_ref, kin_ref, ib_ref, sink_ref,
                    ck_ref, cv_ref, st_ref,
                    att_ref, ob_ref, nk_ref, nv_ref, nst_ref, *, bt):
    key = lax.broadcasted_iota(jnp.int32, (N_HEADS_A, WINDOW), 1)
    sink = sink_ref[...]
    for j in range(bt):
        qm = qm_ref[j]
        kc = ck_ref[j]
        vc = cv_ref[j]
        kn = kn_ref[j:j + 1, :]
        vn = vn_ref[j:j + 1, :]
        s = _dot_nt(qm.astype(BF), kc.astype(BF))
        s = jnp.where(key == 0, NEG, s)
        s_new = jnp.sum(qm * kn, axis=-1, keepdims=True)
        m = jnp.maximum(jnp.maximum(jnp.max(s, axis=-1, keepdims=True), s_new), sink)
        p = jnp.exp(s - m)
        p_new = jnp.exp(s_new - m)
        den = jnp.sum(p, axis=-1, keepdims=True) + p_new + jnp.exp(sink - m)
        o = (_dot(p.astype(BF), vc.astype(BF)) + p_new * vn) * (1.0 / den)
        att_ref[j] = o
        nk_ref[j] = jnp.concatenate([kc[1:, :], kn], axis=0)
        nv_ref[j] = jnp.concatenate([vc[1:, :], vn], axis=0)

    for hd in range(N_HEADS_B):
        sl = slice(hd * LANES, (hd + 1) * LANES)
        kin_t = kin_ref[:, sl].T
        q_t = qb_ref[:, sl].T
        for j in range(bt):
            kcol = kin_t[:, j:j + 1]
            qcol = q_t[:, j:j + 1]
            st = st_ref[j, hd]
            new = st - kcol * (st - ib_ref[j:j + 1, sl])
            nst_ref[j, hd] = new
            ob_ref[j:j + 1, sl] = jnp.sum(qcol * new, axis=0, keepdims=True)


def _dec_mix_call(qm, kn, vn, qb, kin, ib, sink_col, cache_k, cache_v, state, bt):
    b = kn.shape[0]
    rows = lambda n: pl.BlockSpec((bt, n), lambda i: (i, 0))
    cache_spec = pl.BlockSpec((bt, WINDOW, KV_WIDTH), lambda i: (i, 0, 0))
    st_spec = pl.BlockSpec((bt, N_HEADS_B, HEAD_DIM_B, HEAD_DIM_B), lambda i: (i, 0, 0, 0))
    qm_spec = pl.BlockSpec((bt, N_HEADS_A, LANES), lambda i: (i, 0, 0))
    return pl.pallas_call(
        functools.partial(_dec_mix_kernel, bt=bt),
        grid=(b // bt,),
        in_specs=[qm_spec, rows(KV_WIDTH), rows(KV_WIDTH), rows(WIDTH_B), rows(WIDTH_B),
                  rows(WIDTH_B), pl.BlockSpec((N_HEADS_A, 1), lambda i: (0, 0)),
                  cache_spec, cache_spec, st_spec],
        out_specs=[qm_spec, rows(WIDTH_B), cache_spec, cache_spec, st_spec],
        out_shape=[jax.ShapeDtypeStruct((b, N_HEADS_A, LANES), F32),
                   jax.ShapeDtypeStruct((b, WIDTH_B), F32),
                   jax.ShapeDtypeStruct(cache_k.shape, F32),
                   jax.ShapeDtypeStruct(cache_v.shape, F32),
                   jax.ShapeDtypeStruct(state.shape, F32)],
        compiler_params=pltpu.CompilerParams(
            dimension_semantics=("arbitrary",), vmem_limit_bytes=VMEM_LIMIT),
        name="decode_mix",
    )(qm, kn, vn, qb, kin, ib, sink_col, cache_k, cache_v, state)


def _dec_out_kernel(x_ref, gate_ref, att_ref, za_ref, ob_ref, zb_ref, g_ref, og_ref,
                    permt_ref, w_pa_ref, w_pb_ref, w_out_ref, y_ref):
    att = _split_dot(att_ref[...], permt_ref[...])
    ya = att * _silu(za_ref[...])
    yb = _branch_b_out(ob_ref[...], zb_ref[...], og_ref[...])
    g = g_ref[...]
    y = (g[:, :D_MODEL] * _dot(ya.astype(BF), w_pa_ref[...])
         + g[:, D_MODEL:] * _dot(yb.astype(BF), w_pb_ref[...]))
    y_ref[...] = x_ref[...] + gate_ref[...] * _dot(y.astype(BF), w_out_ref[...])


def _dec_out_call(x, gate, att, za, ob, zb, g, consts, w):
    return pl.pallas_call(
        _dec_out_kernel,
        out_shape=jax.ShapeDtypeStruct(x.shape, F32),
        compiler_params=pltpu.CompilerParams(vmem_limit_bytes=VMEM_LIMIT),
        name="decode_out",
    )(x, gate, att, za, ob, zb, g, w["o_g"], consts["permt"],
      w["w_proj_a"], w["w_proj_b"], w["w_out"])


def _rope_consts(n_tiles, tile, past_len):
    half = ROT_DIM // 2
    inv = ROPE_THETA ** (-np.arange(0, ROT_DIM, 2, dtype=np.float64) / ROT_DIM)
    e = np.arange(LANES) % HEAD_DIM_A
    rot = e < ROT_DIM

    def tables(pos):
        ang = np.asarray(pos, np.float64)[:, None] * inv[e % half][None, :]
        return np.where(rot, np.cos(ang), 1.0), np.where(rot, np.sin(ang), 0.0)

    sgn = np.stack([np.where(e < half, -1.0, 0.0),
                    np.where(rot & (e >= half), 1.0, 0.0)])
    cb, sb = tables(np.arange(n_tiles) * tile)
    cl, sl = tables(np.arange(tile))
    c1, s1 = tables([past_len])
    f = lambda a: jnp.asarray(a, F32)
    return {"rope_cb": f(cb), "rope_sb": f(sb), "rope_cl": f(cl), "rope_sl": f(sl),
            "rope_sgn": f(sgn), "rope_c1": f(c1), "rope_u1": f(s1 * sgn[0:1]),
            "rope_d1": f(s1 * sgn[1:2])}


def _static_consts():
    seg = np.kron(np.eye(N_HEADS_A), np.full((HEAD_DIM_A, HEAD_DIM_A), 1.0 / HEAD_DIM_A))
    tri = np.tril(np.ones((BLK, BLK)))
    r = np.arange(4 * BLK)[:, None] % BLK
    c = np.arange(2 * BLK)[None, :]
    ok_prev = (c < BLK) & (c > r)
    ok_cur = (c >= BLK) & (c - BLK <= r)
    bias = np.stack([np.where(ok_cur, 0.0, NEG), np.where(ok_prev | ok_cur, 0.0, NEG)])
    i = np.arange(BLK)[:, None]
    j = np.arange(BLK)[None, :]
    lmask = np.stack([((i // (2 * b)) == (j // (2 * b))) & ((i & b) != 0) & ((j & b) == 0)
                      for b in LEVELS]).astype(np.float32)
    perm = np.zeros((WIDTH_A, N_HEADS_A * LANES), np.float32)
    for hd in range(N_HEADS_A):
        kvh = hd // (N_HEADS_A // N_KV_A)
        for d in range(HEAD_DIM_A):
            perm[hd * HEAD_DIM_A + d, hd * LANES + kvh * HEAD_DIM_A + d] = 1.0
    return {
        "seg": jnp.asarray(seg, BF), "tri": jnp.asarray(tri, BF),
        "bias": jnp.asarray(bias, F32), "lmask": jnp.asarray(lmask, F32),
        "perm": jnp.asarray(perm, BF), "permt": jnp.asarray(perm.T, BF),
    }


def kernel(x_prompt, x_sample, cache_win_k, cache_win_v, state_hgrn, c_prompt, c_sample,
           w_ada, b_ada, norm_g, w_in, q_norm_g, k_norm_g, sinks, lb_logits, o_norm_g,
           w_merge, b_merge, w_proj_a, w_proj_b, w_out):
    depth = w_in.shape[0]
    assert depth == 1 and x_prompt.shape[0] == 1 and x_sample.shape[1] == 1
    t = x_prompt.shape[1]
    nb = x_sample.shape[0]
    past_len = t
    tile = 256
    bt = 8

    consts = _static_consts()
    consts.update(_rope_consts(t // tile, tile, past_len))

    w = {
        "sinks": sinks[0], "norm_g": norm_g[0][None, :],
        "w_in": w_in[0].astype(BF), "w_merge": w_merge[0].astype(BF),
        "b_merge": b_merge[0][None, :],
        "w_proj_a": w_proj_a[0].astype(BF), "w_proj_b": w_proj_b[0].astype(BF),
        "w_out": w_out[0].astype(BF),
        "q_g": jnp.tile(q_norm_g[0], N_HEADS_A)[None, :],
        "k_g": jnp.tile(k_norm_g[0], N_KV_A)[None, :],
        "o_g": jnp.tile(o_norm_g[0], N_HEADS_B)[None, :],
        "lb_logits": lb_logits,
    }

    mod_p, mod_s = _ada_call(c_prompt, c_sample, w_ada[0], b_ada)
    split3 = lambda m: (m[:, k * D_MODEL:(k + 1) * D_MODEL] for k in range(3))
    shift_p, scale_p, gate_p = split3(mod_p)
    shift_s, scale_s, gate_s = split3(mod_s)

    y_p, kwin, vwin, st_p = _prompt_call(
        x_prompt[0], shift_p, scale_p, gate_p, consts, w, tile)

    xs = x_sample[:, 0, :]
    qm, kn, vn, za, qb, kin, ib, zb, g = _dec_in_call(
        xs, shift_s, scale_s, consts, w)
    att, ob, nk, nv, nst = _dec_mix_call(
        qm.reshape(nb, N_HEADS_A, LANES), kn, vn, qb, kin, ib, sinks[0][:, None],
        cache_win_k[0].reshape(nb, WINDOW, KV_WIDTH), cache_win_v[0].reshape(nb, WINDOW, KV_WIDTH),
        state_hgrn[0], bt)
    y_s = _dec_out_call(xs, gate_s, att.reshape(nb, N_HEADS_A * LANES), za, ob, zb, g,
                        consts, w)

    kv_shape = (1, 1, WINDOW, N_KV_A, HEAD_DIM_A)
    kv_shape_s = (1, nb, WINDOW, N_KV_A, HEAD_DIM_A)
    return (y_p[None], y_s[:, None, :],
            kwin.reshape(kv_shape), vwin.reshape(kv_shape), st_p[None, None],
            nk.reshape(kv_shape_s), nv.reshape(kv_shape_s), nst[None])
```

```python
import functools

import numpy as np
import jax
import jax.numpy as jnp
from jax import lax
from jax.experimental import pallas as pl
from jax.experimental.pallas import tpu as pltpu

D_MODEL = 1024
HEAD_DIM_A = 64
N_HEADS_A = 8
N_KV_A = 2
WIDTH_A = N_HEADS_A * HEAD_DIM_A
KV_WIDTH = N_KV_A * HEAD_DIM_A
WINDOW = 128
ROT_DIM = HEAD_DIM_A // 4
ROPE_THETA = 500000.0
HEAD_DIM_B = 128
N_HEADS_B = 4
WIDTH_B = N_HEADS_B * HEAD_DIM_B
EPS = 1e-6
OFF_QA = 0
OFF_KA = OFF_QA + WIDTH_A
OFF_VA = OFF_KA + KV_WIDTH
OFF_ZA = OFF_VA + KV_WIDTH
OFF_QB = OFF_ZA + WIDTH_A
OFF_FB = OFF_QB + WIDTH_B
OFF_IB = OFF_FB + WIDTH_B
OFF_ZB = OFF_IB + WIDTH_B
D_IN = OFF_ZB + WIDTH_B

LANES = 128
BLK = 128
SUB = 8
LEVELS = (64, 32, 16, 8)
HGRN_FAST_MAX = 80.0
MXU_N = 256
W_PREP, W_ATTN, W_GATES, W_RECUR, W_MERGE, W_OUT = 1.2, 1.0, 0.5, 0.0, 0.7, 0.0
NEG = -1e30
VMEM_LIMIT = 56 * 1024 * 1024

BF = jnp.bfloat16
F32 = jnp.float32


def _dot(a, b):
    return jnp.dot(a, b, preferred_element_type=F32)


def _dot_nt(a, b):
    return lax.dot_general(a, b, (((1,), (1,)), ((), ())), preferred_element_type=F32)


def _dot_tn(a, b):
    return lax.dot_general(a, b, (((0,), (0,)), ((), ())), preferred_element_type=F32)


def _split_dot(a_f32, b_bf):
    hi = a_f32.astype(BF)
    lo = (a_f32 - hi.astype(F32)).astype(BF)
    return _dot(hi, b_bf) + _dot(lo, b_bf)


def _sigmoid(x):
    return 1.0 / (1.0 + jnp.exp(-x))


def _silu(x):
    return x * _sigmoid(x)


def _lower_bound(lb_logits):
    l0 = lb_logits[0:1, :]
    l1 = lb_logits[1:2, :]
    m = jnp.maximum(l0, l1)
    e0 = jnp.exp(l0 - m)
    e1 = jnp.exp(l1 - m)
    return e0 / (e0 + e1)


def _rope_tables(cb, sb, cl, sl, sgn_up, sgn_dn):
    c = cb * cl - sb * sl
    s = sb * cl + cb * sl
    return c, s * sgn_up, s * sgn_dn


def _rope(x, c, s_up, s_dn):
    return x * c + pltpu.roll(x, LANES - ROT_DIM // 2, 1) * s_up + pltpu.roll(x, ROT_DIM // 2, 1) * s_dn


def _head_norm_scale(x, seg_mean_bf):
    ms = _dot((x * x).astype(BF), seg_mean_bf)
    return lax.rsqrt(ms + EPS)


def _ada_kernel(cp_ref, cs_ref, w_ref, b_ref, op_ref, os_ref):
    w = w_ref[...]
    b = b_ref[...]
    hp = lax.Precision.HIGHEST
    op_ref[...] = jnp.dot(cp_ref[...], w, preferred_element_type=F32, precision=hp) + b
    os_ref[...] = jnp.dot(cs_ref[...], w, preferred_element_type=F32, precision=hp) + b


def _ada_call(c_p, c_s, w_ada, b_ada):
    mp, ms = c_p.shape[0], c_s.shape[0]
    n = w_ada.shape[1]
    tn = 512
    return pl.pallas_call(
        _ada_kernel,
        grid=(n // tn,),
        in_specs=[pl.BlockSpec((mp, D_MODEL), lambda j: (0, 0)),
                  pl.BlockSpec((ms, D_MODEL), lambda j: (0, 0)),
                  pl.BlockSpec((D_MODEL, tn), lambda j: (0, j)),
                  pl.BlockSpec((1, tn), lambda j: (0, j))],
        out_specs=[pl.BlockSpec((mp, tn), lambda j: (0, j)),
                   pl.BlockSpec((ms, tn), lambda j: (0, j))],
        out_shape=[jax.ShapeDtypeStruct((mp, n), F32), jax.ShapeDtypeStruct((ms, n), F32)],
        name="ada",
    )(c_p, c_s, w_ada, b_ada)


def _attn_block(q_blk, kcat, kcat_sw, vcat, vcat_sw, bias, sink_a, sink_b):
    lane = lax.broadcasted_iota(jnp.int32, (BLK, LANES), 1)
    lo = lane < HEAD_DIM_A
    chunks = [q_blk[:, c * LANES:(c + 1) * LANES] for c in range(4)]
    zero = jnp.zeros((BLK, LANES), F32)
    q_lo = [jnp.where(lo, c, zero).astype(BF) for c in chunks]
    q_hi = [jnp.where(lo, zero, c).astype(BF) for c in chunks]
    qa = jnp.concatenate([q_lo[0], q_lo[1], q_hi[2], q_hi[3]], axis=0)
    qb = jnp.concatenate([q_hi[0], q_hi[1], q_lo[2], q_lo[3]], axis=0)

    def soft(qs, kc, vc, sink):
        s = _dot_nt(qs, kc) + bias
        m = jnp.maximum(jnp.max(s, axis=-1, keepdims=True), sink)
        p = jnp.exp(s - m)
        den = jnp.sum(p, axis=-1, keepdims=True) + jnp.exp(sink - m)
        o = _dot(p.astype(BF), vc)
        return o * (1.0 / den)

    oa = soft(qa, kcat, vcat, sink_a)
    ob = soft(qb, kcat_sw, vcat_sw, sink_b)
    r = lambda o, i: o[i * BLK:(i + 1) * BLK, :]
    return jnp.concatenate([
        jnp.where(lo, r(oa, 0), r(ob, 0)),
        jnp.where(lo, r(oa, 1), r(ob, 1)),
        jnp.where(lo, r(ob, 2), r(oa, 2)),
        jnp.where(lo, r(ob, 3), r(oa, 3)),
    ], axis=1)


def _hgrn_gates(fb, lb, one_m_lb, tri_bf):
    sig = _sigmoid(fb)
    kin = one_m_lb * (1.0 - sig)
    f = lb + one_m_lb * sig
    cum = _split_dot_left(tri_bf, jnp.log(f))
    return kin, f, cum


def _hgrn_span_decay(cum):
    q = BLK // 4
    ends = [cum[(n + 1) * q - 1:(n + 1) * q, :] for n in range(4)]
    d = -ends[0]
    for n in range(1, 4):
        d = jnp.maximum(d, ends[n - 1] - ends[n])
    return d


def _hgrn_finish(qb, kin, cum, ib, amats, extra, st_ref):
    q_dec = (qb * jnp.exp(cum)).astype(BF)
    last = cum[BLK - 1:BLK, :]
    k_dec = (kin * jnp.exp(last - cum)).astype(BF)
    v_bf = ib.astype(BF)
    outs = []
    for h in range(N_HEADS_B):
        sl = slice(h * LANES, (h + 1) * LANES)
        st = st_ref[h]
        o = _dot_nt(q_dec[:, sl], st.astype(BF)) + _dot(amats[h].astype(BF), v_bf[:, sl])
        if extra is not None:
            o = o + extra[h]
        outs.append(o)
        st_ref[h] = st * jnp.exp(last[:, sl]) + _dot_tn(v_bf[:, sl], k_dec[:, sl])
    return jnp.concatenate(outs, axis=1)


def _hgrn_block_fast(qb, kin, cum, ib, st_ref):
    half = BLK // 2
    row = lax.broadcasted_iota(jnp.int32, (BLK, 1), 0)
    upper = row >= half
    piv = cum[half - 1:half, :]
    w_lvl = jnp.exp(jnp.concatenate([piv - cum[:half, :], cum[half:, :] - piv], axis=0))
    p_lvl = (jnp.where(upper, qb, kin) * w_lvl).astype(BF)
    mid = jnp.where(upper, cum[half + half // 2 - 1:half + half // 2, :],
                    cum[half // 2 - 1:half // 2, :])
    e_mid = cum - mid
    q_mid = (qb * jnp.exp(e_mid)).astype(BF)
    k_mid = (kin * jnp.exp(-e_mid)).astype(BF)

    ri = lax.broadcasted_iota(jnp.int32, (BLK, BLK), 0)
    ci = lax.broadcasted_iota(jnp.int32, (BLK, BLK), 1)
    same_half_causal = ((ri >= half) == (ci >= half)) & (ci <= ri)
    cross = (ri >= half) & (ci < half)
    amats = []
    for h in range(N_HEADS_B):
        sl = slice(h * LANES, (h + 1) * LANES)
        a_mid = _dot_nt(q_mid[:, sl], k_mid[:, sl])
        a_lvl = _dot_nt(p_lvl[:, sl], p_lvl[:, sl])
        amats.append(jnp.where(same_half_causal, a_mid, jnp.where(cross, a_lvl, 0.0)))
    return _hgrn_finish(qb, kin, cum, ib, amats, None, st_ref)


def _hgrn_block_robust(qb, kin, f, cum, ib, lvl_mask_ref, st_ref):
    row = lax.broadcasted_iota(jnp.int32, (BLK, 1), 0)
    lvl_ops = []
    for b in LEVELS:
        pieces = []
        for r0 in range(0, BLK, 2 * b):
            piv = cum[r0 + b - 1:r0 + b, :]
            pieces.append(piv - cum[r0:r0 + b, :])
            pieces.append(cum[r0 + b:r0 + 2 * b, :] - piv)
        w = jnp.exp(jnp.concatenate(pieces, axis=0))
        second = (row & b) != 0
        lvl_ops.append((jnp.where(second, qb, kin) * w).astype(BF))

    n8 = BLK // SUB
    q3 = qb.reshape(n8, SUB, WIDTH_B)
    k3 = kin.reshape(n8, SUB, WIDTH_B)
    f3 = f.reshape(n8, SUB, WIDTH_B)
    v3 = ib.reshape(n8, SUB, WIDTH_B)
    subl = lax.broadcasted_iota(jnp.int32, (n8, SUB, 1), 1)

    def head(x, h):
        return x[..., h * LANES:(h + 1) * LANES]

    g = q3 * k3
    acc = [jnp.sum(head(g, h), axis=-1, keepdims=True) * head(v3, h) for h in range(N_HEADS_B)]
    dec = jnp.ones_like(f3)
    kd = k3
    vd = v3
    for d in range(1, SUB):
        dec = f3 * pltpu.roll(dec, 1, 1)
        kd = pltpu.roll(kd, 1, 1)
        vd = pltpu.roll(vd, 1, 1)
        g = q3 * kd * dec
        ok = subl >= d
        for h in range(N_HEADS_B):
            a = jnp.where(ok, jnp.sum(head(g, h), axis=-1, keepdims=True), 0.0)
            acc[h] = acc[h] + a * head(vd, h)

    amats = []
    for h in range(N_HEADS_B):
        sl = slice(h * LANES, (h + 1) * LANES)
        amat = jnp.zeros((BLK, BLK), F32)
        for li in range(len(LEVELS)):
            p = lvl_ops[li][:, sl]
            amat = amat + lvl_mask_ref[li] * _dot_nt(p, p)
        amats.append(amat)
    extra = [acc[h].reshape(BLK, LANES) for h in range(N_HEADS_B)]
    return _hgrn_finish(qb, kin, cum, ib, amats, extra, st_ref)


def _split_dot_left(a_bf, b_f32):
    hi = b_f32.astype(BF)
    lo = (b_f32 - hi.astype(F32)).astype(BF)
    return _dot(a_bf, hi) + _dot(a_bf, lo)


def _branch_b_out(o, zb, og):
    outs = []
    for h in range(N_HEADS_B):
        sl = slice(h * LANES, (h + 1) * LANES)
        oh = o[:, sl]
        ms = jnp.mean(oh * oh, axis=-1, keepdims=True)
        outs.append(oh * lax.rsqrt(ms + EPS))
    return jnp.concatenate(outs, axis=1) * og * _silu(zb)


def _prompt_kernel(sinks_ref,
                   x0_ref, x1_ref, x2_ref, shift_ref, scale_ref, gate_ref, ng_ref,
                   w_in_ref, w_mg_ref, b_mg_ref, w_pa_ref, w_pb_ref, w_out_ref,
                   qg_ref, kg_ref, og_ref, lbl_ref,
                   cb_ref, sb_ref, cl_ref, sl_ref, sgn_ref,
                   seg_ref, tri_ref, bias_ref, lmask_ref,
                   y_ref, kwin_ref, vwin_ref, state_ref,
                   st_ref, kbuf_ref, ksw_ref, vbuf_ref, vsw_ref,
                   p0_ref, p1_ref, g0_ref, g1_ref, h_ref, ob_ref,
                   *, tile, n_steps):
    s = pl.program_id(0)
    nblk = tile // BLK

    def stage_a(x_ref, p_ref, g_ref):
        def prep():
            x = x_ref[...]
            ms = jnp.mean(x * x, axis=-1, keepdims=True)
            h = x * lax.rsqrt(ms + EPS) * (ng_ref[...] * (1.0 + scale_ref[...])) + shift_ref[...]
            h_ref[...] = h.astype(BF)

        def proj_chunk(c):
            def run():
                cs = slice(c * MXU_N, (c + 1) * MXU_N)
                p_ref[:, cs] = _dot(h_ref[...], w_in_ref[:, cs])
            return run

        def gate_chunk(c):
            def run():
                cs = slice(c * MXU_N, (c + 1) * MXU_N)
                g_ref[:, cs] = _sigmoid(_dot(h_ref[...], w_mg_ref[:, cs]) + b_mg_ref[:, cs])
            return run

        return ([prep] + [proj_chunk(c) for c in range(D_IN // MXU_N)]
                + [gate_chunk(c) for c in range(2 * D_MODEL // MXU_N)])

    def stage_b(x_ref, p_ref, g_ref, t_idx, y_rows):
        seg = seg_ref[...]
        rc, ru, rd = _rope_tables(cb_ref[pl.ds(t_idx, 1), :], sb_ref[pl.ds(t_idx, 1), :],
                                  cl_ref[...], sl_ref[...], sgn_ref[0:1, :], sgn_ref[1:2, :])
        qa = p_ref[:, OFF_QA:OFF_QA + WIDTH_A]
        qa = qa * _head_norm_scale(qa, seg) * (qg_ref[...] * (HEAD_DIM_A ** -0.5))
        qa = jnp.concatenate(
            [_rope(qa[:, c * LANES:(c + 1) * LANES], rc, ru, rd) for c in range(4)], axis=1)
        ka = p_ref[:, OFF_KA:OFF_KA + KV_WIDTH]
        ka = ka * _head_norm_scale(ka, seg[:KV_WIDTH, :KV_WIDTH]) * kg_ref[...]
        ka = _rope(ka, rc, ru, rd)
        va = p_ref[:, OFF_VA:OFF_VA + KV_WIDTH]
        ka_sw = pltpu.roll(ka, HEAD_DIM_A, 1)
        va_sw = pltpu.roll(va, HEAD_DIM_A, 1)

        lb = _lower_bound(lbl_ref[...])
        one_m_lb = 1.0 - lb
        tri = tri_ref[...]

        rows4 = lax.broadcasted_iota(jnp.int32, (4 * BLK, 1), 0) // BLK

        def sink_col(heads):
            col = jnp.zeros((4 * BLK, 1), F32)
            for n, hd in enumerate(heads):
                col = jnp.where(rows4 == n, sinks_ref[hd], col)
            return col
        sink_a = sink_col((0, 2, 5, 7))
        sink_b = sink_col((1, 3, 4, 6))
        kwin_ref[...] = ka[tile - WINDOW:, :]
        vwin_ref[...] = va[tile - WINDOW:, :]
        yield

        blocks = [slice(blk * BLK, (blk + 1) * BLK) for blk in range(nblk)]
        ya_parts = []
        for blk, rs in enumerate(blocks):
            cur = slice(BLK, 2 * BLK)
            kbuf_ref[cur, :] = ka[rs].astype(BF)
            ksw_ref[cur, :] = ka_sw[rs].astype(BF)
            vbuf_ref[cur, :] = va[rs].astype(BF)
            vsw_ref[cur, :] = va_sw[rs].astype(BF)
            bias = bias_ref[jnp.where(t_idx == 0, 0, 1)] if blk == 0 else bias_ref[1]
            ya_parts.append(_attn_block(qa[rs], kbuf_ref[...], ksw_ref[...], vbuf_ref[...],
                                        vsw_ref[...], bias, sink_a, sink_b))
            for r in (kbuf_ref, ksw_ref, vbuf_ref, vsw_ref):
                r[0:BLK, :] = r[cur, :]
            yield

        gates = [_hgrn_gates(p_ref[rs, OFF_FB:OFF_FB + WIDTH_B], lb, one_m_lb, tri)
                 for rs in blocks]
        span = _hgrn_span_decay(gates[0][2])
        for g in gates[1:]:
            span = jnp.maximum(span, _hgrn_span_decay(g[2]))
        mild = jnp.max(span) < HGRN_FAST_MAX
        yield

        def recur_fast():
            for rs, (kin, _, cum) in zip(blocks, gates):
                ob_ref[rs, :] = _hgrn_block_fast(
                    p_ref[rs, OFF_QB:OFF_QB + WIDTH_B], kin, cum,
                    p_ref[rs, OFF_IB:OFF_IB + WIDTH_B], st_ref)

        def recur_robust():
            for rs, (kin, f, cum) in zip(blocks, gates):
                ob_ref[rs, :] = _hgrn_block_robust(
                    p_ref[rs, OFF_QB:OFF_QB + WIDTH_B], kin, f, cum,
                    p_ref[rs, OFF_IB:OFF_IB + WIDTH_B], lmask_ref, st_ref)

        lax.cond(mild, recur_fast, recur_robust)
        yield

        ya = jnp.concatenate(ya_parts, axis=0) * _silu(p_ref[:, OFF_ZA:OFF_ZA + WIDTH_A])
        yb = _branch_b_out(ob_ref[...], p_ref[:, OFF_ZB:OFF_ZB + WIDTH_B], og_ref[...])
        ya_bf, yb_bf = ya.astype(BF), yb.astype(BF)
        yield

        y = (g_ref[:, :D_MODEL] * _dot(ya_bf, w_pa_ref[...])
             + g_ref[:, D_MODEL:] * _dot(yb_bf, w_pb_ref[...]))
        y_ref[y_rows, :] = x_ref[...] + gate_ref[...] * _dot(y.astype(BF), w_out_ref[...])
        yield

    def interleave(a_thunks, b_parts):
        weights = [W_PREP] + [W_ATTN] * nblk + [W_GATES, W_RECUR, W_MERGE, W_OUT]
        total, acc, done = sum(weights), 0.0, 0
        for wgt in weights:
            acc += wgt
            upto = int(round(len(a_thunks) * acc / total))
            for th in a_thunks[done:upto]:
                th()
            done = upto
            next(b_parts)
        assert done == len(a_thunks) and next(b_parts, "end") == "end"

    @pl.when(s == 0)
    def _():
        st_ref[...] = jnp.zeros_like(st_ref)
        for r in (kbuf_ref, ksw_ref, vbuf_ref, vsw_ref):
            r[...] = jnp.zeros_like(r)
        for th in stage_a(x0_ref, p0_ref, g0_ref):
            th()

    interleave(stage_a(x1_ref, p1_ref, g1_ref),
               stage_b(x0_ref, p0_ref, g0_ref, 2 * s, slice(0, tile)))
    interleave(stage_a(x2_ref, p0_ref, g0_ref),
               stage_b(x1_ref, p1_ref, g1_ref, 2 * s + 1, slice(tile, 2 * tile)))

    @pl.when(s == n_steps - 1)
    def _():
        for hd in range(N_HEADS_B):
            state_ref[hd] = st_ref[hd].T


def _const_spec(shape):
    nd = len(shape)
    return pl.BlockSpec(shape, lambda i, *_: (0,) * nd, pipeline_mode=pl.Buffered(1))


def _prompt_call(x, shift, scale, gate, consts, w, tile):
    t = x.shape[0]
    n_tiles = t // tile
    n_steps = n_tiles // 2
    assert n_steps * 2 * tile == t
    row = lambda n: _const_spec((1, n))
    in_specs = [
        pl.BlockSpec((tile, D_MODEL), lambda i, *_: (2 * i, 0)),
        pl.BlockSpec((tile, D_MODEL), lambda i, *_: (2 * i + 1, 0)),
        pl.BlockSpec((tile, D_MODEL), lambda i, *_: (jnp.minimum(2 * i + 2, n_tiles - 1), 0)),
        row(D_MODEL), row(D_MODEL), row(D_MODEL), row(D_MODEL),
        _const_spec((D_MODEL, D_IN)), _const_spec((D_MODEL, 2 * D_MODEL)), row(2 * D_MODEL),
        _const_spec((WIDTH_A, D_MODEL)), _const_spec((WIDTH_B, D_MODEL)),
        _const_spec((D_MODEL, D_MODEL)),
        row(WIDTH_A), row(KV_WIDTH), row(WIDTH_B), _const_spec((2, WIDTH_B)),
        _const_spec((n_tiles, LANES)), _const_spec((n_tiles, LANES)),
        _const_spec((tile, LANES)), _const_spec((tile, LANES)), _const_spec((2, LANES)),
        _const_spec((WIDTH_A, WIDTH_A)), _const_spec((BLK, BLK)),
        _const_spec((2, 4 * BLK, 2 * BLK)), _const_spec((len(LEVELS), BLK, BLK)),
    ]
    out_specs = [
        pl.BlockSpec((2 * tile, D_MODEL), lambda i, *_: (i, 0)),
        pl.BlockSpec((WINDOW, KV_WIDTH), lambda i, *_: (0, 0)),
        pl.BlockSpec((WINDOW, KV_WIDTH), lambda i, *_: (0, 0)),
        pl.BlockSpec((N_HEADS_B, HEAD_DIM_B, HEAD_DIM_B), lambda i, *_: (0, 0, 0)),
    ]
    out_shape = [
        jax.ShapeDtypeStruct((t, D_MODEL), F32),
        jax.ShapeDtypeStruct((WINDOW, KV_WIDTH), F32),
        jax.ShapeDtypeStruct((WINDOW, KV_WIDTH), F32),
        jax.ShapeDtypeStruct((N_HEADS_B, HEAD_DIM_B, HEAD_DIM_B), F32),
    ]
    scratch = [
        pltpu.VMEM((N_HEADS_B, HEAD_DIM_B, HEAD_DIM_B), F32),
        pltpu.VMEM((2 * BLK, KV_WIDTH), BF), pltpu.VMEM((2 * BLK, KV_WIDTH), BF),
        pltpu.VMEM((2 * BLK, KV_WIDTH), BF), pltpu.VMEM((2 * BLK, KV_WIDTH), BF),
        pltpu.VMEM((tile, D_IN), F32), pltpu.VMEM((tile, D_IN), F32),
        pltpu.VMEM((tile, 2 * D_MODEL), F32), pltpu.VMEM((tile, 2 * D_MODEL), F32),
        pltpu.VMEM((tile, D_MODEL), BF), pltpu.VMEM((tile, WIDTH_B), F32),
    ]
    return pl.pallas_call(
        functools.partial(_prompt_kernel, tile=tile, n_steps=n_steps),
        grid_spec=pltpu.PrefetchScalarGridSpec(
            num_scalar_prefetch=1, grid=(n_steps,),
            in_specs=in_specs, out_specs=out_specs, scratch_shapes=scratch),
        out_shape=out_shape,
        compiler_params=pltpu.CompilerParams(
            dimension_semantics=("arbitrary",), vmem_limit_bytes=VMEM_LIMIT),
        name="prompt_layer",
    )(w["sinks"], x, x, x, shift, scale, gate, w["norm_g"],
      w["w_in"], w["w_merge"], w["b_merge"], w["w_proj_a"], w["w_proj_b"], w["w_out"],
      w["q_g"], w["k_g"], w["o_g"], w["lb_logits"],
      consts["rope_cb"], consts["rope_sb"], consts["rope_cl"], consts["rope_sl"],
      consts["rope_sgn"],
      consts["seg"], consts["tri"], consts["bias"], consts["lmask"])


def _dec_in_kernel(x_ref, shift_ref, scale_ref, ng_ref, w_in_ref, w_mg_ref, b_mg_ref,
                   qg_ref, kg_ref, lbl_ref, rc_ref, ru_ref, rd_ref, seg_ref, perm_ref,
                   qm_ref, kn_ref, vn_ref, za_ref, qb_ref, kin_ref, ib_ref, zb_ref, g_ref):
    x = x_ref[...]
    ms = jnp.mean(x * x, axis=-1, keepdims=True)
    h = x * lax.rsqrt(ms + EPS) * (ng_ref[...] * (1.0 + scale_ref[...])) + shift_ref[...]
    h_bf = h.astype(BF)
    proj = _dot(h_bf, w_in_ref[...])
    g_ref[...] = _sigmoid(_dot(h_bf, w_mg_ref[...]) + b_mg_ref[...])

    seg = seg_ref[...]
    rc, ru, rd = rc_ref[...], ru_ref[...], rd_ref[...]
    qa = proj[:, OFF_QA:OFF_QA + WIDTH_A]
    qa = qa * _head_norm_scale(qa, seg) * (qg_ref[...] * (HEAD_DIM_A ** -0.5))
    qa = jnp.concatenate(
        [_rope(qa[:, c * LANES:(c + 1) * LANES], rc, ru, rd) for c in range(4)], axis=1)
    qm_ref[...] = _dot(qa.astype(BF), perm_ref[...])
    ka = proj[:, OFF_KA:OFF_KA + KV_WIDTH]
    ka = ka * _head_norm_scale(ka, seg[:KV_WIDTH, :KV_WIDTH]) * kg_ref[...]
    kn_ref[...] = _rope(ka, rc, ru, rd)
    vn_ref[...] = proj[:, OFF_VA:OFF_VA + KV_WIDTH]
    za_ref[...] = proj[:, OFF_ZA:OFF_ZA + WIDTH_A]
    qb_ref[...] = proj[:, OFF_QB:OFF_QB + WIDTH_B]
    lb = _lower_bound(lbl_ref[...])
    kin_ref[...] = (1.0 - lb) * (1.0 - _sigmoid(proj[:, OFF_FB:OFF_FB + WIDTH_B]))
    ib_ref[...] = proj[:, OFF_IB:OFF_IB + WIDTH_B]
    zb_ref[...] = proj[:, OFF_ZB:OFF_ZB + WIDTH_B]


def _dec_in_call(x, shift, scale, consts, w):
    b = x.shape[0]
    f = lambda n: jax.ShapeDtypeStruct((b, n), F32)
    return pl.pallas_call(
        _dec_in_kernel,
        out_shape=[f(N_HEADS_A * LANES), f(KV_WIDTH), f(KV_WIDTH), f(WIDTH_A),
                   f(WIDTH_B), f(WIDTH_B), f(WIDTH_B), f(WIDTH_B), f(2 * D_MODEL)],
        compiler_params=pltpu.CompilerParams(vmem_limit_bytes=VMEM_LIMIT),
        name="decode_in",
    )(x, shift, scale, w["norm_g"], w["w_in"], w["w_merge"], w["b_merge"],
      w["q_g"], w["k_g"], w["lb_logits"],
      consts["rope_c1"], consts["rope_u1"], consts["rope_d1"], consts["seg"], consts["perm"])


def _dec_mix_kernel(qm_ref, kn_ref, vn_ref, qb_ref, kin_ref, ib_ref, sink_ref,
                    ck_ref, cv_ref, st_ref,
                    att_ref, ob_ref, nk_ref, nv_ref, nst_ref, *, bt):
    key = lax.broadcasted_iota(jnp.int32, (N_HEADS_A, WINDOW), 1)
    sink = sink_ref[...]
    for j in range(bt):
        qm = qm_ref[j]
        kc = ck_ref[j]
        vc = cv_ref[j]
        kn = kn_ref[j:j + 1, :]
        vn = vn_ref[j:j + 1, :]
        s = _dot_nt(qm.astype(BF), kc.astype(BF))
        s = jnp.where(key == 0, NEG, s)
        s_new = jnp.sum(qm * kn, axis=-1, keepdims=True)
        m = jnp.maximum(jnp.maximum(jnp.max(s, axis=-1, keepdims=True), s_new), sink)
        p = jnp.exp(s - m)
        p_new = jnp.exp(s_new - m)
        den = jnp.sum(p, axis=-1, keepdims=True) + p_new + jnp.exp(sink - m)
        o = (_dot(p.astype(BF), vc.astype(BF)) + p_new * vn) * (1.0 / den)
        att_ref[j] = o
        nk_ref[j] = jnp.concatenate([kc[1:, :], kn], axis=0)
        nv_ref[j] = jnp.concatenate([vc[1:, :], vn], axis=0)

    for hd in range(N_HEADS_B):
        sl = slice(hd * LANES, (hd + 1) * LANES)
        kin_t = kin_ref[:, sl].T
        q_t = qb_ref[:, sl].T
        for j in range(bt):
            kcol = kin_t[:, j:j + 1]
            qcol = q_t[:, j:j + 1]
            st = st_ref[j, hd]
            new = st - kcol * (st - ib_ref[j:j + 1, sl])
            nst_ref[j, hd] = new
            ob_ref[j:j + 1, sl] = jnp.sum(qcol * new, axis=0, keepdims=True)


def _dec_mix_call(qm, kn, vn, qb, kin, ib, sink_col, cache_k, cache_v, state, bt):
    b = kn.shape[0]
    rows = lambda n: pl.BlockSpec((bt, n), lambda i: (i, 0))
    cache_spec = pl.BlockSpec((bt, WINDOW, KV_WIDTH), lambda i: (i, 0, 0))
    st_spec = pl.BlockSpec((bt, N_HEADS_B, HEAD_DIM_B, HEAD_DIM_B), lambda i: (i, 0, 0, 0))
    qm_spec = pl.BlockSpec((bt, N_HEADS_A, LANES), lambda i: (i, 0, 0))
    return pl.pallas_call(
        functools.partial(_dec_mix_kernel, bt=bt),
        grid=(b // bt,),
        in_specs=[qm_spec, rows(KV_WIDTH), rows(KV_WIDTH), rows(WIDTH_B), rows(WIDTH_B),
                  rows(WIDTH_B), pl.BlockSpec((N_HEADS_A, 1), lambda i: (0, 0)),
                  cache_spec, cache_spec, st_spec],
        out_specs=[qm_spec, rows(WIDTH_B), cache_spec, cache_spec, st_spec],
        out_shape=[jax.ShapeDtypeStruct((b, N_HEADS_A, LANES), F32),
                   jax.ShapeDtypeStruct((b, WIDTH_B), F32),
                   jax.ShapeDtypeStruct(cache_k.shape, F32),
                   jax.ShapeDtypeStruct(cache_v.shape, F32),
                   jax.ShapeDtypeStruct(state.shape, F32)],
        compiler_params=pltpu.CompilerParams(
            dimension_semantics=("arbitrary",), vmem_limit_bytes=VMEM_LIMIT),
        name="decode_mix",
    )(qm, kn, vn, qb, kin, ib, sink_col, cache_k, cache_v, state)


def _dec_out_kernel(x_ref, gate_ref, att_ref, za_ref, ob_ref, zb_ref, g_ref, og_ref,
                    permt_ref, w_pa_ref, w_pb_ref, w_out_ref, y_ref):
    att = _split_dot(att_ref[...], permt_ref[...])
    ya = att * _silu(za_ref[...])
    yb = _branch_b_out(ob_ref[...], zb_ref[...], og_ref[...])
    g = g_ref[...]
    y = (g[:, :D_MODEL] * _dot(ya.astype(BF), w_pa_ref[...])
         + g[:, D_MODEL:] * _dot(yb.astype(BF), w_pb_ref[...]))
    y_ref[...] = x_ref[...] + gate_ref[...] * _dot(y.astype(BF), w_out_ref[...])


def _dec_out_call(x, gate, att, za, ob, zb, g, consts, w):
    return pl.pallas_call(
        _dec_out_kernel,
        out_shape=jax.ShapeDtypeStruct(x.shape, F32),
        compiler_params=pltpu.CompilerParams(vmem_limit_bytes=VMEM_LIMIT),
        name="decode_out",
    )(x, gate, att, za, ob, zb, g, w["o_g"], consts["permt"],
      w["w_proj_a"], w["w_proj_b"], w["w_out"])


def _rope_consts(n_tiles, tile, past_len):
    half = ROT_DIM // 2
    inv = ROPE_THETA ** (-np.arange(0, ROT_DIM, 2, dtype=np.float64) / ROT_DIM)
    e = np.arange(LANES) % HEAD_DIM_A
    rot = e < ROT_DIM

    def tables(pos):
        ang = np.asarray(pos, np.float64)[:, None] * inv[e % half][None, :]
        return np.where(rot, np.cos(ang), 1.0), np.where(rot, np.sin(ang), 0.0)

    sgn = np.stack([np.where(e < half, -1.0, 0.0),
                    np.where(rot & (e >= half), 1.0, 0.0)])
    cb, sb = tables(np.arange(n_tiles) * tile)
    cl, sl = tables(np.arange(tile))
    c1, s1 = tables([past_len])
    f = lambda a: jnp.asarray(a, F32)
    return {"rope_cb": f(cb), "rope_sb": f(sb), "rope_cl": f(cl), "rope_sl": f(sl),
            "rope_sgn": f(sgn), "rope_c1": f(c1), "rope_u1": f(s1 * sgn[0:1]),
            "rope_d1": f(s1 * sgn[1:2])}


def _static_consts():
    seg = np.kron(np.eye(N_HEADS_A), np.full((HEAD_DIM_A, HEAD_DIM_A), 1.0 / HEAD_DIM_A))
    tri = np.tril(np.ones((BLK, BLK)))
    r = np.arange(4 * BLK)[:, None] % BLK
    c = np.arange(2 * BLK)[None, :]
    ok_prev = (c < BLK) & (c > r)
    ok_cur = (c >= BLK) & (c - BLK <= r)
    bias = np.stack([np.where(ok_cur, 0.0, NEG), np.where(ok_prev | ok_cur, 0.0, NEG)])
    i = np.arange(BLK)[:, None]
    j = np.arange(BLK)[None, :]
    lmask = np.stack([((i // (2 * b)) == (j // (2 * b))) & ((i & b) != 0) & ((j & b) == 0)
                      for b in LEVELS]).astype(np.float32)
    perm = np.zeros((WIDTH_A, N_HEADS_A * LANES), np.float32)
    for hd in range(N_HEADS_A):
        kvh = hd // (N_HEADS_A // N_KV_A)
        for d in range(HEAD_DIM_A):
            perm[hd * HEAD_DIM_A + d, hd * LANES + kvh * HEAD_DIM_A + d] = 1.0
    return {
        "seg": jnp.asarray(seg, BF), "tri": jnp.asarray(tri, BF),
        "bias": jnp.asarray(bias, F32), "lmask": jnp.asarray(lmask, F32),
        "perm": jnp.asarray(perm, BF), "permt": jnp.asarray(perm.T, BF),
    }


def kernel(x_prompt, x_sample, cache_win_k, cache_win_v, state_hgrn, c_prompt, c_sample,
           w_ada, b_ada, norm_g, w_in, q_norm_g, k_norm_g, sinks, lb_logits, o_norm_g,
           w_merge, b_merge, w_proj_a, w_proj_b, w_out):
    depth = w_in.shape[0]
    assert depth == 1 and x_prompt.shape[0] == 1 and x_sample.shape[1] == 1
    t = x_prompt.shape[1]
    nb = x_sample.shape[0]
    past_len = t
    tile = 256
    bt = 8

    consts = _static_consts()
    consts.update(_rope_consts(t // tile, tile, past_len))

    w = {
        "sinks": sinks[0], "norm_g": norm_g[0][None, :],
        "w_in": w_in[0].astype(BF), "w_merge": w_merge[0].astype(BF),
        "b_merge": b_merge[0][None, :],
        "w_proj_a": w_proj_a[0].astype(BF), "w_proj_b": w_proj_b[0].astype(BF),
        "w_out": w_out[0].astype(BF),
        "q_g": jnp.tile(q_norm_g[0], N_HEADS_A)[None, :],
        "k_g": jnp.tile(k_norm_g[0], N_KV_A)[None, :],
        "o_g": jnp.tile(o_norm_g[0], N_HEADS_B)[None, :],
        "lb_logits": lb_logits,
    }

    mod_p, mod_s = _ada_call(c_prompt, c_sample, w_ada[0], b_ada)
    split3 = lambda m: (m[:, k * D_MODEL:(k + 1) * D_MODEL] for k in range(3))
    shift_p, scale_p, gate_p = split3(mod_p)
    shift_s, scale_s, gate_s = split3(mod_s)

    y_p, kwin, vwin, st_p = _prompt_call(
        x_prompt[0], shift_p, scale_p, gate_p, consts, w, tile)

    xs = x_sample[:, 0, :]
    qm, kn, vn, za, qb, kin, ib, zb, g = _dec_in_call(
        xs, shift_s, scale_s, consts, w)
    att, ob, nk, nv, nst = _dec_mix_call(
        qm.reshape(nb, N_HEADS_A, LANES), kn, vn, qb, kin, ib, sinks[0][:, None],
        cache_win_k[0].reshape(nb, WINDOW, KV_WIDTH), cache_win_v[0].reshape(nb, WINDOW, KV_WIDTH),
        state_hgrn[0], bt)
    y_s = _dec_out_call(xs, gate_s, att.reshape(nb, N_HEADS_A * LANES), za, ob, zb, g,
                        consts, w)

    kv_shape = (1, 1, WINDOW, N_KV_A, HEAD_DIM_A)
    kv_shape_s = (1, nb, WINDOW, N_KV_A, HEAD_DIM_A)
    return (y_p[None], y_s[:, None, :],
            kwin.reshape(kv_shape), vwin.reshape(kv_shape), st_p[None, None],
            nk.reshape(kv_shape_s), nv.reshape(kv_shape_s), nst[None])
```

```python
import functools

import numpy as np
import jax
import jax.numpy as jnp
from jax import lax
from jax.experimental import pallas as pl
from jax.experimental.pallas import tpu as pltpu

D_MODEL = 1024
HEAD_DIM_A = 64
N_HEADS_A = 8
N_KV_A = 2
WIDTH_A = N_HEADS_A * HEAD_DIM_A
KV_WIDTH = N_KV_A * HEAD_DIM_A
WINDOW = 128
ROT_DIM = HEAD_DIM_A // 4
ROPE_THETA = 500000.0
HEAD_DIM_B = 128
N_HEADS_B = 4
WIDTH_B = N_HEADS_B * HEAD_DIM_B
EPS = 1e-6
OFF_QA = 0
OFF_KA = OFF_QA + WIDTH_A
OFF_VA = OFF_KA + KV_WIDTH
OFF_ZA = OFF_VA + KV_WIDTH
OFF_QB = OFF_ZA + WIDTH_A
OFF_FB = OFF_QB + WIDTH_B
OFF_IB = OFF_FB + WIDTH_B
OFF_ZB = OFF_IB + WIDTH_B
D_IN = OFF_ZB + WIDTH_B

LANES = 128
BLK = 128
SUB = 8
LEVELS = (64, 32, 16, 8)
HGRN_FAST_MAX = 80.0
MXU_N = 256
W_PREP, W_GATES, W_ATTN, W_RECUR, W_MERGE, W_OUT = 1.0, 0.8, 1.0, 0.5, 0.8, 0.0
NEG = -1e30
VMEM_LIMIT = 56 * 1024 * 1024

BF = jnp.bfloat16
F32 = jnp.float32


def _dot(a, b):
    return jnp.dot(a, b, preferred_element_type=F32)


def _dot_nt(a, b):
    return lax.dot_general(a, b, (((1,), (1,)), ((), ())), preferred_element_type=F32)


def _dot_tn(a, b):
    return lax.dot_general(a, b, (((0,), (0,)), ((), ())), preferred_element_type=F32)


def _split_dot(a_f32, b_bf):
    hi = a_f32.astype(BF)
    lo = (a_f32 - hi.astype(F32)).astype(BF)
    return _dot(hi, b_bf) + _dot(lo, b_bf)


def _sigmoid(x):
    return 1.0 / (1.0 + jnp.exp(-x))


def _silu(x):
    return x * _sigmoid(x)


def _lower_bound(lb_logits):
    l0 = lb_logits[0:1, :]
    l1 = lb_logits[1:2, :]
    m = jnp.maximum(l0, l1)
    e0 = jnp.exp(l0 - m)
    e1 = jnp.exp(l1 - m)
    return e0 / (e0 + e1)


def _rope_tables(cb, sb, cl, sl, sgn_up, sgn_dn):
    c = cb * cl - sb * sl
    s = sb * cl + cb * sl
    return c, s * sgn_up, s * sgn_dn


def _rope(x, c, s_up, s_dn):
    return x * c + pltpu.roll(x, LANES - ROT_DIM // 2, 1) * s_up + pltpu.roll(x, ROT_DIM // 2, 1) * s_dn


def _head_norm_scale(x, seg_mean_bf):
    ms = _dot((x * x).astype(BF), seg_mean_bf)
    return lax.rsqrt(ms + EPS)


def _ada_kernel(cp_ref, cs_ref, w_ref, b_ref, op_ref, os_ref):
    w = w_ref[...]
    b = b_ref[...]
    hp = lax.Precision.HIGHEST
    op_ref[...] = jnp.dot(cp_ref[...], w, preferred_element_type=F32, precision=hp) + b
    os_ref[...] = jnp.dot(cs_ref[...], w, preferred_element_type=F32, precision=hp) + b


def _ada_call(c_p, c_s, w_ada, b_ada):
    mp, ms = c_p.shape[0], c_s.shape[0]
    n = w_ada.shape[1]
    tn = 512
    return pl.pallas_call(
        _ada_kernel,
        grid=(n // tn,),
        in_specs=[pl.BlockSpec((mp, D_MODEL), lambda j: (0, 0)),
                  pl.BlockSpec((ms, D_MODEL), lambda j: (0, 0)),
                  pl.BlockSpec((D_MODEL, tn), lambda j: (0, j)),
                  pl.BlockSpec((1, tn), lambda j: (0, j))],
        out_specs=[pl.BlockSpec((mp, tn), lambda j: (0, j)),
                   pl.BlockSpec((ms, tn), lambda j: (0, j))],
        out_shape=[jax.ShapeDtypeStruct((mp, n), F32), jax.ShapeDtypeStruct((ms, n), F32)],
        name="ada",
    )(c_p, c_s, w_ada, b_ada)


def _attn_block(q_blk, kcat, kcat_sw, vcat, vcat_sw, bias, sink_a, sink_b):
    lane = lax.broadcasted_iota(jnp.int32, (BLK, LANES), 1)
    lo = lane < HEAD_DIM_A
    chunks = [q_blk[:, c * LANES:(c + 1) * LANES] for c in range(4)]
    zero = jnp.zeros((BLK, LANES), F32)
    q_lo = [jnp.where(lo, c, zero).astype(BF) for c in chunks]
    q_hi = [jnp.where(lo, zero, c).astype(BF) for c in chunks]
    qa = jnp.concatenate([q_lo[0], q_lo[1], q_hi[2], q_hi[3]], axis=0)
    qb = jnp.concatenate([q_hi[0], q_hi[1], q_lo[2], q_lo[3]], axis=0)

    def soft(qs, kc, vc, sink):
        s = _dot_nt(qs, kc) + bias
        m = jnp.maximum(jnp.max(s, axis=-1, keepdims=True), sink)
        p = jnp.exp(s - m)
        den = jnp.sum(p, axis=-1, keepdims=True) + jnp.exp(sink - m)
        o = _dot(p.astype(BF), vc)
        return o * (1.0 / den)

    oa = soft(qa, kcat, vcat, sink_a)
    ob = soft(qb, kcat_sw, vcat_sw, sink_b)
    r = lambda o, i: o[i * BLK:(i + 1) * BLK, :]
    return jnp.concatenate([
        jnp.where(lo, r(oa, 0), r(ob, 0)),
        jnp.where(lo, r(oa, 1), r(ob, 1)),
        jnp.where(lo, r(ob, 2), r(oa, 2)),
        jnp.where(lo, r(ob, 3), r(oa, 3)),
    ], axis=1)


def _hgrn_gates(fb, lb, one_m_lb, tri_bf):
    sig = _sigmoid(fb)
    kin = one_m_lb * (1.0 - sig)
    f = lb + one_m_lb * sig
    cum = _split_dot_left(tri_bf, jnp.log(f))
    return kin, f, cum


def _hgrn_span_decay(cum):
    q = BLK // 4
    ends = [cum[(n + 1) * q - 1:(n + 1) * q, :] for n in range(4)]
    d = -ends[0]
    for n in range(1, 4):
        d = jnp.maximum(d, ends[n - 1] - ends[n])
    return d


def _hgrn_state_step(qb, kin, cum, ib, st_ref):
    q_dec = (qb * jnp.exp(cum)).astype(BF)
    last = cum[BLK - 1:BLK, :]
    k_dec = (kin * jnp.exp(last - cum)).astype(BF)
    v_bf = ib.astype(BF)
    outs = []
    for h in range(N_HEADS_B):
        sl = slice(h * LANES, (h + 1) * LANES)
        st = st_ref[h]
        outs.append(_dot_nt(q_dec[:, sl], st.astype(BF)))
        st_ref[h] = st * jnp.exp(last[:, sl]) + _dot_tn(v_bf[:, sl], k_dec[:, sl])
    return jnp.concatenate(outs, axis=1)


def _hgrn_apply(amats, ib):
    v_bf = ib.astype(BF)
    return jnp.concatenate(
        [_dot(amats[h].astype(BF), v_bf[:, h * LANES:(h + 1) * LANES])
         for h in range(N_HEADS_B)], axis=1)


def _hgrn_intra_fast(qb, kin, cum, ib):
    half = BLK // 2
    row = lax.broadcasted_iota(jnp.int32, (BLK, 1), 0)
    upper = row >= half
    piv = cum[half - 1:half, :]
    w_lvl = jnp.exp(jnp.concatenate([piv - cum[:half, :], cum[half:, :] - piv], axis=0))
    p_lvl = (jnp.where(upper, qb, kin) * w_lvl).astype(BF)
    mid = jnp.where(upper, cum[half + half // 2 - 1:half + half // 2, :],
                    cum[half // 2 - 1:half // 2, :])
    e_mid = cum - mid
    q_mid = (qb * jnp.exp(e_mid)).astype(BF)
    k_mid = (kin * jnp.exp(-e_mid)).astype(BF)

    ri = lax.broadcasted_iota(jnp.int32, (BLK, BLK), 0)
    ci = lax.broadcasted_iota(jnp.int32, (BLK, BLK), 1)
    same_half_causal = ((ri >= half) == (ci >= half)) & (ci <= ri)
    cross = (ri >= half) & (ci < half)
    amats = []
    for h in range(N_HEADS_B):
        sl = slice(h * LANES, (h + 1) * LANES)
        a_mid = _dot_nt(q_mid[:, sl], k_mid[:, sl])
        a_lvl = _dot_nt(p_lvl[:, sl], p_lvl[:, sl])
        amats.append(jnp.where(same_half_causal, a_mid, jnp.where(cross, a_lvl, 0.0)))
    return _hgrn_apply(amats, ib)


def _hgrn_intra_robust(qb, kin, f, cum, ib, lvl_mask_ref):
    row = lax.broadcasted_iota(jnp.int32, (BLK, 1), 0)
    lvl_ops = []
    for b in LEVELS:
        pieces = []
        for r0 in range(0, BLK, 2 * b):
            piv = cum[r0 + b - 1:r0 + b, :]
            pieces.append(piv - cum[r0:r0 + b, :])
            pieces.append(cum[r0 + b:r0 + 2 * b, :] - piv)
        w = jnp.exp(jnp.concatenate(pieces, axis=0))
        second = (row & b) != 0
        lvl_ops.append((jnp.where(second, qb, kin) * w).astype(BF))

    n8 = BLK // SUB
    q3 = qb.reshape(n8, SUB, WIDTH_B)
    k3 = kin.reshape(n8, SUB, WIDTH_B)
    f3 = f.reshape(n8, SUB, WIDTH_B)
    v3 = ib.reshape(n8, SUB, WIDTH_B)
    subl = lax.broadcasted_iota(jnp.int32, (n8, SUB, 1), 1)

    def head(x, h):
        return x[..., h * LANES:(h + 1) * LANES]

    g = q3 * k3
    acc = [jnp.sum(head(g, h), axis=-1, keepdims=True) * head(v3, h) for h in range(N_HEADS_B)]
    dec = jnp.ones_like(f3)
    kd = k3
    vd = v3
    for d in range(1, SUB):
        dec = f3 * pltpu.roll(dec, 1, 1)
        kd = pltpu.roll(kd, 1, 1)
        vd = pltpu.roll(vd, 1, 1)
        g = q3 * kd * dec
        ok = subl >= d
        for h in range(N_HEADS_B):
            a = jnp.where(ok, jnp.sum(head(g, h), axis=-1, keepdims=True), 0.0)
            acc[h] = acc[h] + a * head(vd, h)

    amats = []
    for h in range(N_HEADS_B):
        sl = slice(h * LANES, (h + 1) * LANES)
        amat = jnp.zeros((BLK, BLK), F32)
        for li in range(len(LEVELS)):
            p = lvl_ops[li][:, sl]
            amat = amat + lvl_mask_ref[li] * _dot_nt(p, p)
        amats.append(amat)
    diag = jnp.concatenate([acc[h].reshape(BLK, LANES) for h in range(N_HEADS_B)], axis=1)
    return _hgrn_apply(amats, ib) + diag


def _split_dot_left(a_bf, b_f32):
    hi = b_f32.astype(BF)
    lo = (b_f32 - hi.astype(F32)).astype(BF)
    return _dot(a_bf, hi) + _dot(a_bf, lo)


def _branch_b_out(o, zb, og):
    outs = []
    for h in range(N_HEADS_B):
        sl = slice(h * LANES, (h + 1) * LANES)
        oh = o[:, sl]
        ms = jnp.mean(oh * oh, axis=-1, keepdims=True)
        outs.append(oh * lax.rsqrt(ms + EPS))
    return jnp.concatenate(outs, axis=1) * og * _silu(zb)


def _prompt_kernel(sinks_ref,
                   x0_ref, x1_ref, x2_ref, shift_ref, scale_ref, gate_ref, ng_ref,
                   w_in_ref, w_mg_ref, b_mg_ref, w_pa_ref, w_pb_ref, w_out_ref,
                   qg_ref, kg_ref, og_ref, lbl_ref,
                   cb_ref, sb_ref, cl_ref, sl_ref, sgn_ref,
                   seg_ref, tri_ref, bias_ref, lmask_ref,
                   y_ref, kwin_ref, vwin_ref, state_ref,
                   st_ref, kbuf_ref, ksw_ref, vbuf_ref, vsw_ref,
                   p0_ref, p1_ref, g0_ref, g1_ref, h_ref, ob_ref, obase_ref,
                   *, tile, n_steps):
    s = pl.program_id(0)
    nblk = tile // BLK

    def stage_a(x_ref, p_ref, g_ref):
        def prep():
            x = x_ref[...]
            ms = jnp.mean(x * x, axis=-1, keepdims=True)
            h = x * lax.rsqrt(ms + EPS) * (ng_ref[...] * (1.0 + scale_ref[...])) + shift_ref[...]
            h_ref[...] = h.astype(BF)

        def proj_chunk(c):
            def run():
                cs = slice(c * MXU_N, (c + 1) * MXU_N)
                p_ref[:, cs] = _dot(h_ref[...], w_in_ref[:, cs])
            return run

        def gate_chunk(c):
            def run():
                cs = slice(c * MXU_N, (c + 1) * MXU_N)
                g_ref[:, cs] = _sigmoid(_dot(h_ref[...], w_mg_ref[:, cs]) + b_mg_ref[:, cs])
            return run

        return ([prep] + [proj_chunk(c) for c in range(D_IN // MXU_N)]
                + [gate_chunk(c) for c in range(2 * D_MODEL // MXU_N)])

    def stage_b(x_ref, p_ref, g_ref, t_idx, y_rows):
        seg = seg_ref[...]
        rc, ru, rd = _rope_tables(cb_ref[pl.ds(t_idx, 1), :], sb_ref[pl.ds(t_idx, 1), :],
                                  cl_ref[...], sl_ref[...], sgn_ref[0:1, :], sgn_ref[1:2, :])
        qa = p_ref[:, OFF_QA:OFF_QA + WIDTH_A]
        qa = qa * _head_norm_scale(qa, seg) * (qg_ref[...] * (HEAD_DIM_A ** -0.5))
        qa = jnp.concatenate(
            [_rope(qa[:, c * LANES:(c + 1) * LANES], rc, ru, rd) for c in range(4)], axis=1)
        ka = p_ref[:, OFF_KA:OFF_KA + KV_WIDTH]
        ka = ka * _head_norm_scale(ka, seg[:KV_WIDTH, :KV_WIDTH]) * kg_ref[...]
        ka = _rope(ka, rc, ru, rd)
        va = p_ref[:, OFF_VA:OFF_VA + KV_WIDTH]
        ka_sw = pltpu.roll(ka, HEAD_DIM_A, 1)
        va_sw = pltpu.roll(va, HEAD_DIM_A, 1)

        lb = _lower_bound(lbl_ref[...])
        one_m_lb = 1.0 - lb
        tri = tri_ref[...]

        rows4 = lax.broadcasted_iota(jnp.int32, (4 * BLK, 1), 0) // BLK

        def sink_col(heads):
            col = jnp.zeros((4 * BLK, 1), F32)
            for n, hd in enumerate(heads):
                col = jnp.where(rows4 == n, sinks_ref[hd], col)
            return col
        sink_a = sink_col((0, 2, 5, 7))
        sink_b = sink_col((1, 3, 4, 6))
        kwin_ref[...] = ka[tile - WINDOW:, :]
        vwin_ref[...] = va[tile - WINDOW:, :]
        yield

        blocks = [slice(blk * BLK, (blk + 1) * BLK) for blk in range(nblk)]
        gates = [_hgrn_gates(p_ref[rs, OFF_FB:OFF_FB + WIDTH_B], lb, one_m_lb, tri)
                 for rs in blocks]
        span = _hgrn_span_decay(gates[0][2])
        for g in gates[1:]:
            span = jnp.maximum(span, _hgrn_span_decay(g[2]))
        mild = jnp.max(span) < HGRN_FAST_MAX
        yield

        ya_parts = []
        for blk, rs in enumerate(blocks):
            cur = slice(BLK, 2 * BLK)
            kbuf_ref[cur, :] = ka[rs].astype(BF)
            ksw_ref[cur, :] = ka_sw[rs].astype(BF)
            vbuf_ref[cur, :] = va[rs].astype(BF)
            vsw_ref[cur, :] = va_sw[rs].astype(BF)
            bias = bias_ref[jnp.where(t_idx == 0, 0, 1)] if blk == 0 else bias_ref[1]
            ya_parts.append(_attn_block(qa[rs], kbuf_ref[...], ksw_ref[...], vbuf_ref[...],
                                        vsw_ref[...], bias, sink_a, sink_b))
            for r in (kbuf_ref, ksw_ref, vbuf_ref, vsw_ref):
                r[0:BLK, :] = r[cur, :]
            yield

        for rs, (kin, _, cum) in zip(blocks, gates):
            qb, ib = p_ref[rs, OFF_QB:OFF_QB + WIDTH_B], p_ref[rs, OFF_IB:OFF_IB + WIDTH_B]
            base = _hgrn_state_step(qb, kin, cum, ib, st_ref)
            obase_ref[rs, :] = base
            ob_ref[rs, :] = base + _hgrn_intra_fast(qb, kin, cum, ib)
            yield

        @pl.when(jnp.logical_not(mild))
        def _():
            for rs, (kin, f, cum) in zip(blocks, gates):
                qb, ib = p_ref[rs, OFF_QB:OFF_QB + WIDTH_B], p_ref[rs, OFF_IB:OFF_IB + WIDTH_B]
                ob_ref[rs, :] = obase_ref[rs, :] + _hgrn_intra_robust(
                    qb, kin, f, cum, ib, lmask_ref)
        yield

        ya = jnp.concatenate(ya_parts, axis=0) * _silu(p_ref[:, OFF_ZA:OFF_ZA + WIDTH_A])
        yb = _branch_b_out(ob_ref[...], p_ref[:, OFF_ZB:OFF_ZB + WIDTH_B], og_ref[...])
        ya_bf, yb_bf = ya.astype(BF), yb.astype(BF)
        yield

        y = (g_ref[:, :D_MODEL] * _dot(ya_bf, w_pa_ref[...])
             + g_ref[:, D_MODEL:] * _dot(yb_bf, w_pb_ref[...]))
        y_ref[y_rows, :] = x_ref[...] + gate_ref[...] * _dot(y.astype(BF), w_out_ref[...])
        yield

    def interleave(a_thunks, b_parts):
        weights = ([W_PREP, W_GATES] + [W_ATTN] * nblk + [W_RECUR] * nblk
                   + [0.0, W_MERGE, W_OUT])
        total, acc, done = sum(weights), 0.0, 0
        for wgt in weights:
            acc += wgt
            upto = int(round(len(a_thunks) * acc / total))
            for th in a_thunks[done:upto]:
                th()
            done = upto
            next(b_parts)
        assert done == len(a_thunks) and next(b_parts, "end") == "end"

    @pl.when(s == 0)
    def _():
        st_ref[...] = jnp.zeros_like(st_ref)
        for r in (kbuf_ref, ksw_ref, vbuf_ref, vsw_ref):
            r[...] = jnp.zeros_like(r)
        for th in stage_a(x0_ref, p0_ref, g0_ref):
            th()

    interleave(stage_a(x1_ref, p1_ref, g1_ref),
               stage_b(x0_ref, p0_ref, g0_ref, 2 * s, slice(0, tile)))
    interleave(stage_a(x2_ref, p0_ref, g0_ref),
               stage_b(x1_ref, p1_ref, g1_ref, 2 * s + 1, slice(tile, 2 * tile)))

    @pl.when(s == n_steps - 1)
    def _():
        for hd in range(N_HEADS_B):
            state_ref[hd] = st_ref[hd].T


def _const_spec(shape):
    nd = len(shape)
    return pl.BlockSpec(shape, lambda i, *_: (0,) * nd, pipeline_mode=pl.Buffered(1))


def _prompt_call(x, shift, scale, gate, consts, w, tile):
    t = x.shape[0]
    n_tiles = t // tile
    n_steps = n_tiles // 2
    assert n_steps * 2 * tile == t
    row = lambda n: _const_spec((1, n))
    in_specs = [
        pl.BlockSpec((tile, D_MODEL), lambda i, *_: (2 * i, 0)),
        pl.BlockSpec((tile, D_MODEL), lambda i, *_: (2 * i + 1, 0)),
        pl.BlockSpec((tile, D_MODEL), lambda i, *_: (jnp.minimum(2 * i + 2, n_tiles - 1), 0)),
        row(D_MODEL), row(D_MODEL), row(D_MODEL), row(D_MODEL),
        _const_spec((D_MODEL, D_IN)), _const_spec((D_MODEL, 2 * D_MODEL)), row(2 * D_MODEL),
        _const_spec((WIDTH_A, D_MODEL)), _const_spec((WIDTH_B, D_MODEL)),
        _const_spec((D_MODEL, D_MODEL)),
        row(WIDTH_A), row(KV_WIDTH), row(WIDTH_B), _const_spec((2, WIDTH_B)),
        _const_spec((n_tiles, LANES)), _const_spec((n_tiles, LANES)),
        _const_spec((tile, LANES)), _const_spec((tile, LANES)), _const_spec((2, LANES)),
        _const_spec((WIDTH_A, WIDTH_A)), _const_spec((BLK, BLK)),
        _const_spec((2, 4 * BLK, 2 * BLK)), _const_spec((len(LEVELS), BLK, BLK)),
    ]
    out_specs = [
        pl.BlockSpec((2 * tile, D_MODEL), lambda i, *_: (i, 0)),
        pl.BlockSpec((WINDOW, KV_WIDTH), lambda i, *_: (0, 0)),
        pl.BlockSpec((WINDOW, KV_WIDTH), lambda i, *_: (0, 0)),
        pl.BlockSpec((N_HEADS_B, HEAD_DIM_B, HEAD_DIM_B), lambda i, *_: (0, 0, 0)),
    ]
    out_shape = [
        jax.ShapeDtypeStruct((t, D_MODEL), F32),
        jax.ShapeDtypeStruct((WINDOW, KV_WIDTH), F32),
        jax.ShapeDtypeStruct((WINDOW, KV_WIDTH), F32),
        jax.ShapeDtypeStruct((N_HEADS_B, HEAD_DIM_B, HEAD_DIM_B), F32),
    ]
    scratch = [
        pltpu.VMEM((N_HEADS_B, HEAD_DIM_B, HEAD_DIM_B), F32),
        pltpu.VMEM((2 * BLK, KV_WIDTH), BF), pltpu.VMEM((2 * BLK, KV_WIDTH), BF),
        pltpu.VMEM((2 * BLK, KV_WIDTH), BF), pltpu.VMEM((2 * BLK, KV_WIDTH), BF),
        pltpu.VMEM((tile, D_IN), F32), pltpu.VMEM((tile, D_IN), F32),
        pltpu.VMEM((tile, 2 * D_MODEL), F32), pltpu.VMEM((tile, 2 * D_MODEL), F32),
        pltpu.VMEM((tile, D_MODEL), BF),
        pltpu.VMEM((tile, WIDTH_B), F32), pltpu.VMEM((tile, WIDTH_B), F32),
    ]
    return pl.pallas_call(
        functools.partial(_prompt_kernel, tile=tile, n_steps=n_steps),
        grid_spec=pltpu.PrefetchScalarGridSpec(
            num_scalar_prefetch=1, grid=(n_steps,),
            in_specs=in_specs, out_specs=out_specs, scratch_shapes=scratch),
        out_shape=out_shape,
        compiler_params=pltpu.CompilerParams(
            dimension_semantics=("arbitrary",), vmem_limit_bytes=VMEM_LIMIT),
        name="prompt_layer",
    )(w["sinks"], x, x, x, shift, scale, gate, w["norm_g"],
      w["w_in"], w["w_merge"], w["b_merge"], w["w_proj_a"], w["w_proj_b"], w["w_out"],
      w["q_g"], w["k_g"], w["o_g"], w["lb_logits"],
      consts["rope_cb"], consts["rope_sb"], consts["rope_cl"], consts["rope_sl"],
      consts["rope_sgn"],
      consts["seg"], consts["tri"], consts["bias"], consts["lmask"])


def _dec_in_kernel(x_ref, shift_ref, scale_ref, ng_ref, w_in_ref, w_mg_ref, b_mg_ref,
                   qg_ref, kg_ref, lbl_ref, rc_ref, ru_ref, rd_ref, seg_ref, perm_ref,
                   qm_ref, kn_ref, vn_ref, za_ref, qb_ref, kin_ref, ib_ref, zb_ref, g_ref):
    x = x_ref[...]
    ms = jnp.mean(x * x, axis=-1, keepdims=True)
    h = x * lax.rsqrt(ms + EPS) * (ng_ref[...] * (1.0 + scale_ref[...])) + shift_ref[...]
    h_bf = h.astype(BF)
    proj = _dot(h_bf, w_in_ref[...])
    g_ref[...] = _sigmoid(_dot(h_bf, w_mg_ref[...]) + b_mg_ref[...])

    seg = seg_ref[...]
    rc, ru, rd = rc_ref[...], ru_ref[...], rd_ref[...]
    qa = proj[:, OFF_QA:OFF_QA + WIDTH_A]
    qa = qa * _head_norm_scale(qa, seg) * (qg_ref[...] * (HEAD_DIM_A ** -0.5))
    qa = jnp.concatenate(
        [_rope(qa[:, c * LANES:(c + 1) * LANES], rc, ru, rd) for c in range(4)], axis=1)
    qm_ref[...] = _dot(qa.astype(BF), perm_ref[...])
    ka = proj[:, OFF_KA:OFF_KA + KV_WIDTH]
    ka = ka * _head_norm_scale(ka, seg[:KV_WIDTH, :KV_WIDTH]) * kg_ref[...]
    kn_ref[...] = _rope(ka, rc, ru, rd)
    vn_ref[...] = proj[:, OFF_VA:OFF_VA + KV_WIDTH]
    za_ref[...] = proj[:, OFF_ZA:OFF_ZA + WIDTH_A]
    qb_ref[...] = proj[:, OFF_QB:OFF_QB + WIDTH_B]
    lb = _lower_bound(lbl_ref[...])
    kin_ref[...] = (1.0 - lb) * (1.0 - _sigmoid(proj[:, OFF_FB:OFF_FB + WIDTH_B]))
    ib_ref[...] = proj[:, OFF_IB:OFF_IB + WIDTH_B]
    zb_ref[...] = proj[:, OFF_ZB:OFF_ZB + WIDTH_B]


def _dec_in_call(x, shift, scale, consts, w):
    b = x.shape[0]
    f = lambda n: jax.ShapeDtypeStruct((b, n), F32)
    return pl.pallas_call(
        _dec_in_kernel,
        out_shape=[f(N_HEADS_A * LANES), f(KV_WIDTH), f(KV_WIDTH), f(WIDTH_A),
                   f(WIDTH_B), f(WIDTH_B), f(WIDTH_B), f(WIDTH_B), f(2 * D_MODEL)],
        compiler_params=pltpu.CompilerParams(vmem_limit_bytes=VMEM_LIMIT),
        name="decode_in",
    )(x, shift, scale, w["norm_g"], w["w_in"], w["w_merge"], w["b_merge"],
      w["q_g"], w["k_g"], w["lb_logits"],
      consts["rope_c1"], consts["rope_u1"], consts["rope_d1"], consts["seg"], consts["perm"])


def _dec_mix_kernel(qm_ref, kn_ref, vn_ref, qb_ref, kin_ref, ib_ref, sink_ref,
                    ck_ref, cv_ref, st_ref,
                    att_ref, ob_ref, nk_ref, nv_ref, nst_ref, *, bt):
    nh = N_HEADS_A
    s = jnp.concatenate([_dot_nt(qm_ref[j].astype(BF), ck_ref[j].astype(BF))
                         for j in range(bt)], axis=0)
    key = lax.broadcasted_iota(jnp.int32, (bt * nh, WINDOW), 1)
    s = jnp.where(key == 0, NEG, s)
    rep = lambda r: jnp.broadcast_to(r[...][:, None, :], (bt, nh, r.shape[-1])).reshape(
        bt * nh, r.shape[-1])
    kn_rows, vn_rows = rep(kn_ref), rep(vn_ref)
    sink = jnp.broadcast_to(sink_ref[...][None], (bt, nh, 1)).reshape(bt * nh, 1)
    s_new = jnp.sum(qm_ref[...].reshape(bt * nh, LANES) * kn_rows, axis=-1, keepdims=True)
    m = jnp.maximum(jnp.maximum(jnp.max(s, axis=-1, keepdims=True), s_new), sink)
    p = jnp.exp(s - m)
    p_new = jnp.exp(s_new - m)
    den = jnp.sum(p, axis=-1, keepdims=True) + p_new + jnp.exp(sink - m)
    p_bf = p.astype(BF)
    pv = jnp.concatenate([_dot(p_bf[j * nh:(j + 1) * nh, :], cv_ref[j].astype(BF))
                          for j in range(bt)], axis=0)
    att_ref[...] = ((pv + p_new * vn_rows) * (1.0 / den)).reshape(bt, nh, LANES)
    for j in range(bt):
        nk_ref[j] = jnp.concatenate([ck_ref[j, 1:, :], kn_ref[j:j + 1, :]], axis=0)
        nv_ref[j] = jnp.concatenate([cv_ref[j, 1:, :], vn_ref[j:j + 1, :]], axis=0)

    for hd in range(N_HEADS_B):
        sl = slice(hd * LANES, (hd + 1) * LANES)
        kin_t = kin_ref[:, sl].T
        q_bf = qb_ref[:, sl].astype(BF)
        for j in range(bt):
            kcol = kin_t[:, j:j + 1]
            st = st_ref[j, hd]
            new = st - kcol * (st - ib_ref[j:j + 1, sl])
            nst_ref[j, hd] = new
            ob_ref[j:j + 1, sl] = _dot(q_bf, new.astype(BF))[j:j + 1, :]


def _dec_mix_call(qm, kn, vn, qb, kin, ib, sink_col, cache_k, cache_v, state, bt):
    b = kn.shape[0]
    rows = lambda n: pl.BlockSpec((bt, n), lambda i: (i, 0))
    cache_spec = pl.BlockSpec((bt, WINDOW, KV_WIDTH), lambda i: (i, 0, 0))
    st_spec = pl.BlockSpec((bt, N_HEADS_B, HEAD_DIM_B, HEAD_DIM_B), lambda i: (i, 0, 0, 0))
    qm_spec = pl.BlockSpec((bt, N_HEADS_A, LANES), lambda i: (i, 0, 0))
    return pl.pallas_call(
        functools.partial(_dec_mix_kernel, bt=bt),
        grid=(b // bt,),
        in_specs=[qm_spec, rows(KV_WIDTH), rows(KV_WIDTH), rows(WIDTH_B), rows(WIDTH_B),
                  rows(WIDTH_B), pl.BlockSpec((N_HEADS_A, 1), lambda i: (0, 0)),
                  cache_spec, cache_spec, st_spec],
        out_specs=[qm_spec, rows(WIDTH_B), cache_spec, cache_spec, st_spec],
        out_shape=[jax.ShapeDtypeStruct((b, N_HEADS_A, LANES), F32),
                   jax.ShapeDtypeStruct((b, WIDTH_B), F32),
                   jax.ShapeDtypeStruct(cache_k.shape, F32),
                   jax.ShapeDtypeStruct(cache_v.shape, F32),
                   jax.ShapeDtypeStruct(state.shape, F32)],
        compiler_params=pltpu.CompilerParams(
            dimension_semantics=("arbitrary",), vmem_limit_bytes=VMEM_LIMIT),
        name="decode_mix",
    )(qm, kn, vn, qb, kin, ib, sink_col, cache_k, cache_v, state)


def _dec_out_kernel(x_ref, gate_ref, att_ref, za_ref, ob_ref, zb_ref, g_ref, og_ref,
                    permt_ref, w_pa_ref, w_pb_ref, w_out_ref, y_ref):
    att = _split_dot(att_ref[...], permt_ref[...])
    ya = att * _silu(za_ref[...])
    yb = _branch_b_out(ob_ref[...], zb_ref[...], og_ref[...])
    g = g_ref[...]
    y = (g[:, :D_MODEL] * _dot(ya.astype(BF), w_pa_ref[...])
         + g[:, D_MODEL:] * _dot(yb.astype(BF), w_pb_ref[...]))
    y_ref[...] = x_ref[...] + gate_ref[...] * _dot(y.astype(BF), w_out_ref[...])


def _dec_out_call(x, gate, att, za, ob, zb, g, consts, w):
    return pl.pallas_call(
        _dec_out_kernel,
        out_shape=jax.ShapeDtypeStruct(x.shape, F32),
        compiler_params=pltpu.CompilerParams(vmem_limit_bytes=VMEM_LIMIT),
        name="decode_out",
    )(x, gate, att, za, ob, zb, g, w["o_g"], consts["permt"],
      w["w_proj_a"], w["w_proj_b"], w["w_out"])


def _rope_consts(n_tiles, tile, past_len):
    half = ROT_DIM // 2
    inv = ROPE_THETA ** (-np.arange(0, ROT_DIM, 2, dtype=np.float64) / ROT_DIM)
    e = np.arange(LANES) % HEAD_DIM_A
    rot = e < ROT_DIM

    def tables(pos):
        ang = np.asarray(pos, np.float64)[:, None] * inv[e % half][None, :]
        return np.where(rot, np.cos(ang), 1.0), np.where(rot, np.sin(ang), 0.0)

    sgn = np.stack([np.where(e < half, -1.0, 0.0),
                    np.where(rot & (e >= half), 1.0, 0.0)])
    cb, sb = tables(np.arange(n_tiles) * tile)
    cl, sl = tables(np.arange(tile))
    c1, s1 = tables([past_len])
    f = lambda a: jnp.asarray(a, F32)
    return {"rope_cb": f(cb), "rope_sb": f(sb), "rope_cl": f(cl), "rope_sl": f(sl),
            "rope_sgn": f(sgn), "rope_c1": f(c1), "rope_u1": f(s1 * sgn[0:1]),
            "rope_d1": f(s1 * sgn[1:2])}


def _static_consts():
    seg = np.kron(np.eye(N_HEADS_A), np.full((HEAD_DIM_A, HEAD_DIM_A), 1.0 / HEAD_DIM_A))
    tri = np.tril(np.ones((BLK, BLK)))
    r = np.arange(4 * BLK)[:, None] % BLK
    c = np.arange(2 * BLK)[None, :]
    ok_prev = (c < BLK) & (c > r)
    ok_cur = (c >= BLK) & (c - BLK <= r)
    bias = np.stack([np.where(ok_cur, 0.0, NEG), np.where(ok_prev | ok_cur, 0.0, NEG)])
    i = np.arange(BLK)[:, None]
    j = np.arange(BLK)[None, :]
    lmask = np.stack([((i // (2 * b)) == (j // (2 * b))) & ((i & b) != 0) & ((j & b) == 0)
                      for b in LEVELS]).astype(np.float32)
    perm = np.zeros((WIDTH_A, N_HEADS_A * LANES), np.float32)
    for hd in range(N_HEADS_A):
        kvh = hd // (N_HEADS_A // N_KV_A)
        for d in range(HEAD_DIM_A):
            perm[hd * HEAD_DIM_A + d, hd * LANES + kvh * HEAD_DIM_A + d] = 1.0
    return {
        "seg": jnp.asarray(seg, BF), "tri": jnp.asarray(tri, BF),
        "bias": jnp.asarray(bias, F32), "lmask": jnp.asarray(lmask, F32),
        "perm": jnp.asarray(perm, BF), "permt": jnp.asarray(perm.T, BF),
    }


def kernel(x_prompt, x_sample, cache_win_k, cache_win_v, state_hgrn, c_prompt, c_sample,
           w_ada, b_ada, norm_g, w_in, q_norm_g, k_norm_g, sinks, lb_logits, o_norm_g,
           w_merge, b_merge, w_proj_a, w_proj_b, w_out):
    depth = w_in.shape[0]
    assert depth == 1 and x_prompt.shape[0] == 1 and x_sample.shape[1] == 1
    t = x_prompt.shape[1]
    nb = x_sample.shape[0]
    past_len = t
    tile = 256
    bt = 8

    consts = _static_consts()
    consts.update(_rope_consts(t // tile, tile, past_len))

    w = {
        "sinks": sinks[0], "norm_g": norm_g[0][None, :],
        "w_in": w_in[0].astype(BF), "w_merge": w_merge[0].astype(BF),
        "b_merge": b_merge[0][None, :],
        "w_proj_a": w_proj_a[0].astype(BF), "w_proj_b": w_proj_b[0].astype(BF),
        "w_out": w_out[0].astype(BF),
        "q_g": jnp.tile(q_norm_g[0], N_HEADS_A)[None, :],
        "k_g": jnp.tile(k_norm_g[0], N_KV_A)[None, :],
        "o_g": jnp.tile(o_norm_g[0], N_HEADS_B)[None, :],
        "lb_logits": lb_logits,
    }

    mod_p, mod_s = _ada_call(c_prompt, c_sample, w_ada[0], b_ada)
    split3 = lambda m: (m[:, k * D_MODEL:(k + 1) * D_MODEL] for k in range(3))
    shift_p, scale_p, gate_p = split3(mod_p)
    shift_s, scale_s, gate_s = split3(mod_s)

    y_p, kwin, vwin, st_p = _prompt_call(
        x_prompt[0], shift_p, scale_p, gate_p, consts, w, tile)

    xs = x_sample[:, 0, :]
    qm, kn, vn, za, qb, kin, ib, zb, g = _dec_in_call(
        xs, shift_s, scale_s, consts, w)
    att, ob, nk, nv, nst = _dec_mix_call(
        qm.reshape(nb, N_HEADS_A, LANES), kn, vn, qb, kin, ib, sinks[0][:, None],
        cache_win_k[0].reshape(nb, WINDOW, KV_WIDTH), cache_win_v[0].reshape(nb, WINDOW, KV_WIDTH),
        state_hgrn[0], bt)
    y_s = _dec_out_call(xs, gate_s, att.reshape(nb, N_HEADS_A * LANES), za, ob, zb, g,
                        consts, w)

    kv_shape = (1, 1, WINDOW, N_KV_A, HEAD_DIM_A)
    kv_shape_s = (1, nb, WINDOW, N_KV_A, HEAD_DIM_A)
    return (y_p[None], y_s[:, None, :],
            kwin.reshape(kv_shape), vwin.reshape(kv_shape), st_p[None, None],
            nk.reshape(kv_shape_s), nv.reshape(kv_shape_s), nst[None])
```

```python
import functools

import numpy as np
import jax
import jax.numpy as jnp
from jax import lax
from jax.experimental import pallas as pl
from jax.experimental.pallas import tpu as pltpu

D_MODEL = 1024
HEAD_DIM_A = 64
N_HEADS_A = 8
N_KV_A = 2
WIDTH_A = N_HEADS_A * HEAD_DIM_A
KV_WIDTH = N_KV_A * HEAD_DIM_A
WINDOW = 128
ROT_DIM = HEAD_DIM_A // 4
ROPE_THETA = 500000.0
HEAD_DIM_B = 128
N_HEADS_B = 4
WIDTH_B = N_HEADS_B * HEAD_DIM_B
EPS = 1e-6
OFF_QA = 0
OFF_KA = OFF_QA + WIDTH_A
OFF_VA = OFF_KA + KV_WIDTH
OFF_ZA = OFF_VA + KV_WIDTH
OFF_QB = OFF_ZA + WIDTH_A
OFF_FB = OFF_QB + WIDTH_B
OFF_IB = OFF_FB + WIDTH_B
OFF_ZB = OFF_IB + WIDTH_B
D_IN = OFF_ZB + WIDTH_B

LANES = 128
BLK = 128
SUB = 8
LEVELS = (64, 32, 16, 8)
HGRN_FAST_MAX = 80.0
MXU_N = 256
W_PREP, W_GATES, W_ATTN, W_RECUR, W_MERGE, W_OUT = 1.0, 0.8, 1.0, 0.5, 0.8, 0.0
NEG = -1e30
VMEM_LIMIT = 56 * 1024 * 1024
PROMPT_TILE = 256
DECODE_BATCH_TILE = 16

BF = jnp.bfloat16
F32 = jnp.float32


def _dot(a, b):
    return jnp.dot(a, b, preferred_element_type=F32)


def _dot_nt(a, b):
    return lax.dot_general(a, b, (((1,), (1,)), ((), ())), preferred_element_type=F32)


def _dot_tn(a, b):
    return lax.dot_general(a, b, (((0,), (0,)), ((), ())), preferred_element_type=F32)


def _split_dot(a_f32, b_bf):
    hi = a_f32.astype(BF)
    lo = (a_f32 - hi.astype(F32)).astype(BF)
    return _dot(hi, b_bf) + _dot(lo, b_bf)


def _sigmoid(x):
    return 1.0 / (1.0 + jnp.exp(-x))


def _silu(x):
    return x * _sigmoid(x)


def _lower_bound(lb_logits):
    l0 = lb_logits[0:1, :]
    l1 = lb_logits[1:2, :]
    m = jnp.maximum(l0, l1)
    e0 = jnp.exp(l0 - m)
    e1 = jnp.exp(l1 - m)
    return e0 / (e0 + e1)


def _rope_tables(cb, sb, cl, sl, sgn_up, sgn_dn):
    c = cb * cl - sb * sl
    s = sb * cl + cb * sl
    return c, s * sgn_up, s * sgn_dn


def _rope(x, c, s_up, s_dn):
    return x * c + pltpu.roll(x, LANES - ROT_DIM // 2, 1) * s_up + pltpu.roll(x, ROT_DIM // 2, 1) * s_dn


def _head_norm_scale(x, seg_mean_bf):
    ms = _dot((x * x).astype(BF), seg_mean_bf)
    return lax.rsqrt(ms + EPS)


def _ada_kernel(cp_ref, cs_ref, w_ref, b_ref, op_ref, os_ref):
    w = w_ref[...]
    w_hi = w.astype(BF)
    w_lo = (w - w_hi.astype(F32)).astype(BF)
    b = b_ref[...]

    def dot3(c):
        c_hi = c.astype(BF)
        c_lo = (c - c_hi.astype(F32)).astype(BF)
        return _dot(c_hi, w_hi) + (_dot(c_hi, w_lo) + _dot(c_lo, w_hi))

    op_ref[...] = dot3(cp_ref[...]) + b
    os_ref[...] = dot3(cs_ref[...]) + b


def _ada_call(c_p, c_s, w_ada, b_ada):
    mp, ms = c_p.shape[0], c_s.shape[0]
    n = w_ada.shape[1]
    tn = 512
    return pl.pallas_call(
        _ada_kernel,
        grid=(n // tn,),
        in_specs=[pl.BlockSpec((mp, D_MODEL), lambda j: (0, 0)),
                  pl.BlockSpec((ms, D_MODEL), lambda j: (0, 0)),
                  pl.BlockSpec((D_MODEL, tn), lambda j: (0, j)),
                  pl.BlockSpec((1, tn), lambda j: (0, j))],
        out_specs=[pl.BlockSpec((mp, tn), lambda j: (0, j)),
                   pl.BlockSpec((ms, tn), lambda j: (0, j))],
        out_shape=[jax.ShapeDtypeStruct((mp, n), F32), jax.ShapeDtypeStruct((ms, n), F32)],
        name="ada",
    )(c_p, c_s, w_ada, b_ada)


def _attn_block(q_blk, kcat, kcat_sw, vcat, vcat_sw, bias, sink_a, sink_b):
    lane = lax.broadcasted_iota(jnp.int32, (BLK, LANES), 1)
    lo = lane < HEAD_DIM_A
    chunks = [q_blk[:, c * LANES:(c + 1) * LANES] for c in range(4)]
    zero = jnp.zeros((BLK, LANES), F32)
    q_lo = [jnp.where(lo, c, zero).astype(BF) for c in chunks]
    q_hi = [jnp.where(lo, zero, c).astype(BF) for c in chunks]
    qa = jnp.concatenate([q_lo[0], q_lo[1], q_hi[2], q_hi[3]], axis=0)
    qb = jnp.concatenate([q_hi[0], q_hi[1], q_lo[2], q_lo[3]], axis=0)

    def soft(qs, kc, vc, sink):
        s = _dot_nt(qs, kc) + bias
        m = jnp.maximum(jnp.max(s, axis=-1, keepdims=True), sink)
        p = jnp.exp(s - m)
        den = jnp.sum(p, axis=-1, keepdims=True) + jnp.exp(sink - m)
        o = _dot(p.astype(BF), vc)
        return o * (1.0 / den)

    oa = soft(qa, kcat, vcat, sink_a)
    ob = soft(qb, kcat_sw, vcat_sw, sink_b)
    r = lambda o, i: o[i * BLK:(i + 1) * BLK, :]
    return jnp.concatenate([
        jnp.where(lo, r(oa, 0), r(ob, 0)),
        jnp.where(lo, r(oa, 1), r(ob, 1)),
        jnp.where(lo, r(ob, 2), r(oa, 2)),
        jnp.where(lo, r(ob, 3), r(oa, 3)),
    ], axis=1)


def _hgrn_gates(fb, lb, one_m_lb, tri_bf):
    sig = _sigmoid(fb)
    kin = one_m_lb * (1.0 - sig)
    f = lb + one_m_lb * sig
    cum = _split_dot_left(tri_bf, jnp.log(f))
    return kin, f, cum


def _hgrn_span_decay(cum):
    q = BLK // 4
    ends = [cum[(n + 1) * q - 1:(n + 1) * q, :] for n in range(4)]
    d = -ends[0]
    for n in range(1, 4):
        d = jnp.maximum(d, ends[n - 1] - ends[n])
    return d


def _hgrn_state_step(qb, kin, cum, ib, st_ref):
    q_dec = (qb * jnp.exp(cum)).astype(BF)
    last = cum[BLK - 1:BLK, :]
    k_dec = (kin * jnp.exp(last - cum)).astype(BF)
    v_bf = ib.astype(BF)
    outs = []
    for h in range(N_HEADS_B):
        sl = slice(h * LANES, (h + 1) * LANES)
        st = st_ref[h]
        outs.append(_dot_nt(q_dec[:, sl], st.astype(BF)))
        st_ref[h] = st * jnp.exp(last[:, sl]) + _dot_tn(v_bf[:, sl], k_dec[:, sl])
    return jnp.concatenate(outs, axis=1)


def _hgrn_apply(amats, ib):
    v_bf = ib.astype(BF)
    return jnp.concatenate(
        [_dot(amats[h].astype(BF), v_bf[:, h * LANES:(h + 1) * LANES])
         for h in range(N_HEADS_B)], axis=1)


def _hgrn_intra_fast(qb, kin, cum, ib):
    half = BLK // 2
    row = lax.broadcasted_iota(jnp.int32, (BLK, 1), 0)
    upper = row >= half
    piv = cum[half - 1:half, :]
    w_lvl = jnp.exp(jnp.concatenate([piv - cum[:half, :], cum[half:, :] - piv], axis=0))
    p_lvl = (jnp.where(upper, qb, kin) * w_lvl).astype(BF)
    mid = jnp.where(upper, cum[half + half // 2 - 1:half + half // 2, :],
                    cum[half // 2 - 1:half // 2, :])
    e_mid = cum - mid
    q_mid = (qb * jnp.exp(e_mid)).astype(BF)
    k_mid = (kin * jnp.exp(-e_mid)).astype(BF)

    ri = lax.broadcasted_iota(jnp.int32, (BLK, BLK), 0)
    ci = lax.broadcasted_iota(jnp.int32, (BLK, BLK), 1)
    same_half_causal = ((ri >= half) == (ci >= half)) & (ci <= ri)
    cross = (ri >= half) & (ci < half)
    amats = []
    for h in range(N_HEADS_B):
        sl = slice(h * LANES, (h + 1) * LANES)
        a_mid = _dot_nt(q_mid[:, sl], k_mid[:, sl])
        a_lvl = _dot_nt(p_lvl[:, sl], p_lvl[:, sl])
        amats.append(jnp.where(same_half_causal, a_mid, jnp.where(cross, a_lvl, 0.0)))
    return _hgrn_apply(amats, ib)


def _hgrn_intra_robust(qb, kin, f, cum, ib, lvl_mask_ref):
    row = lax.broadcasted_iota(jnp.int32, (BLK, 1), 0)
    lvl_ops = []
    for b in LEVELS:
        pieces = []
        for r0 in range(0, BLK, 2 * b):
            piv = cum[r0 + b - 1:r0 + b, :]
            pieces.append(piv - cum[r0:r0 + b, :])
            pieces.append(cum[r0 + b:r0 + 2 * b, :] - piv)
        w = jnp.exp(jnp.concatenate(pieces, axis=0))
        second = (row & b) != 0
        lvl_ops.append((jnp.where(second, qb, kin) * w).astype(BF))

    n8 = BLK // SUB
    q3 = qb.reshape(n8, SUB, WIDTH_B)
    k3 = kin.reshape(n8, SUB, WIDTH_B)
    f3 = f.reshape(n8, SUB, WIDTH_B)
    v3 = ib.reshape(n8, SUB, WIDTH_B)
    subl = lax.broadcasted_iota(jnp.int32, (n8, SUB, 1), 1)

    def head(x, h):
        return x[..., h * LANES:(h + 1) * LANES]

    g = q3 * k3
    acc = [jnp.sum(head(g, h), axis=-1, keepdims=True) * head(v3, h) for h in range(N_HEADS_B)]
    dec = jnp.ones_like(f3)
    kd = k3
    vd = v3
    for d in range(1, SUB):
        dec = f3 * pltpu.roll(dec, 1, 1)
        kd = pltpu.roll(kd, 1, 1)
        vd = pltpu.roll(vd, 1, 1)
        g = q3 * kd * dec
        ok = subl >= d
        for h in range(N_HEADS_B):
            a = jnp.where(ok, jnp.sum(head(g, h), axis=-1, keepdims=True), 0.0)
            acc[h] = acc[h] + a * head(vd, h)

    amats = []
    for h in range(N_HEADS_B):
        sl = slice(h * LANES, (h + 1) * LANES)
        amat = jnp.zeros((BLK, BLK), F32)
        for li in range(len(LEVELS)):
            p = lvl_ops[li][:, sl]
            amat = amat + lvl_mask_ref[li] * _dot_nt(p, p)
        amats.append(amat)
    diag = jnp.concatenate([acc[h].reshape(BLK, LANES) for h in range(N_HEADS_B)], axis=1)
    return _hgrn_apply(amats, ib) + diag


def _split_dot_left(a_bf, b_f32):
    hi = b_f32.astype(BF)
    lo = (b_f32 - hi.astype(F32)).astype(BF)
    return _dot(a_bf, hi) + _dot(a_bf, lo)


def _branch_b_out(o, zb, og):
    outs = []
    for h in range(N_HEADS_B):
        sl = slice(h * LANES, (h + 1) * LANES)
        oh = o[:, sl]
        ms = jnp.mean(oh * oh, axis=-1, keepdims=True)
        outs.append(oh * lax.rsqrt(ms + EPS))
    return jnp.concatenate(outs, axis=1) * og * _silu(zb)


def _prompt_kernel(sinks_ref,
                   x0_ref, x1_ref, x2_ref, shift_ref, scale_ref, gate_ref, ng_ref,
                   w_in_ref, w_mg_ref, b_mg_ref, w_pa_ref, w_pb_ref, w_out_ref,
                   qg_ref, kg_ref, og_ref, lbl_ref,
                   cb_ref, sb_ref, cl_ref, sl_ref, sgn_ref,
                   seg_ref, tri_ref, bias_ref, lmask_ref,
                   y_ref, kwin_ref, vwin_ref, state_ref,
                   st_ref, kbuf_ref, ksw_ref, vbuf_ref, vsw_ref,
                   p0_ref, p1_ref, g_ref, h0_ref, h1_ref, ob_ref, obase_ref,
                   *, tile, n_steps):
    s = pl.program_id(0)
    nblk = tile // BLK

    def stage_a(x_ref, h_ref, p_ref):
        def prep():
            x = x_ref[...]
            ms = jnp.mean(x * x, axis=-1, keepdims=True)
            h = x * lax.rsqrt(ms + EPS) * (ng_ref[...] * (1.0 + scale_ref[...])) + shift_ref[...]
            h_ref[...] = h.astype(BF)

        def proj_chunk(c):
            def run():
                cs = slice(c * MXU_N, (c + 1) * MXU_N)
                p_ref[:, cs] = _dot(h_ref[...], w_in_ref[:, cs])
            return run

        return [prep] + [proj_chunk(c) for c in range(D_IN // MXU_N)]

    def gate_chunks(h_ref):
        def gate_chunk(c):
            def run():
                cs = slice(c * MXU_N, (c + 1) * MXU_N)
                g_ref[:, cs] = _sigmoid(_dot(h_ref[...], w_mg_ref[:, cs]) + b_mg_ref[:, cs])
            return run

        return [gate_chunk(c) for c in range(2 * D_MODEL // MXU_N)]

    def phase(h_cur_ref, a_next, b_parts):
        gc = gate_chunks(h_cur_ref)
        interleave(gc[:2] + a_next[:1] + gc[2:] + a_next[1:], b_parts)

    def stage_b(x_ref, p_ref, t_idx, y_rows):
        seg = seg_ref[...]
        rc, ru, rd = _rope_tables(cb_ref[pl.ds(t_idx, 1), :], sb_ref[pl.ds(t_idx, 1), :],
                                  cl_ref[...], sl_ref[...], sgn_ref[0:1, :], sgn_ref[1:2, :])
        qa = p_ref[:, OFF_QA:OFF_QA + WIDTH_A]
        qa = qa * _head_norm_scale(qa, seg) * (qg_ref[...] * (HEAD_DIM_A ** -0.5))
        qa = jnp.concatenate(
            [_rope(qa[:, c * LANES:(c + 1) * LANES], rc, ru, rd) for c in range(4)], axis=1)
        ka = p_ref[:, OFF_KA:OFF_KA + KV_WIDTH]
        ka = ka * _head_norm_scale(ka, seg[:KV_WIDTH, :KV_WIDTH]) * kg_ref[...]
        ka = _rope(ka, rc, ru, rd)
        va = p_ref[:, OFF_VA:OFF_VA + KV_WIDTH]
        ka_sw = pltpu.roll(ka, HEAD_DIM_A, 1)
        va_sw = pltpu.roll(va, HEAD_DIM_A, 1)

        lb = _lower_bound(lbl_ref[...])
        one_m_lb = 1.0 - lb
        tri = tri_ref[...]

        rows4 = lax.broadcasted_iota(jnp.int32, (4 * BLK, 1), 0) // BLK

        def sink_col(heads):
            col = jnp.zeros((4 * BLK, 1), F32)
            for n, hd in enumerate(heads):
                col = jnp.where(rows4 == n, sinks_ref[hd], col)
            return col
        sink_a = sink_col((0, 2, 5, 7))
        sink_b = sink_col((1, 3, 4, 6))
        kwin_ref[...] = ka[tile - WINDOW:, :]
        vwin_ref[...] = va[tile - WINDOW:, :]
        yield

        blocks = [slice(blk * BLK, (blk + 1) * BLK) for blk in range(nblk)]
        gates = [_hgrn_gates(p_ref[rs, OFF_FB:OFF_FB + WIDTH_B], lb, one_m_lb, tri)
                 for rs in blocks]
        span = _hgrn_span_decay(gates[0][2])
        for g in gates[1:]:
            span = jnp.maximum(span, _hgrn_span_decay(g[2]))
        mild = jnp.max(span) < HGRN_FAST_MAX
        yield

        ya_parts = []
        for blk, rs in enumerate(blocks):
            cur = slice(BLK, 2 * BLK)
            kbuf_ref[cur, :] = ka[rs].astype(BF)
            ksw_ref[cur, :] = ka_sw[rs].astype(BF)
            vbuf_ref[cur, :] = va[rs].astype(BF)
            vsw_ref[cur, :] = va_sw[rs].astype(BF)
            bias = bias_ref[jnp.where(t_idx == 0, 0, 1)] if blk == 0 else bias_ref[1]
            ya_parts.append(_attn_block(qa[rs], kbuf_ref[...], ksw_ref[...], vbuf_ref[...],
                                        vsw_ref[...], bias, sink_a, sink_b))
            for r in (kbuf_ref, ksw_ref, vbuf_ref, vsw_ref):
                r[0:BLK, :] = r[cur, :]
            yield

        for rs, (kin, _, cum) in zip(blocks, gates):
            qb, ib = p_ref[rs, OFF_QB:OFF_QB + WIDTH_B], p_ref[rs, OFF_IB:OFF_IB + WIDTH_B]
            base = _hgrn_state_step(qb, kin, cum, ib, st_ref)
            obase_ref[rs, :] = base
            ob_ref[rs, :] = base + _hgrn_intra_fast(qb, kin, cum, ib)
            yield

        @pl.when(jnp.logical_not(mild))
        def _():
            for rs, (kin, f, cum) in zip(blocks, gates):
                qb, ib = p_ref[rs, OFF_QB:OFF_QB + WIDTH_B], p_ref[rs, OFF_IB:OFF_IB + WIDTH_B]
                ob_ref[rs, :] = obase_ref[rs, :] + _hgrn_intra_robust(
                    qb, kin, f, cum, ib, lmask_ref)
        yield

        ya = jnp.concatenate(ya_parts, axis=0) * _silu(p_ref[:, OFF_ZA:OFF_ZA + WIDTH_A])
        yb = _branch_b_out(ob_ref[...], p_ref[:, OFF_ZB:OFF_ZB + WIDTH_B], og_ref[...])
        ya_bf, yb_bf = ya.astype(BF), yb.astype(BF)
        yield

        y = (g_ref[:, :D_MODEL] * _dot(ya_bf, w_pa_ref[...])
             + g_ref[:, D_MODEL:] * _dot(yb_bf, w_pb_ref[...]))
        y_ref[y_rows, :] = x_ref[...] + gate_ref[...] * _dot(y.astype(BF), w_out_ref[...])
        yield

    def interleave(a_thunks, b_parts):
        weights = ([W_PREP, W_GATES] + [W_ATTN] * nblk + [W_RECUR] * nblk
                   + [0.0, W_MERGE, W_OUT])
        total, acc, done = sum(weights), 0.0, 0
        for wgt in weights:
            acc += wgt
            upto = int(round(len(a_thunks) * acc / total))
            for th in a_thunks[done:upto]:
                th()
            done = upto
            next(b_parts)
        assert done == len(a_thunks) and next(b_parts, "end") == "end"

    @pl.when(s == 0)
    def _():
        st_ref[...] = jnp.zeros_like(st_ref)
        for r in (kbuf_ref, ksw_ref, vbuf_ref, vsw_ref):
            r[...] = jnp.zeros_like(r)
        for th in stage_a(x0_ref, h0_ref, p0_ref):
            th()

    phase(h0_ref, stage_a(x1_ref, h1_ref, p1_ref),
          stage_b(x0_ref, p0_ref, 2 * s, slice(0, tile)))
    phase(h1_ref, stage_a(x2_ref, h0_ref, p0_ref),
          stage_b(x1_ref, p1_ref, 2 * s + 1, slice(tile, 2 * tile)))

    @pl.when(s == n_steps - 1)
    def _():
        for hd in range(N_HEADS_B):
            state_ref[hd] = st_ref[hd].T


def _const_spec(shape):
    nd = len(shape)
    return pl.BlockSpec(shape, lambda i, *_: (0,) * nd, pipeline_mode=pl.Buffered(1))


def _prompt_call(x, shift, scale, gate, consts, w, tile):
    t = x.shape[0]
    n_tiles = t // tile
    n_steps = n_tiles // 2
    assert n_steps * 2 * tile == t
    row = lambda n: _const_spec((1, n))
    in_specs = [
        pl.BlockSpec((tile, D_MODEL), lambda i, *_: (2 * i, 0)),
        pl.BlockSpec((tile, D_MODEL), lambda i, *_: (2 * i + 1, 0)),
        pl.BlockSpec((tile, D_MODEL), lambda i, *_: (jnp.minimum(2 * i + 2, n_tiles - 1), 0)),
        row(D_MODEL), row(D_MODEL), row(D_MODEL), row(D_MODEL),
        _const_spec((D_MODEL, D_IN)), _const_spec((D_MODEL, 2 * D_MODEL)), row(2 * D_MODEL),
        _const_spec((WIDTH_A, D_MODEL)), _const_spec((WIDTH_B, D_MODEL)),
        _const_spec((D_MODEL, D_MODEL)),
        row(WIDTH_A), row(KV_WIDTH), row(WIDTH_B), _const_spec((2, WIDTH_B)),
        _const_spec((n_tiles, LANES)), _const_spec((n_tiles, LANES)),
        _const_spec((tile, LANES)), _const_spec((tile, LANES)), _const_spec((2, LANES)),
        _const_spec((WIDTH_A, WIDTH_A)), _const_spec((BLK, BLK)),
        _const_spec((2, 4 * BLK, 2 * BLK)), _const_spec((len(LEVELS), BLK, BLK)),
    ]
    out_specs = [
        pl.BlockSpec((2 * tile, D_MODEL), lambda i, *_: (i, 0)),
        pl.BlockSpec((WINDOW, KV_WIDTH), lambda i, *_: (0, 0)),
        pl.BlockSpec((WINDOW, KV_WIDTH), lambda i, *_: (0, 0)),
        pl.BlockSpec((N_HEADS_B, HEAD_DIM_B, HEAD_DIM_B), lambda i, *_: (0, 0, 0)),
    ]
    out_shape = [
        jax.ShapeDtypeStruct((t, D_MODEL), F32),
        jax.ShapeDtypeStruct((WINDOW, KV_WIDTH), F32),
        jax.ShapeDtypeStruct((WINDOW, KV_WIDTH), F32),
        jax.ShapeDtypeStruct((N_HEADS_B, HEAD_DIM_B, HEAD_DIM_B), F32),
    ]
    scratch = [
        pltpu.VMEM((N_HEADS_B, HEAD_DIM_B, HEAD_DIM_B), F32),
        pltpu.VMEM((2 * BLK, KV_WIDTH), BF), pltpu.VMEM((2 * BLK, KV_WIDTH), BF),
        pltpu.VMEM((2 * BLK, KV_WIDTH), BF), pltpu.VMEM((2 * BLK, KV_WIDTH), BF),
        pltpu.VMEM((tile, D_IN), F32), pltpu.VMEM((tile, D_IN), F32),
        pltpu.VMEM((tile, 2 * D_MODEL), F32),
        pltpu.VMEM((tile, D_MODEL), BF), pltpu.VMEM((tile, D_MODEL), BF),
        pltpu.VMEM((tile, WIDTH_B), F32), pltpu.VMEM((tile, WIDTH_B), F32),
    ]
    return pl.pallas_call(
        functools.partial(_prompt_kernel, tile=tile, n_steps=n_steps),
        grid_spec=pltpu.PrefetchScalarGridSpec(
            num_scalar_prefetch=1, grid=(n_steps,),
            in_specs=in_specs, out_specs=out_specs, scratch_shapes=scratch),
        out_shape=out_shape,
        compiler_params=pltpu.CompilerParams(
            dimension_semantics=("arbitrary",), vmem_limit_bytes=VMEM_LIMIT),
        name="prompt_layer",
    )(w["sinks"], x, x, x, shift, scale, gate, w["norm_g"],
      w["w_in"], w["w_merge"], w["b_merge"], w["w_proj_a"], w["w_proj_b"], w["w_out"],
      w["q_g"], w["k_g"], w["o_g"], w["lb_logits"],
      consts["rope_cb"], consts["rope_sb"], consts["rope_cl"], consts["rope_sl"],
      consts["rope_sgn"],
      consts["seg"], consts["tri"], consts["bias"], consts["lmask"])


def _dec_in_kernel(x_ref, shift_ref, scale_ref, ng_ref, w_in_ref, w_mg_ref, b_mg_ref,
                   qg_ref, kg_ref, lbl_ref, rc_ref, ru_ref, rd_ref, seg_ref, perm_ref,
                   qm_ref, kn_ref, vn_ref, za_ref, qb_ref, kin_ref, ib_ref, zb_ref, g_ref):
    x = x_ref[...]
    ms = jnp.mean(x * x, axis=-1, keepdims=True)
    h = x * lax.rsqrt(ms + EPS) * (ng_ref[...] * (1.0 + scale_ref[...])) + shift_ref[...]
    h_bf = h.astype(BF)
    proj = _dot(h_bf, w_in_ref[...])
    g_ref[...] = _sigmoid(_dot(h_bf, w_mg_ref[...]) + b_mg_ref[...])

    seg = seg_ref[...]
    rc, ru, rd = rc_ref[...], ru_ref[...], rd_ref[...]
    qa = proj[:, OFF_QA:OFF_QA + WIDTH_A]
    qa = qa * _head_norm_scale(qa, seg) * (qg_ref[...] * (HEAD_DIM_A ** -0.5))
    qa = jnp.concatenate(
        [_rope(qa[:, c * LANES:(c + 1) * LANES], rc, ru, rd) for c in range(4)], axis=1)
    qm_ref[...] = _dot(qa.astype(BF), perm_ref[...])
    ka = proj[:, OFF_KA:OFF_KA + KV_WIDTH]
    ka = ka * _head_norm_scale(ka, seg[:KV_WIDTH, :KV_WIDTH]) * kg_ref[...]
    kn_ref[...] = _rope(ka, rc, ru, rd)
    vn_ref[...] = proj[:, OFF_VA:OFF_VA + KV_WIDTH]
    za_ref[...] = proj[:, OFF_ZA:OFF_ZA + WIDTH_A]
    qb_ref[...] = proj[:, OFF_QB:OFF_QB + WIDTH_B]
    lb = _lower_bound(lbl_ref[...])
    kin_ref[...] = (1.0 - lb) * (1.0 - _sigmoid(proj[:, OFF_FB:OFF_FB + WIDTH_B]))
    ib_ref[...] = proj[:, OFF_IB:OFF_IB + WIDTH_B]
    zb_ref[...] = proj[:, OFF_ZB:OFF_ZB + WIDTH_B]


def _dec_in_call(x, shift, scale, consts, w):
    b = x.shape[0]
    f = lambda n: jax.ShapeDtypeStruct((b, n), F32)
    return pl.pallas_call(
        _dec_in_kernel,
        out_shape=[f(N_HEADS_A * LANES), f(KV_WIDTH), f(KV_WIDTH), f(WIDTH_A),
                   f(WIDTH_B), f(WIDTH_B), f(WIDTH_B), f(WIDTH_B), f(2 * D_MODEL)],
        compiler_params=pltpu.CompilerParams(vmem_limit_bytes=VMEM_LIMIT),
        name="decode_in",
    )(x, shift, scale, w["norm_g"], w["w_in"], w["w_merge"], w["b_merge"],
      w["q_g"], w["k_g"], w["lb_logits"],
      consts["rope_c1"], consts["rope_u1"], consts["rope_d1"], consts["seg"], consts["perm"])


def _dec_mix_kernel(qm_ref, kn_ref, vn_ref, qb_ref, kin_ref, ib_ref, sink_ref,
                    ck_ref, cv_ref, st_ref,
                    att_ref, ob_ref, nk_ref, nv_ref, nst_ref, *, bt):
    nh = N_HEADS_A
    s = jnp.concatenate([_dot_nt(qm_ref[j].astype(BF), ck_ref[j].astype(BF))
                         for j in range(bt)], axis=0)
    key = lax.broadcasted_iota(jnp.int32, (bt * nh, WINDOW), 1)
    s = jnp.where(key == 0, NEG, s)
    rep = lambda r: jnp.broadcast_to(r[...][:, None, :], (bt, nh, r.shape[-1])).reshape(
        bt * nh, r.shape[-1])
    kn_rows, vn_rows = rep(kn_ref), rep(vn_ref)
    sink = jnp.broadcast_to(sink_ref[...][None], (bt, nh, 1)).reshape(bt * nh, 1)
    s_new = jnp.sum(qm_ref[...].reshape(bt * nh, LANES) * kn_rows, axis=-1, keepdims=True)
    m = jnp.maximum(jnp.maximum(jnp.max(s, axis=-1, keepdims=True), s_new), sink)
    p = jnp.exp(s - m)
    p_new = jnp.exp(s_new - m)
    den = jnp.sum(p, axis=-1, keepdims=True) + p_new + jnp.exp(sink - m)
    p_bf = p.astype(BF)
    pv = jnp.concatenate([_dot(p_bf[j * nh:(j + 1) * nh, :], cv_ref[j].astype(BF))
                          for j in range(bt)], axis=0)
    att_ref[...] = ((pv + p_new * vn_rows) * (1.0 / den)).reshape(bt, nh, LANES)
    for j in range(bt):
        nk_ref[j] = jnp.concatenate([ck_ref[j, 1:, :], kn_ref[j:j + 1, :]], axis=0)
        nv_ref[j] = jnp.concatenate([cv_ref[j, 1:, :], vn_ref[j:j + 1, :]], axis=0)

    for hd in range(N_HEADS_B):
        sl = slice(hd * LANES, (hd + 1) * LANES)
        kin_t = kin_ref[:, sl].T
        q_bf = qb_ref[:, sl].astype(BF)
        for j in range(bt):
            kcol = kin_t[:, j:j + 1]
            st = st_ref[j, hd]
            new = st - kcol * (st - ib_ref[j:j + 1, sl])
            nst_ref[j, hd] = new
            ob_ref[j:j + 1, sl] = _dot(q_bf, new.astype(BF))[j:j + 1, :]


def _dec_mix_call(qm, kn, vn, qb, kin, ib, sink_col, cache_k, cache_v, state, bt):
    b = kn.shape[0]
    rows = lambda n: pl.BlockSpec((bt, n), lambda i: (i, 0))
    cache_spec = pl.BlockSpec((bt, WINDOW, KV_WIDTH), lambda i: (i, 0, 0))
    st_spec = pl.BlockSpec((bt, N_HEADS_B, HEAD_DIM_B, HEAD_DIM_B), lambda i: (i, 0, 0, 0))
    qm_spec = pl.BlockSpec((bt, N_HEADS_A, LANES), lambda i: (i, 0, 0))
    return pl.pallas_call(
        functools.partial(_dec_mix_kernel, bt=bt),
        grid=(b // bt,),
        in_specs=[qm_spec, rows(KV_WIDTH), rows(KV_WIDTH), rows(WIDTH_B), rows(WIDTH_B),
                  rows(WIDTH_B), pl.BlockSpec((N_HEADS_A, 1), lambda i: (0, 0)),
                  cache_spec, cache_spec, st_spec],
        out_specs=[qm_spec, rows(WIDTH_B), cache_spec, cache_spec, st_spec],
        out_shape=[jax.ShapeDtypeStruct((b, N_HEADS_A, LANES), F32),
                   jax.ShapeDtypeStruct((b, WIDTH_B), F32),
                   jax.ShapeDtypeStruct(cache_k.shape, F32),
                   jax.ShapeDtypeStruct(cache_v.shape, F32),
                   jax.ShapeDtypeStruct(state.shape, F32)],
        compiler_params=pltpu.CompilerParams(
            dimension_semantics=("arbitrary",), vmem_limit_bytes=VMEM_LIMIT),
        name="decode_mix",
    )(qm, kn, vn, qb, kin, ib, sink_col, cache_k, cache_v, state)


def _dec_out_kernel(x_ref, gate_ref, att_ref, za_ref, ob_ref, zb_ref, g_ref, og_ref,
                    permt_ref, w_pa_ref, w_pb_ref, w_out_ref, y_ref):
    att = _split_dot(att_ref[...], permt_ref[...])
    ya = att * _silu(za_ref[...])
    yb = _branch_b_out(ob_ref[...], zb_ref[...], og_ref[...])
    g = g_ref[...]
    y = (g[:, :D_MODEL] * _dot(ya.astype(BF), w_pa_ref[...])
         + g[:, D_MODEL:] * _dot(yb.astype(BF), w_pb_ref[...]))
    y_ref[...] = x_ref[...] + gate_ref[...] * _dot(y.astype(BF), w_out_ref[...])


def _dec_out_call(x, gate, att, za, ob, zb, g, consts, w):
    return pl.pallas_call(
        _dec_out_kernel,
        out_shape=jax.ShapeDtypeStruct(x.shape, F32),
        compiler_params=pltpu.CompilerParams(vmem_limit_bytes=VMEM_LIMIT),
        name="decode_out",
    )(x, gate, att, za, ob, zb, g, w["o_g"], consts["permt"],
      w["w_proj_a"], w["w_proj_b"], w["w_out"])


def _rope_consts(n_tiles, tile, past_len):
    half = ROT_DIM // 2
    inv = ROPE_THETA ** (-np.arange(0, ROT_DIM, 2, dtype=np.float64) / ROT_DIM)
    e = np.arange(LANES) % HEAD_DIM_A
    rot = e < ROT_DIM

    def tables(pos):
        ang = np.asarray(pos, np.float64)[:, None] * inv[e % half][None, :]
        return np.where(rot, np.cos(ang), 1.0), np.where(rot, np.sin(ang), 0.0)

    sgn = np.stack([np.where(e < half, -1.0, 0.0),
                    np.where(rot & (e >= half), 1.0, 0.0)])
    cb, sb = tables(np.arange(n_tiles) * tile)
    cl, sl = tables(np.arange(tile))
    c1, s1 = tables([past_len])
    f = lambda a: jnp.asarray(a, F32)
    return {"rope_cb": f(cb), "rope_sb": f(sb), "rope_cl": f(cl), "rope_sl": f(sl),
            "rope_sgn": f(sgn), "rope_c1": f(c1), "rope_u1": f(s1 * sgn[0:1]),
            "rope_d1": f(s1 * sgn[1:2])}


def _static_consts():
    seg = np.kron(np.eye(N_HEADS_A), np.full((HEAD_DIM_A, HEAD_DIM_A), 1.0 / HEAD_DIM_A))
    tri = np.tril(np.ones((BLK, BLK)))
    r = np.arange(4 * BLK)[:, None] % BLK
    c = np.arange(2 * BLK)[None, :]
    ok_prev = (c < BLK) & (c > r)
    ok_cur = (c >= BLK) & (c - BLK <= r)
    bias = np.stack([np.where(ok_cur, 0.0, NEG), np.where(ok_prev | ok_cur, 0.0, NEG)])
    i = np.arange(BLK)[:, None]
    j = np.arange(BLK)[None, :]
    lmask = np.stack([((i // (2 * b)) == (j // (2 * b))) & ((i & b) != 0) & ((j & b) == 0)
                      for b in LEVELS]).astype(np.float32)
    perm = np.zeros((WIDTH_A, N_HEADS_A * LANES), np.float32)
    for hd in range(N_HEADS_A):
        kvh = hd // (N_HEADS_A // N_KV_A)
        for d in range(HEAD_DIM_A):
            perm[hd * HEAD_DIM_A + d, hd * LANES + kvh * HEAD_DIM_A + d] = 1.0
    return {
        "seg": jnp.asarray(seg, BF), "tri": jnp.asarray(tri, BF),
        "bias": jnp.asarray(bias, F32), "lmask": jnp.asarray(lmask, F32),
        "perm": jnp.asarray(perm, BF), "permt": jnp.asarray(perm.T, BF),
    }


def kernel(x_prompt, x_sample, cache_win_k, cache_win_v, state_hgrn, c_prompt, c_sample,
           w_ada, b_ada, norm_g, w_in, q_norm_g, k_norm_g, sinks, lb_logits, o_norm_g,
           w_merge, b_merge, w_proj_a, w_proj_b, w_out):
    depth = w_in.shape[0]
    assert depth == 1 and x_prompt.shape[0] == 1 and x_sample.shape[1] == 1
    t = x_prompt.shape[1]
    nb = x_sample.shape[0]
    past_len = t
    tile = PROMPT_TILE
    bt = DECODE_BATCH_TILE

    consts = _static_consts()
    consts.update(_rope_consts(t // tile, tile, past_len))

    w = {
        "sinks": sinks[0], "norm_g": norm_g[0][None, :],
        "w_in": w_in[0].astype(BF), "w_merge": w_merge[0].astype(BF),
        "b_merge": b_merge[0][None, :],
        "w_proj_a": w_proj_a[0].astype(BF), "w_proj_b": w_proj_b[0].astype(BF),
        "w_out": w_out[0].astype(BF),
        "q_g": jnp.tile(q_norm_g[0], N_HEADS_A)[None, :],
        "k_g": jnp.tile(k_norm_g[0], N_KV_A)[None, :],
        "o_g": jnp.tile(o_norm_g[0], N_HEADS_B)[None, :],
        "lb_logits": lb_logits,
    }

    mod_p, mod_s = _ada_call(c_prompt, c_sample, w_ada[0], b_ada)
    split3 = lambda m: (m[:, k * D_MODEL:(k + 1) * D_MODEL] for k in range(3))
    shift_p, scale_p, gate_p = split3(mod_p)
    shift_s, scale_s, gate_s = split3(mod_s)

    y_p, kwin, vwin, st_p = _prompt_call(
        x_prompt[0], shift_p, scale_p, gate_p, consts, w, tile)

    xs = x_sample[:, 0, :]
    qm, kn, vn, za, qb, kin, ib, zb, g = _dec_in_call(
        xs, shift_s, scale_s, consts, w)
    att, ob, nk, nv, nst = _dec_mix_call(
        qm.reshape(nb, N_HEADS_A, LANES), kn, vn, qb, kin, ib, sinks[0][:, None],
        cache_win_k[0].reshape(nb, WINDOW, KV_WIDTH), cache_win_v[0].reshape(nb, WINDOW, KV_WIDTH),
        state_hgrn[0], bt)
    y_s = _dec_out_call(xs, gate_s, att.reshape(nb, N_HEADS_A * LANES), za, ob, zb, g,
                        consts, w)

    kv_shape = (1, 1, WINDOW, N_KV_A, HEAD_DIM_A)
    kv_shape_s = (1, nb, WINDOW, N_KV_A, HEAD_DIM_A)
    return (y_p[None], y_s[:, None, :],
            kwin.reshape(kv_shape), vwin.reshape(kv_shape), st_p[None, None],
            nk.reshape(kv_shape_s), nv.reshape(kv_shape_s), nst[None])
```

```python
import functools

import numpy as np
import jax
import jax.numpy as jnp
from jax import lax
from jax.experimental import pallas as pl
from jax.experimental.pallas import tpu as pltpu

D_MODEL = 1024
HEAD_DIM_A = 64
N_HEADS_A = 8
N_KV_A = 2
WIDTH_A = N_HEADS_A * HEAD_DIM_A
KV_WIDTH = N_KV_A * HEAD_DIM_A
WINDOW = 128
ROT_DIM = HEAD_DIM_A // 4
ROPE_THETA = 500000.0
HEAD_DIM_B = 128
N_HEADS_B = 4
WIDTH_B = N_HEADS_B * HEAD_DIM_B
EPS = 1e-6
OFF_QA = 0
OFF_KA = OFF_QA + WIDTH_A
OFF_VA = OFF_KA + KV_WIDTH
OFF_ZA = OFF_VA + KV_WIDTH
OFF_QB = OFF_ZA + WIDTH_A
OFF_FB = OFF_QB + WIDTH_B
OFF_IB = OFF_FB + WIDTH_B
OFF_ZB = OFF_IB + WIDTH_B
D_IN = OFF_ZB + WIDTH_B

LANES = 128
BLK = 128
SUB = 8
LEVELS = (64, 32, 16, 8)
HGRN_FAST_MAX = 80.0
MXU_N = 256
W_PREP, W_GATES, W_ATTN, W_RECUR, W_MERGE, W_OUT = 1.0, 0.8, 1.0, 0.5, 0.8, 0.0
NEG = -1e30
VMEM_LIMIT = 56 * 1024 * 1024
PROMPT_TILE = 256
DECODE_BATCH_TILE = 16

BF = jnp.bfloat16
F32 = jnp.float32


def _dot(a, b):
    return jnp.dot(a, b, preferred_element_type=F32)


def _dot_nt(a, b):
    return lax.dot_general(a, b, (((1,), (1,)), ((), ())), preferred_element_type=F32)


def _dot_tn(a, b):
    return lax.dot_general(a, b, (((0,), (0,)), ((), ())), preferred_element_type=F32)


def _split_dot(a_f32, b_bf):
    hi = a_f32.astype(BF)
    lo = (a_f32 - hi.astype(F32)).astype(BF)
    return _dot(hi, b_bf) + _dot(lo, b_bf)


def _sigmoid(x):
    return 1.0 / (1.0 + jnp.exp(-x))


def _silu(x):
    return x * _sigmoid(x)


def _lower_bound(lb_logits):
    l0 = lb_logits[0:1, :]
    l1 = lb_logits[1:2, :]
    m = jnp.maximum(l0, l1)
    e0 = jnp.exp(l0 - m)
    e1 = jnp.exp(l1 - m)
    return e0 / (e0 + e1)


def _rope_tables(cb, sb, cl, sl, sgn_up, sgn_dn):
    c = cb * cl - sb * sl
    s = sb * cl + cb * sl
    return c, s * sgn_up, s * sgn_dn


def _rope(x, c, s_up, s_dn):
    return x * c + pltpu.roll(x, LANES - ROT_DIM // 2, 1) * s_up + pltpu.roll(x, ROT_DIM // 2, 1) * s_dn


def _head_norm_scale(x, seg_mean_bf):
    ms = _dot((x * x).astype(BF), seg_mean_bf)
    return lax.rsqrt(ms + EPS)


def _ada_kernel(cp_ref, cs_ref, w_ref, b_ref, op_ref, os_ref):
    w = w_ref[...]
    w_hi = w.astype(BF)
    w_lo = (w - w_hi.astype(F32)).astype(BF)
    b = b_ref[...]

    def dot3(c):
        c_hi = c.astype(BF)
        c_lo = (c - c_hi.astype(F32)).astype(BF)
        return _dot(c_hi, w_hi) + (_dot(c_hi, w_lo) + _dot(c_lo, w_hi))

    op_ref[...] = dot3(cp_ref[...]) + b
    os_ref[...] = dot3(cs_ref[...]) + b


def _ada_call(c_p, c_s, w_ada, b_ada):
    mp, ms = c_p.shape[0], c_s.shape[0]
    n = w_ada.shape[1]
    tn = 512
    return pl.pallas_call(
        _ada_kernel,
        grid=(n // tn,),
        in_specs=[pl.BlockSpec((mp, D_MODEL), lambda j: (0, 0)),
                  pl.BlockSpec((ms, D_MODEL), lambda j: (0, 0)),
                  pl.BlockSpec((D_MODEL, tn), lambda j: (0, j)),
                  pl.BlockSpec((1, tn), lambda j: (0, j))],
        out_specs=[pl.BlockSpec((mp, tn), lambda j: (0, j)),
                   pl.BlockSpec((ms, tn), lambda j: (0, j))],
        out_shape=[jax.ShapeDtypeStruct((mp, n), F32), jax.ShapeDtypeStruct((ms, n), F32)],
        name="ada",
    )(c_p, c_s, w_ada, b_ada)


def _attn_block(q_blk, kcat, kcat_sw, vcat, vcat_sw, bias, sink_a, sink_b):
    lane = lax.broadcasted_iota(jnp.int32, (BLK, LANES), 1)
    lo = lane < HEAD_DIM_A
    chunks = [q_blk[:, c * LANES:(c + 1) * LANES] for c in range(4)]
    zero = jnp.zeros((BLK, LANES), F32)
    q_lo = [jnp.where(lo, c, zero).astype(BF) for c in chunks]
    q_hi = [jnp.where(lo, zero, c).astype(BF) for c in chunks]
    qa = jnp.concatenate([q_lo[0], q_lo[1], q_hi[2], q_hi[3]], axis=0)
    qb = jnp.concatenate([q_hi[0], q_hi[1], q_lo[2], q_lo[3]], axis=0)

    def soft(qs, kc, vc, sink):
        s = _dot_nt(qs, kc) + bias
        m = jnp.maximum(jnp.max(s, axis=-1, keepdims=True), sink)
        p = jnp.exp(s - m)
        den = jnp.sum(p, axis=-1, keepdims=True) + jnp.exp(sink - m)
        o = _dot(p.astype(BF), vc)
        return o * (1.0 / den)

    oa = soft(qa, kcat, vcat, sink_a)
    ob = soft(qb, kcat_sw, vcat_sw, sink_b)
    r = lambda o, i: o[i * BLK:(i + 1) * BLK, :]
    return jnp.concatenate([
        jnp.where(lo, r(oa, 0), r(ob, 0)),
        jnp.where(lo, r(oa, 1), r(ob, 1)),
        jnp.where(lo, r(ob, 2), r(oa, 2)),
        jnp.where(lo, r(ob, 3), r(oa, 3)),
    ], axis=1)


def _hgrn_gates(fb, lb, one_m_lb, tri_bf):
    sig = _sigmoid(fb)
    kin = one_m_lb * (1.0 - sig)
    f = lb + one_m_lb * sig
    cum = _split_dot_left(tri_bf, jnp.log(f))
    return kin, f, cum


def _hgrn_span_decay(cum):
    q = BLK // 4
    ends = [cum[(n + 1) * q - 1:(n + 1) * q, :] for n in range(4)]
    d = -ends[0]
    for n in range(1, 4):
        d = jnp.maximum(d, ends[n - 1] - ends[n])
    return d


def _hgrn_state_step(qb, kin, cum, ib, st_ref):
    q_dec = (qb * jnp.exp(cum)).astype(BF)
    last = cum[BLK - 1:BLK, :]
    k_dec = (kin * jnp.exp(last - cum)).astype(BF)
    v_bf = ib.astype(BF)
    outs = []
    for h in range(N_HEADS_B):
        sl = slice(h * LANES, (h + 1) * LANES)
        st = st_ref[h]
        outs.append(_dot_nt(q_dec[:, sl], st.astype(BF)))
        st_ref[h] = st * jnp.exp(last[:, sl]) + _dot_tn(v_bf[:, sl], k_dec[:, sl])
    return jnp.concatenate(outs, axis=1)


def _hgrn_apply(amats, ib):
    v_bf = ib.astype(BF)
    return jnp.concatenate(
        [_dot(amats[h].astype(BF), v_bf[:, h * LANES:(h + 1) * LANES])
         for h in range(N_HEADS_B)], axis=1)


def _hgrn_intra_fast(qb, kin, cum, ib):
    half = BLK // 2
    row = lax.broadcasted_iota(jnp.int32, (BLK, 1), 0)
    upper = row >= half
    piv = cum[half - 1:half, :]
    w_lvl = jnp.exp(jnp.concatenate([piv - cum[:half, :], cum[half:, :] - piv], axis=0))
    p_lvl = (jnp.where(upper, qb, kin) * w_lvl).astype(BF)
    mid = jnp.where(upper, cum[half + half // 2 - 1:half + half // 2, :],
                    cum[half // 2 - 1:half // 2, :])
    e_mid = cum - mid
    q_mid = (qb * jnp.exp(e_mid)).astype(BF)
    k_mid = (kin * jnp.exp(-e_mid)).astype(BF)

    ri = lax.broadcasted_iota(jnp.int32, (BLK, BLK), 0)
    ci = lax.broadcasted_iota(jnp.int32, (BLK, BLK), 1)
    same_half_causal = ((ri >= half) == (ci >= half)) & (ci <= ri)
    cross = (ri >= half) & (ci < half)
    amats = []
    for h in range(N_HEADS_B):
        sl = slice(h * LANES, (h + 1) * LANES)
        a_mid = _dot_nt(q_mid[:, sl], k_mid[:, sl])
        a_lvl = _dot_nt(p_lvl[:, sl], p_lvl[:, sl])
        amats.append(jnp.where(same_half_causal, a_mid, jnp.where(cross, a_lvl, 0.0)))
    return _hgrn_apply(amats, ib)


def _hgrn_intra_robust(qb, kin, f, cum, ib, lvl_mask_ref):
    row = lax.broadcasted_iota(jnp.int32, (BLK, 1), 0)
    lvl_ops = []
    for b in LEVELS:
        pieces = []
        for r0 in range(0, BLK, 2 * b):
            piv = cum[r0 + b - 1:r0 + b, :]
            pieces.append(piv - cum[r0:r0 + b, :])
            pieces.append(cum[r0 + b:r0 + 2 * b, :] - piv)
        w = jnp.exp(jnp.concatenate(pieces, axis=0))
        second = (row & b) != 0
        lvl_ops.append((jnp.where(second, qb, kin) * w).astype(BF))

    n8 = BLK // SUB
    q3 = qb.reshape(n8, SUB, WIDTH_B)
    k3 = kin.reshape(n8, SUB, WIDTH_B)
    f3 = f.reshape(n8, SUB, WIDTH_B)
    v3 = ib.reshape(n8, SUB, WIDTH_B)
    subl = lax.broadcasted_iota(jnp.int32, (n8, SUB, 1), 1)

    def head(x, h):
        return x[..., h * LANES:(h + 1) * LANES]

    g = q3 * k3
    acc = [jnp.sum(head(g, h), axis=-1, keepdims=True) * head(v3, h) for h in range(N_HEADS_B)]
    dec = jnp.ones_like(f3)
    kd = k3
    vd = v3
    for d in range(1, SUB):
        dec = f3 * pltpu.roll(dec, 1, 1)
        kd = pltpu.roll(kd, 1, 1)
        vd = pltpu.roll(vd, 1, 1)
        g = q3 * kd * dec
        ok = subl >= d
        for h in range(N_HEADS_B):
            a = jnp.where(ok, jnp.sum(head(g, h), axis=-1, keepdims=True), 0.0)
            acc[h] = acc[h] + a * head(vd, h)

    amats = []
    for h in range(N_HEADS_B):
        sl = slice(h * LANES, (h + 1) * LANES)
        amat = jnp.zeros((BLK, BLK), F32)
        for li in range(len(LEVELS)):
            p = lvl_ops[li][:, sl]
            amat = amat + lvl_mask_ref[li] * _dot_nt(p, p)
        amats.append(amat)
    diag = jnp.concatenate([acc[h].reshape(BLK, LANES) for h in range(N_HEADS_B)], axis=1)
    return _hgrn_apply(amats, ib) + diag


def _split_dot_left(a_bf, b_f32):
    hi = b_f32.astype(BF)
    lo = (b_f32 - hi.astype(F32)).astype(BF)
    return _dot(a_bf, hi) + _dot(a_bf, lo)


def _branch_b_out(o, zb, og):
    outs = []
    for h in range(N_HEADS_B):
        sl = slice(h * LANES, (h + 1) * LANES)
        oh = o[:, sl]
        ms = jnp.mean(oh * oh, axis=-1, keepdims=True)
        outs.append(oh * lax.rsqrt(ms + EPS))
    return jnp.concatenate(outs, axis=1) * og * _silu(zb)


def _prompt_kernel(sinks_ref,
                   x0_ref, x1_ref, x2_ref, shift_ref, scale_ref, gate_ref, ng_ref,
                   w_in_ref, w_mg_ref, b_mg_ref, w_pa_ref, w_pb_ref, w_out_ref,
                   qg_ref, kg_ref, og_ref, lbl_ref,
                   cb_ref, sb_ref, cl_ref, sl_ref, sgn_ref,
                   seg_ref, tri_ref, bias_ref, lmask_ref,
                   y_ref, kwin_ref, vwin_ref, state_ref,
                   st_ref, kbuf_ref, ksw_ref, vbuf_ref, vsw_ref,
                   p0_ref, p1_ref, g_ref, h0_ref, h1_ref, ob_ref, obase_ref,
                   *, tile, n_steps):
    s = pl.program_id(0)
    nblk = tile // BLK

    def stage_a(x_ref, h_ref, p_ref):
        def prep():
            x = x_ref[...]
            ms = jnp.mean(x * x, axis=-1, keepdims=True)
            h = x * lax.rsqrt(ms + EPS) * (ng_ref[...] * (1.0 + scale_ref[...])) + shift_ref[...]
            h_ref[...] = h.astype(BF)

        def proj_chunk(c):
            def run():
                cs = slice(c * MXU_N, (c + 1) * MXU_N)
                p_ref[:, cs] = _dot(h_ref[...], w_in_ref[:, cs])
            return run

        return [prep] + [proj_chunk(c) for c in range(D_IN // MXU_N)]

    def gate_chunks(h_ref):
        def gate_chunk(c):
            def run():
                cs = slice(c * MXU_N, (c + 1) * MXU_N)
                g_ref[:, cs] = _sigmoid(_dot(h_ref[...], w_mg_ref[:, cs]) + b_mg_ref[:, cs])
            return run

        return [gate_chunk(c) for c in range(2 * D_MODEL // MXU_N)]

    def phase(h_cur_ref, a_next, b_parts):
        gc = gate_chunks(h_cur_ref)
        interleave(gc[:2] + a_next[:1] + gc[2:] + a_next[1:], b_parts)

    def stage_b(x_ref, p_ref, t_idx, y_rows):
        seg = seg_ref[...]
        rc, ru, rd = _rope_tables(cb_ref[pl.ds(t_idx, 1), :], sb_ref[pl.ds(t_idx, 1), :],
                                  cl_ref[...], sl_ref[...], sgn_ref[0:1, :], sgn_ref[1:2, :])
        qa = p_ref[:, OFF_QA:OFF_QA + WIDTH_A]
        qa = qa * _head_norm_scale(qa, seg) * (qg_ref[...] * (HEAD_DIM_A ** -0.5))
        qa = jnp.concatenate(
            [_rope(qa[:, c * LANES:(c + 1) * LANES], rc, ru, rd) for c in range(4)], axis=1)
        ka = p_ref[:, OFF_KA:OFF_KA + KV_WIDTH]
        ka = ka * _head_norm_scale(ka, seg[:KV_WIDTH, :KV_WIDTH]) * kg_ref[...]
        ka = _rope(ka, rc, ru, rd)
        va = p_ref[:, OFF_VA:OFF_VA + KV_WIDTH]
        ka_sw = pltpu.roll(ka, HEAD_DIM_A, 1)
        va_sw = pltpu.roll(va, HEAD_DIM_A, 1)

        lb = _lower_bound(lbl_ref[...])
        one_m_lb = 1.0 - lb
        tri = tri_ref[...]

        rows4 = lax.broadcasted_iota(jnp.int32, (4 * BLK, 1), 0) // BLK

        def sink_col(heads):
            col = jnp.zeros((4 * BLK, 1), F32)
            for n, hd in enumerate(heads):
                col = jnp.where(rows4 == n, sinks_ref[hd], col)
            return col
        sink_a = sink_col((0, 2, 5, 7))
        sink_b = sink_col((1, 3, 4, 6))
        kwin_ref[...] = ka[tile - WINDOW:, :]
        vwin_ref[...] = va[tile - WINDOW:, :]
        yield

        blocks = [slice(blk * BLK, (blk + 1) * BLK) for blk in range(nblk)]
        gates = [_hgrn_gates(p_ref[rs, OFF_FB:OFF_FB + WIDTH_B], lb, one_m_lb, tri)
                 for rs in blocks]
        span = _hgrn_span_decay(gates[0][2])
        for g in gates[1:]:
            span = jnp.maximum(span, _hgrn_span_decay(g[2]))
        mild = jnp.max(span) < HGRN_FAST_MAX
        yield

        ya_parts = []
        for blk, rs in enumerate(blocks):
            cur = slice(BLK, 2 * BLK)
            kbuf_ref[cur, :] = ka[rs].astype(BF)
            ksw_ref[cur, :] = ka_sw[rs].astype(BF)
            vbuf_ref[cur, :] = va[rs].astype(BF)
            vsw_ref[cur, :] = va_sw[rs].astype(BF)
            bias = bias_ref[jnp.where(t_idx == 0, 0, 1)] if blk == 0 else bias_ref[1]
            ya_parts.append(_attn_block(qa[rs], kbuf_ref[...], ksw_ref[...], vbuf_ref[...],
                                        vsw_ref[...], bias, sink_a, sink_b))
            for r in (kbuf_ref, ksw_ref, vbuf_ref, vsw_ref):
                r[0:BLK, :] = r[cur, :]
            yield

        for rs, (kin, _, cum) in zip(blocks, gates):
            qb, ib = p_ref[rs, OFF_QB:OFF_QB + WIDTH_B], p_ref[rs, OFF_IB:OFF_IB + WIDTH_B]
            base = _hgrn_state_step(qb, kin, cum, ib, st_ref)
            obase_ref[rs, :] = base
            ob_ref[rs, :] = base + _hgrn_intra_fast(qb, kin, cum, ib)
            yield

        @pl.when(jnp.logical_not(mild))
        def _():
            for rs, (kin, f, cum) in zip(blocks, gates):
                qb, ib = p_ref[rs, OFF_QB:OFF_QB + WIDTH_B], p_ref[rs, OFF_IB:OFF_IB + WIDTH_B]
                ob_ref[rs, :] = obase_ref[rs, :] + _hgrn_intra_robust(
                    qb, kin, f, cum, ib, lmask_ref)
        yield

        ya = jnp.concatenate(ya_parts, axis=0) * _silu(p_ref[:, OFF_ZA:OFF_ZA + WIDTH_A])
        yb = _branch_b_out(ob_ref[...], p_ref[:, OFF_ZB:OFF_ZB + WIDTH_B], og_ref[...])
        ya_bf, yb_bf = ya.astype(BF), yb.astype(BF)
        yield

        y = (g_ref[:, :D_MODEL] * _dot(ya_bf, w_pa_ref[...])
             + g_ref[:, D_MODEL:] * _dot(yb_bf, w_pb_ref[...]))
        y_ref[y_rows, :] = x_ref[...] + gate_ref[...] * _dot(y.astype(BF), w_out_ref[...])
        yield

    def interleave(a_thunks, b_parts):
        weights = ([W_PREP, W_GATES] + [W_ATTN] * nblk + [W_RECUR] * nblk
                   + [0.0, W_MERGE, W_OUT])
        total, acc, done = sum(weights), 0.0, 0
        for wgt in weights:
            acc += wgt
            upto = int(round(len(a_thunks) * acc / total))
            for th in a_thunks[done:upto]:
                th()
            done = upto
            next(b_parts)
        assert done == len(a_thunks) and next(b_parts, "end") == "end"

    @pl.when(s == 0)
    def _():
        st_ref[...] = jnp.zeros_like(st_ref)
        for r in (kbuf_ref, ksw_ref, vbuf_ref, vsw_ref):
            r[...] = jnp.zeros_like(r)
        for th in stage_a(x0_ref, h0_ref, p0_ref):
            th()

    phase(h0_ref, stage_a(x1_ref, h1_ref, p1_ref),
          stage_b(x0_ref, p0_ref, 2 * s, slice(0, tile)))
    phase(h1_ref, stage_a(x2_ref, h0_ref, p0_ref),
          stage_b(x1_ref, p1_ref, 2 * s + 1, slice(tile, 2 * tile)))

    @pl.when(s == n_steps - 1)
    def _():
        for hd in range(N_HEADS_B):
            state_ref[hd] = st_ref[hd].T


def _const_spec(shape):
    nd = len(shape)
    return pl.BlockSpec(shape, lambda i, *_: (0,) * nd, pipeline_mode=pl.Buffered(1))


def _prompt_call(x, shift, scale, gate, consts, w, tile):
    t = x.shape[0]
    n_tiles = t // tile
    n_steps = n_tiles // 2
    assert n_steps * 2 * tile == t
    row = lambda n: _const_spec((1, n))
    in_specs = [
        pl.BlockSpec((tile, D_MODEL), lambda i, *_: (2 * i, 0)),
        pl.BlockSpec((tile, D_MODEL), lambda i, *_: (2 * i + 1, 0)),
        pl.BlockSpec((tile, D_MODEL), lambda i, *_: (jnp.minimum(2 * i + 2, n_tiles - 1), 0)),
        row(D_MODEL), row(D_MODEL), row(D_MODEL), row(D_MODEL),
        _const_spec((D_MODEL, D_IN)), _const_spec((D_MODEL, 2 * D_MODEL)), row(2 * D_MODEL),
        _const_spec((WIDTH_A, D_MODEL)), _const_spec((WIDTH_B, D_MODEL)),
        _const_spec((D_MODEL, D_MODEL)),
        row(WIDTH_A), row(KV_WIDTH), row(WIDTH_B), _const_spec((2, WIDTH_B)),
        _const_spec((n_tiles, LANES)), _const_spec((n_tiles, LANES)),
        _const_spec((tile, LANES)), _const_spec((tile, LANES)), _const_spec((2, LANES)),
        _const_spec((WIDTH_A, WIDTH_A)), _const_spec((BLK, BLK)),
        _const_spec((2, 4 * BLK, 2 * BLK)), _const_spec((len(LEVELS), BLK, BLK)),
    ]
    out_specs = [
        pl.BlockSpec((2 * tile, D_MODEL), lambda i, *_: (i, 0)),
        pl.BlockSpec((WINDOW, KV_WIDTH), lambda i, *_: (0, 0)),
        pl.BlockSpec((WINDOW, KV_WIDTH), lambda i, *_: (0, 0)),
        pl.BlockSpec((N_HEADS_B, HEAD_DIM_B, HEAD_DIM_B), lambda i, *_: (0, 0, 0)),
    ]
    out_shape = [
        jax.ShapeDtypeStruct((t, D_MODEL), F32),
        jax.ShapeDtypeStruct((WINDOW, KV_WIDTH), F32),
        jax.ShapeDtypeStruct((WINDOW, KV_WIDTH), F32),
        jax.ShapeDtypeStruct((N_HEADS_B, HEAD_DIM_B, HEAD_DIM_B), F32),
    ]
    scratch = [
        pltpu.VMEM((N_HEADS_B, HEAD_DIM_B, HEAD_DIM_B), F32),
        pltpu.VMEM((2 * BLK, KV_WIDTH), BF), pltpu.VMEM((2 * BLK, KV_WIDTH), BF),
        pltpu.VMEM((2 * BLK, KV_WIDTH), BF), pltpu.VMEM((2 * BLK, KV_WIDTH), BF),
        pltpu.VMEM((tile, D_IN), F32), pltpu.VMEM((tile, D_IN), F32),
        pltpu.VMEM((tile, 2 * D_MODEL), F32),
        pltpu.VMEM((tile, D_MODEL), BF), pltpu.VMEM((tile, D_MODEL), BF),
        pltpu.VMEM((tile, WIDTH_B), F32), pltpu.VMEM((tile, WIDTH_B), F32),
    ]
    return pl.pallas_call(
        functools.partial(_prompt_kernel, tile=tile, n_steps=n_steps),
        grid_spec=pltpu.PrefetchScalarGridSpec(
            num_scalar_prefetch=1, grid=(n_steps,),
            in_specs=in_specs, out_specs=out_specs, scratch_shapes=scratch),
        out_shape=out_shape,
        compiler_params=pltpu.CompilerParams(
            dimension_semantics=("arbitrary",), vmem_limit_bytes=VMEM_LIMIT),
        name="prompt_layer",
    )(w["sinks"], x, x, x, shift, scale, gate, w["norm_g"],
      w["w_in"], w["w_merge"], w["b_merge"], w["w_proj_a"], w["w_proj_b"], w["w_out"],
      w["q_g"], w["k_g"], w["o_g"], w["lb_logits"],
      consts["rope_cb"], consts["rope_sb"], consts["rope_cl"], consts["rope_sl"],
      consts["rope_sgn"],
      consts["seg"], consts["tri"], consts["bias"], consts["lmask"])


def _dec_in_kernel(x_ref, shift_ref, scale_ref, ng_ref, w_in_ref, w_mg_ref, b_mg_ref,
                   qg_ref, kg_ref, lbl_ref, rc_ref, ru_ref, rd_ref, seg_ref, perm_ref,
                   qm_ref, kn_ref, vn_ref, za_ref, qb_ref, kin_ref, ib_ref, zb_ref, g_ref):
    x = x_ref[...]
    ms = jnp.mean(x * x, axis=-1, keepdims=True)
    h = x * lax.rsqrt(ms + EPS) * (ng_ref[...] * (1.0 + scale_ref[...])) + shift_ref[...]
    h_bf = h.astype(BF)
    proj = _dot(h_bf, w_in_ref[...])
    g_ref[...] = _sigmoid(_dot(h_bf, w_mg_ref[...]) + b_mg_ref[...])

    seg = seg_ref[...]
    rc, ru, rd = rc_ref[...], ru_ref[...], rd_ref[...]
    qa = proj[:, OFF_QA:OFF_QA + WIDTH_A]
    qa = qa * _head_norm_scale(qa, seg) * (qg_ref[...] * (HEAD_DIM_A ** -0.5))
    qa = jnp.concatenate(
        [_rope(qa[:, c * LANES:(c + 1) * LANES], rc, ru, rd) for c in range(4)], axis=1)
    qm_ref[...] = _dot(qa.astype(BF), perm_ref[...])
    ka = proj[:, OFF_KA:OFF_KA + KV_WIDTH]
    ka = ka * _head_norm_scale(ka, seg[:KV_WIDTH, :KV_WIDTH]) * kg_ref[...]
    kn_ref[...] = _rope(ka, rc, ru, rd)
    vn_ref[...] = proj[:, OFF_VA:OFF_VA + KV_WIDTH]
    za_ref[...] = proj[:, OFF_ZA:OFF_ZA + WIDTH_A]
    qb_ref[...] = proj[:, OFF_QB:OFF_QB + WIDTH_B]
    lb = _lower_bound(lbl_ref[...])
    kin_ref[...] = (1.0 - lb) * (1.0 - _sigmoid(proj[:, OFF_FB:OFF_FB + WIDTH_B]))
    ib_ref[...] = proj[:, OFF_IB:OFF_IB + WIDTH_B]
    zb_ref[...] = proj[:, OFF_ZB:OFF_ZB + WIDTH_B]


def _dec_in_call(x, shift, scale, consts, w):
    b = x.shape[0]
    f = lambda n: jax.ShapeDtypeStruct((b, n), F32)
    return pl.pallas_call(
        _dec_in_kernel,
        out_shape=[f(N_HEADS_A * LANES), f(KV_WIDTH), f(KV_WIDTH), f(WIDTH_A),
                   f(WIDTH_B), f(WIDTH_B), f(WIDTH_B), f(WIDTH_B), f(2 * D_MODEL)],
        compiler_params=pltpu.CompilerParams(vmem_limit_bytes=VMEM_LIMIT),
        name="decode_in",
    )(x, shift, scale, w["norm_g"], w["w_in"], w["w_merge"], w["b_merge"],
      w["q_g"], w["k_g"], w["lb_logits"],
      consts["rope_c1"], consts["rope_u1"], consts["rope_d1"], consts["seg"], consts["perm"])


def _dec_mix_kernel(qm_ref, kn_ref, vn_ref, qb_ref, kin_ref, ib_ref, sink_ref,
                    ck_ref, cv_ref, st_ref,
                    att_ref, ob_ref, nk_ref, nv_ref, nst_ref, *, bt):
    nh = N_HEADS_A
    s = jnp.concatenate([_dot(qm_ref[j].astype(BF), ck_ref[j].astype(BF))
                         for j in range(bt)], axis=0)
    key = lax.broadcasted_iota(jnp.int32, (bt * nh, WINDOW), 1)
    s = jnp.where(key == 0, NEG, s)
    rep = lambda r: jnp.broadcast_to(r[...][:, None, :], (bt, nh, r.shape[-1])).reshape(
        bt * nh, r.shape[-1])
    kn_rows, vn_rows = rep(kn_ref), rep(vn_ref)
    sink = jnp.broadcast_to(sink_ref[...][None], (bt, nh, 1)).reshape(bt * nh, 1)
    s_new = jnp.sum(qm_ref[...].reshape(bt * nh, LANES) * kn_rows, axis=-1, keepdims=True)
    m = jnp.maximum(jnp.maximum(jnp.max(s, axis=-1, keepdims=True), s_new), sink)
    p = jnp.exp(s - m)
    p_new = jnp.exp(s_new - m)
    den = jnp.sum(p, axis=-1, keepdims=True) + p_new + jnp.exp(sink - m)
    p_bf = p.astype(BF)
    pv = jnp.concatenate([_dot_nt(p_bf[j * nh:(j + 1) * nh, :], cv_ref[j].astype(BF))
                          for j in range(bt)], axis=0)
    att_ref[...] = ((pv + p_new * vn_rows) * (1.0 / den)).reshape(bt, nh, LANES)
    kn_t, vn_t = kn_ref[...].T, vn_ref[...].T
    newest = lax.broadcasted_iota(jnp.int32, (KV_WIDTH, WINDOW), 1) == WINDOW - 1
    for j in range(bt):
        nk_ref[j] = jnp.where(newest, kn_t[:, j:j + 1], pltpu.roll(ck_ref[j], WINDOW - 1, 1))
        nv_ref[j] = jnp.where(newest, vn_t[:, j:j + 1], pltpu.roll(cv_ref[j], WINDOW - 1, 1))

    for hd in range(N_HEADS_B):
        sl = slice(hd * LANES, (hd + 1) * LANES)
        kin_t = kin_ref[:, sl].T
        q_bf = qb_ref[:, sl].astype(BF)
        for j in range(bt):
            kcol = kin_t[:, j:j + 1]
            st = st_ref[j, hd]
            new = st - kcol * (st - ib_ref[j:j + 1, sl])
            nst_ref[j, hd] = new
            ob_ref[j:j + 1, sl] = _dot(q_bf, new.astype(BF))[j:j + 1, :]


def _dec_mix_call(qm, kn, vn, qb, kin, ib, sink_col, cache_k, cache_v, state, bt):
    b = kn.shape[0]
    rows = lambda n: pl.BlockSpec((bt, n), lambda i: (i, 0))
    cache_spec = pl.BlockSpec((bt, WINDOW, KV_WIDTH), lambda i: (i, 0, 0))
    st_spec = pl.BlockSpec((bt, N_HEADS_B, HEAD_DIM_B, HEAD_DIM_B), lambda i: (i, 0, 0, 0))
    qm_spec = pl.BlockSpec((bt, N_HEADS_A, LANES), lambda i: (i, 0, 0))
    return pl.pallas_call(
        functools.partial(_dec_mix_kernel, bt=bt),
        grid=(b // bt,),
        in_specs=[qm_spec, rows(KV_WIDTH), rows(KV_WIDTH), rows(WIDTH_B), rows(WIDTH_B),
                  rows(WIDTH_B), pl.BlockSpec((N_HEADS_A, 1), lambda i: (0, 0)),
                  cache_spec, cache_spec, st_spec],
        out_specs=[qm_spec, rows(WIDTH_B), cache_spec, cache_spec, st_spec],
        out_shape=[jax.ShapeDtypeStruct((b, N_HEADS_A, LANES), F32),
                   jax.ShapeDtypeStruct((b, WIDTH_B), F32),
                   jax.ShapeDtypeStruct(cache_k.shape, F32),
                   jax.ShapeDtypeStruct(cache_v.shape, F32),
                   jax.ShapeDtypeStruct(state.shape, F32)],
        compiler_params=pltpu.CompilerParams(
            dimension_semantics=("arbitrary",), vmem_limit_bytes=VMEM_LIMIT),
        name="decode_mix",
    )(qm, kn, vn, qb, kin, ib, sink_col, cache_k, cache_v, state)


def _dec_out_kernel(x_ref, gate_ref, att_ref, za_ref, ob_ref, zb_ref, g_ref, og_ref,
                    permt_ref, w_pa_ref, w_pb_ref, w_out_ref, y_ref):
    att = _split_dot(att_ref[...], permt_ref[...])
    ya = att * _silu(za_ref[...])
    yb = _branch_b_out(ob_ref[...], zb_ref[...], og_ref[...])
    g = g_ref[...]
    y = (g[:, :D_MODEL] * _dot(ya.astype(BF), w_pa_ref[...])
         + g[:, D_MODEL:] * _dot(yb.astype(BF), w_pb_ref[...]))
    y_ref[...] = x_ref[...] + gate_ref[...] * _dot(y.astype(BF), w_out_ref[...])


def _dec_out_call(x, gate, att, za, ob, zb, g, consts, w):
    return pl.pallas_call(
        _dec_out_kernel,
        out_shape=jax.ShapeDtypeStruct(x.shape, F32),
        compiler_params=pltpu.CompilerParams(vmem_limit_bytes=VMEM_LIMIT),
        name="decode_out",
    )(x, gate, att, za, ob, zb, g, w["o_g"], consts["permt"],
      w["w_proj_a"], w["w_proj_b"], w["w_out"])


def _rope_consts(n_tiles, tile, past_len):
    half = ROT_DIM // 2
    inv = ROPE_THETA ** (-np.arange(0, ROT_DIM, 2, dtype=np.float64) / ROT_DIM)
    e = np.arange(LANES) % HEAD_DIM_A
    rot = e < ROT_DIM

    def tables(pos):
        ang = np.asarray(pos, np.float64)[:, None] * inv[e % half][None, :]
        return np.where(rot, np.cos(ang), 1.0), np.where(rot, np.sin(ang), 0.0)

    sgn = np.stack([np.where(e < half, -1.0, 0.0),
                    np.where(rot & (e >= half), 1.0, 0.0)])
    cb, sb = tables(np.arange(n_tiles) * tile)
    cl, sl = tables(np.arange(tile))
    c1, s1 = tables([past_len])
    f = lambda a: jnp.asarray(a, F32)
    return {"rope_cb": f(cb), "rope_sb": f(sb), "rope_cl": f(cl), "rope_sl": f(sl),
            "rope_sgn": f(sgn), "rope_c1": f(c1), "rope_u1": f(s1 * sgn[0:1]),
            "rope_d1": f(s1 * sgn[1:2])}


def _static_consts():
    seg = np.kron(np.eye(N_HEADS_A), np.full((HEAD_DIM_A, HEAD_DIM_A), 1.0 / HEAD_DIM_A))
    tri = np.tril(np.ones((BLK, BLK)))
    r = np.arange(4 * BLK)[:, None] % BLK
    c = np.arange(2 * BLK)[None, :]
    ok_prev = (c < BLK) & (c > r)
    ok_cur = (c >= BLK) & (c - BLK <= r)
    bias = np.stack([np.where(ok_cur, 0.0, NEG), np.where(ok_prev | ok_cur, 0.0, NEG)])
    i = np.arange(BLK)[:, None]
    j = np.arange(BLK)[None, :]
    lmask = np.stack([((i // (2 * b)) == (j // (2 * b))) & ((i & b) != 0) & ((j & b) == 0)
                      for b in LEVELS]).astype(np.float32)
    perm = np.zeros((WIDTH_A, N_HEADS_A * LANES), np.float32)
    for hd in range(N_HEADS_A):
        kvh = hd // (N_HEADS_A // N_KV_A)
        for d in range(HEAD_DIM_A):
            perm[hd * HEAD_DIM_A + d, hd * LANES + kvh * HEAD_DIM_A + d] = 1.0
    return {
        "seg": jnp.asarray(seg, BF), "tri": jnp.asarray(tri, BF),
        "bias": jnp.asarray(bias, F32), "lmask": jnp.asarray(lmask, F32),
        "perm": jnp.asarray(perm, BF), "permt": jnp.asarray(perm.T, BF),
    }


def kernel(x_prompt, x_sample, cache_win_k, cache_win_v, state_hgrn, c_prompt, c_sample,
           w_ada, b_ada, norm_g, w_in, q_norm_g, k_norm_g, sinks, lb_logits, o_norm_g,
           w_merge, b_merge, w_proj_a, w_proj_b, w_out):
    depth = w_in.shape[0]
    assert depth == 1 and x_prompt.shape[0] == 1 and x_sample.shape[1] == 1
    t = x_prompt.shape[1]
    nb = x_sample.shape[0]
    past_len = t
    tile = PROMPT_TILE
    bt = DECODE_BATCH_TILE

    consts = _static_consts()
    consts.update(_rope_consts(t // tile, tile, past_len))

    w = {
        "sinks": sinks[0], "norm_g": norm_g[0][None, :],
        "w_in": w_in[0].astype(BF), "w_merge": w_merge[0].astype(BF),
        "b_merge": b_merge[0][None, :],
        "w_proj_a": w_proj_a[0].astype(BF), "w_proj_b": w_proj_b[0].astype(BF),
        "w_out": w_out[0].astype(BF),
        "q_g": jnp.tile(q_norm_g[0], N_HEADS_A)[None, :],
        "k_g": jnp.tile(k_norm_g[0], N_KV_A)[None, :],
        "o_g": jnp.tile(o_norm_g[0], N_HEADS_B)[None, :],
        "lb_logits": lb_logits,
    }

    mod_p, mod_s = _ada_call(c_prompt, c_sample, w_ada[0], b_ada)
    split3 = lambda m: (m[:, k * D_MODEL:(k + 1) * D_MODEL] for k in range(3))
    shift_p, scale_p, gate_p = split3(mod_p)
    shift_s, scale_s, gate_s = split3(mod_s)

    y_p, kwin, vwin, st_p = _prompt_call(
        x_prompt[0], shift_p, scale_p, gate_p, consts, w, tile)

    xs = x_sample[:, 0, :]
    qm, kn, vn, za, qb, kin, ib, zb, g = _dec_in_call(
        xs, shift_s, scale_s, consts, w)
    to_t = lambda c: jnp.transpose(c[0], (0, 2, 3, 1)).reshape(nb, KV_WIDTH, WINDOW)
    from_t = lambda c: jnp.transpose(
        c.reshape(nb, N_KV_A, HEAD_DIM_A, WINDOW), (0, 3, 1, 2))[None]
    att, ob, nk, nv, nst = _dec_mix_call(
        qm.reshape(nb, N_HEADS_A, LANES), kn, vn, qb, kin, ib, sinks[0][:, None],
        to_t(cache_win_k), to_t(cache_win_v), state_hgrn[0], bt)
    y_s = _dec_out_call(xs, gate_s, att.reshape(nb, N_HEADS_A * LANES), za, ob, zb, g,
                        consts, w)

    kv_shape = (1, 1, WINDOW, N_KV_A, HEAD_DIM_A)
    return (y_p[None], y_s[:, None, :],
            kwin.reshape(kv_shape), vwin.reshape(kv_shape), st_p[None, None],
            from_t(nk), from_t(nv), nst[None])
```

```python
import functools

import numpy as np
import jax
import jax.numpy as jnp
from jax import lax
from jax.experimental import pallas as pl
from jax.experimental.pallas import tpu as pltpu

D_MODEL = 1024
HEAD_DIM_A = 64
N_HEADS_A = 8
N_KV_A = 2
WIDTH_A = N_HEADS_A * HEAD_DIM_A
KV_WIDTH = N_KV_A * HEAD_DIM_A
WINDOW = 128
ROT_DIM = HEAD_DIM_A // 4
ROPE_THETA = 500000.0
HEAD_DIM_B = 128
N_HEADS_B = 4
WIDTH_B = N_HEADS_B * HEAD_DIM_B
EPS = 1e-6
OFF_QA = 0
OFF_KA = OFF_QA + WIDTH_A
OFF_VA = OFF_KA + KV_WIDTH
OFF_ZA = OFF_VA + KV_WIDTH
OFF_QB = OFF_ZA + WIDTH_A
OFF_FB = OFF_QB + WIDTH_B
OFF_IB = OFF_FB + WIDTH_B
OFF_ZB = OFF_IB + WIDTH_B
D_IN = OFF_ZB + WIDTH_B

LANES = 128
BLK = 128
SUB = 8
LEVELS = (64, 32, 16, 8)
HGRN_FAST_MAX = 80.0
MXU_N = 256
W_PREP, W_GATES, W_ATTN, W_RECUR, W_MERGE, W_OUT = 1.0, 0.8, 1.0, 0.5, 0.8, 0.0
NEG = -1e30
VMEM_LIMIT = 56 * 1024 * 1024
PROMPT_TILE = 256
DECODE_BATCH_TILE = 16

BF = jnp.bfloat16
F32 = jnp.float32


def _dot(a, b):
    return jnp.dot(a, b, preferred_element_type=F32)


def _dot_nt(a, b):
    return lax.dot_general(a, b, (((1,), (1,)), ((), ())), preferred_element_type=F32)


def _dot_tn(a, b):
    return lax.dot_general(a, b, (((0,), (0,)), ((), ())), preferred_element_type=F32)


def _split_dot(a_f32, b_bf):
    hi = a_f32.astype(BF)
    lo = (a_f32 - hi.astype(F32)).astype(BF)
    return _dot(hi, b_bf) + _dot(lo, b_bf)


def _sigmoid(x):
    return 1.0 / (1.0 + jnp.exp(-x))


def _silu(x):
    return x * _sigmoid(x)


def _lower_bound(lb_logits):
    l0 = lb_logits[0:1, :]
    l1 = lb_logits[1:2, :]
    m = jnp.maximum(l0, l1)
    e0 = jnp.exp(l0 - m)
    e1 = jnp.exp(l1 - m)
    return e0 / (e0 + e1)


def _rope_tables(cb, sb, cl, sl, sgn_up, sgn_dn):
    c = cb * cl - sb * sl
    s = sb * cl + cb * sl
    return c, s * sgn_up, s * sgn_dn


def _rope(x, c, s_up, s_dn):
    return x * c + pltpu.roll(x, LANES - ROT_DIM // 2, 1) * s_up + pltpu.roll(x, ROT_DIM // 2, 1) * s_dn


def _head_norm_scale(x, seg_mean_bf):
    ms = _dot((x * x).astype(BF), seg_mean_bf)
    return lax.rsqrt(ms + EPS)


def _ada_kernel(cp_ref, cs_ref, w_ref, b_ref, op_ref, os_ref):
    w = w_ref[...]
    w_hi = w.astype(BF)
    w_lo = (w - w_hi.astype(F32)).astype(BF)
    b = b_ref[...]

    def dot3(c):
        c_hi = c.astype(BF)
        c_lo = (c - c_hi.astype(F32)).astype(BF)
        return _dot(c_hi, w_hi) + (_dot(c_hi, w_lo) + _dot(c_lo, w_hi))

    op_ref[...] = dot3(cp_ref[...]) + b
    os_ref[...] = dot3(cs_ref[...]) + b


def _ada_call(c_p, c_s, w_ada, b_ada):
    mp, ms = c_p.shape[0], c_s.shape[0]
    n = w_ada.shape[1]
    tn = 512
    return pl.pallas_call(
        _ada_kernel,
        grid=(n // tn,),
        in_specs=[pl.BlockSpec((mp, D_MODEL), lambda j: (0, 0)),
                  pl.BlockSpec((ms, D_MODEL), lambda j: (0, 0)),
                  pl.BlockSpec((D_MODEL, tn), lambda j: (0, j)),
                  pl.BlockSpec((1, tn), lambda j: (0, j))],
        out_specs=[pl.BlockSpec((mp, tn), lambda j: (0, j)),
                   pl.BlockSpec((ms, tn), lambda j: (0, j))],
        out_shape=[jax.ShapeDtypeStruct((mp, n), F32), jax.ShapeDtypeStruct((ms, n), F32)],
        name="ada",
    )(c_p, c_s, w_ada, b_ada)


ATTN_PARTS = 5
RECUR_PARTS = 4


def _round_robin(*gens):
    results = [None] * len(gens)
    live = list(range(len(gens)))
    while live:
        for n in list(live):
            try:
                next(gens[n])
            except StopIteration as stop:
                results[n] = stop.value
                live.remove(n)
            yield
    return results


def _attn_block(q_blk, kcat, kcat_sw, vcat, vcat_sw, bias, sink_a, sink_b):
    lane = lax.broadcasted_iota(jnp.int32, (BLK, LANES), 1)
    lo = lane < HEAD_DIM_A
    chunks = [q_blk[:, c * LANES:(c + 1) * LANES] for c in range(4)]
    zero = jnp.zeros((BLK, LANES), F32)
    q_lo = [jnp.where(lo, c, zero).astype(BF) for c in chunks]
    q_hi = [jnp.where(lo, zero, c).astype(BF) for c in chunks]
    qa = jnp.concatenate([q_lo[0], q_lo[1], q_hi[2], q_hi[3]], axis=0)
    qb = jnp.concatenate([q_hi[0], q_hi[1], q_lo[2], q_lo[3]], axis=0)

    def probs(qs, kc, sink):
        s = _dot_nt(qs, kc) + bias
        m = jnp.maximum(jnp.max(s, axis=-1, keepdims=True), sink)
        p = jnp.exp(s - m)
        den = jnp.sum(p, axis=-1, keepdims=True) + jnp.exp(sink - m)
        return p.astype(BF), 1.0 / den

    pa, ra = probs(qa, kcat, sink_a)
    yield
    pb, rb = probs(qb, kcat_sw, sink_b)
    yield
    oa = _dot(pa, vcat) * ra
    yield
    ob = _dot(pb, vcat_sw) * rb
    yield
    r = lambda o, i: o[i * BLK:(i + 1) * BLK, :]
    return jnp.concatenate([
        jnp.where(lo, r(oa, 0), r(ob, 0)),
        jnp.where(lo, r(oa, 1), r(ob, 1)),
        jnp.where(lo, r(ob, 2), r(oa, 2)),
        jnp.where(lo, r(ob, 3), r(oa, 3)),
    ], axis=1)


def _hgrn_gates(fb, lb, one_m_lb, tri_bf):
    sig = _sigmoid(fb)
    kin = one_m_lb * (1.0 - sig)
    f = lb + one_m_lb * sig
    cum = _split_dot_left(tri_bf, jnp.log(f))
    return kin, f, cum


def _hgrn_span_decay(cum):
    q = BLK // 4
    ends = [cum[(n + 1) * q - 1:(n + 1) * q, :] for n in range(4)]
    d = -ends[0]
    for n in range(1, 4):
        d = jnp.maximum(d, ends[n - 1] - ends[n])
    return d


def _hgrn_state_step(qb, kin, cum, ib, st_ref):
    q_dec = (qb * jnp.exp(cum)).astype(BF)
    last = cum[BLK - 1:BLK, :]
    k_dec = (kin * jnp.exp(last - cum)).astype(BF)
    v_bf = ib.astype(BF)
    outs = []
    for h in range(N_HEADS_B):
        sl = slice(h * LANES, (h + 1) * LANES)
        st = st_ref[h]
        outs.append(_dot_nt(q_dec[:, sl], st.astype(BF)))
        st_ref[h] = st * jnp.exp(last[:, sl]) + _dot_tn(v_bf[:, sl], k_dec[:, sl])
    return jnp.concatenate(outs, axis=1)


def _hgrn_apply(amats, ib):
    v_bf = ib.astype(BF)
    return jnp.concatenate(
        [_dot(amats[h].astype(BF), v_bf[:, h * LANES:(h + 1) * LANES])
         for h in range(N_HEADS_B)], axis=1)


def _recur_block_fast(qb, kin, cum, ib, st_ref):
    base = _hgrn_state_step(qb, kin, cum, ib, st_ref)
    yield
    half = BLK // 2
    row = lax.broadcasted_iota(jnp.int32, (BLK, 1), 0)
    upper = row >= half
    piv = cum[half - 1:half, :]
    w_lvl = jnp.exp(jnp.concatenate([piv - cum[:half, :], cum[half:, :] - piv], axis=0))
    p_lvl = (jnp.where(upper, qb, kin) * w_lvl).astype(BF)
    mid = jnp.where(upper, cum[half + half // 2 - 1:half + half // 2, :],
                    cum[half // 2 - 1:half // 2, :])
    e_mid = cum - mid
    q_mid = (qb * jnp.exp(e_mid)).astype(BF)
    k_mid = (kin * jnp.exp(-e_mid)).astype(BF)
    yield

    ri = lax.broadcasted_iota(jnp.int32, (BLK, BLK), 0)
    ci = lax.broadcasted_iota(jnp.int32, (BLK, BLK), 1)
    same_half_causal = ((ri >= half) == (ci >= half)) & (ci <= ri)
    cross = (ri >= half) & (ci < half)
    amats = []
    for h in range(N_HEADS_B):
        sl = slice(h * LANES, (h + 1) * LANES)
        a_mid = _dot_nt(q_mid[:, sl], k_mid[:, sl])
        a_lvl = _dot_nt(p_lvl[:, sl], p_lvl[:, sl])
        amats.append(jnp.where(same_half_causal, a_mid, jnp.where(cross, a_lvl, 0.0)))
    yield
    return base, base + _hgrn_apply(amats, ib)


def _hgrn_intra_robust(qb, kin, f, cum, ib, lvl_mask_ref):
    row = lax.broadcasted_iota(jnp.int32, (BLK, 1), 0)
    lvl_ops = []
    for b in LEVELS:
        pieces = []
        for r0 in range(0, BLK, 2 * b):
            piv = cum[r0 + b - 1:r0 + b, :]
            pieces.append(piv - cum[r0:r0 + b, :])
            pieces.append(cum[r0 + b:r0 + 2 * b, :] - piv)
        w = jnp.exp(jnp.concatenate(pieces, axis=0))
        second = (row & b) != 0
        lvl_ops.append((jnp.where(second, qb, kin) * w).astype(BF))

    n8 = BLK // SUB
    q3 = qb.reshape(n8, SUB, WIDTH_B)
    k3 = kin.reshape(n8, SUB, WIDTH_B)
    f3 = f.reshape(n8, SUB, WIDTH_B)
    v3 = ib.reshape(n8, SUB, WIDTH_B)
    subl = lax.broadcasted_iota(jnp.int32, (n8, SUB, 1), 1)

    def head(x, h):
        return x[..., h * LANES:(h + 1) * LANES]

    g = q3 * k3
    acc = [jnp.sum(head(g, h), axis=-1, keepdims=True) * head(v3, h) for h in range(N_HEADS_B)]
    dec = jnp.ones_like(f3)
    kd = k3
    vd = v3
    for d in range(1, SUB):
        dec = f3 * pltpu.roll(dec, 1, 1)
        kd = pltpu.roll(kd, 1, 1)
        vd = pltpu.roll(vd, 1, 1)
        g = q3 * kd * dec
        ok = subl >= d
        for h in range(N_HEADS_B):
            a = jnp.where(ok, jnp.sum(head(g, h), axis=-1, keepdims=True), 0.0)
            acc[h] = acc[h] + a * head(vd, h)

    amats = []
    for h in range(N_HEADS_B):
        sl = slice(h * LANES, (h + 1) * LANES)
        amat = jnp.zeros((BLK, BLK), F32)
        for li in range(len(LEVELS)):
            p = lvl_ops[li][:, sl]
            amat = amat + lvl_mask_ref[li] * _dot_nt(p, p)
        amats.append(amat)
    diag = jnp.concatenate([acc[h].reshape(BLK, LANES) for h in range(N_HEADS_B)], axis=1)
    return _hgrn_apply(amats, ib) + diag


def _split_dot_left(a_bf, b_f32):
    hi = b_f32.astype(BF)
    lo = (b_f32 - hi.astype(F32)).astype(BF)
    return _dot(a_bf, hi) + _dot(a_bf, lo)


def _branch_b_out(o, zb, og):
    outs = []
    for h in range(N_HEADS_B):
        sl = slice(h * LANES, (h + 1) * LANES)
        oh = o[:, sl]
        ms = jnp.mean(oh * oh, axis=-1, keepdims=True)
        outs.append(oh * lax.rsqrt(ms + EPS))
    return jnp.concatenate(outs, axis=1) * og * _silu(zb)


def _prompt_kernel(sinks_ref,
                   x0_ref, x1_ref, x2_ref, shift_ref, scale_ref, gate_ref, ng_ref,
                   w_in_ref, w_mg_ref, b_mg_ref, w_pa_ref, w_pb_ref, w_out_ref,
                   qg_ref, kg_ref, og_ref, lbl_ref,
                   cb_ref, sb_ref, cl_ref, sl_ref, sgn_ref,
                   seg_ref, tri_ref, bias_ref, lmask_ref,
                   y_ref, kwin_ref, vwin_ref, state_ref,
                   st_ref, kprev_ref, kprev_sw_ref, vprev_ref, vprev_sw_ref,
                   p0_ref, p1_ref, g_ref, h0_ref, h1_ref, ob_ref, obase_ref,
                   *, tile, n_steps):
    s = pl.program_id(0)
    nblk = tile // BLK

    def stage_a(x_ref, h_ref, p_ref):
        def prep():
            x = x_ref[...]
            ms = jnp.mean(x * x, axis=-1, keepdims=True)
            h = x * lax.rsqrt(ms + EPS) * (ng_ref[...] * (1.0 + scale_ref[...])) + shift_ref[...]
            h_ref[...] = h.astype(BF)

        def proj_chunk(c):
            def run():
                cs = slice(c * MXU_N, (c + 1) * MXU_N)
                p_ref[:, cs] = _dot(h_ref[...], w_in_ref[:, cs])
            return run

        return [prep] + [proj_chunk(c) for c in range(D_IN // MXU_N)]

    def gate_chunks(h_ref):
        def gate_chunk(c):
            def run():
                cs = slice(c * MXU_N, (c + 1) * MXU_N)
                g_ref[:, cs] = _sigmoid(_dot(h_ref[...], w_mg_ref[:, cs]) + b_mg_ref[:, cs])
            return run

        return [gate_chunk(c) for c in range(2 * D_MODEL // MXU_N)]

    def phase(h_cur_ref, a_next, b_parts):
        gc = gate_chunks(h_cur_ref)
        interleave(gc[:2] + a_next[:1] + gc[2:] + a_next[1:], b_parts)

    def stage_b(x_ref, p_ref, t_idx, y_rows):
        seg = seg_ref[...]
        rc, ru, rd = _rope_tables(cb_ref[pl.ds(t_idx, 1), :], sb_ref[pl.ds(t_idx, 1), :],
                                  cl_ref[...], sl_ref[...], sgn_ref[0:1, :], sgn_ref[1:2, :])
        qa = p_ref[:, OFF_QA:OFF_QA + WIDTH_A]
        qa = qa * _head_norm_scale(qa, seg) * (qg_ref[...] * (HEAD_DIM_A ** -0.5))
        qa = jnp.concatenate(
            [_rope(qa[:, c * LANES:(c + 1) * LANES], rc, ru, rd) for c in range(4)], axis=1)
        ka = p_ref[:, OFF_KA:OFF_KA + KV_WIDTH]
        ka = ka * _head_norm_scale(ka, seg[:KV_WIDTH, :KV_WIDTH]) * kg_ref[...]
        ka = _rope(ka, rc, ru, rd)
        va = p_ref[:, OFF_VA:OFF_VA + KV_WIDTH]
        ka_sw = pltpu.roll(ka, HEAD_DIM_A, 1)
        va_sw = pltpu.roll(va, HEAD_DIM_A, 1)

        lb = _lower_bound(lbl_ref[...])
        one_m_lb = 1.0 - lb
        tri = tri_ref[...]

        rows4 = lax.broadcasted_iota(jnp.int32, (4 * BLK, 1), 0) // BLK

        def sink_col(heads):
            col = jnp.zeros((4 * BLK, 1), F32)
            for n, hd in enumerate(heads):
                col = jnp.where(rows4 == n, sinks_ref[hd], col)
            return col
        sink_a = sink_col((0, 2, 5, 7))
        sink_b = sink_col((1, 3, 4, 6))
        kwin_ref[...] = ka[tile - WINDOW:, :]
        vwin_ref[...] = va[tile - WINDOW:, :]
        yield

        blocks = [slice(blk * BLK, (blk + 1) * BLK) for blk in range(nblk)]
        gates = [_hgrn_gates(p_ref[rs, OFF_FB:OFF_FB + WIDTH_B], lb, one_m_lb, tri)
                 for rs in blocks]
        span = _hgrn_span_decay(gates[0][2])
        for g in gates[1:]:
            span = jnp.maximum(span, _hgrn_span_decay(g[2]))
        mild = jnp.max(span) < HGRN_FAST_MAX
        yield

        kv_refs = (kprev_ref, kprev_sw_ref, vprev_ref, vprev_sw_ref)
        kv_new = [a.astype(BF) for a in (ka, ka_sw, va, va_sw)]
        kv_old = [r[...] for r in kv_refs]
        for r, a in zip(kv_refs, kv_new):
            r[...] = a[blocks[-1]]

        def attend(blk, rs):
            cats = [jnp.concatenate([old if blk == 0 else new[blocks[blk - 1]], new[rs]], axis=0)
                    for old, new in zip(kv_old, kv_new)]
            bias = bias_ref[jnp.where(t_idx == 0, 0, 1)] if blk == 0 else bias_ref[1]
            return (yield from _attn_block(qa[rs], *cats, bias, sink_a, sink_b))

        def recur(blk, rs):
            kin, _, cum = gates[blk]
            qb, ib = p_ref[rs, OFF_QB:OFF_QB + WIDTH_B], p_ref[rs, OFF_IB:OFF_IB + WIDTH_B]
            base, full = yield from _recur_block_fast(qb, kin, cum, ib, st_ref)
            obase_ref[rs, :] = base
            ob_ref[rs, :] = full

        ya_parts = []
        for blk, rs in enumerate(blocks):
            res = yield from _round_robin(attend(blk, rs), recur(blk, rs))
            ya_parts.append(res[0])

        @pl.when(jnp.logical_not(mild))
        def _():
            for rs, (kin, f, cum) in zip(blocks, gates):
                qb, ib = p_ref[rs, OFF_QB:OFF_QB + WIDTH_B], p_ref[rs, OFF_IB:OFF_IB + WIDTH_B]
                ob_ref[rs, :] = obase_ref[rs, :] + _hgrn_intra_robust(
                    qb, kin, f, cum, ib, lmask_ref)
        yield

        ya = jnp.concatenate(ya_parts, axis=0) * _silu(p_ref[:, OFF_ZA:OFF_ZA + WIDTH_A])
        yb = _branch_b_out(ob_ref[...], p_ref[:, OFF_ZB:OFF_ZB + WIDTH_B], og_ref[...])
        ya_bf, yb_bf = ya.astype(BF), yb.astype(BF)
        yield

        y = (g_ref[:, :D_MODEL] * _dot(ya_bf, w_pa_ref[...])
             + g_ref[:, D_MODEL:] * _dot(yb_bf, w_pb_ref[...]))
        y_ref[y_rows, :] = x_ref[...] + gate_ref[...] * _dot(y.astype(BF), w_out_ref[...])
        yield

    def interleave(a_thunks, b_parts):
        mix_parts = ATTN_PARTS + RECUR_PARTS
        weights = ([W_PREP, W_GATES] + [(W_ATTN + W_RECUR) / mix_parts] * (mix_parts * nblk)
                   + [0.0, W_MERGE, W_OUT])
        total, acc, done = sum(weights), 0.0, 0
        for wgt in weights:
            acc += wgt
            upto = int(round(len(a_thunks) * acc / total))
            for th in a_thunks[done:upto]:
                th()
            done = upto
            next(b_parts)
        assert done == len(a_thunks) and next(b_parts, "end") == "end"

    @pl.when(s == 0)
    def _():
        st_ref[...] = jnp.zeros_like(st_ref)
        for r in (kprev_ref, kprev_sw_ref, vprev_ref, vprev_sw_ref):
            r[...] = jnp.zeros_like(r)
        for th in stage_a(x0_ref, h0_ref, p0_ref):
            th()

    phase(h0_ref, stage_a(x1_ref, h1_ref, p1_ref),
          stage_b(x0_ref, p0_ref, 2 * s, slice(0, tile)))
    phase(h1_ref, stage_a(x2_ref, h0_ref, p0_ref),
          stage_b(x1_ref, p1_ref, 2 * s + 1, slice(tile, 2 * tile)))

    @pl.when(s == n_steps - 1)
    def _():
        for hd in range(N_HEADS_B):
            state_ref[hd] = st_ref[hd].T


def _const_spec(shape):
    nd = len(shape)
    return pl.BlockSpec(shape, lambda i, *_: (0,) * nd, pipeline_mode=pl.Buffered(1))


def _prompt_call(x, shift, scale, gate, consts, w, tile):
    t = x.shape[0]
    n_tiles = t // tile
    n_steps = n_tiles // 2
    assert n_steps * 2 * tile == t
    row = lambda n: _const_spec((1, n))
    in_specs = [
        pl.BlockSpec((tile, D_MODEL), lambda i, *_: (2 * i, 0)),
        pl.BlockSpec((tile, D_MODEL), lambda i, *_: (2 * i + 1, 0)),
        pl.BlockSpec((tile, D_MODEL), lambda i, *_: (jnp.minimum(2 * i + 2, n_tiles - 1), 0)),
        row(D_MODEL), row(D_MODEL), row(D_MODEL), row(D_MODEL),
        _const_spec((D_MODEL, D_IN)), _const_spec((D_MODEL, 2 * D_MODEL)), row(2 * D_MODEL),
        _const_spec((WIDTH_A, D_MODEL)), _const_spec((WIDTH_B, D_MODEL)),
        _const_spec((D_MODEL, D_MODEL)),
        row(WIDTH_A), row(KV_WIDTH), row(WIDTH_B), _const_spec((2, WIDTH_B)),
        _const_spec((n_tiles, LANES)), _const_spec((n_tiles, LANES)),
        _const_spec((tile, LANES)), _const_spec((tile, LANES)), _const_spec((2, LANES)),
        _const_spec((WIDTH_A, WIDTH_A)), _const_spec((BLK, BLK)),
        _const_spec((2, 4 * BLK, 2 * BLK)), _const_spec((len(LEVELS), BLK, BLK)),
    ]
    out_specs = [
        pl.BlockSpec((2 * tile, D_MODEL), lambda i, *_: (i, 0)),
        pl.BlockSpec((WINDOW, KV_WIDTH), lambda i, *_: (0, 0)),
        pl.BlockSpec((WINDOW, KV_WIDTH), lambda i, *_: (0, 0)),
        pl.BlockSpec((N_HEADS_B, HEAD_DIM_B, HEAD_DIM_B), lambda i, *_: (0, 0, 0)),
    ]
    out_shape = [
        jax.ShapeDtypeStruct((t, D_MODEL), F32),
        jax.ShapeDtypeStruct((WINDOW, KV_WIDTH), F32),
        jax.ShapeDtypeStruct((WINDOW, KV_WIDTH), F32),
        jax.ShapeDtypeStruct((N_HEADS_B, HEAD_DIM_B, HEAD_DIM_B), F32),
    ]
    scratch = [
        pltpu.VMEM((N_HEADS_B, HEAD_DIM_B, HEAD_DIM_B), F32),
        pltpu.VMEM((BLK, KV_WIDTH), BF), pltpu.VMEM((BLK, KV_WIDTH), BF),
        pltpu.VMEM((BLK, KV_WIDTH), BF), pltpu.VMEM((BLK, KV_WIDTH), BF),
        pltpu.VMEM((tile, D_IN), F32), pltpu.VMEM((tile, D_IN), F32),
        pltpu.VMEM((tile, 2 * D_MODEL), F32),
        pltpu.VMEM((tile, D_MODEL), BF), pltpu.VMEM((tile, D_MODEL), BF),
        pltpu.VMEM((tile, WIDTH_B), F32), pltpu.VMEM((tile, WIDTH_B), F32),
    ]
    return pl.pallas_call(
        functools.partial(_prompt_kernel, tile=tile, n_steps=n_steps),
        grid_spec=pltpu.PrefetchScalarGridSpec(
            num_scalar_prefetch=1, grid=(n_steps,),
            in_specs=in_specs, out_specs=out_specs, scratch_shapes=scratch),
        out_shape=out_shape,
        compiler_params=pltpu.CompilerParams(
            dimension_semantics=("arbitrary",), vmem_limit_bytes=VMEM_LIMIT),
        name="prompt_layer",
    )(w["sinks"], x, x, x, shift, scale, gate, w["norm_g"],
      w["w_in"], w["w_merge"], w["b_merge"], w["w_proj_a"], w["w_proj_b"], w["w_out"],
      w["q_g"], w["k_g"], w["o_g"], w["lb_logits"],
      consts["rope_cb"], consts["rope_sb"], consts["rope_cl"], consts["rope_sl"],
      consts["rope_sgn"],
      consts["seg"], consts["tri"], consts["bias"], consts["lmask"])


def _dec_in_kernel(x_ref, shift_ref, scale_ref, ng_ref, w_in_ref, w_mg_ref, b_mg_ref,
                   qg_ref, kg_ref, lbl_ref, rc_ref, ru_ref, rd_ref, seg_ref, perm_ref,
                   qm_ref, kn_ref, vn_ref, za_ref, qb_ref, kin_ref, ib_ref, zb_ref, g_ref):
    x = x_ref[...]
    ms = jnp.mean(x * x, axis=-1, keepdims=True)
    h = x * lax.rsqrt(ms + EPS) * (ng_ref[...] * (1.0 + scale_ref[...])) + shift_ref[...]
    h_bf = h.astype(BF)
    proj = _dot(h_bf, w_in_ref[...])
    g_ref[...] = _sigmoid(_dot(h_bf, w_mg_ref[...]) + b_mg_ref[...])

    seg = seg_ref[...]
    rc, ru, rd = rc_ref[...], ru_ref[...], rd_ref[...]
    qa = proj[:, OFF_QA:OFF_QA + WIDTH_A]
    qa = qa * _head_norm_scale(qa, seg) * (qg_ref[...] * (HEAD_DIM_A ** -0.5))
    qa = jnp.concatenate(
        [_rope(qa[:, c * LANES:(c + 1) * LANES], rc, ru, rd) for c in range(4)], axis=1)
    qm_ref[...] = _dot(qa.astype(BF), perm_ref[...])
    ka = proj[:, OFF_KA:OFF_KA + KV_WIDTH]
    ka = ka * _head_norm_scale(ka, seg[:KV_WIDTH, :KV_WIDTH]) * kg_ref[...]
    kn_ref[...] = _rope(ka, rc, ru, rd)
    vn_ref[...] = proj[:, OFF_VA:OFF_VA + KV_WIDTH]
    za_ref[...] = proj[:, OFF_ZA:OFF_ZA + WIDTH_A]
    qb_ref[...] = proj[:, OFF_QB:OFF_QB + WIDTH_B]
    lb = _lower_bound(lbl_ref[...])
    kin_ref[...] = (1.0 - lb) * (1.0 - _sigmoid(proj[:, OFF_FB:OFF_FB + WIDTH_B]))
    ib_ref[...] = proj[:, OFF_IB:OFF_IB + WIDTH_B]
    zb_ref[...] = proj[:, OFF_ZB:OFF_ZB + WIDTH_B]


def _dec_in_call(x, shift, scale, consts, w):
    b = x.shape[0]
    f = lambda n: jax.ShapeDtypeStruct((b, n), F32)
    return pl.pallas_call(
        _dec_in_kernel,
        out_shape=[f(N_HEADS_A * LANES), f(KV_WIDTH), f(KV_WIDTH), f(WIDTH_A),
                   f(WIDTH_B), f(WIDTH_B), f(WIDTH_B), f(WIDTH_B), f(2 * D_MODEL)],
        compiler_params=pltpu.CompilerParams(vmem_limit_bytes=VMEM_LIMIT),
        name="decode_in",
    )(x, shift, scale, w["norm_g"], w["w_in"], w["w_merge"], w["b_merge"],
      w["q_g"], w["k_g"], w["lb_logits"],
      consts["rope_c1"], consts["rope_u1"], consts["rope_d1"], consts["seg"], consts["perm"])


def _dec_mix_kernel(qm_ref, kn_ref, vn_ref, qb_ref, kin_ref, ib_ref, sink_ref,
                    ck_ref, cv_ref, st_ref,
                    att_ref, ob_ref, nk_ref, nv_ref, nst_ref, *, bt):
    nh = N_HEADS_A
    s = jnp.concatenate([_dot(qm_ref[j].astype(BF), ck_ref[j].astype(BF))
                         for j in range(bt)], axis=0)
    key = lax.broadcasted_iota(jnp.int32, (bt * nh, WINDOW), 1)
    s = jnp.where(key == 0, NEG, s)
    rep = lambda r: jnp.broadcast_to(r[...][:, None, :], (bt, nh, r.shape[-1])).reshape(
        bt * nh, r.shape[-1])
    kn_rows, vn_rows = rep(kn_ref), rep(vn_ref)
    sink = jnp.broadcast_to(sink_ref[...][None], (bt, nh, 1)).reshape(bt * nh, 1)
    s_new = jnp.sum(qm_ref[...].reshape(bt * nh, LANES) * kn_rows, axis=-1, keepdims=True)
    m = jnp.maximum(jnp.maximum(jnp.max(s, axis=-1, keepdims=True), s_new), sink)
    p = jnp.exp(s - m)
    p_new = jnp.exp(s_new - m)
    den = jnp.sum(p, axis=-1, keepdims=True) + p_new + jnp.exp(sink - m)
    p_bf = p.astype(BF)
    pv = jnp.concatenate([_dot_nt(p_bf[j * nh:(j + 1) * nh, :], cv_ref[j].astype(BF))
                          for j in range(bt)], axis=0)
    att_ref[...] = ((pv + p_new * vn_rows) * (1.0 / den)).reshape(bt, nh, LANES)
    kn_t, vn_t = kn_ref[...].T, vn_ref[...].T
    newest = lax.broadcasted_iota(jnp.int32, (KV_WIDTH, WINDOW), 1) == WINDOW - 1
    for j in range(bt):
        nk_ref[j] = jnp.where(newest, kn_t[:, j:j + 1], pltpu.roll(ck_ref[j], WINDOW - 1, 1))
        nv_ref[j] = jnp.where(newest, vn_t[:, j:j + 1], pltpu.roll(cv_ref[j], WINDOW - 1, 1))

    for hd in range(N_HEADS_B):
        sl = slice(hd * LANES, (hd + 1) * LANES)
        kin_t = kin_ref[:, sl].T
        q_bf = qb_ref[:, sl].astype(BF)
        for j in range(bt):
            kcol = kin_t[:, j:j + 1]
            st = st_ref[j, hd]
            new = st - kcol * (st - ib_ref[j:j + 1, sl])
            nst_ref[j, hd] = new
            ob_ref[j:j + 1, sl] = _dot(q_bf, new.astype(BF))[j:j + 1, :]


def _dec_mix_call(qm, kn, vn, qb, kin, ib, sink_col, cache_k, cache_v, state, bt):
    b = kn.shape[0]
    rows = lambda n: pl.BlockSpec((bt, n), lambda i: (i, 0))
    cache_spec = pl.BlockSpec((bt, WINDOW, KV_WIDTH), lambda i: (i, 0, 0))
    st_spec = pl.BlockSpec((bt, N_HEADS_B, HEAD_DIM_B, HEAD_DIM_B), lambda i: (i, 0, 0, 0))
    qm_spec = pl.BlockSpec((bt, N_HEADS_A, LANES), lambda i: (i, 0, 0))
    return pl.pallas_call(
        functools.partial(_dec_mix_kernel, bt=bt),
        grid=(b // bt,),
        in_specs=[qm_spec, rows(KV_WIDTH), rows(KV_WIDTH), rows(WIDTH_B), rows(WIDTH_B),
                  rows(WIDTH_B), pl.BlockSpec((N_HEADS_A, 1), lambda i: (0, 0)),
                  cache_spec, cache_spec, st_spec],
        out_specs=[qm_spec, rows(WIDTH_B), cache_spec, cache_spec, st_spec],
        out_shape=[jax.ShapeDtypeStruct((b, N_HEADS_A, LANES), F32),
                   jax.ShapeDtypeStruct((b, WIDTH_B), F32),
                   jax.ShapeDtypeStruct(cache_k.shape, F32),
                   jax.ShapeDtypeStruct(cache_v.shape, F32),
                   jax.ShapeDtypeStruct(state.shape, F32)],
        compiler_params=pltpu.CompilerParams(
            dimension_semantics=("arbitrary",), vmem_limit_bytes=VMEM_LIMIT),
        name="decode_mix",
    )(qm, kn, vn, qb, kin, ib, sink_col, cache_k, cache_v, state)


def _dec_out_kernel(x_ref, gate_ref, att_ref, za_ref, ob_ref, zb_ref, g_ref, og_ref,
                    permt_ref, w_pa_ref, w_pb_ref, w_out_ref, y_ref):
    att = _split_dot(att_ref[...], permt_ref[...])
    ya = att * _silu(za_ref[...])
    yb = _branch_b_out(ob_ref[...], zb_ref[...], og_ref[...])
    g = g_ref[...]
    y = (g[:, :D_MODEL] * _dot(ya.astype(BF), w_pa_ref[...])
         + g[:, D_MODEL:] * _dot(yb.astype(BF), w_pb_ref[...]))
    y_ref[...] = x_ref[...] + gate_ref[...] * _dot(y.astype(BF), w_out_ref[...])


def _dec_out_call(x, gate, att, za, ob, zb, g, consts, w):
    return pl.pallas_call(
        _dec_out_kernel,
        out_shape=jax.ShapeDtypeStruct(x.shape, F32),
        compiler_params=pltpu.CompilerParams(vmem_limit_bytes=VMEM_LIMIT),
        name="decode_out",
    )(x, gate, att, za, ob, zb, g, w["o_g"], consts["permt"],
      w["w_proj_a"], w["w_proj_b"], w["w_out"])


def _rope_consts(n_tiles, tile, past_len):
    half = ROT_DIM // 2
    inv = ROPE_THETA ** (-np.arange(0, ROT_DIM, 2, dtype=np.float64) / ROT_DIM)
    e = np.arange(LANES) % HEAD_DIM_A
    rot = e < ROT_DIM

    def tables(pos):
        ang = np.asarray(pos, np.float64)[:, None] * inv[e % half][None, :]
        return np.where(rot, np.cos(ang), 1.0), np.where(rot, np.sin(ang), 0.0)

    sgn = np.stack([np.where(e < half, -1.0, 0.0),
                    np.where(rot & (e >= half), 1.0, 0.0)])
    cb, sb = tables(np.arange(n_tiles) * tile)
    cl, sl = tables(np.arange(tile))
    c1, s1 = tables([past_len])
    f = lambda a: jnp.asarray(a, F32)
    return {"rope_cb": f(cb), "rope_sb": f(sb), "rope_cl": f(cl), "rope_sl": f(sl),
            "rope_sgn": f(sgn), "rope_c1": f(c1), "rope_u1": f(s1 * sgn[0:1]),
            "rope_d1": f(s1 * sgn[1:2])}


def _static_consts():
    seg = np.kron(np.eye(N_HEADS_A), np.full((HEAD_DIM_A, HEAD_DIM_A), 1.0 / HEAD_DIM_A))
    tri = np.tril(np.ones((BLK, BLK)))
    r = np.arange(4 * BLK)[:, None] % BLK
    c = np.arange(2 * BLK)[None, :]
    ok_prev = (c < BLK) & (c > r)
    ok_cur = (c >= BLK) & (c - BLK <= r)
    bias = np.stack([np.where(ok_cur, 0.0, NEG), np.where(ok_prev | ok_cur, 0.0, NEG)])
    i = np.arange(BLK)[:, None]
    j = np.arange(BLK)[None, :]
    lmask = np.stack([((i // (2 * b)) == (j // (2 * b))) & ((i & b) != 0) & ((j & b) == 0)
                      for b in LEVELS]).astype(np.float32)
    perm = np.zeros((WIDTH_A, N_HEADS_A * LANES), np.float32)
    for hd in range(N_HEADS_A):
        kvh = hd // (N_HEADS_A // N_KV_A)
        for d in range(HEAD_DIM_A):
            perm[hd * HEAD_DIM_A + d, hd * LANES + kvh * HEAD_DIM_A + d] = 1.0
    return {
        "seg": jnp.asarray(seg, BF), "tri": jnp.asarray(tri, BF),
        "bias": jnp.asarray(bias, F32), "lmask": jnp.asarray(lmask, F32),
        "perm": jnp.asarray(perm, BF), "permt": jnp.asarray(perm.T, BF),
    }


def kernel(x_prompt, x_sample, cache_win_k, cache_win_v, state_hgrn, c_prompt, c_sample,
           w_ada, b_ada, norm_g, w_in, q_norm_g, k_norm_g, sinks, lb_logits, o_norm_g,
           w_merge, b_merge, w_proj_a, w_proj_b, w_out):
    depth = w_in.shape[0]
    assert depth == 1 and x_prompt.shape[0] == 1 and x_sample.shape[1] == 1
    t = x_prompt.shape[1]
    nb = x_sample.shape[0]
    past_len = t
    tile = PROMPT_TILE
    bt = DECODE_BATCH_TILE

    consts = _static_consts()
    consts.update(_rope_consts(t // tile, tile, past_len))

    w = {
        "sinks": sinks[0], "norm_g": norm_g[0][None, :],
        "w_in": w_in[0].astype(BF), "w_merge": w_merge[0].astype(BF),
        "b_merge": b_merge[0][None, :],
        "w_proj_a": w_proj_a[0].astype(BF), "w_proj_b": w_proj_b[0].astype(BF),
        "w_out": w_out[0].astype(BF),
        "q_g": jnp.tile(q_norm_g[0], N_HEADS_A)[None, :],
        "k_g": jnp.tile(k_norm_g[0], N_KV_A)[None, :],
        "o_g": jnp.tile(o_norm_g[0], N_HEADS_B)[None, :],
        "lb_logits": lb_logits,
    }

    mod_p, mod_s = _ada_call(c_prompt, c_sample, w_ada[0], b_ada)
    split3 = lambda m: (m[:, k * D_MODEL:(k + 1) * D_MODEL] for k in range(3))
    shift_p, scale_p, gate_p = split3(mod_p)
    shift_s, scale_s, gate_s = split3(mod_s)

    y_p, kwin, vwin, st_p = _prompt_call(
        x_prompt[0], shift_p, scale_p, gate_p, consts, w, tile)

    xs = x_sample[:, 0, :]
    qm, kn, vn, za, qb, kin, ib, zb, g = _dec_in_call(
        xs, shift_s, scale_s, consts, w)
    to_t = lambda c: jnp.transpose(c[0], (0, 2, 3, 1)).reshape(nb, KV_WIDTH, WINDOW)
    from_t = lambda c: jnp.transpose(
        c.reshape(nb, N_KV_A, HEAD_DIM_A, WINDOW), (0, 3, 1, 2))[None]
    att, ob, nk, nv, nst = _dec_mix_call(
        qm.reshape(nb, N_HEADS_A, LANES), kn, vn, qb, kin, ib, sinks[0][:, None],
        to_t(cache_win_k), to_t(cache_win_v), state_hgrn[0], bt)
    y_s = _dec_out_call(xs, gate_s, att.reshape(nb, N_HEADS_A * LANES), za, ob, zb, g,
                        consts, w)

    kv_shape = (1, 1, WINDOW, N_KV_A, HEAD_DIM_A)
    return (y_p[None], y_s[:, None, :],
            kwin.reshape(kv_shape), vwin.reshape(kv_shape), st_p[None, None],
            from_t(nk), from_t(nv), nst[None])
```

```python
import functools

import numpy as np
import jax
import jax.numpy as jnp
from jax import lax
from jax.experimental import pallas as pl
from jax.experimental.pallas import tpu as pltpu

D_MODEL = 1024
HEAD_DIM_A = 64
N_HEADS_A = 8
N_KV_A = 2
WIDTH_A = N_HEADS_A * HEAD_DIM_A
KV_WIDTH = N_KV_A * HEAD_DIM_A
WINDOW = 128
ROT_DIM = HEAD_DIM_A // 4
ROPE_THETA = 500000.0
HEAD_DIM_B = 128
N_HEADS_B = 4
WIDTH_B = N_HEADS_B * HEAD_DIM_B
EPS = 1e-6
OFF_QA = 0
OFF_KA = OFF_QA + WIDTH_A
OFF_VA = OFF_KA + KV_WIDTH
OFF_ZA = OFF_VA + KV_WIDTH
OFF_QB = OFF_ZA + WIDTH_A
OFF_FB = OFF_QB + WIDTH_B
OFF_IB = OFF_FB + WIDTH_B
OFF_ZB = OFF_IB + WIDTH_B
D_IN = OFF_ZB + WIDTH_B

LANES = 128
BLK = 128
SUB = 8
LEVELS = (64, 32, 16, 8)
HGRN_FAST_MAX = 80.0
MXU_N = 256
W_PREP, W_GATES, W_ATTN, W_RECUR, W_MERGE, W_OUT = 1.0, 0.8, 1.0, 0.5, 0.8, 0.0
NEG = -1e30
VMEM_LIMIT = 56 * 1024 * 1024
PROMPT_TILE = 256

BF = jnp.bfloat16
F32 = jnp.float32


def _dot(a, b):
    return jnp.dot(a, b, preferred_element_type=F32)


def _dot_nt(a, b):
    return lax.dot_general(a, b, (((1,), (1,)), ((), ())), preferred_element_type=F32)


def _dot_tn(a, b):
    return lax.dot_general(a, b, (((0,), (0,)), ((), ())), preferred_element_type=F32)


def _split_dot(a_f32, b_bf):
    hi = a_f32.astype(BF)
    lo = (a_f32 - hi.astype(F32)).astype(BF)
    return _dot(hi, b_bf) + _dot(lo, b_bf)


def _sigmoid(x):
    return 1.0 / (1.0 + jnp.exp(-x))


def _silu(x):
    return x * _sigmoid(x)


def _lower_bound(lb_logits):
    l0 = lb_logits[0:1, :]
    l1 = lb_logits[1:2, :]
    m = jnp.maximum(l0, l1)
    e0 = jnp.exp(l0 - m)
    e1 = jnp.exp(l1 - m)
    return e0 / (e0 + e1)


def _rope_tables(cb, sb, cl, sl, sgn_up, sgn_dn):
    c = cb * cl - sb * sl
    s = sb * cl + cb * sl
    return c, s * sgn_up, s * sgn_dn


def _rope(x, c, s_up, s_dn):
    return x * c + pltpu.roll(x, LANES - ROT_DIM // 2, 1) * s_up + pltpu.roll(x, ROT_DIM // 2, 1) * s_dn


def _head_norm_scale(x, seg_mean_bf):
    ms = _dot((x * x).astype(BF), seg_mean_bf)
    return lax.rsqrt(ms + EPS)


def _ada_kernel(cp_ref, cs_ref, w_ref, b_ref, op_ref, os_ref):
    w = w_ref[...]
    w_hi = w.astype(BF)
    w_lo = (w - w_hi.astype(F32)).astype(BF)
    b = b_ref[...]

    def dot3(c):
        c_hi = c.astype(BF)
        c_lo = (c - c_hi.astype(F32)).astype(BF)
        return _dot(c_hi, w_hi) + (_dot(c_hi, w_lo) + _dot(c_lo, w_hi))

    op_ref[...] = dot3(cp_ref[...]) + b
    os_ref[...] = dot3(cs_ref[...]) + b


def _ada_call(c_p, c_s, w_ada, b_ada):
    mp, ms = c_p.shape[0], c_s.shape[0]
    n = w_ada.shape[1]
    tn = 512
    return pl.pallas_call(
        _ada_kernel,
        grid=(n // tn,),
        in_specs=[pl.BlockSpec((mp, D_MODEL), lambda j: (0, 0)),
                  pl.BlockSpec((ms, D_MODEL), lambda j: (0, 0)),
                  pl.BlockSpec((D_MODEL, tn), lambda j: (0, j)),
                  pl.BlockSpec((1, tn), lambda j: (0, j))],
        out_specs=[pl.BlockSpec((mp, tn), lambda j: (0, j)),
                   pl.BlockSpec((ms, tn), lambda j: (0, j))],
        out_shape=[jax.ShapeDtypeStruct((mp, n), F32), jax.ShapeDtypeStruct((ms, n), F32)],
        name="ada",
    )(c_p, c_s, w_ada, b_ada)


ATTN_PARTS = 5
RECUR_PARTS = 4


def _round_robin(*gens):
    results = [None] * len(gens)
    live = list(range(len(gens)))
    while live:
        for n in list(live):
            try:
                next(gens[n])
            except StopIteration as stop:
                results[n] = stop.value
                live.remove(n)
            yield
    return results


def _attn_block(q_blk, kcat, kcat_sw, vcat, vcat_sw, bias, sink_a, sink_b):
    lane = lax.broadcasted_iota(jnp.int32, (BLK, LANES), 1)
    lo = lane < HEAD_DIM_A
    chunks = [q_blk[:, c * LANES:(c + 1) * LANES] for c in range(4)]
    zero = jnp.zeros((BLK, LANES), F32)
    q_lo = [jnp.where(lo, c, zero).astype(BF) for c in chunks]
    q_hi = [jnp.where(lo, zero, c).astype(BF) for c in chunks]
    qa = jnp.concatenate([q_lo[0], q_lo[1], q_hi[2], q_hi[3]], axis=0)
    qb = jnp.concatenate([q_hi[0], q_hi[1], q_lo[2], q_lo[3]], axis=0)

    def probs(qs, kc, sink):
        s = _dot_nt(qs, kc) + bias
        m = jnp.maximum(jnp.max(s, axis=-1, keepdims=True), sink)
        p = jnp.exp(s - m)
        den = jnp.sum(p, axis=-1, keepdims=True) + jnp.exp(sink - m)
        return p.astype(BF), 1.0 / den

    pa, ra = probs(qa, kcat, sink_a)
    yield
    pb, rb = probs(qb, kcat_sw, sink_b)
    yield
    oa = _dot(pa, vcat) * ra
    yield
    ob = _dot(pb, vcat_sw) * rb
    yield
    r = lambda o, i: o[i * BLK:(i + 1) * BLK, :]
    return jnp.concatenate([
        jnp.where(lo, r(oa, 0), r(ob, 0)),
        jnp.where(lo, r(oa, 1), r(ob, 1)),
        jnp.where(lo, r(ob, 2), r(oa, 2)),
        jnp.where(lo, r(ob, 3), r(oa, 3)),
    ], axis=1)


def _hgrn_gates(fb, lb, one_m_lb, tri_bf):
    sig = _sigmoid(fb)
    kin = one_m_lb * (1.0 - sig)
    f = lb + one_m_lb * sig
    cum = _split_dot_left(tri_bf, jnp.log(f))
    return kin, f, cum


def _hgrn_span_decay(cum):
    q = BLK // 4
    ends = [cum[(n + 1) * q - 1:(n + 1) * q, :] for n in range(4)]
    d = -ends[0]
    for n in range(1, 4):
        d = jnp.maximum(d, ends[n - 1] - ends[n])
    return d


def _hgrn_state_step(qb, kin, cum, ib, st_ref):
    q_dec = (qb * jnp.exp(cum)).astype(BF)
    last = cum[BLK - 1:BLK, :]
    k_dec = (kin * jnp.exp(last - cum)).astype(BF)
    v_bf = ib.astype(BF)
    outs = []
    for h in range(N_HEADS_B):
        sl = slice(h * LANES, (h + 1) * LANES)
        st = st_ref[h]
        outs.append(_dot_nt(q_dec[:, sl], st.astype(BF)))
        st_ref[h] = st * jnp.exp(last[:, sl]) + _dot_tn(v_bf[:, sl], k_dec[:, sl])
    return jnp.concatenate(outs, axis=1)


def _hgrn_apply(amats, ib):
    v_bf = ib.astype(BF)
    return jnp.concatenate(
        [_dot(amats[h].astype(BF), v_bf[:, h * LANES:(h + 1) * LANES])
         for h in range(N_HEADS_B)], axis=1)


def _recur_block_fast(qb, kin, cum, ib, st_ref):
    base = _hgrn_state_step(qb, kin, cum, ib, st_ref)
    yield
    half = BLK // 2
    row = lax.broadcasted_iota(jnp.int32, (BLK, 1), 0)
    upper = row >= half
    piv = cum[half - 1:half, :]
    w_lvl = jnp.exp(jnp.concatenate([piv - cum[:half, :], cum[half:, :] - piv], axis=0))
    p_lvl = (jnp.where(upper, qb, kin) * w_lvl).astype(BF)
    mid = jnp.where(upper, cum[half + half // 2 - 1:half + half // 2, :],
                    cum[half // 2 - 1:half // 2, :])
    e_mid = cum - mid
    q_mid = (qb * jnp.exp(e_mid)).astype(BF)
    k_mid = (kin * jnp.exp(-e_mid)).astype(BF)
    yield

    ri = lax.broadcasted_iota(jnp.int32, (BLK, BLK), 0)
    ci = lax.broadcasted_iota(jnp.int32, (BLK, BLK), 1)
    same_half_causal = ((ri >= half) == (ci >= half)) & (ci <= ri)
    cross = (ri >= half) & (ci < half)
    amats = []
    for h in range(N_HEADS_B):
        sl = slice(h * LANES, (h + 1) * LANES)
        a_mid = _dot_nt(q_mid[:, sl], k_mid[:, sl])
        a_lvl = _dot_nt(p_lvl[:, sl], p_lvl[:, sl])
        amats.append(jnp.where(same_half_causal, a_mid, jnp.where(cross, a_lvl, 0.0)))
    yield
    return base, base + _hgrn_apply(amats, ib)


def _hgrn_intra_robust(qb, kin, f, cum, ib, lvl_mask_ref):
    row = lax.broadcasted_iota(jnp.int32, (BLK, 1), 0)
    lvl_ops = []
    for b in LEVELS:
        pieces = []
        for r0 in range(0, BLK, 2 * b):
            piv = cum[r0 + b - 1:r0 + b, :]
            pieces.append(piv - cum[r0:r0 + b, :])
            pieces.append(cum[r0 + b:r0 + 2 * b, :] - piv)
        w = jnp.exp(jnp.concatenate(pieces, axis=0))
        second = (row & b) != 0
        lvl_ops.append((jnp.where(second, qb, kin) * w).astype(BF))

    n8 = BLK // SUB
    q3 = qb.reshape(n8, SUB, WIDTH_B)
    k3 = kin.reshape(n8, SUB, WIDTH_B)
    f3 = f.reshape(n8, SUB, WIDTH_B)
    v3 = ib.reshape(n8, SUB, WIDTH_B)
    subl = lax.broadcasted_iota(jnp.int32, (n8, SUB, 1), 1)

    def head(x, h):
        return x[..., h * LANES:(h + 1) * LANES]

    g = q3 * k3
    acc = [jnp.sum(head(g, h), axis=-1, keepdims=True) * head(v3, h) for h in range(N_HEADS_B)]
    dec = jnp.ones_like(f3)
    kd = k3
    vd = v3
    for d in range(1, SUB):
        dec = f3 * pltpu.roll(dec, 1, 1)
        kd = pltpu.roll(kd, 1, 1)
        vd = pltpu.roll(vd, 1, 1)
        g = q3 * kd * dec
        ok = subl >= d
        for h in range(N_HEADS_B):
            a = jnp.where(ok, jnp.sum(head(g, h), axis=-1, keepdims=True), 0.0)
            acc[h] = acc[h] + a * head(vd, h)

    amats = []
    for h in range(N_HEADS_B):
        sl = slice(h * LANES, (h + 1) * LANES)
        amat = jnp.zeros((BLK, BLK), F32)
        for li in range(len(LEVELS)):
            p = lvl_ops[li][:, sl]
            amat = amat + lvl_mask_ref[li] * _dot_nt(p, p)
        amats.append(amat)
    diag = jnp.concatenate([acc[h].reshape(BLK, LANES) for h in range(N_HEADS_B)], axis=1)
    return _hgrn_apply(amats, ib) + diag


def _split_dot_left(a_bf, b_f32):
    hi = b_f32.astype(BF)
    lo = (b_f32 - hi.astype(F32)).astype(BF)
    return _dot(a_bf, hi) + _dot(a_bf, lo)


def _branch_b_out(o, zb, og):
    outs = []
    for h in range(N_HEADS_B):
        sl = slice(h * LANES, (h + 1) * LANES)
        oh = o[:, sl]
        ms = jnp.mean(oh * oh, axis=-1, keepdims=True)
        outs.append(oh * lax.rsqrt(ms + EPS))
    return jnp.concatenate(outs, axis=1) * og * _silu(zb)


def _decode_mix_thunks(refs, js):
    (qm_ref, kn_ref, vn_ref, qb_ref, kin_ref, ib_ref, sink_ref, ck_ref, cv_ref, st_ref,
     att_ref, ob_ref, nk_ref, nv_ref, nst_ref) = refs
    nh = N_HEADS_A
    n = len(js)

    def pad_rows(a):
        return jnp.concatenate([a, jnp.zeros((SUB - a.shape[0], a.shape[1]), a.dtype)], axis=0)

    def attend():
        qm = jnp.concatenate([qm_ref[0, j] for j in js], axis=0)
        s = jnp.concatenate([_dot(qm_ref[0, j].astype(BF), ck_ref[j].astype(BF))
                             for j in js], axis=0)
        key = lax.broadcasted_iota(jnp.int32, (n * nh, WINDOW), 1)
        s = jnp.where(key == 0, NEG, s)
        rep = lambda r: jnp.concatenate(
            [jnp.broadcast_to(r[0, j:j + 1, :], (nh, r.shape[-1])) for j in js], axis=0)
        kn_rows, vn_rows = rep(kn_ref), rep(vn_ref)
        sink = jnp.concatenate([sink_ref[...]] * n, axis=0)
        s_new = jnp.sum(qm * kn_rows, axis=-1, keepdims=True)
        m = jnp.maximum(jnp.maximum(jnp.max(s, axis=-1, keepdims=True), s_new), sink)
        p = jnp.exp(s - m)
        p_new = jnp.exp(s_new - m)
        den = jnp.sum(p, axis=-1, keepdims=True) + p_new + jnp.exp(sink - m)
        p_bf = p.astype(BF)
        pv = jnp.concatenate([_dot_nt(p_bf[i * nh:(i + 1) * nh, :], cv_ref[j].astype(BF))
                              for i, j in enumerate(js)], axis=0)
        att = (pv + p_new * vn_rows) * (1.0 / den)
        for i, j in enumerate(js):
            att_ref[0, j] = att[i * nh:(i + 1) * nh, :]

    def window(new_ref, c_ref, out_ref):
        def run():
            new_t = pad_rows(new_ref[0]).T
            newest = lax.broadcasted_iota(jnp.int32, (KV_WIDTH, WINDOW), 1) == WINDOW - 1
            for j in js:
                out_ref[j] = jnp.where(newest, new_t[:, j:j + 1],
                                       pltpu.roll(c_ref[j], WINDOW - 1, 1))
        return run

    def recur(hd):
        def run():
            sl = slice(hd * LANES, (hd + 1) * LANES)
            kin_t = pad_rows(kin_ref[0, :, sl]).T
            q_bf = pad_rows(qb_ref[0, :, sl]).astype(BF)
            for j in js:
                st = st_ref[j, hd]
                new = st - kin_t[:, j:j + 1] * (st - ib_ref[0, j:j + 1, sl])
                nst_ref[j, hd] = new
                ob_ref[0, j:j + 1, sl] = _dot(q_bf, new.astype(BF))[j:j + 1, :]
        return run

    return ([attend, window(kn_ref, ck_ref, nk_ref), window(vn_ref, cv_ref, nv_ref)]
            + [recur(hd) for hd in range(N_HEADS_B)])


def _prompt_kernel(sinks_ref,
                   x0_ref, x1_ref, x2_ref, shift_ref, scale_ref, gate_ref, ng_ref,
                   w_in_ref, w_mg_ref, b_mg_ref, w_pa_ref, w_pb_ref, w_out_ref,
                   qg_ref, kg_ref, og_ref, lbl_ref,
                   cb_ref, sb_ref, cl_ref, sl_ref, sgn_ref,
                   seg_ref, tri_ref, bias_ref, lmask_ref,
                   dqm_ref, dkn_ref, dvn_ref, dqb_ref, dkin_ref, dib_ref, dsink_ref,
                   dck_ref, dcv_ref, dst_ref,
                   y_ref, kwin_ref, vwin_ref, state_ref,
                   datt_ref, dob_ref, dnk_ref, dnv_ref, dnst_ref,
                   st_ref, kprev_ref, kprev_sw_ref, vprev_ref, vprev_sw_ref,
                   p0_ref, p1_ref, g_ref, h0_ref, h1_ref, ob_ref, obase_ref,
                   *, tile, n_steps, dec_rows):
    s = pl.program_id(0)
    nblk = tile // BLK
    dec_refs = (dqm_ref, dkn_ref, dvn_ref, dqb_ref, dkin_ref, dib_ref, dsink_ref,
                dck_ref, dcv_ref, dst_ref, datt_ref, dob_ref, dnk_ref, dnv_ref, dnst_ref)

    def stage_a(x_ref, h_ref, p_ref):
        def prep():
            x = x_ref[...]
            ms = jnp.mean(x * x, axis=-1, keepdims=True)
            h = x * lax.rsqrt(ms + EPS) * (ng_ref[...] * (1.0 + scale_ref[...])) + shift_ref[...]
            h_ref[...] = h.astype(BF)

        def proj_chunk(c):
            def run():
                cs = slice(c * MXU_N, (c + 1) * MXU_N)
                p_ref[:, cs] = _dot(h_ref[...], w_in_ref[:, cs])
            return run

        return [prep] + [proj_chunk(c) for c in range(D_IN // MXU_N)]

    def gate_chunks(h_ref):
        def gate_chunk(c):
            def run():
                cs = slice(c * MXU_N, (c + 1) * MXU_N)
                g_ref[:, cs] = _sigmoid(_dot(h_ref[...], w_mg_ref[:, cs]) + b_mg_ref[:, cs])
            return run

        return [gate_chunk(c) for c in range(2 * D_MODEL // MXU_N)]

    def phase(h_cur_ref, a_next, b_parts, dec_rows):
        gc = gate_chunks(h_cur_ref)
        interleave(gc[:2] + a_next[:1] + gc[2:] + a_next[1:], b_parts,
                   _decode_mix_thunks(dec_refs, dec_rows))

    def stage_b(x_ref, p_ref, t_idx, y_rows):
        seg = seg_ref[...]
        rc, ru, rd = _rope_tables(cb_ref[pl.ds(t_idx, 1), :], sb_ref[pl.ds(t_idx, 1), :],
                                  cl_ref[...], sl_ref[...], sgn_ref[0:1, :], sgn_ref[1:2, :])
        qa = p_ref[:, OFF_QA:OFF_QA + WIDTH_A]
        qa = qa * _head_norm_scale(qa, seg) * (qg_ref[...] * (HEAD_DIM_A ** -0.5))
        qa = jnp.concatenate(
            [_rope(qa[:, c * LANES:(c + 1) * LANES], rc, ru, rd) for c in range(4)], axis=1)
        ka = p_ref[:, OFF_KA:OFF_KA + KV_WIDTH]
        ka = ka * _head_norm_scale(ka, seg[:KV_WIDTH, :KV_WIDTH]) * kg_ref[...]
        ka = _rope(ka, rc, ru, rd)
        va = p_ref[:, OFF_VA:OFF_VA + KV_WIDTH]
        ka_sw = pltpu.roll(ka, HEAD_DIM_A, 1)
        va_sw = pltpu.roll(va, HEAD_DIM_A, 1)

        lb = _lower_bound(lbl_ref[...])
        one_m_lb = 1.0 - lb
        tri = tri_ref[...]

        rows4 = lax.broadcasted_iota(jnp.int32, (4 * BLK, 1), 0) // BLK

        def sink_col(heads):
            col = jnp.zeros((4 * BLK, 1), F32)
            for n, hd in enumerate(heads):
                col = jnp.where(rows4 == n, sinks_ref[hd], col)
            return col
        sink_a = sink_col((0, 2, 5, 7))
        sink_b = sink_col((1, 3, 4, 6))
        kwin_ref[...] = ka[tile - WINDOW:, :]
        vwin_ref[...] = va[tile - WINDOW:, :]
        yield

        blocks = [slice(blk * BLK, (blk + 1) * BLK) for blk in range(nblk)]
        gates = [_hgrn_gates(p_ref[rs, OFF_FB:OFF_FB + WIDTH_B], lb, one_m_lb, tri)
                 for rs in blocks]
        span = _hgrn_span_decay(gates[0][2])
        for g in gates[1:]:
            span = jnp.maximum(span, _hgrn_span_decay(g[2]))
        mild = jnp.max(span) < HGRN_FAST_MAX
        yield

        kv_refs = (kprev_ref, kprev_sw_ref, vprev_ref, vprev_sw_ref)
        kv_new = [a.astype(BF) for a in (ka, ka_sw, va, va_sw)]
        kv_old = [r[...] for r in kv_refs]
        for r, a in zip(kv_refs, kv_new):
            r[...] = a[blocks[-1]]

        def attend(blk, rs):
            cats = [jnp.concatenate([old if blk == 0 else new[blocks[blk - 1]], new[rs]], axis=0)
                    for old, new in zip(kv_old, kv_new)]
            bias = bias_ref[jnp.where(t_idx == 0, 0, 1)] if blk == 0 else bias_ref[1]
            return (yield from _attn_block(qa[rs], *cats, bias, sink_a, sink_b))

        def recur(blk, rs):
            kin, _, cum = gates[blk]
            qb, ib = p_ref[rs, OFF_QB:OFF_QB + WIDTH_B], p_ref[rs, OFF_IB:OFF_IB + WIDTH_B]
            base, full = yield from _recur_block_fast(qb, kin, cum, ib, st_ref)
            obase_ref[rs, :] = base
            ob_ref[rs, :] = full

        ya_parts = []
        for blk, rs in enumerate(blocks):
            res = yield from _round_robin(attend(blk, rs), recur(blk, rs))
            ya_parts.append(res[0])

        @pl.when(jnp.logical_not(mild))
        def _():
            for rs, (kin, f, cum) in zip(blocks, gates):
                qb, ib = p_ref[rs, OFF_QB:OFF_QB + WIDTH_B], p_ref[rs, OFF_IB:OFF_IB + WIDTH_B]
                ob_ref[rs, :] = obase_ref[rs, :] + _hgrn_intra_robust(
                    qb, kin, f, cum, ib, lmask_ref)
        yield

        ya = jnp.concatenate(ya_parts, axis=0) * _silu(p_ref[:, OFF_ZA:OFF_ZA + WIDTH_A])
        yb = _branch_b_out(ob_ref[...], p_ref[:, OFF_ZB:OFF_ZB + WIDTH_B], og_ref[...])
        ya_bf, yb_bf = ya.astype(BF), yb.astype(BF)
        yield

        y = (g_ref[:, :D_MODEL] * _dot(ya_bf, w_pa_ref[...])
             + g_ref[:, D_MODEL:] * _dot(yb_bf, w_pb_ref[...]))
        y_ref[y_rows, :] = x_ref[...] + gate_ref[...] * _dot(y.astype(BF), w_out_ref[...])
        yield

    def interleave(a_thunks, b_parts, tail_thunks):
        mix_parts = ATTN_PARTS + RECUR_PARTS
        weights = ([W_PREP, W_GATES] + [(W_ATTN + W_RECUR) / mix_parts] * (mix_parts * nblk)
                   + [0.0, W_MERGE, W_OUT])
        total, acc, done = sum(weights), 0.0, 0
        for n, wgt in enumerate(weights):
            acc += wgt
            upto = int(round(len(a_thunks) * acc / total))
            for th in a_thunks[done:upto]:
                th()
            done = upto
            next(b_parts)
        for th in tail_thunks:
            th()
        assert done == len(a_thunks) and next(b_parts, "end") == "end"

    @pl.when(s == 0)
    def _():
        st_ref[...] = jnp.zeros_like(st_ref)
        for r in (kprev_ref, kprev_sw_ref, vprev_ref, vprev_sw_ref):
            r[...] = jnp.zeros_like(r)
        for th in stage_a(x0_ref, h0_ref, p0_ref):
            th()

    half = dec_rows // 2
    phase(h0_ref, stage_a(x1_ref, h1_ref, p1_ref),
          stage_b(x0_ref, p0_ref, 2 * s, slice(0, tile)), tuple(range(half)))
    phase(h1_ref, stage_a(x2_ref, h0_ref, p0_ref),
          stage_b(x1_ref, p1_ref, 2 * s + 1, slice(tile, 2 * tile)),
          tuple(range(half, dec_rows)))

    @pl.when(s == n_steps - 1)
    def _():
        for hd in range(N_HEADS_B):
            state_ref[hd] = st_ref[hd].T


def _const_spec(shape):
    nd = len(shape)
    return pl.BlockSpec(shape, lambda i, *_: (0,) * nd, pipeline_mode=pl.Buffered(1))


def _prompt_call(x, shift, scale, gate, consts, w, tile, dec):
    t = x.shape[0]
    n_tiles = t // tile
    n_steps = n_tiles // 2
    assert n_steps * 2 * tile == t
    qm, kn, vn, qb, kin, ib, sink_col, cache_k, cache_v, state = dec
    nb = kn.shape[0]
    dec_rows = nb // n_steps
    assert dec_rows * n_steps == nb and dec_rows % 2 == 0 and dec_rows <= SUB
    per_step = lambda a: a.reshape((n_steps, dec_rows) + a.shape[1:])
    step_spec = lambda a: pl.BlockSpec(
        (1,) + a.shape[1:], lambda i, *_: (i,) + (0,) * (a.ndim - 1))
    lead_spec = lambda a: pl.BlockSpec(
        (dec_rows,) + a.shape[1:], lambda i, *_: (i,) + (0,) * (a.ndim - 1))
    dec_small = [per_step(a) for a in (qm, kn, vn, qb, kin, ib)]
    row = lambda n: _const_spec((1, n))
    in_specs = [
        pl.BlockSpec((tile, D_MODEL), lambda i, *_: (2 * i, 0)),
        pl.BlockSpec((tile, D_MODEL), lambda i, *_: (2 * i + 1, 0)),
        pl.BlockSpec((tile, D_MODEL), lambda i, *_: (jnp.minimum(2 * i + 2, n_tiles - 1), 0)),
        row(D_MODEL), row(D_MODEL), row(D_MODEL), row(D_MODEL),
        _const_spec((D_MODEL, D_IN)), _const_spec((D_MODEL, 2 * D_MODEL)), row(2 * D_MODEL),
        _const_spec((WIDTH_A, D_MODEL)), _const_spec((WIDTH_B, D_MODEL)),
        _const_spec((D_MODEL, D_MODEL)),
        row(WIDTH_A), row(KV_WIDTH), row(WIDTH_B), _const_spec((2, WIDTH_B)),
        _const_spec((n_tiles, LANES)), _const_spec((n_tiles, LANES)),
        _const_spec((tile, LANES)), _const_spec((tile, LANES)), _const_spec((2, LANES)),
        _const_spec((WIDTH_A, WIDTH_A)), _const_spec((BLK, BLK)),
        _const_spec((2, 4 * BLK, 2 * BLK)), _const_spec((len(LEVELS), BLK, BLK)),
    ] + [step_spec(a) for a in dec_small] + [
        _const_spec(sink_col.shape), lead_spec(cache_k), lead_spec(cache_v), lead_spec(state),
    ]
    dec_outs = [dec_small[0], dec_small[3], cache_k, cache_v, state]
    out_specs = [
        pl.BlockSpec((2 * tile, D_MODEL), lambda i, *_: (i, 0)),
        pl.BlockSpec((WINDOW, KV_WIDTH), lambda i, *_: (0, 0)),
        pl.BlockSpec((WINDOW, KV_WIDTH), lambda i, *_: (0, 0)),
        pl.BlockSpec((N_HEADS_B, HEAD_DIM_B, HEAD_DIM_B), lambda i, *_: (0, 0, 0)),
        step_spec(dec_outs[0]), step_spec(dec_outs[1]),
        lead_spec(cache_k), lead_spec(cache_v), lead_spec(state),
    ]
    out_shape = [
        jax.ShapeDtypeStruct((t, D_MODEL), F32),
        jax.ShapeDtypeStruct((WINDOW, KV_WIDTH), F32),
        jax.ShapeDtypeStruct((WINDOW, KV_WIDTH), F32),
        jax.ShapeDtypeStruct((N_HEADS_B, HEAD_DIM_B, HEAD_DIM_B), F32),
    ] + [jax.ShapeDtypeStruct(a.shape, F32) for a in dec_outs]
    scratch = [
        pltpu.VMEM((N_HEADS_B, HEAD_DIM_B, HEAD_DIM_B), F32),
        pltpu.VMEM((BLK, KV_WIDTH), BF), pltpu.VMEM((BLK, KV_WIDTH), BF),
        pltpu.VMEM((BLK, KV_WIDTH), BF), pltpu.VMEM((BLK, KV_WIDTH), BF),
        pltpu.VMEM((tile, D_IN), F32), pltpu.VMEM((tile, D_IN), F32),
        pltpu.VMEM((tile, 2 * D_MODEL), F32),
        pltpu.VMEM((tile, D_MODEL), BF), pltpu.VMEM((tile, D_MODEL), BF),
        pltpu.VMEM((tile, WIDTH_B), F32), pltpu.VMEM((tile, WIDTH_B), F32),
    ]
    return pl.pallas_call(
        functools.partial(_prompt_kernel, tile=tile, n_steps=n_steps, dec_rows=dec_rows),
        grid_spec=pltpu.PrefetchScalarGridSpec(
            num_scalar_prefetch=1, grid=(n_steps,),
            in_specs=in_specs, out_specs=out_specs, scratch_shapes=scratch),
        out_shape=out_shape,
        compiler_params=pltpu.CompilerParams(
            dimension_semantics=("arbitrary",), vmem_limit_bytes=VMEM_LIMIT),
        name="prompt_layer",
    )(w["sinks"], x, x, x, shift, scale, gate, w["norm_g"],
      w["w_in"], w["w_merge"], w["b_merge"], w["w_proj_a"], w["w_proj_b"], w["w_out"],
      w["q_g"], w["k_g"], w["o_g"], w["lb_logits"],
      consts["rope_cb"], consts["rope_sb"], consts["rope_cl"], consts["rope_sl"],
      consts["rope_sgn"],
      consts["seg"], consts["tri"], consts["bias"], consts["lmask"],
      *dec_small, sink_col, cache_k, cache_v, state)


def _dec_in_kernel(x_ref, shift_ref, scale_ref, ng_ref, w_in_ref, w_mg_ref, b_mg_ref,
                   qg_ref, kg_ref, lbl_ref, rc_ref, ru_ref, rd_ref, seg_ref, perm_ref,
                   qm_ref, kn_ref, vn_ref, za_ref, qb_ref, kin_ref, ib_ref, zb_ref, g_ref):
    x = x_ref[...]
    ms = jnp.mean(x * x, axis=-1, keepdims=True)
    h = x * lax.rsqrt(ms + EPS) * (ng_ref[...] * (1.0 + scale_ref[...])) + shift_ref[...]
    h_bf = h.astype(BF)
    proj = _dot(h_bf, w_in_ref[...])
    g_ref[...] = _sigmoid(_dot(h_bf, w_mg_ref[...]) + b_mg_ref[...])

    seg = seg_ref[...]
    rc, ru, rd = rc_ref[...], ru_ref[...], rd_ref[...]
    qa = proj[:, OFF_QA:OFF_QA + WIDTH_A]
    qa = qa * _head_norm_scale(qa, seg) * (qg_ref[...] * (HEAD_DIM_A ** -0.5))
    qa = jnp.concatenate(
        [_rope(qa[:, c * LANES:(c + 1) * LANES], rc, ru, rd) for c in range(4)], axis=1)
    qm_ref[...] = _dot(qa.astype(BF), perm_ref[...])
    ka = proj[:, OFF_KA:OFF_KA + KV_WIDTH]
    ka = ka * _head_norm_scale(ka, seg[:KV_WIDTH, :KV_WIDTH]) * kg_ref[...]
    kn_ref[...] = _rope(ka, rc, ru, rd)
    vn_ref[...] = proj[:, OFF_VA:OFF_VA + KV_WIDTH]
    za_ref[...] = proj[:, OFF_ZA:OFF_ZA + WIDTH_A]
    qb_ref[...] = proj[:, OFF_QB:OFF_QB + WIDTH_B]
    lb = _lower_bound(lbl_ref[...])
    kin_ref[...] = (1.0 - lb) * (1.0 - _sigmoid(proj[:, OFF_FB:OFF_FB + WIDTH_B]))
    ib_ref[...] = proj[:, OFF_IB:OFF_IB + WIDTH_B]
    zb_ref[...] = proj[:, OFF_ZB:OFF_ZB + WIDTH_B]


def _dec_in_call(x, shift, scale, consts, w):
    b = x.shape[0]
    f = lambda n: jax.ShapeDtypeStruct((b, n), F32)
    return pl.pallas_call(
        _dec_in_kernel,
        out_shape=[f(N_HEADS_A * LANES), f(KV_WIDTH), f(KV_WIDTH), f(WIDTH_A),
                   f(WIDTH_B), f(WIDTH_B), f(WIDTH_B), f(WIDTH_B), f(2 * D_MODEL)],
        compiler_params=pltpu.CompilerParams(vmem_limit_bytes=VMEM_LIMIT),
        name="decode_in",
    )(x, shift, scale, w["norm_g"], w["w_in"], w["w_merge"], w["b_merge"],
      w["q_g"], w["k_g"], w["lb_logits"],
      consts["rope_c1"], consts["rope_u1"], consts["rope_d1"], consts["seg"], consts["perm"])


def _dec_out_kernel(x_ref, gate_ref, att_ref, za_ref, ob_ref, zb_ref, g_ref, og_ref,
                    permt_ref, w_pa_ref, w_pb_ref, w_out_ref, y_ref):
    att = _split_dot(att_ref[...], permt_ref[...])
    ya = att * _silu(za_ref[...])
    yb = _branch_b_out(ob_ref[...], zb_ref[...], og_ref[...])
    g = g_ref[...]
    y = (g[:, :D_MODEL] * _dot(ya.astype(BF), w_pa_ref[...])
         + g[:, D_MODEL:] * _dot(yb.astype(BF), w_pb_ref[...]))
    y_ref[...] = x_ref[...] + gate_ref[...] * _dot(y.astype(BF), w_out_ref[...])


def _dec_out_call(x, gate, att, za, ob, zb, g, consts, w):
    return pl.pallas_call(
        _dec_out_kernel,
        out_shape=jax.ShapeDtypeStruct(x.shape, F32),
        compiler_params=pltpu.CompilerParams(vmem_limit_bytes=VMEM_LIMIT),
        name="decode_out",
    )(x, gate, att, za, ob, zb, g, w["o_g"], consts["permt"],
      w["w_proj_a"], w["w_proj_b"], w["w_out"])


def _rope_consts(n_tiles, tile, past_len):
    half = ROT_DIM // 2
    inv = ROPE_THETA ** (-np.arange(0, ROT_DIM, 2, dtype=np.float64) / ROT_DIM)
    e = np.arange(LANES) % HEAD_DIM_A
    rot = e < ROT_DIM

    def tables(pos):
        ang = np.asarray(pos, np.float64)[:, None] * inv[e % half][None, :]
        return np.where(rot, np.cos(ang), 1.0), np.where(rot, np.sin(ang), 0.0)

    sgn = np.stack([np.where(e < half, -1.0, 0.0),
                    np.where(rot & (e >= half), 1.0, 0.0)])
    cb, sb = tables(np.arange(n_tiles) * tile)
    cl, sl = tables(np.arange(tile))
    c1, s1 = tables([past_len])
    f = lambda a: jnp.asarray(a, F32)
    return {"rope_cb": f(cb), "rope_sb": f(sb), "rope_cl": f(cl), "rope_sl": f(sl),
            "rope_sgn": f(sgn), "rope_c1": f(c1), "rope_u1": f(s1 * sgn[0:1]),
            "rope_d1": f(s1 * sgn[1:2])}


def _static_consts():
    seg = np.kron(np.eye(N_HEADS_A), np.full((HEAD_DIM_A, HEAD_DIM_A), 1.0 / HEAD_DIM_A))
    tri = np.tril(np.ones((BLK, BLK)))
    r = np.arange(4 * BLK)[:, None] % BLK
    c = np.arange(2 * BLK)[None, :]
    ok_prev = (c < BLK) & (c > r)
    ok_cur = (c >= BLK) & (c - BLK <= r)
    bias = np.stack([np.where(ok_cur, 0.0, NEG), np.where(ok_prev | ok_cur, 0.0, NEG)])
    i = np.arange(BLK)[:, None]
    j = np.arange(BLK)[None, :]
    lmask = np.stack([((i // (2 * b)) == (j // (2 * b))) & ((i & b) != 0) & ((j & b) == 0)
                      for b in LEVELS]).astype(np.float32)
    perm = np.zeros((WIDTH_A, N_HEADS_A * LANES), np.float32)
    for hd in range(N_HEADS_A):
        kvh = hd // (N_HEADS_A // N_KV_A)
        for d in range(HEAD_DIM_A):
            perm[hd * HEAD_DIM_A + d, hd * LANES + kvh * HEAD_DIM_A + d] = 1.0
    return {
        "seg": jnp.asarray(seg, BF), "tri": jnp.asarray(tri, BF),
        "bias": jnp.asarray(bias, F32), "lmask": jnp.asarray(lmask, F32),
        "perm": jnp.asarray(perm, BF), "permt": jnp.asarray(perm.T, BF),
    }


def kernel(x_prompt, x_sample, cache_win_k, cache_win_v, state_hgrn, c_prompt, c_sample,
           w_ada, b_ada, norm_g, w_in, q_norm_g, k_norm_g, sinks, lb_logits, o_norm_g,
           w_merge, b_merge, w_proj_a, w_proj_b, w_out):
    depth = w_in.shape[0]
    assert depth == 1 and x_prompt.shape[0] == 1 and x_sample.shape[1] == 1
    t = x_prompt.shape[1]
    nb = x_sample.shape[0]
    past_len = t
    tile = PROMPT_TILE

    consts = _static_consts()
    consts.update(_rope_consts(t // tile, tile, past_len))

    w = {
        "sinks": sinks[0], "norm_g": norm_g[0][None, :],
        "w_in": w_in[0].astype(BF), "w_merge": w_merge[0].astype(BF),
        "b_merge": b_merge[0][None, :],
        "w_proj_a": w_proj_a[0].astype(BF), "w_proj_b": w_proj_b[0].astype(BF),
        "w_out": w_out[0].astype(BF),
        "q_g": jnp.tile(q_norm_g[0], N_HEADS_A)[None, :],
        "k_g": jnp.tile(k_norm_g[0], N_KV_A)[None, :],
        "o_g": jnp.tile(o_norm_g[0], N_HEADS_B)[None, :],
        "lb_logits": lb_logits,
    }

    mod_p, mod_s = _ada_call(c_prompt, c_sample, w_ada[0], b_ada)
    split3 = lambda m: (m[:, k * D_MODEL:(k + 1) * D_MODEL] for k in range(3))
    shift_p, scale_p, gate_p = split3(mod_p)
    shift_s, scale_s, gate_s = split3(mod_s)

    xs = x_sample[:, 0, :]
    qm, kn, vn, za, qb, kin, ib, zb, g = _dec_in_call(
        xs, shift_s, scale_s, consts, w)
    to_t = lambda c: jnp.transpose(c[0], (0, 2, 3, 1)).reshape(nb, KV_WIDTH, WINDOW)
    from_t = lambda c: jnp.transpose(
        c.reshape(nb, N_KV_A, HEAD_DIM_A, WINDOW), (0, 3, 1, 2))[None]
    dec = (qm.reshape(nb, N_HEADS_A, LANES), kn, vn, qb, kin, ib, sinks[0][:, None],
           to_t(cache_win_k), to_t(cache_win_v), state_hgrn[0])
    y_p, kwin, vwin, st_p, att, ob, nk, nv, nst = _prompt_call(
        x_prompt[0], shift_p, scale_p, gate_p, consts, w, tile, dec)
    y_s = _dec_out_call(xs, gate_s, att.reshape(nb, N_HEADS_A * LANES), za,
                        ob.reshape(nb, WIDTH_B), zb, g, consts, w)

    kv_shape = (1, 1, WINDOW, N_KV_A, HEAD_DIM_A)
    return (y_p[None], y_s[:, None, :],
            kwin.reshape(kv_shape), vwin.reshape(kv_shape), st_p[None, None],
            from_t(nk), from_t(nv), nst[None])
```

```python
import functools

import numpy as np
import jax
import jax.numpy as jnp
from jax import lax
from jax.experimental import pallas as pl
from jax.experimental.pallas import tpu as pltpu

D_MODEL = 1024
HEAD_DIM_A = 64
N_HEADS_A = 8
N_KV_A = 2
WIDTH_A = N_HEADS_A * HEAD_DIM_A
KV_WIDTH = N_KV_A * HEAD_DIM_A
WINDOW = 128
ROT_DIM = HEAD_DIM_A // 4
ROPE_THETA = 500000.0
HEAD_DIM_B = 128
N_HEADS_B = 4
WIDTH_B = N_HEADS_B * HEAD_DIM_B
EPS = 1e-6
OFF_QA = 0
OFF_KA = OFF_QA + WIDTH_A
OFF_VA = OFF_KA + KV_WIDTH
OFF_ZA = OFF_VA + KV_WIDTH
OFF_QB = OFF_ZA + WIDTH_A
OFF_FB = OFF_QB + WIDTH_B
OFF_IB = OFF_FB + WIDTH_B
OFF_ZB = OFF_IB + WIDTH_B
D_IN = OFF_ZB + WIDTH_B

LANES = 128
BLK = 128
SUB = 8
LEVELS = (64, 32, 16, 8)
HGRN_FAST_MAX = 80.0
MXU_N = 256
W_PREP, W_GATES, W_ATTN, W_RECUR, W_MERGE, W_OUT = 1.0, 0.8, 1.0, 0.5, 0.8, 0.0
NEG = -1e30
VMEM_LIMIT = 56 * 1024 * 1024
PROMPT_TILE = 256
DECODE_BATCH_TILE = 16
G_Q, G_K, G_O = 0, WIDTH_A, WIDTH_A + KV_WIDTH
G_END = G_O + WIDTH_B

BF = jnp.bfloat16
F32 = jnp.float32


def _dot(a, b):
    return jnp.dot(a, b, preferred_element_type=F32)


def _dot_nt(a, b):
    return lax.dot_general(a, b, (((1,), (1,)), ((), ())), preferred_element_type=F32)


def _dot_tn(a, b):
    return lax.dot_general(a, b, (((0,), (0,)), ((), ())), preferred_element_type=F32)


def _split_dot(a_f32, b_bf):
    hi = a_f32.astype(BF)
    lo = (a_f32 - hi.astype(F32)).astype(BF)
    return _dot(hi, b_bf) + _dot(lo, b_bf)


def _sigmoid(x):
    return 1.0 / (1.0 + jnp.exp(-x))


def _silu(x):
    return x * _sigmoid(x)


def _lower_bound(lb_logits):
    l0 = lb_logits[0:1, :]
    l1 = lb_logits[1:2, :]
    m = jnp.maximum(l0, l1)
    e0 = jnp.exp(l0 - m)
    e1 = jnp.exp(l1 - m)
    return e0 / (e0 + e1)


def _rope_tables(cb, sb, cl, sl, sgn_up, sgn_dn):
    c = cb * cl - sb * sl
    s = sb * cl + cb * sl
    return c, s * sgn_up, s * sgn_dn


def _rope(x, c, s_up, s_dn):
    return x * c + pltpu.roll(x, LANES - ROT_DIM // 2, 1) * s_up + pltpu.roll(x, ROT_DIM // 2, 1) * s_dn


def _modulation(mod_ref):
    return (mod_ref[:, 0:D_MODEL], mod_ref[:, D_MODEL:2 * D_MODEL],
            mod_ref[:, 2 * D_MODEL:3 * D_MODEL])


def _norm_modulate(x, ng, mod_ref):
    shift, scale, _ = _modulation(mod_ref)
    ms = jnp.mean(x * x, axis=-1, keepdims=True)
    return (x * lax.rsqrt(ms + EPS) * (ng * (1.0 + scale)) + shift).astype(BF)


def _head_norm_scale(x, seg_mean_bf):
    ms = _dot((x * x).astype(BF), seg_mean_bf)
    return lax.rsqrt(ms + EPS)


def _ada_kernel(cp_ref, cs_ref, w_ref, b_ref, op_ref, os_ref):
    w = w_ref[...]
    w_hi = w.astype(BF)
    w_lo = (w - w_hi.astype(F32)).astype(BF)
    b = b_ref[...]

    def dot3(c):
        c_hi = c.astype(BF)
        c_lo = (c - c_hi.astype(F32)).astype(BF)
        return _dot(c_hi, w_hi) + (_dot(c_hi, w_lo) + _dot(c_lo, w_hi))

    op_ref[...] = dot3(cp_ref[...]) + b
    os_ref[...] = dot3(cs_ref[...]) + b


def _ada_call(c_p, c_s, w_ada, b_ada):
    mp, ms = c_p.shape[0], c_s.shape[0]
    n = w_ada.shape[1]
    tn = 512
    return pl.pallas_call(
        _ada_kernel,
        grid=(n // tn,),
        in_specs=[pl.BlockSpec((mp, D_MODEL), lambda j: (0, 0)),
                  pl.BlockSpec((ms, D_MODEL), lambda j: (0, 0)),
                  pl.BlockSpec((D_MODEL, tn), lambda j: (0, j)),
                  pl.BlockSpec((1, tn), lambda j: (0, j))],
        out_specs=[pl.BlockSpec((mp, tn), lambda j: (0, j)),
                   pl.BlockSpec((ms, tn), lambda j: (0, j))],
        out_shape=[jax.ShapeDtypeStruct((mp, n), F32), jax.ShapeDtypeStruct((ms, n), F32)],
        name="ada",
    )(c_p, c_s, w_ada, b_ada)


ATTN_PARTS = 5
RECUR_PARTS = 4


def _round_robin(*gens):
    results = [None] * len(gens)
    live = list(range(len(gens)))
    while live:
        for n in list(live):
            try:
                next(gens[n])
            except StopIteration as stop:
                results[n] = stop.value
                live.remove(n)
            yield
    return results


def _attn_block(q_blk, kcat, kcat_sw, vcat, vcat_sw, bias, sink_a, sink_b):
    lane = lax.broadcasted_iota(jnp.int32, (BLK, LANES), 1)
    lo = lane < HEAD_DIM_A
    chunks = [q_blk[:, c * LANES:(c + 1) * LANES] for c in range(4)]
    zero = jnp.zeros((BLK, LANES), F32)
    q_lo = [jnp.where(lo, c, zero).astype(BF) for c in chunks]
    q_hi = [jnp.where(lo, zero, c).astype(BF) for c in chunks]
    qa = jnp.concatenate([q_lo[0], q_lo[1], q_hi[2], q_hi[3]], axis=0)
    qb = jnp.concatenate([q_hi[0], q_hi[1], q_lo[2], q_lo[3]], axis=0)

    def probs(qs, kc, sink):
        s = _dot_nt(qs, kc) + bias
        m = jnp.maximum(jnp.max(s, axis=-1, keepdims=True), sink)
        p = jnp.exp(s - m)
        den = jnp.sum(p, axis=-1, keepdims=True) + jnp.exp(sink - m)
        return p.astype(BF), 1.0 / den

    pa, ra = probs(qa, kcat, sink_a)
    yield
    pb, rb = probs(qb, kcat_sw, sink_b)
    yield
    oa = _dot(pa, vcat) * ra
    yield
    ob = _dot(pb, vcat_sw) * rb
    yield
    r = lambda o, i: o[i * BLK:(i + 1) * BLK, :]
    return jnp.concatenate([
        jnp.where(lo, r(oa, 0), r(ob, 0)),
        jnp.where(lo, r(oa, 1), r(ob, 1)),
        jnp.where(lo, r(ob, 2), r(oa, 2)),
        jnp.where(lo, r(ob, 3), r(oa, 3)),
    ], axis=1)


def _hgrn_gates(fb, lb, one_m_lb, tri_bf):
    sig = _sigmoid(fb)
    kin = one_m_lb * (1.0 - sig)
    f = lb + one_m_lb * sig
    cum = _split_dot_left(tri_bf, jnp.log(f))
    return kin, f, cum


def _hgrn_span_decay(cum):
    q = BLK // 4
    ends = [cum[(n + 1) * q - 1:(n + 1) * q, :] for n in range(4)]
    d = -ends[0]
    for n in range(1, 4):
        d = jnp.maximum(d, ends[n - 1] - ends[n])
    return d


def _hgrn_state_step(qb, kin, cum, ib, st_ref):
    q_dec = (qb * jnp.exp(cum)).astype(BF)
    last = cum[BLK - 1:BLK, :]
    k_dec = (kin * jnp.exp(last - cum)).astype(BF)
    v_bf = ib.astype(BF)
    outs = []
    for h in range(N_HEADS_B):
        sl = slice(h * LANES, (h + 1) * LANES)
        st = st_ref[h]
        outs.append(_dot_nt(q_dec[:, sl], st.astype(BF)))
        st_ref[h] = st * jnp.exp(last[:, sl]) + _dot_tn(v_bf[:, sl], k_dec[:, sl])
    return jnp.concatenate(outs, axis=1)


def _hgrn_apply(amats, ib):
    v_bf = ib.astype(BF)
    return jnp.concatenate(
        [_dot(amats[h].astype(BF), v_bf[:, h * LANES:(h + 1) * LANES])
         for h in range(N_HEADS_B)], axis=1)


def _recur_block_fast(qb, kin, cum, ib, st_ref):
    base = _hgrn_state_step(qb, kin, cum, ib, st_ref)
    yield
    half = BLK // 2
    row = lax.broadcasted_iota(jnp.int32, (BLK, 1), 0)
    upper = row >= half
    piv = cum[half - 1:half, :]
    w_lvl = jnp.exp(jnp.concatenate([piv - cum[:half, :], cum[half:, :] - piv], axis=0))
    p_lvl = (jnp.where(upper, qb, kin) * w_lvl).astype(BF)
    mid = jnp.where(upper, cum[half + half // 2 - 1:half + half // 2, :],
                    cum[half // 2 - 1:half // 2, :])
    e_mid = cum - mid
    q_mid = (qb * jnp.exp(e_mid)).astype(BF)
    k_mid = (kin * jnp.exp(-e_mid)).astype(BF)
    yield

    ri = lax.broadcasted_iota(jnp.int32, (BLK, BLK), 0)
    ci = lax.broadcasted_iota(jnp.int32, (BLK, BLK), 1)
    same_half_causal = ((ri >= half) == (ci >= half)) & (ci <= ri)
    cross = (ri >= half) & (ci < half)
    amats = []
    for h in range(N_HEADS_B):
        sl = slice(h * LANES, (h + 1) * LANES)
        a_mid = _dot_nt(q_mid[:, sl], k_mid[:, sl])
        a_lvl = _dot_nt(p_lvl[:, sl], p_lvl[:, sl])
        amats.append(jnp.where(same_half_causal, a_mid, jnp.where(cross, a_lvl, 0.0)))
    yield
    return base, base + _hgrn_apply(amats, ib)


def _hgrn_intra_robust(qb, kin, f, cum, ib, lvl_mask_ref):
    row = lax.broadcasted_iota(jnp.int32, (BLK, 1), 0)
    lvl_ops = []
    for b in LEVELS:
        pieces = []
        for r0 in range(0, BLK, 2 * b):
            piv = cum[r0 + b - 1:r0 + b, :]
            pieces.append(piv - cum[r0:r0 + b, :])
            pieces.append(cum[r0 + b:r0 + 2 * b, :] - piv)
        w = jnp.exp(jnp.concatenate(pieces, axis=0))
        second = (row & b) != 0
        lvl_ops.append((jnp.where(second, qb, kin) * w).astype(BF))

    n8 = BLK // SUB
    q3 = qb.reshape(n8, SUB, WIDTH_B)
    k3 = kin.reshape(n8, SUB, WIDTH_B)
    f3 = f.reshape(n8, SUB, WIDTH_B)
    v3 = ib.reshape(n8, SUB, WIDTH_B)
    subl = lax.broadcasted_iota(jnp.int32, (n8, SUB, 1), 1)

    def head(x, h):
        return x[..., h * LANES:(h + 1) * LANES]

    g = q3 * k3
    acc = [jnp.sum(head(g, h), axis=-1, keepdims=True) * head(v3, h) for h in range(N_HEADS_B)]
    dec = jnp.ones_like(f3)
    kd = k3
    vd = v3
    for d in range(1, SUB):
        dec = f3 * pltpu.roll(dec, 1, 1)
        kd = pltpu.roll(kd, 1, 1)
        vd = pltpu.roll(vd, 1, 1)
        g = q3 * kd * dec
        ok = subl >= d
        for h in range(N_HEADS_B):
            a = jnp.where(ok, jnp.sum(head(g, h), axis=-1, keepdims=True), 0.0)
            acc[h] = acc[h] + a * head(vd, h)

    amats = []
    for h in range(N_HEADS_B):
        sl = slice(h * LANES, (h + 1) * LANES)
        amat = jnp.zeros((BLK, BLK), F32)
        for li in range(len(LEVELS)):
            p = lvl_ops[li][:, sl]
            amat = amat + lvl_mask_ref[li] * _dot_nt(p, p)
        amats.append(amat)
    diag = jnp.concatenate([acc[h].reshape(BLK, LANES) for h in range(N_HEADS_B)], axis=1)
    return _hgrn_apply(amats, ib) + diag


def _split_dot_left(a_bf, b_f32):
    hi = b_f32.astype(BF)
    lo = (b_f32 - hi.astype(F32)).astype(BF)
    return _dot(a_bf, hi) + _dot(a_bf, lo)


def _branch_b_out(o, zb, og):
    outs = []
    for h in range(N_HEADS_B):
        sl = slice(h * LANES, (h + 1) * LANES)
        oh = o[:, sl]
        ms = jnp.mean(oh * oh, axis=-1, keepdims=True)
        outs.append(oh * lax.rsqrt(ms + EPS))
    return jnp.concatenate(outs, axis=1) * og * _silu(zb)


def _prompt_kernel(sinks_ref,
                   x0_ref, x1_ref, x2_ref, mod_ref, ng_ref,
                   w_in_ref, w_mg_ref, b_mg_ref, w_pa_ref, w_pb_ref, w_out_ref,
                   gains_ref, lbl_ref,
                   cb_ref, sb_ref, cl_ref, sl_ref, sgn_ref,
                   seg_ref, tri_ref, bias_ref, lmask_ref,
                   y_ref, kwin_ref, vwin_ref, state_ref,
                   st_ref, kprev_ref, kprev_sw_ref, vprev_ref, vprev_sw_ref,
                   p0_ref, p1_ref, g_ref, h0_ref, h1_ref, ob_ref, obase_ref,
                   *, tile, n_steps):
    s = pl.program_id(0)
    nblk = tile // BLK

    def stage_a(x_ref, h_ref, p_ref):
        def prep():
            h_ref[...] = _norm_modulate(x_ref[...], ng_ref[...], mod_ref)

        def proj_chunk(c):
            def run():
                cs = slice(c * MXU_N, (c + 1) * MXU_N)
                p_ref[:, cs] = _dot(h_ref[...], w_in_ref[:, cs])
            return run

        return [prep] + [proj_chunk(c) for c in range(D_IN // MXU_N)]

    def gate_chunks(h_ref):
        def gate_chunk(c):
            def run():
                cs = slice(c * MXU_N, (c + 1) * MXU_N)
                g_ref[:, cs] = _sigmoid(_dot(h_ref[...], w_mg_ref[:, cs]) + b_mg_ref[:, cs])
            return run

        return [gate_chunk(c) for c in range(2 * D_MODEL // MXU_N)]

    def phase(h_cur_ref, a_next, b_parts):
        gc = gate_chunks(h_cur_ref)
        interleave(gc[:2] + a_next[:1] + gc[2:] + a_next[1:], b_parts)

    def stage_b(x_ref, p_ref, t_idx, y_rows):
        seg = seg_ref[...]
        rc, ru, rd = _rope_tables(cb_ref[pl.ds(t_idx, 1), :], sb_ref[pl.ds(t_idx, 1), :],
                                  cl_ref[...], sl_ref[...], sgn_ref[0:1, :], sgn_ref[1:2, :])
        qa = p_ref[:, OFF_QA:OFF_QA + WIDTH_A]
        qa = qa * _head_norm_scale(qa, seg) * (gains_ref[:, G_Q:G_K] * (HEAD_DIM_A ** -0.5))
        qa = jnp.concatenate(
            [_rope(qa[:, c * LANES:(c + 1) * LANES], rc, ru, rd) for c in range(4)], axis=1)
        ka = p_ref[:, OFF_KA:OFF_KA + KV_WIDTH]
        ka = ka * _head_norm_scale(ka, seg[:KV_WIDTH, :KV_WIDTH]) * gains_ref[:, G_K:G_O]
        ka = _rope(ka, rc, ru, rd)
        va = p_ref[:, OFF_VA:OFF_VA + KV_WIDTH]
        ka_sw = pltpu.roll(ka, HEAD_DIM_A, 1)
        va_sw = pltpu.roll(va, HEAD_DIM_A, 1)

        lb = _lower_bound(lbl_ref[...])
        one_m_lb = 1.0 - lb
        tri = tri_ref[...]

        rows4 = lax.broadcasted_iota(jnp.int32, (4 * BLK, 1), 0) // BLK

        def sink_col(heads):
            col = jnp.zeros((4 * BLK, 1), F32)
            for n, hd in enumerate(heads):
                col = jnp.where(rows4 == n, sinks_ref[hd], col)
            return col
        sink_a = sink_col((0, 2, 5, 7))
        sink_b = sink_col((1, 3, 4, 6))
        kwin_ref[...] = ka[tile - WINDOW:, :]
        vwin_ref[...] = va[tile - WINDOW:, :]
        yield

        blocks = [slice(blk * BLK, (blk + 1) * BLK) for blk in range(nblk)]
        gates = [_hgrn_gates(p_ref[rs, OFF_FB:OFF_FB + WIDTH_B], lb, one_m_lb, tri)
                 for rs in blocks]
        span = _hgrn_span_decay(gates[0][2])
        for g in gates[1:]:
            span = jnp.maximum(span, _hgrn_span_decay(g[2]))
        mild = jnp.max(span) < HGRN_FAST_MAX
        yield

        kv_refs = (kprev_ref, kprev_sw_ref, vprev_ref, vprev_sw_ref)
        kv_new = [a.astype(BF) for a in (ka, ka_sw, va, va_sw)]
        kv_old = [r[...] for r in kv_refs]
        for r, a in zip(kv_refs, kv_new):
            r[...] = a[blocks[-1]]

        def attend(blk, rs):
            cats = [jnp.concatenate([old if blk == 0 else new[blocks[blk - 1]], new[rs]], axis=0)
                    for old, new in zip(kv_old, kv_new)]
            bias = bias_ref[jnp.where(t_idx == 0, 0, 1)] if blk == 0 else bias_ref[1]
            return (yield from _attn_block(qa[rs], *cats, bias, sink_a, sink_b))

        def recur(blk, rs):
            kin, _, cum = gates[blk]
            qb, ib = p_ref[rs, OFF_QB:OFF_QB + WIDTH_B], p_ref[rs, OFF_IB:OFF_IB + WIDTH_B]
            base, full = yield from _recur_block_fast(qb, kin, cum, ib, st_ref)
            obase_ref[rs, :] = base
            ob_ref[rs, :] = full

        ya_parts = []
        for blk, rs in enumerate(blocks):
            res = yield from _round_robin(attend(blk, rs), recur(blk, rs))
            ya_parts.append(res[0])

        @pl.when(jnp.logical_not(mild))
        def _():
            for rs, (kin, f, cum) in zip(blocks, gates):
                qb, ib = p_ref[rs, OFF_QB:OFF_QB + WIDTH_B], p_ref[rs, OFF_IB:OFF_IB + WIDTH_B]
                ob_ref[rs, :] = obase_ref[rs, :] + _hgrn_intra_robust(
                    qb, kin, f, cum, ib, lmask_ref)
        yield

        ya = jnp.concatenate(ya_parts, axis=0) * _silu(p_ref[:, OFF_ZA:OFF_ZA + WIDTH_A])
        yb = _branch_b_out(ob_ref[...], p_ref[:, OFF_ZB:OFF_ZB + WIDTH_B],
                           gains_ref[:, G_O:G_END])
        ya_bf, yb_bf = ya.astype(BF), yb.astype(BF)
        yield

        y = (g_ref[:, :D_MODEL] * _dot(ya_bf, w_pa_ref[...])
             + g_ref[:, D_MODEL:] * _dot(yb_bf, w_pb_ref[...]))
        y_ref[y_rows, :] = x_ref[...] + _modulation(mod_ref)[2] * _dot(y.astype(BF), w_out_ref[...])
        yield

    def interleave(a_thunks, b_parts):
        mix_parts = ATTN_PARTS + RECUR_PARTS
        weights = ([W_PREP, W_GATES] + [(W_ATTN + W_RECUR) / mix_parts] * (mix_parts * nblk)
                   + [0.0, W_MERGE, W_OUT])
        total, acc, done = sum(weights), 0.0, 0
        for wgt in weights:
            acc += wgt
            upto = int(round(len(a_thunks) * acc / total))
            for th in a_thunks[done:upto]:
                th()
            done = upto
            next(b_parts)
        assert done == len(a_thunks) and next(b_parts, "end") == "end"

    @pl.when(s == 0)
    def _():
        st_ref[...] = jnp.zeros_like(st_ref)
        for r in (kprev_ref, kprev_sw_ref, vprev_ref, vprev_sw_ref):
            r[...] = jnp.zeros_like(r)
        for th in stage_a(x0_ref, h0_ref, p0_ref):
            th()

    phase(h0_ref, stage_a(x1_ref, h1_ref, p1_ref),
          stage_b(x0_ref, p0_ref, 2 * s, slice(0, tile)))
    phase(h1_ref, stage_a(x2_ref, h0_ref, p0_ref),
          stage_b(x1_ref, p1_ref, 2 * s + 1, slice(tile, 2 * tile)))

    @pl.when(s == n_steps - 1)
    def _():
        for hd in range(N_HEADS_B):
            state_ref[hd] = st_ref[hd].T


def _const_spec(shape):
    nd = len(shape)
    return pl.BlockSpec(shape, lambda i, *_: (0,) * nd, pipeline_mode=pl.Buffered(1))


def _prompt_call(x, mod, consts, w, tile):
    t = x.shape[0]
    n_tiles = t // tile
    n_steps = n_tiles // 2
    assert n_steps * 2 * tile == t
    row = lambda n: _const_spec((1, n))
    in_specs = [
        pl.BlockSpec((tile, D_MODEL), lambda i, *_: (2 * i, 0)),
        pl.BlockSpec((tile, D_MODEL), lambda i, *_: (2 * i + 1, 0)),
        pl.BlockSpec((tile, D_MODEL), lambda i, *_: (jnp.minimum(2 * i + 2, n_tiles - 1), 0)),
        row(3 * D_MODEL), row(D_MODEL),
        _const_spec((D_MODEL, D_IN)), _const_spec((D_MODEL, 2 * D_MODEL)), row(2 * D_MODEL),
        _const_spec((WIDTH_A, D_MODEL)), _const_spec((WIDTH_B, D_MODEL)),
        _const_spec((D_MODEL, D_MODEL)),
        row(G_END), _const_spec((2, WIDTH_B)),
        _const_spec((n_tiles, LANES)), _const_spec((n_tiles, LANES)),
        _const_spec((tile, LANES)), _const_spec((tile, LANES)), _const_spec((2, LANES)),
        _const_spec((WIDTH_A, WIDTH_A)), _const_spec((BLK, BLK)),
        _const_spec((2, 4 * BLK, 2 * BLK)), _const_spec((len(LEVELS), BLK, BLK)),
    ]
    out_specs = [
        pl.BlockSpec((2 * tile, D_MODEL), lambda i, *_: (i, 0)),
        pl.BlockSpec((WINDOW, KV_WIDTH), lambda i, *_: (0, 0)),
        pl.BlockSpec((WINDOW, KV_WIDTH), lambda i, *_: (0, 0)),
        pl.BlockSpec((N_HEADS_B, HEAD_DIM_B, HEAD_DIM_B), lambda i, *_: (0, 0, 0)),
    ]
    out_shape = [
        jax.ShapeDtypeStruct((t, D_MODEL), F32),
        jax.ShapeDtypeStruct((WINDOW, KV_WIDTH), F32),
        jax.ShapeDtypeStruct((WINDOW, KV_WIDTH), F32),
        jax.ShapeDtypeStruct((N_HEADS_B, HEAD_DIM_B, HEAD_DIM_B), F32),
    ]
    scratch = [
        pltpu.VMEM((N_HEADS_B, HEAD_DIM_B, HEAD_DIM_B), F32),
        pltpu.VMEM((BLK, KV_WIDTH), BF), pltpu.VMEM((BLK, KV_WIDTH), BF),
        pltpu.VMEM((BLK, KV_WIDTH), BF), pltpu.VMEM((BLK, KV_WIDTH), BF),
        pltpu.VMEM((tile, D_IN), F32), pltpu.VMEM((tile, D_IN), F32),
        pltpu.VMEM((tile, 2 * D_MODEL), F32),
        pltpu.VMEM((tile, D_MODEL), BF), pltpu.VMEM((tile, D_MODEL), BF),
        pltpu.VMEM((tile, WIDTH_B), F32), pltpu.VMEM((tile, WIDTH_B), F32),
    ]
    return pl.pallas_call(
        functools.partial(_prompt_kernel, tile=tile, n_steps=n_steps),
        grid_spec=pltpu.PrefetchScalarGridSpec(
            num_scalar_prefetch=1, grid=(n_steps,),
            in_specs=in_specs, out_specs=out_specs, scratch_shapes=scratch),
        out_shape=out_shape,
        compiler_params=pltpu.CompilerParams(
            dimension_semantics=("arbitrary",), vmem_limit_bytes=VMEM_LIMIT),
        name="prompt_layer",
    )(w["sinks"], x, x, x, mod, w["norm_g"],
      w["w_in"], w["w_merge"], w["b_merge"], w["w_proj_a"], w["w_proj_b"], w["w_out"],
      w["gains"], w["lb_logits"],
      consts["rope_cb"], consts["rope_sb"], consts["rope_cl"], consts["rope_sl"],
      consts["rope_sgn"],
      consts["seg"], consts["tri"], consts["bias"], consts["lmask"])


def _dec_in_kernel(x_ref, mod_ref, ng_ref, w_in_ref, w_mg_ref, b_mg_ref, gains_ref, lbl_ref,
                   rc_ref, ru_ref, rd_ref, seg_ref, perm_ref,
                   w_in_bf_ref, w_mg_bf_ref, g_ref,
                   qm_ref, kn_ref, vn_ref, za_ref, qb_ref, kin_ref, ib_ref, zb_ref,
                   h_ref, p_ref):
    j = pl.program_id(0)
    half_in = D_IN // 2

    @pl.when(j == 0)
    def _():
        h_ref[...] = _norm_modulate(x_ref[...], ng_ref[...], mod_ref)

    w_in_bf = w_in_ref[...].astype(BF)
    w_mg_bf = w_mg_ref[...].astype(BF)
    w_in_bf_ref[...] = w_in_bf
    w_mg_bf_ref[...] = w_mg_bf
    h_bf = h_ref[...]
    g_ref[...] = _sigmoid(_dot(h_bf, w_mg_bf) + b_mg_ref[...])
    part = _dot(h_bf, w_in_bf)

    @pl.when(j == 0)
    def _():
        p_ref[:, :half_in] = part

    @pl.when(j == 1)
    def _():
        p_ref[:, half_in:] = part
        seg = seg_ref[...]
        rc, ru, rd = rc_ref[...], ru_ref[...], rd_ref[...]
        qa = p_ref[:, OFF_QA:OFF_QA + WIDTH_A]
        qa = qa * _head_norm_scale(qa, seg) * (gains_ref[:, G_Q:G_K] * (HEAD_DIM_A ** -0.5))
        qa = jnp.concatenate(
            [_rope(qa[:, c * LANES:(c + 1) * LANES], rc, ru, rd) for c in range(4)], axis=1)
        qm_ref[...] = _dot(qa.astype(BF), perm_ref[...])
        ka = p_ref[:, OFF_KA:OFF_KA + KV_WIDTH]
        ka = ka * _head_norm_scale(ka, seg[:KV_WIDTH, :KV_WIDTH]) * gains_ref[:, G_K:G_O]
        kn_ref[...] = _rope(ka, rc, ru, rd)
        vn_ref[...] = p_ref[:, OFF_VA:OFF_VA + KV_WIDTH]
        za_ref[...] = p_ref[:, OFF_ZA:OFF_ZA + WIDTH_A]
        qb_ref[...] = p_ref[:, OFF_QB:OFF_QB + WIDTH_B]
        lb = _lower_bound(lbl_ref[...])
        kin_ref[...] = (1.0 - lb) * (1.0 - _sigmoid(p_ref[:, OFF_FB:OFF_FB + WIDTH_B]))
        ib_ref[...] = p_ref[:, OFF_IB:OFF_IB + WIDTH_B]
        zb_ref[...] = p_ref[:, OFF_ZB:OFF_ZB + WIDTH_B]


def _dec_in_call(x, mod, consts, w):
    b = x.shape[0]
    n_steps = 2
    half_in, half_mg = D_IN // n_steps, 2 * D_MODEL // n_steps
    assert half_in % LANES == 0
    const = lambda a: pl.BlockSpec(a.shape, lambda j: (0,) * a.ndim)
    cols = lambda rows, n: pl.BlockSpec((rows, n), lambda j: (0, j))
    row_out = lambda n: pl.BlockSpec((b, n), lambda j: (0, 0))
    f = lambda n: jax.ShapeDtypeStruct((b, n), F32)
    widths = [N_HEADS_A * LANES, KV_WIDTH, KV_WIDTH, WIDTH_A, WIDTH_B, WIDTH_B, WIDTH_B, WIDTH_B]
    small = [w["norm_g"]]
    tail = [w["gains"], w["lb_logits"], consts["rope_c1"], consts["rope_u1"], consts["rope_d1"],
            consts["seg"], consts["perm"]]
    return pl.pallas_call(
        _dec_in_kernel,
        grid=(n_steps,),
        in_specs=[const(x), const(mod)] + [const(a) for a in small]
        + [cols(D_MODEL, half_in), cols(D_MODEL, half_mg), cols(1, half_mg)]
        + [const(a) for a in tail],
        out_specs=[cols(D_MODEL, half_in), cols(D_MODEL, half_mg), cols(b, half_mg)]
        + [row_out(n) for n in widths],
        out_shape=[jax.ShapeDtypeStruct((D_MODEL, D_IN), BF),
                   jax.ShapeDtypeStruct((D_MODEL, 2 * D_MODEL), BF), f(2 * D_MODEL)]
        + [f(n) for n in widths],
        scratch_shapes=[pltpu.VMEM((b, D_MODEL), BF), pltpu.VMEM((b, D_IN), F32)],
        compiler_params=pltpu.CompilerParams(
            dimension_semantics=("arbitrary",), vmem_limit_bytes=VMEM_LIMIT),
        name="decode_in",
    )(x, mod, *small, w["w_in_f32"], w["w_merge_f32"], w["b_merge"], *tail)


def _dec_mix_kernel(sinks_ref, qm_ref, kn_ref, vn_ref, qb_ref, kin_ref, ib_ref,
                    ck_ref, cv_ref, st_ref,
                    att_ref, ob_ref, nk_ref, nv_ref, nst_ref, *, bt):
    nh = N_HEADS_A
    s = jnp.concatenate([_dot(qm_ref[j].astype(BF), ck_ref[j].astype(BF))
                         for j in range(bt)], axis=0)
    key = lax.broadcasted_iota(jnp.int32, (bt * nh, WINDOW), 1)
    s = jnp.where(key == 0, NEG, s)
    rep = lambda r: jnp.broadcast_to(r[...][:, None, :], (bt, nh, r.shape[-1])).reshape(
        bt * nh, r.shape[-1])
    kn_rows, vn_rows = rep(kn_ref), rep(vn_ref)
    head = lax.broadcasted_iota(jnp.int32, (bt * nh, 1), 0) % nh
    sink = jnp.zeros((bt * nh, 1), F32)
    for hd in range(nh):
        sink = jnp.where(head == hd, sinks_ref[hd], sink)
    s_new = jnp.sum(qm_ref[...].reshape(bt * nh, LANES) * kn_rows, axis=-1, keepdims=True)
    m = jnp.maximum(jnp.maximum(jnp.max(s, axis=-1, keepdims=True), s_new), sink)
    p = jnp.exp(s - m)
    p_new = jnp.exp(s_new - m)
    den = jnp.sum(p, axis=-1, keepdims=True) + p_new + jnp.exp(sink - m)
    p_bf = p.astype(BF)
    pv = jnp.concatenate([_dot_nt(p_bf[j * nh:(j + 1) * nh, :], cv_ref[j].astype(BF))
                          for j in range(bt)], axis=0)
    att_ref[...] = ((pv + p_new * vn_rows) * (1.0 / den)).reshape(bt, nh, LANES)
    kn_t, vn_t = kn_ref[...].T, vn_ref[...].T
    newest = lax.broadcasted_iota(jnp.int32, (KV_WIDTH, WINDOW), 1) == WINDOW - 1
    for j in range(bt):
        nk_ref[j] = jnp.where(newest, kn_t[:, j:j + 1], pltpu.roll(ck_ref[j], WINDOW - 1, 1))
        nv_ref[j] = jnp.where(newest, vn_t[:, j:j + 1], pltpu.roll(cv_ref[j], WINDOW - 1, 1))

    for hd in range(N_HEADS_B):
        sl = slice(hd * LANES, (hd + 1) * LANES)
        kin_t = kin_ref[:, sl].T
        q_bf = qb_ref[:, sl].astype(BF)
        for j in range(bt):
            kcol = kin_t[:, j:j + 1]
            st = st_ref[j, hd]
            new = st - kcol * (st - ib_ref[j:j + 1, sl])
            nst_ref[j, hd] = new
            ob_ref[j:j + 1, sl] = _dot(q_bf, new.astype(BF))[j:j + 1, :]


def _dec_mix_call(sinks, qm, kn, vn, qb, kin, ib, cache_k, cache_v, state, bt):
    b = kn.shape[0]
    rows = lambda n: pl.BlockSpec((bt, n), lambda i, *_: (i, 0))
    cache_spec = pl.BlockSpec((bt, WINDOW, KV_WIDTH), lambda i, *_: (i, 0, 0))
    st_spec = pl.BlockSpec((bt, N_HEADS_B, HEAD_DIM_B, HEAD_DIM_B), lambda i, *_: (i, 0, 0, 0))
    qm_spec = pl.BlockSpec((bt, N_HEADS_A, LANES), lambda i, *_: (i, 0, 0))
    return pl.pallas_call(
        functools.partial(_dec_mix_kernel, bt=bt),
        grid_spec=pltpu.PrefetchScalarGridSpec(
            num_scalar_prefetch=1, grid=(b // bt,),
            in_specs=[qm_spec, rows(KV_WIDTH), rows(KV_WIDTH), rows(WIDTH_B), rows(WIDTH_B),
                      rows(WIDTH_B), cache_spec, cache_spec, st_spec],
            out_specs=[qm_spec, rows(WIDTH_B), cache_spec, cache_spec, st_spec]),
        out_shape=[jax.ShapeDtypeStruct((b, N_HEADS_A, LANES), F32),
                   jax.ShapeDtypeStruct((b, WIDTH_B), F32),
                   jax.ShapeDtypeStruct(cache_k.shape, F32),
                   jax.ShapeDtypeStruct(cache_v.shape, F32),
                   jax.ShapeDtypeStruct(state.shape, F32)],
        compiler_params=pltpu.CompilerParams(
            dimension_semantics=("arbitrary",), vmem_limit_bytes=VMEM_LIMIT),
        name="decode_mix",
    )(sinks, qm, kn, vn, qb, kin, ib, cache_k, cache_v, state)


def _dec_out_kernel(x_ref, mod_ref, att_ref, za_ref, ob_ref, zb_ref, g_ref, gains_ref,
                    permt_ref, w_pa_ref, w_pb_ref, w_out_ref,
                    y_ref, w_pa_bf_ref, w_pb_bf_ref, w_out_bf_ref):
    w_pa, w_pb, w_out = (r[...].astype(BF) for r in (w_pa_ref, w_pb_ref, w_out_ref))
    w_pa_bf_ref[...] = w_pa
    w_pb_bf_ref[...] = w_pb
    w_out_bf_ref[...] = w_out
    att = _split_dot(att_ref[...], permt_ref[...])
    ya = att * _silu(za_ref[...])
    yb = _branch_b_out(ob_ref[...], zb_ref[...], gains_ref[:, G_O:G_END])
    g = g_ref[...]
    y = (g[:, :D_MODEL] * _dot(ya.astype(BF), w_pa) + g[:, D_MODEL:] * _dot(yb.astype(BF), w_pb))
    y_ref[...] = x_ref[...] + _modulation(mod_ref)[2] * _dot(y.astype(BF), w_out)


def _dec_out_call(x, mod, att, za, ob, zb, g, consts, w):
    bf = lambda a: jax.ShapeDtypeStruct(a.shape, BF)
    ws = (w["w_proj_a_f32"], w["w_proj_b_f32"], w["w_out_f32"])
    return pl.pallas_call(
        _dec_out_kernel,
        out_shape=[jax.ShapeDtypeStruct(x.shape, F32)] + [bf(a) for a in ws],
        compiler_params=pltpu.CompilerParams(vmem_limit_bytes=VMEM_LIMIT),
        name="decode_out",
    )(x, mod, att, za, ob, zb, g, w["gains"], consts["permt"], *ws)


def _rope_consts(n_tiles, tile, past_len):
    half = ROT_DIM // 2
    inv = ROPE_THETA ** (-np.arange(0, ROT_DIM, 2, dtype=np.float64) / ROT_DIM)
    e = np.arange(LANES) % HEAD_DIM_A
    rot = e < ROT_DIM

    def tables(pos):
        ang = np.asarray(pos, np.float64)[:, None] * inv[e % half][None, :]
        return np.where(rot, np.cos(ang), 1.0), np.where(rot, np.sin(ang), 0.0)

    sgn = np.stack([np.where(e < half, -1.0, 0.0),
                    np.where(rot & (e >= half), 1.0, 0.0)])
    cb, sb = tables(np.arange(n_tiles) * tile)
    cl, sl = tables(np.arange(tile))
    c1, s1 = tables([past_len])
    f = lambda a: jnp.asarray(a, F32)
    return {"rope_cb": f(cb), "rope_sb": f(sb), "rope_cl": f(cl), "rope_sl": f(sl),
            "rope_sgn": f(sgn), "rope_c1": f(c1), "rope_u1": f(s1 * sgn[0:1]),
            "rope_d1": f(s1 * sgn[1:2])}


def _static_consts():
    seg = np.kron(np.eye(N_HEADS_A), np.full((HEAD_DIM_A, HEAD_DIM_A), 1.0 / HEAD_DIM_A))
    tri = np.tril(np.ones((BLK, BLK)))
    r = np.arange(4 * BLK)[:, None] % BLK
    c = np.arange(2 * BLK)[None, :]
    ok_prev = (c < BLK) & (c > r)
    ok_cur = (c >= BLK) & (c - BLK <= r)
    bias = np.stack([np.where(ok_cur, 0.0, NEG), np.where(ok_prev | ok_cur, 0.0, NEG)])
    i = np.arange(BLK)[:, None]
    j = np.arange(BLK)[None, :]
    lmask = np.stack([((i // (2 * b)) == (j // (2 * b))) & ((i & b) != 0) & ((j & b) == 0)
                      for b in LEVELS]).astype(np.float32)
    perm = np.zeros((WIDTH_A, N_HEADS_A * LANES), np.float32)
    for hd in range(N_HEADS_A):
        kvh = hd // (N_HEADS_A // N_KV_A)
        for d in range(HEAD_DIM_A):
            perm[hd * HEAD_DIM_A + d, hd * LANES + kvh * HEAD_DIM_A + d] = 1.0
    return {
        "seg": jnp.asarray(seg, BF), "tri": jnp.asarray(tri, BF),
        "bias": jnp.asarray(bias, F32), "lmask": jnp.asarray(lmask, F32),
        "perm": jnp.asarray(perm, BF), "permt": jnp.asarray(perm.T, BF),
    }


def kernel(x_prompt, x_sample, cache_win_k, cache_win_v, state_hgrn, c_prompt, c_sample,
           w_ada, b_ada, norm_g, w_in, q_norm_g, k_norm_g, sinks, lb_logits, o_norm_g,
           w_merge, b_merge, w_proj_a, w_proj_b, w_out):
    depth = w_in.shape[0]
    assert depth == 1 and x_prompt.shape[0] == 1 and x_sample.shape[1] == 1
    t = x_prompt.shape[1]
    nb = x_sample.shape[0]
    past_len = t
    tile = PROMPT_TILE
    bt = DECODE_BATCH_TILE

    consts = _static_consts()
    consts.update(_rope_consts(t // tile, tile, past_len))

    w = {
        "sinks": sinks[0], "norm_g": norm_g,
        "w_in_f32": w_in[0], "w_merge_f32": w_merge[0], "b_merge": b_merge,
        "w_proj_a_f32": w_proj_a[0], "w_proj_b_f32": w_proj_b[0], "w_out_f32": w_out[0],
        "gains": jnp.concatenate([jnp.tile(q_norm_g[0], N_HEADS_A), jnp.tile(k_norm_g[0], N_KV_A),
                                  jnp.tile(o_norm_g[0], N_HEADS_B)])[None, :],
        "lb_logits": lb_logits,
    }

    mod_p, mod_s = _ada_call(c_prompt, c_sample, w_ada[0], b_ada)

    xs = x_sample[:, 0, :]
    (w["w_in"], w["w_merge"], g,
     qm, kn, vn, za, qb, kin, ib, zb) = _dec_in_call(xs, mod_s, consts, w)
    to_t = lambda c: jnp.transpose(c[0], (0, 2, 3, 1)).reshape(nb, KV_WIDTH, WINDOW)
    from_t = lambda c: jnp.transpose(
        c.reshape(nb, N_KV_A, HEAD_DIM_A, WINDOW), (0, 3, 1, 2))[None]
    att, ob, nk, nv, nst = _dec_mix_call(
        w["sinks"], qm.reshape(nb, N_HEADS_A, LANES), kn, vn, qb, kin, ib,
        to_t(cache_win_k), to_t(cache_win_v), state_hgrn[0], bt)
    y_s, w["w_proj_a"], w["w_proj_b"], w["w_out"] = _dec_out_call(
        xs, mod_s, att.reshape(nb, N_HEADS_A * LANES), za, ob, zb, g, consts, w)

    y_p, kwin, vwin, st_p = _prompt_call(x_prompt[0], mod_p, consts, w, tile)

    kv_shape = (1, 1, WINDOW, N_KV_A, HEAD_DIM_A)
    return (y_p[None], y_s[:, None, :],
            kwin.reshape(kv_shape), vwin.reshape(kv_shape), st_p[None, None],
            from_t(nk), from_t(nv), nst[None])
```

```python
import functools

import numpy as np
import jax
import jax.numpy as jnp
from jax import lax
from jax.experimental import pallas as pl
from jax.experimental.pallas import tpu as pltpu

D_MODEL = 1024
HEAD_DIM_A = 64
N_HEADS_A = 8
N_KV_A = 2
WIDTH_A = N_HEADS_A * HEAD_DIM_A
KV_WIDTH = N_KV_A * HEAD_DIM_A
WINDOW = 128
ROT_DIM = HEAD_DIM_A // 4
ROPE_THETA = 500000.0
HEAD_DIM_B = 128
N_HEADS_B = 4
WIDTH_B = N_HEADS_B * HEAD_DIM_B
EPS = 1e-6
OFF_QA = 0
OFF_KA = OFF_QA + WIDTH_A
OFF_VA = OFF_KA + KV_WIDTH
OFF_ZA = OFF_VA + KV_WIDTH
OFF_QB = OFF_ZA + WIDTH_A
OFF_FB = OFF_QB + WIDTH_B
OFF_IB = OFF_FB + WIDTH_B
OFF_ZB = OFF_IB + WIDTH_B
D_IN = OFF_ZB + WIDTH_B

LANES = 128
BLK = 128
SUB = 8
LEVELS = (64, 32, 16, 8)
HGRN_FAST_MAX = 80.0
MXU_N = 256
W_PREP, W_GATES, W_ATTN, W_RECUR, W_MERGE, W_OUT = 1.0, 0.8, 1.0, 0.5, 0.8, 0.0
NEG = -1e30
VMEM_LIMIT = 56 * 1024 * 1024
PROMPT_TILE = 256
DECODE_BATCH_TILE = 16
G_Q, G_K, G_O = 0, WIDTH_A, WIDTH_A + KV_WIDTH
G_END = G_O + WIDTH_B

BF = jnp.bfloat16
F32 = jnp.float32


def _dot(a, b):
    return jnp.dot(a, b, preferred_element_type=F32)


def _dot_nt(a, b):
    return lax.dot_general(a, b, (((1,), (1,)), ((), ())), preferred_element_type=F32)


def _dot_tn(a, b):
    return lax.dot_general(a, b, (((0,), (0,)), ((), ())), preferred_element_type=F32)


def _split_dot(a_f32, b_bf):
    hi = a_f32.astype(BF)
    lo = (a_f32 - hi.astype(F32)).astype(BF)
    return _dot(hi, b_bf) + _dot(lo, b_bf)


def _sigmoid(x):
    return 1.0 / (1.0 + jnp.exp(-x))


def _silu(x):
    return x * _sigmoid(x)


def _lower_bound(lb_logits):
    l0 = lb_logits[0:1, :]
    l1 = lb_logits[1:2, :]
    m = jnp.maximum(l0, l1)
    e0 = jnp.exp(l0 - m)
    e1 = jnp.exp(l1 - m)
    return e0 / (e0 + e1)


def _rope_tables(cb, sb, cl, sl, sgn_up, sgn_dn):
    c = cb * cl - sb * sl
    s = sb * cl + cb * sl
    return c, s * sgn_up, s * sgn_dn


def _rope(x, c, s_up, s_dn):
    return x * c + pltpu.roll(x, LANES - ROT_DIM // 2, 1) * s_up + pltpu.roll(x, ROT_DIM // 2, 1) * s_dn


def _modulation(mod_ref):
    return (mod_ref[:, 0:D_MODEL], mod_ref[:, D_MODEL:2 * D_MODEL],
            mod_ref[:, 2 * D_MODEL:3 * D_MODEL])


def _norm_modulate(x, ng, mod_ref):
    shift, scale, _ = _modulation(mod_ref)
    ms = jnp.mean(x * x, axis=-1, keepdims=True)
    return (x * lax.rsqrt(ms + EPS) * (ng * (1.0 + scale)) + shift).astype(BF)


def _head_norm_scale(x, seg_mean_bf):
    ms = _dot((x * x).astype(BF), seg_mean_bf)
    return lax.rsqrt(ms + EPS)


def _ada_kernel(cp_ref, cs_ref, w_ref, b_ref, op_ref, os_ref):
    w = w_ref[...]
    w_hi = w.astype(BF)
    w_lo = (w - w_hi.astype(F32)).astype(BF)
    b = b_ref[...]

    ns, n_p = cs_ref.shape[0], cp_ref.shape[0]
    c = jnp.concatenate([cs_ref[...], cp_ref[...],
                         jnp.zeros(((-n_p) % SUB, D_MODEL), F32)], axis=0)
    c_hi = c.astype(BF)
    c_lo = (c - c_hi.astype(F32)).astype(BF)
    out = _dot(c_hi, w_hi) + (_dot(c_hi, w_lo) + _dot(c_lo, w_hi)) + b
    os_ref[...] = out[:ns, :]
    op_ref[...] = out[ns:ns + n_p, :]


def _ada_call(c_p, c_s, w_ada, b_ada):
    mp, ms = c_p.shape[0], c_s.shape[0]
    n = w_ada.shape[1]
    tn = 512
    return pl.pallas_call(
        _ada_kernel,
        grid=(n // tn,),
        in_specs=[pl.BlockSpec((mp, D_MODEL), lambda j: (0, 0)),
                  pl.BlockSpec((ms, D_MODEL), lambda j: (0, 0)),
                  pl.BlockSpec((D_MODEL, tn), lambda j: (0, j)),
                  pl.BlockSpec((1, tn), lambda j: (0, j))],
        out_specs=[pl.BlockSpec((mp, tn), lambda j: (0, j)),
                   pl.BlockSpec((ms, tn), lambda j: (0, j))],
        out_shape=[jax.ShapeDtypeStruct((mp, n), F32), jax.ShapeDtypeStruct((ms, n), F32)],
        name="ada",
    )(c_p, c_s, w_ada, b_ada)


ATTN_PARTS = 5
RECUR_PARTS = 4
QKV_PARTS = 3


def _round_robin(*gens):
    results = [None] * len(gens)
    live = list(range(len(gens)))
    while live:
        for n in list(live):
            try:
                next(gens[n])
            except StopIteration as stop:
                results[n] = stop.value
                live.remove(n)
            yield
    return results


def _attn_block(q_blk, kcat, kcat_sw, vcat, vcat_sw, bias, sink_a, sink_b):
    lane = lax.broadcasted_iota(jnp.int32, (BLK, LANES), 1)
    lo = lane < HEAD_DIM_A
    chunks = [q_blk[:, c * LANES:(c + 1) * LANES] for c in range(4)]
    zero = jnp.zeros((BLK, LANES), F32)
    q_lo = [jnp.where(lo, c, zero).astype(BF) for c in chunks]
    q_hi = [jnp.where(lo, zero, c).astype(BF) for c in chunks]
    qa = jnp.concatenate([q_lo[0], q_lo[1], q_hi[2], q_hi[3]], axis=0)
    qb = jnp.concatenate([q_hi[0], q_hi[1], q_lo[2], q_lo[3]], axis=0)

    def probs(qs, kc, sink):
        s = _dot_nt(qs, kc) + bias
        m = jnp.maximum(jnp.max(s, axis=-1, keepdims=True), sink)
        p = jnp.exp(s - m)
        den = jnp.sum(p, axis=-1, keepdims=True) + jnp.exp(sink - m)
        return p.astype(BF), 1.0 / den

    pa, ra = probs(qa, kcat, sink_a)
    yield
    pb, rb = probs(qb, kcat_sw, sink_b)
    yield
    oa = _dot(pa, vcat) * ra
    yield
    ob = _dot(pb, vcat_sw) * rb
    yield
    r = lambda o, i: o[i * BLK:(i + 1) * BLK, :]
    return jnp.concatenate([
        jnp.where(lo, r(oa, 0), r(ob, 0)),
        jnp.where(lo, r(oa, 1), r(ob, 1)),
        jnp.where(lo, r(ob, 2), r(oa, 2)),
        jnp.where(lo, r(ob, 3), r(oa, 3)),
    ], axis=1)


def _hgrn_gates(fb, lb, one_m_lb, tri_bf):
    sig = _sigmoid(fb)
    kin = one_m_lb * (1.0 - sig)
    f = lb + one_m_lb * sig
    cum = _split_dot_left(tri_bf, jnp.log(f))
    return kin, f, cum


def _hgrn_span_decay(cum):
    q = BLK // 4
    ends = [cum[(n + 1) * q - 1:(n + 1) * q, :] for n in range(4)]
    d = -ends[0]
    for n in range(1, 4):
        d = jnp.maximum(d, ends[n - 1] - ends[n])
    return d


def _hgrn_state_step(qb, kin, cum, ib, st_ref):
    q_dec = (qb * jnp.exp(cum)).astype(BF)
    last = cum[BLK - 1:BLK, :]
    k_dec = (kin * jnp.exp(last - cum)).astype(BF)
    v_bf = ib.astype(BF)
    outs = []
    for h in range(N_HEADS_B):
        sl = slice(h * LANES, (h + 1) * LANES)
        st = st_ref[h]
        outs.append(_dot_nt(q_dec[:, sl], st.astype(BF)))
        st_ref[h] = st * jnp.exp(last[:, sl]) + _dot_tn(v_bf[:, sl], k_dec[:, sl])
    return jnp.concatenate(outs, axis=1)


def _hgrn_apply(amats, ib):
    v_bf = ib.astype(BF)
    return jnp.concatenate(
        [_dot(amats[h].astype(BF), v_bf[:, h * LANES:(h + 1) * LANES])
         for h in range(N_HEADS_B)], axis=1)


def _recur_block_fast(qb, kin, cum, ib, st_ref):
    base = _hgrn_state_step(qb, kin, cum, ib, st_ref)
    yield
    half = BLK // 2
    row = lax.broadcasted_iota(jnp.int32, (BLK, 1), 0)
    upper = row >= half
    piv = cum[half - 1:half, :]
    w_lvl = jnp.exp(jnp.concatenate([piv - cum[:half, :], cum[half:, :] - piv], axis=0))
    p_lvl = (jnp.where(upper, qb, kin) * w_lvl).astype(BF)
    mid = jnp.where(upper, cum[half + half // 2 - 1:half + half // 2, :],
                    cum[half // 2 - 1:half // 2, :])
    e_mid = cum - mid
    q_mid = (qb * jnp.exp(e_mid)).astype(BF)
    k_mid = (kin * jnp.exp(-e_mid)).astype(BF)
    yield

    ri = lax.broadcasted_iota(jnp.int32, (BLK, BLK), 0)
    ci = lax.broadcasted_iota(jnp.int32, (BLK, BLK), 1)
    same_half_causal = ((ri >= half) == (ci >= half)) & (ci <= ri)
    cross = (ri >= half) & (ci < half)
    amats = []
    for h in range(N_HEADS_B):
        sl = slice(h * LANES, (h + 1) * LANES)
        a_mid = _dot_nt(q_mid[:, sl], k_mid[:, sl])
        a_lvl = _dot_nt(p_lvl[:, sl], p_lvl[:, sl])
        amats.append(jnp.where(same_half_causal, a_mid, jnp.where(cross, a_lvl, 0.0)))
    yield
    return base, base + _hgrn_apply(amats, ib)


def _hgrn_intra_robust(qb, kin, f, cum, ib, lvl_mask_ref):
    row = lax.broadcasted_iota(jnp.int32, (BLK, 1), 0)
    lvl_ops = []
    for b in LEVELS:
        pieces = []
        for r0 in range(0, BLK, 2 * b):
            piv = cum[r0 + b - 1:r0 + b, :]
            pieces.append(piv - cum[r0:r0 + b, :])
            pieces.append(cum[r0 + b:r0 + 2 * b, :] - piv)
        w = jnp.exp(jnp.concatenate(pieces, axis=0))
        second = (row & b) != 0
        lvl_ops.append((jnp.where(second, qb, kin) * w).astype(BF))

    n8 = BLK // SUB
    q3 = qb.reshape(n8, SUB, WIDTH_B)
    k3 = kin.reshape(n8, SUB, WIDTH_B)
    f3 = f.reshape(n8, SUB, WIDTH_B)
    v3 = ib.reshape(n8, SUB, WIDTH_B)
    subl = lax.broadcasted_iota(jnp.int32, (n8, SUB, 1), 1)

    def head(x, h):
        return x[..., h * LANES:(h + 1) * LANES]

    g = q3 * k3
    acc = [jnp.sum(head(g, h), axis=-1, keepdims=True) * head(v3, h) for h in range(N_HEADS_B)]
    dec = jnp.ones_like(f3)
    kd = k3
    vd = v3
    for d in range(1, SUB):
        dec = f3 * pltpu.roll(dec, 1, 1)
        kd = pltpu.roll(kd, 1, 1)
        vd = pltpu.roll(vd, 1, 1)
        g = q3 * kd * dec
        ok = subl >= d
        for h in range(N_HEADS_B):
            a = jnp.where(ok, jnp.sum(head(g, h), axis=-1, keepdims=True), 0.0)
            acc[h] = acc[h] + a * head(vd, h)

    amats = []
    for h in range(N_HEADS_B):
        sl = slice(h * LANES, (h + 1) * LANES)
        amat = jnp.zeros((BLK, BLK), F32)
        for li in range(len(LEVELS)):
            p = lvl_ops[li][:, sl]
            amat = amat + lvl_mask_ref[li] * _dot_nt(p, p)
        amats.append(amat)
    diag = jnp.concatenate([acc[h].reshape(BLK, LANES) for h in range(N_HEADS_B)], axis=1)
    return _hgrn_apply(amats, ib) + diag


def _split_dot_left(a_bf, b_f32):
    hi = b_f32.astype(BF)
    lo = (b_f32 - hi.astype(F32)).astype(BF)
    return _dot(a_bf, hi) + _dot(a_bf, lo)


def _branch_b_out(o, zb, og):
    outs = []
    for h in range(N_HEADS_B):
        sl = slice(h * LANES, (h + 1) * LANES)
        oh = o[:, sl]
        ms = jnp.mean(oh * oh, axis=-1, keepdims=True)
        outs.append(oh * lax.rsqrt(ms + EPS))
    return jnp.concatenate(outs, axis=1) * og * _silu(zb)


def _prompt_kernel(sinks_ref,
                   x0_ref, x1_ref, x2_ref, mod_ref, ng_ref,
                   w_in_ref, w_mg_ref, b_mg_ref, w_pa_ref, w_pb_ref, w_out_ref,
                   gains_ref, lbl_ref,
                   cb_ref, sb_ref, cl_ref, sl_ref, sgn_ref,
                   seg_ref, tri_ref, bias_ref, lmask_ref,
                   y_ref, kwin_ref, vwin_ref, state_ref,
                   st_ref, kprev_ref, kprev_sw_ref, vprev_ref, vprev_sw_ref,
                   p0_ref, p1_ref, g_ref, h0_ref, h1_ref, ob_ref, obase_ref,
                   *, tile, n_steps):
    s = pl.program_id(0)
    nblk = tile // BLK

    def stage_a(x_ref, h_ref, p_ref):
        def prep():
            h_ref[...] = _norm_modulate(x_ref[...], ng_ref[...], mod_ref)

        def proj_chunk(c):
            def run():
                cs = slice(c * MXU_N, (c + 1) * MXU_N)
                p_ref[:, cs] = _dot(h_ref[...], w_in_ref[:, cs])
            return run

        return [prep] + [proj_chunk(c) for c in range(D_IN // MXU_N)]

    def gate_chunks(h_ref):
        def gate_chunk(c):
            def run():
                cs = slice(c * MXU_N, (c + 1) * MXU_N)
                g_ref[:, cs] = _sigmoid(_dot(h_ref[...], w_mg_ref[:, cs]) + b_mg_ref[:, cs])
            return run

        return [gate_chunk(c) for c in range(2 * D_MODEL // MXU_N)]

    def phase(h_cur_ref, a_next, b_parts):
        gc = gate_chunks(h_cur_ref)
        interleave(gc[:2] + a_next[:1] + gc[2:] + a_next[1:], b_parts)

    def stage_b(x_ref, p_ref, t_idx, y_rows):
        blocks = [slice(blk * BLK, (blk + 1) * BLK) for blk in range(nblk)]

        def qkv_stream():
            seg = seg_ref[...]
            rc, ru, rd = _rope_tables(cb_ref[pl.ds(t_idx, 1), :], sb_ref[pl.ds(t_idx, 1), :],
                                      cl_ref[...], sl_ref[...], sgn_ref[0:1, :], sgn_ref[1:2, :])
            qa = p_ref[:, OFF_QA:OFF_QA + WIDTH_A]
            qa = qa * _head_norm_scale(qa, seg) * (gains_ref[:, G_Q:G_K] * (HEAD_DIM_A ** -0.5))
            yield
            qa = jnp.concatenate(
                [_rope(qa[:, c * LANES:(c + 1) * LANES], rc, ru, rd) for c in range(4)], axis=1)
            yield
            ka = p_ref[:, OFF_KA:OFF_KA + KV_WIDTH]
            ka = ka * _head_norm_scale(ka, seg[:KV_WIDTH, :KV_WIDTH]) * gains_ref[:, G_K:G_O]
            ka = _rope(ka, rc, ru, rd)
            va = p_ref[:, OFF_VA:OFF_VA + KV_WIDTH]
            ka_sw = pltpu.roll(ka, HEAD_DIM_A, 1)
            va_sw = pltpu.roll(va, HEAD_DIM_A, 1)
            rows4 = lax.broadcasted_iota(jnp.int32, (4 * BLK, 1), 0) // BLK

            def sink_col(heads):
                col = jnp.zeros((4 * BLK, 1), F32)
                for n, hd in enumerate(heads):
                    col = jnp.where(rows4 == n, sinks_ref[hd], col)
                return col
            kwin_ref[...] = ka[tile - WINDOW:, :]
            vwin_ref[...] = va[tile - WINDOW:, :]
            return qa, ka, va, ka_sw, va_sw, sink_col((0, 2, 5, 7)), sink_col((1, 3, 4, 6))

        def gates_stream():
            lb = _lower_bound(lbl_ref[...])
            tri = tri_ref[...]
            gates = []
            for rs in blocks:
                gates.append(_hgrn_gates(p_ref[rs, OFF_FB:OFF_FB + WIDTH_B], lb, 1.0 - lb, tri))
                yield
            span = _hgrn_span_decay(gates[0][2])
            for g in gates[1:]:
                span = jnp.maximum(span, _hgrn_span_decay(g[2]))
            return gates, jnp.max(span) < HGRN_FAST_MAX

        (qa, ka, va, ka_sw, va_sw, sink_a, sink_b), (gates, mild) = (
            yield from _round_robin(qkv_stream(), gates_stream()))

        kv_refs = (kprev_ref, kprev_sw_ref, vprev_ref, vprev_sw_ref)
        kv_new = [a.astype(BF) for a in (ka, ka_sw, va, va_sw)]
        kv_old = [r[...] for r in kv_refs]
        for r, a in zip(kv_refs, kv_new):
            r[...] = a[blocks[-1]]

        def attend(blk, rs):
            cats = [jnp.concatenate([old if blk == 0 else new[blocks[blk - 1]], new[rs]], axis=0)
                    for old, new in zip(kv_old, kv_new)]
            bias = bias_ref[jnp.where(t_idx == 0, 0, 1)] if blk == 0 else bias_ref[1]
            return (yield from _attn_block(qa[rs], *cats, bias, sink_a, sink_b))

        def recur(blk, rs):
            kin, _, cum = gates[blk]
            qb, ib = p_ref[rs, OFF_QB:OFF_QB + WIDTH_B], p_ref[rs, OFF_IB:OFF_IB + WIDTH_B]
            base, full = yield from _recur_block_fast(qb, kin, cum, ib, st_ref)
            obase_ref[rs, :] = base
            ob_ref[rs, :] = full

        ya_parts = []
        for blk, rs in enumerate(blocks):
            res = yield from _round_robin(attend(blk, rs), recur(blk, rs))
            ya_parts.append(res[0])

        @pl.when(jnp.logical_not(mild))
        def _():
            for rs, (kin, f, cum) in zip(blocks, gates):
                qb, ib = p_ref[rs, OFF_QB:OFF_QB + WIDTH_B], p_ref[rs, OFF_IB:OFF_IB + WIDTH_B]
                ob_ref[rs, :] = obase_ref[rs, :] + _hgrn_intra_robust(
                    qb, kin, f, cum, ib, lmask_ref)
        yield

        ya = jnp.concatenate(ya_parts, axis=0) * _silu(p_ref[:, OFF_ZA:OFF_ZA + WIDTH_A])
        yb = _branch_b_out(ob_ref[...], p_ref[:, OFF_ZB:OFF_ZB + WIDTH_B],
                           gains_ref[:, G_O:G_END])
        ya_bf, yb_bf = ya.astype(BF), yb.astype(BF)
        yield

        y = (g_ref[:, :D_MODEL] * _dot(ya_bf, w_pa_ref[...])
             + g_ref[:, D_MODEL:] * _dot(yb_bf, w_pb_ref[...]))
        y_ref[y_rows, :] = x_ref[...] + _modulation(mod_ref)[2] * _dot(y.astype(BF), w_out_ref[...])
        yield

    def interleave(a_thunks, b_parts):
        mix_parts = ATTN_PARTS + RECUR_PARTS
        prep_parts = QKV_PARTS + nblk + 1
        weights = ([(W_PREP + W_GATES) / prep_parts] * prep_parts
                   + [(W_ATTN + W_RECUR) / mix_parts] * (mix_parts * nblk)
                   + [0.0, W_MERGE, W_OUT])
        total, acc, done = sum(weights), 0.0, 0
        for wgt in weights:
            acc += wgt
            upto = int(round(len(a_thunks) * acc / total))
            for th in a_thunks[done:upto]:
                th()
            done = upto
            next(b_parts)
        assert done == len(a_thunks) and next(b_parts, "end") == "end"

    @pl.when(s == 0)
    def _():
        st_ref[...] = jnp.zeros_like(st_ref)
        for r in (kprev_ref, kprev_sw_ref, vprev_ref, vprev_sw_ref):
            r[...] = jnp.zeros_like(r)
        for th in stage_a(x0_ref, h0_ref, p0_ref):
            th()

    phase(h0_ref, stage_a(x1_ref, h1_ref, p1_ref),
          stage_b(x0_ref, p0_ref, 2 * s, slice(0, tile)))
    phase(h1_ref, stage_a(x2_ref, h0_ref, p0_ref),
          stage_b(x1_ref, p1_ref, 2 * s + 1, slice(tile, 2 * tile)))

    @pl.when(s == n_steps - 1)
    def _():
        for hd in range(N_HEADS_B):
            state_ref[hd] = st_ref[hd].T


def _const_spec(shape):
    nd = len(shape)
    return pl.BlockSpec(shape, lambda i, *_: (0,) * nd, pipeline_mode=pl.Buffered(1))


def _prompt_call(x, mod, consts, w, tile):
    t = x.shape[0]
    n_tiles = t // tile
    n_steps = n_tiles // 2
    assert n_steps * 2 * tile == t
    row = lambda n: _const_spec((1, n))
    in_specs = [
        pl.BlockSpec((tile, D_MODEL), lambda i, *_: (2 * i, 0)),
        pl.BlockSpec((tile, D_MODEL), lambda i, *_: (2 * i + 1, 0)),
        pl.BlockSpec((tile, D_MODEL), lambda i, *_: (jnp.minimum(2 * i + 2, n_tiles - 1), 0)),
        row(3 * D_MODEL), row(D_MODEL),
        _const_spec((D_MODEL, D_IN)), _const_spec((D_MODEL, 2 * D_MODEL)), row(2 * D_MODEL),
        _const_spec((WIDTH_A, D_MODEL)), _const_spec((WIDTH_B, D_MODEL)),
        _const_spec((D_MODEL, D_MODEL)),
        row(G_END), _const_spec((2, WIDTH_B)),
        _const_spec((n_tiles, LANES)), _const_spec((n_tiles, LANES)),
        _const_spec((tile, LANES)), _const_spec((tile, LANES)), _const_spec((2, LANES)),
        _const_spec((WIDTH_A, WIDTH_A)), _const_spec((BLK, BLK)),
        _const_spec((2, 4 * BLK, 2 * BLK)), _const_spec((len(LEVELS), BLK, BLK)),
    ]
    out_specs = [
        pl.BlockSpec((2 * tile, D_MODEL), lambda i, *_: (i, 0)),
        pl.BlockSpec((WINDOW, KV_WIDTH), lambda i, *_: (0, 0)),
        pl.BlockSpec((WINDOW, KV_WIDTH), lambda i, *_: (0, 0)),
        pl.BlockSpec((N_HEADS_B, HEAD_DIM_B, HEAD_DIM_B), lambda i, *_: (0, 0, 0)),
    ]
    out_shape = [
        jax.ShapeDtypeStruct((t, D_MODEL), F32),
        jax.ShapeDtypeStruct((WINDOW, KV_WIDTH), F32),
        jax.ShapeDtypeStruct((WINDOW, KV_WIDTH), F32),
        jax.ShapeDtypeStruct((N_HEADS_B, HEAD_DIM_B, HEAD_DIM_B), F32),
    ]
    scratch = [
        pltpu.VMEM((N_HEADS_B, HEAD_DIM_B, HEAD_DIM_B), F32),
        pltpu.VMEM((BLK, KV_WIDTH), BF), pltpu.VMEM((BLK, KV_WIDTH), BF),
        pltpu.VMEM((BLK, KV_WIDTH), BF), pltpu.VMEM((BLK, KV_WIDTH), BF),
        pltpu.VMEM((tile, D_IN), F32), pltpu.VMEM((tile, D_IN), F32),
        pltpu.VMEM((tile, 2 * D_MODEL), F32),
        pltpu.VMEM((tile, D_MODEL), BF), pltpu.VMEM((tile, D_MODEL), BF),
        pltpu.VMEM((tile, WIDTH_B), F32), pltpu.VMEM((tile, WIDTH_B), F32),
    ]
    return pl.pallas_call(
        functools.partial(_prompt_kernel, tile=tile, n_steps=n_steps),
        grid_spec=pltpu.PrefetchScalarGridSpec(
            num_scalar_prefetch=1, grid=(n_steps,),
            in_specs=in_specs, out_specs=out_specs, scratch_shapes=scratch),
        out_shape=out_shape,
        compiler_params=pltpu.CompilerParams(
            dimension_semantics=("arbitrary",), vmem_limit_bytes=VMEM_LIMIT),
        name="prompt_layer",
    )(w["sinks"], x, x, x, mod, w["norm_g"],
      w["w_in"], w["w_merge"], w["b_merge"], w["w_proj_a"], w["w_proj_b"], w["w_out"],
      w["gains"], w["lb_logits"],
      consts["rope_cb"], consts["rope_sb"], consts["rope_cl"], consts["rope_sl"],
      consts["rope_sgn"],
      consts["seg"], consts["tri"], consts["bias"], consts["lmask"])


def _dec_in_kernel(x_ref, mod_ref, ng_ref, w_in_ref, w_mg_ref, b_mg_ref, gains_ref, lbl_ref,
                   rc_ref, ru_ref, rd_ref, seg_ref, perm_ref,
                   w_in_bf_ref, w_mg_bf_ref, g_ref,
                   qm_ref, kn_ref, vn_ref, za_ref, qb_ref, kin_ref, ib_ref, zb_ref,
                   h_ref, p_ref):
    j = pl.program_id(0)
    half_in = D_IN // 2

    @pl.when(j == 0)
    def _():
        h_ref[...] = _norm_modulate(x_ref[...], ng_ref[...], mod_ref)

    w_in_bf = w_in_ref[...].astype(BF)
    w_mg_bf = w_mg_ref[...].astype(BF)
    w_in_bf_ref[...] = w_in_bf
    w_mg_bf_ref[...] = w_mg_bf
    h_bf = h_ref[...]
    g_ref[...] = _sigmoid(_dot(h_bf, w_mg_bf) + b_mg_ref[...])
    part = _dot(h_bf, w_in_bf)

    @pl.when(j == 0)
    def _():
        p_ref[:, :half_in] = part

    @pl.when(j == 1)
    def _():
        p_ref[:, half_in:] = part
        seg = seg_ref[...]
        rc, ru, rd = rc_ref[...], ru_ref[...], rd_ref[...]
        qa = p_ref[:, OFF_QA:OFF_QA + WIDTH_A]
        qa = qa * _head_norm_scale(qa, seg) * (gains_ref[:, G_Q:G_K] * (HEAD_DIM_A ** -0.5))
        qa = jnp.concatenate(
            [_rope(qa[:, c * LANES:(c + 1) * LANES], rc, ru, rd) for c in range(4)], axis=1)
        qm_ref[...] = _dot(qa.astype(BF), perm_ref[...])
        ka = p_ref[:, OFF_KA:OFF_KA + KV_WIDTH]
        ka = ka * _head_norm_scale(ka, seg[:KV_WIDTH, :KV_WIDTH]) * gains_ref[:, G_K:G_O]
        kn_ref[...] = _rope(ka, rc, ru, rd)
        vn_ref[...] = p_ref[:, OFF_VA:OFF_VA + KV_WIDTH]
        za_ref[...] = p_ref[:, OFF_ZA:OFF_ZA + WIDTH_A]
        qb_ref[...] = p_ref[:, OFF_QB:OFF_QB + WIDTH_B]
        lb = _lower_bound(lbl_ref[...])
        kin_ref[...] = (1.0 - lb) * (1.0 - _sigmoid(p_ref[:, OFF_FB:OFF_FB + WIDTH_B]))
        ib_ref[...] = p_ref[:, OFF_IB:OFF_IB + WIDTH_B]
        zb_ref[...] = p_ref[:, OFF_ZB:OFF_ZB + WIDTH_B]


def _dec_in_call(x, mod, consts, w):
    b = x.shape[0]
    n_steps = 2
    half_in, half_mg = D_IN // n_steps, 2 * D_MODEL // n_steps
    assert half_in % LANES == 0
    const = lambda a: pl.BlockSpec(a.shape, lambda j: (0,) * a.ndim)
    cols = lambda rows, n: pl.BlockSpec((rows, n), lambda j: (0, j))
    row_out = lambda n: pl.BlockSpec((b, n), lambda j: (0, 0))
    f = lambda n: jax.ShapeDtypeStruct((b, n), F32)
    widths = [N_HEADS_A * LANES, KV_WIDTH, KV_WIDTH, WIDTH_A, WIDTH_B, WIDTH_B, WIDTH_B, WIDTH_B]
    small = [w["norm_g"]]
    tail = [w["gains"], w["lb_logits"], consts["rope_c1"], consts["rope_u1"], consts["rope_d1"],
            consts["seg"], consts["perm"]]
    return pl.pallas_call(
        _dec_in_kernel,
        grid=(n_steps,),
        in_specs=[const(x), const(mod)] + [const(a) for a in small]
        + [cols(D_MODEL, half_in), cols(D_MODEL, half_mg), cols(1, half_mg)]
        + [const(a) for a in tail],
        out_specs=[cols(D_MODEL, half_in), cols(D_MODEL, half_mg), cols(b, half_mg)]
        + [row_out(n) for n in widths],
        out_shape=[jax.ShapeDtypeStruct((D_MODEL, D_IN), BF),
                   jax.ShapeDtypeStruct((D_MODEL, 2 * D_MODEL), BF), f(2 * D_MODEL)]
        + [f(n) for n in widths],
        scratch_shapes=[pltpu.VMEM((b, D_MODEL), BF), pltpu.VMEM((b, D_IN), F32)],
        compiler_params=pltpu.CompilerParams(
            dimension_semantics=("arbitrary",), vmem_limit_bytes=VMEM_LIMIT),
        name="decode_in",
    )(x, mod, *small, w["w_in_f32"], w["w_merge_f32"], w["b_merge"], *tail)


def _dec_mix_kernel(sinks_ref, qm_ref, kn_ref, vn_ref, qb_ref, kin_ref, ib_ref,
                    ck_ref, cv_ref, st_ref,
                    att_ref, ob_ref, nk_ref, nv_ref, nst_ref, *, bt):
    nh = N_HEADS_A
    s = jnp.concatenate([_dot(qm_ref[j].astype(BF), ck_ref[j].astype(BF))
                         for j in range(bt)], axis=0)
    key = lax.broadcasted_iota(jnp.int32, (bt * nh, WINDOW), 1)
    s = jnp.where(key == 0, NEG, s)
    rep = lambda r: jnp.broadcast_to(r[...][:, None, :], (bt, nh, r.shape[-1])).reshape(
        bt * nh, r.shape[-1])
    kn_rows, vn_rows = rep(kn_ref), rep(vn_ref)
    head = lax.broadcasted_iota(jnp.int32, (bt * nh, 1), 0) % nh
    sink = jnp.zeros((bt * nh, 1), F32)
    for hd in range(nh):
        sink = jnp.where(head == hd, sinks_ref[hd], sink)
    s_new = jnp.sum(qm_ref[...].reshape(bt * nh, LANES) * kn_rows, axis=-1, keepdims=True)
    m = jnp.maximum(jnp.maximum(jnp.max(s, axis=-1, keepdims=True), s_new), sink)
    p = jnp.exp(s - m)
    p_new = jnp.exp(s_new - m)
    den = jnp.sum(p, axis=-1, keepdims=True) + p_new + jnp.exp(sink - m)
    p_bf = p.astype(BF)
    pv = jnp.concatenate([_dot_nt(p_bf[j * nh:(j + 1) * nh, :], cv_ref[j].astype(BF))
                          for j in range(bt)], axis=0)
    att_ref[...] = ((pv + p_new * vn_rows) * (1.0 / den)).reshape(bt, nh, LANES)
    kn_t, vn_t = kn_ref[...].T, vn_ref[...].T
    newest = lax.broadcasted_iota(jnp.int32, (KV_WIDTH, WINDOW), 1) == WINDOW - 1
    for j in range(bt):
        nk_ref[j] = jnp.where(newest, kn_t[:, j:j + 1], pltpu.roll(ck_ref[j], WINDOW - 1, 1))
        nv_ref[j] = jnp.where(newest, vn_t[:, j:j + 1], pltpu.roll(cv_ref[j], WINDOW - 1, 1))

    for hd in range(N_HEADS_B):
        sl = slice(hd * LANES, (hd + 1) * LANES)
        kin_t = kin_ref[:, sl].T
        q_bf = qb_ref[:, sl].astype(BF)
        for j in range(bt):
            kcol = kin_t[:, j:j + 1]
            st = st_ref[j, hd]
            new = st - kcol * (st - ib_ref[j:j + 1, sl])
            nst_ref[j, hd] = new
            ob_ref[j:j + 1, sl] = _dot(q_bf, new.astype(BF))[j:j + 1, :]


def _dec_mix_call(sinks, qm, kn, vn, qb, kin, ib, cache_k, cache_v, state, bt):
    b = kn.shape[0]
    rows = lambda n: pl.BlockSpec((bt, n), lambda i, *_: (i, 0))
    cache_spec = pl.BlockSpec((bt, WINDOW, KV_WIDTH), lambda i, *_: (i, 0, 0))
    st_spec = pl.BlockSpec((bt, N_HEADS_B, HEAD_DIM_B, HEAD_DIM_B), lambda i, *_: (i, 0, 0, 0))
    qm_spec = pl.BlockSpec((bt, N_HEADS_A, LANES), lambda i, *_: (i, 0, 0))
    return pl.pallas_call(
        functools.partial(_dec_mix_kernel, bt=bt),
        grid_spec=pltpu.PrefetchScalarGridSpec(
            num_scalar_prefetch=1, grid=(b // bt,),
            in_specs=[qm_spec, rows(KV_WIDTH), rows(KV_WIDTH), rows(WIDTH_B), rows(WIDTH_B),
                      rows(WIDTH_B), cache_spec, cache_spec, st_spec],
            out_specs=[qm_spec, rows(WIDTH_B), cache_spec, cache_spec, st_spec]),
        out_shape=[jax.ShapeDtypeStruct((b, N_HEADS_A, LANES), F32),
                   jax.ShapeDtypeStruct((b, WIDTH_B), F32),
                   jax.ShapeDtypeStruct(cache_k.shape, F32),
                   jax.ShapeDtypeStruct(cache_v.shape, F32),
                   jax.ShapeDtypeStruct(state.shape, F32)],
        compiler_params=pltpu.CompilerParams(
            dimension_semantics=("arbitrary",), vmem_limit_bytes=VMEM_LIMIT),
        name="decode_mix",
    )(sinks, qm, kn, vn, qb, kin, ib, cache_k, cache_v, state)


def _dec_out_kernel(x_ref, mod_ref, att_ref, za_ref, ob_ref, zb_ref, g_ref, gains_ref,
                    permt_ref, w_pa_ref, w_pb_ref, w_out_ref,
                    y_ref, w_pa_bf_ref, w_pb_bf_ref, w_out_bf_ref):
    w_pa, w_pb, w_out = (r[...].astype(BF) for r in (w_pa_ref, w_pb_ref, w_out_ref))
    w_pa_bf_ref[...] = w_pa
    w_pb_bf_ref[...] = w_pb
    w_out_bf_ref[...] = w_out
    att = _split_dot(att_ref[...], permt_ref[...])
    ya = att * _silu(za_ref[...])
    yb = _branch_b_out(ob_ref[...], zb_ref[...], gains_ref[:, G_O:G_END])
    g = g_ref[...]
    y = (g[:, :D_MODEL] * _dot(ya.astype(BF), w_pa) + g[:, D_MODEL:] * _dot(yb.astype(BF), w_pb))
    y_ref[...] = x_ref[...] + _modulation(mod_ref)[2] * _dot(y.astype(BF), w_out)


def _dec_out_call(x, mod, att, za, ob, zb, g, consts, w):
    bf = lambda a: jax.ShapeDtypeStruct(a.shape, BF)
    ws = (w["w_proj_a_f32"], w["w_proj_b_f32"], w["w_out_f32"])
    return pl.pallas_call(
        _dec_out_kernel,
        out_shape=[jax.ShapeDtypeStruct(x.shape, F32)] + [bf(a) for a in ws],
        compiler_params=pltpu.CompilerParams(vmem_limit_bytes=VMEM_LIMIT),
        name="decode_out",
    )(x, mod, att, za, ob, zb, g, w["gains"], consts["permt"], *ws)


def _rope_consts(n_tiles, tile, past_len):
    half = ROT_DIM // 2
    inv = ROPE_THETA ** (-np.arange(0, ROT_DIM, 2, dtype=np.float64) / ROT_DIM)
    e = np.arange(LANES) % HEAD_DIM_A
    rot = e < ROT_DIM

    def tables(pos):
        ang = np.asarray(pos, np.float64)[:, None] * inv[e % half][None, :]
        return np.where(rot, np.cos(ang), 1.0), np.where(rot, np.sin(ang), 0.0)

    sgn = np.stack([np.where(e < half, -1.0, 0.0),
                    np.where(rot & (e >= half), 1.0, 0.0)])
    cb, sb = tables(np.arange(n_tiles) * tile)
    cl, sl = tables(np.arange(tile))
    c1, s1 = tables([past_len])
    f = lambda a: jnp.asarray(a, F32)
    return {"rope_cb": f(cb), "rope_sb": f(sb), "rope_cl": f(cl), "rope_sl": f(sl),
            "rope_sgn": f(sgn), "rope_c1": f(c1), "rope_u1": f(s1 * sgn[0:1]),
            "rope_d1": f(s1 * sgn[1:2])}


def _static_consts():
    seg = np.kron(np.eye(N_HEADS_A), np.full((HEAD_DIM_A, HEAD_DIM_A), 1.0 / HEAD_DIM_A))
    tri = np.tril(np.ones((BLK, BLK)))
    r = np.arange(4 * BLK)[:, None] % BLK
    c = np.arange(2 * BLK)[None, :]
    ok_prev = (c < BLK) & (c > r)
    ok_cur = (c >= BLK) & (c - BLK <= r)
    bias = np.stack([np.where(ok_cur, 0.0, NEG), np.where(ok_prev | ok_cur, 0.0, NEG)])
    i = np.arange(BLK)[:, None]
    j = np.arange(BLK)[None, :]
    lmask = np.stack([((i // (2 * b)) == (j // (2 * b))) & ((i & b) != 0) & ((j & b) == 0)
                      for b in LEVELS]).astype(np.float32)
    perm = np.zeros((WIDTH_A, N_HEADS_A * LANES), np.float32)
    for hd in range(N_HEADS_A):
        kvh = hd // (N_HEADS_A // N_KV_A)
        for d in range(HEAD_DIM_A):
            perm[hd * HEAD_DIM_A + d, hd * LANES + kvh * HEAD_DIM_A + d] = 1.0
    return {
        "seg": jnp.asarray(seg, BF), "tri": jnp.asarray(tri, BF),
        "bias": jnp.asarray(bias, F32), "lmask": jnp.asarray(lmask, F32),
        "perm": jnp.asarray(perm, BF), "permt": jnp.asarray(perm.T, BF),
    }


def kernel(x_prompt, x_sample, cache_win_k, cache_win_v, state_hgrn, c_prompt, c_sample,
           w_ada, b_ada, norm_g, w_in, q_norm_g, k_norm_g, sinks, lb_logits, o_norm_g,
           w_merge, b_merge, w_proj_a, w_proj_b, w_out):
    depth = w_in.shape[0]
    assert depth == 1 and x_prompt.shape[0] == 1 and x_sample.shape[1] == 1
    t = x_prompt.shape[1]
    nb = x_sample.shape[0]
    past_len = t
    tile = PROMPT_TILE
    bt = DECODE_BATCH_TILE

    consts = _static_consts()
    consts.update(_rope_consts(t // tile, tile, past_len))

    w = {
        "sinks": sinks[0], "norm_g": norm_g,
        "w_in_f32": w_in[0], "w_merge_f32": w_merge[0], "b_merge": b_merge,
        "w_proj_a_f32": w_proj_a[0], "w_proj_b_f32": w_proj_b[0], "w_out_f32": w_out[0],
        "gains": jnp.concatenate([jnp.tile(q_norm_g[0], N_HEADS_A), jnp.tile(k_norm_g[0], N_KV_A),
                                  jnp.tile(o_norm_g[0], N_HEADS_B)])[None, :],
        "lb_logits": lb_logits,
    }

    mod_p, mod_s = _ada_call(c_prompt, c_sample, w_ada[0], b_ada)

    xs = x_sample[:, 0, :]
    (w["w_in"], w["w_merge"], g,
     qm, kn, vn, za, qb, kin, ib, zb) = _dec_in_call(xs, mod_s, consts, w)
    to_t = lambda c: jnp.transpose(c[0], (0, 2, 3, 1)).reshape(nb, KV_WIDTH, WINDOW)
    from_t = lambda c: jnp.transpose(
        c.reshape(nb, N_KV_A, HEAD_DIM_A, WINDOW), (0, 3, 1, 2))[None]
    att, ob, nk, nv, nst = _dec_mix_call(
        w["sinks"], qm.reshape(nb, N_HEADS_A, LANES), kn, vn, qb, kin, ib,
        to_t(cache_win_k), to_t(cache_win_v), state_hgrn[0], bt)
    y_s, w["w_proj_a"], w["w_proj_b"], w["w_out"] = _dec_out_call(
        xs, mod_s, att.reshape(nb, N_HEADS_A * LANES), za, ob, zb, g, consts, w)

    y_p, kwin, vwin, st_p = _prompt_call(x_prompt[0], mod_p, consts, w, tile)

    kv_shape = (1, 1, WINDOW, N_KV_A, HEAD_DIM_A)
    return (y_p[None], y_s[:, None, :],
            kwin.reshape(kv_shape), vwin.reshape(kv_shape), st_p[None, None],
            from_t(nk), from_t(nv), nst[None])
```

```python
import functools

import numpy as np
import jax
import jax.numpy as jnp
from jax import lax
from jax.experimental import pallas as pl
from jax.experimental.pallas import tpu as pltpu

D_MODEL = 1024
HEAD_DIM_A = 64
N_HEADS_A = 8
N_KV_A = 2
WIDTH_A = N_HEADS_A * HEAD_DIM_A
KV_WIDTH = N_KV_A * HEAD_DIM_A
WINDOW = 128
ROT_DIM = HEAD_DIM_A // 4
ROPE_THETA = 500000.0
HEAD_DIM_B = 128
N_HEADS_B = 4
WIDTH_B = N_HEADS_B * HEAD_DIM_B
EPS = 1e-6
OFF_QA = 0
OFF_KA = OFF_QA + WIDTH_A
OFF_VA = OFF_KA + KV_WIDTH
OFF_ZA = OFF_VA + KV_WIDTH
OFF_QB = OFF_ZA + WIDTH_A
OFF_FB = OFF_QB + WIDTH_B
OFF_IB = OFF_FB + WIDTH_B
OFF_ZB = OFF_IB + WIDTH_B
D_IN = OFF_ZB + WIDTH_B

LANES = 128
BLK = 128
SUB = 8
LEVELS = (64, 32, 16, 8)
HGRN_FAST_MAX = 80.0
MXU_N = 256
NEG = -1e30
VMEM_LIMIT = 56 * 1024 * 1024
PROMPT_TILE = 256
DECODE_BATCH_TILE = 16
G_Q, G_K, G_O = 0, WIDTH_A, WIDTH_A + KV_WIDTH
G_END = G_O + WIDTH_B

BF = jnp.bfloat16
F32 = jnp.float32


def _dot(a, b):
    return jnp.dot(a, b, preferred_element_type=F32)


def _dot_nt(a, b):
    return lax.dot_general(a, b, (((1,), (1,)), ((), ())), preferred_element_type=F32)


def _dot_tn(a, b):
    return lax.dot_general(a, b, (((0,), (0,)), ((), ())), preferred_element_type=F32)


def _split_dot(a_f32, b_bf):
    hi = a_f32.astype(BF)
    lo = (a_f32 - hi.astype(F32)).astype(BF)
    return _dot(hi, b_bf) + _dot(lo, b_bf)


def _sigmoid(x):
    return 1.0 / (1.0 + jnp.exp(-x))


def _silu(x):
    return x * _sigmoid(x)


def _lower_bound(lb_logits):
    l0 = lb_logits[0:1, :]
    l1 = lb_logits[1:2, :]
    m = jnp.maximum(l0, l1)
    e0 = jnp.exp(l0 - m)
    e1 = jnp.exp(l1 - m)
    return e0 / (e0 + e1)


def _rope_tables(cb, sb, cl, sl, sgn_up, sgn_dn):
    c = cb * cl - sb * sl
    s = sb * cl + cb * sl
    return c, s * sgn_up, s * sgn_dn


def _rope(x, c, s_up, s_dn):
    return x * c + pltpu.roll(x, LANES - ROT_DIM // 2, 1) * s_up + pltpu.roll(x, ROT_DIM // 2, 1) * s_dn


def _modulation(mod_ref):
    return (mod_ref[:, 0:D_MODEL], mod_ref[:, D_MODEL:2 * D_MODEL],
            mod_ref[:, 2 * D_MODEL:3 * D_MODEL])


def _norm_modulate(x, ng, mod_ref):
    shift, scale, _ = _modulation(mod_ref)
    ms = jnp.mean(x * x, axis=-1, keepdims=True)
    return (x * lax.rsqrt(ms + EPS) * (ng * (1.0 + scale)) + shift).astype(BF)


def _head_norm_scale(x, seg_mean_bf):
    ms = _dot((x * x).astype(BF), seg_mean_bf)
    return lax.rsqrt(ms + EPS)


def _ada_kernel(cp_ref, cs_ref, w_ref, b_ref, op_ref, os_ref):
    w = w_ref[...]
    w_hi = w.astype(BF)
    w_lo = (w - w_hi.astype(F32)).astype(BF)
    b = b_ref[...]

    ns, n_p = cs_ref.shape[0], cp_ref.shape[0]
    c = jnp.concatenate([cs_ref[...], cp_ref[...],
                         jnp.zeros(((-n_p) % SUB, D_MODEL), F32)], axis=0)
    c_hi = c.astype(BF)
    c_lo = (c - c_hi.astype(F32)).astype(BF)
    out = _dot(c_hi, w_hi) + (_dot(c_hi, w_lo) + _dot(c_lo, w_hi)) + b
    os_ref[...] = out[:ns, :]
    op_ref[...] = out[ns:ns + n_p, :]


def _ada_call(c_p, c_s, w_ada, b_ada):
    mp, ms = c_p.shape[0], c_s.shape[0]
    n = w_ada.shape[1]
    tn = 512
    return pl.pallas_call(
        _ada_kernel,
        grid=(n // tn,),
        in_specs=[pl.BlockSpec((mp, D_MODEL), lambda j: (0, 0)),
                  pl.BlockSpec((ms, D_MODEL), lambda j: (0, 0)),
                  pl.BlockSpec((D_MODEL, tn), lambda j: (0, j)),
                  pl.BlockSpec((1, tn), lambda j: (0, j))],
        out_specs=[pl.BlockSpec((mp, tn), lambda j: (0, j)),
                   pl.BlockSpec((ms, tn), lambda j: (0, j))],
        out_shape=[jax.ShapeDtypeStruct((mp, n), F32), jax.ShapeDtypeStruct((ms, n), F32)],
        name="ada",
    )(c_p, c_s, w_ada, b_ada)


STAGE_B_COST = 3840


def _round_robin(*gens):
    results = [None] * len(gens)
    live = list(range(len(gens)))
    while live:
        for n in list(live):
            try:
                yield next(gens[n])
            except StopIteration as stop:
                results[n] = stop.value
                live.remove(n)
    return results


def _attn_block(q_blk, kcat, kcat_sw, vcat, vcat_sw, bias, sink_a, sink_b):
    lane = lax.broadcasted_iota(jnp.int32, (BLK, LANES), 1)
    lo = lane < HEAD_DIM_A
    chunks = [q_blk[:, c * LANES:(c + 1) * LANES] for c in range(4)]
    zero = jnp.zeros((BLK, LANES), F32)
    q_lo = [jnp.where(lo, c, zero).astype(BF) for c in chunks]
    q_hi = [jnp.where(lo, zero, c).astype(BF) for c in chunks]
    qa = jnp.concatenate([q_lo[0], q_lo[1], q_hi[2], q_hi[3]], axis=0)
    qb = jnp.concatenate([q_hi[0], q_hi[1], q_lo[2], q_lo[3]], axis=0)

    def probs(qs, kc, sink):
        s = _dot_nt(qs, kc) + bias
        m = jnp.maximum(jnp.max(s, axis=-1, keepdims=True), sink)
        p = jnp.exp(s - m)
        den = jnp.sum(p, axis=-1, keepdims=True) + jnp.exp(sink - m)
        return p.astype(BF), 1.0 / den

    pa, ra = probs(qa, kcat, sink_a)
    yield 200
    pb, rb = probs(qb, kcat_sw, sink_b)
    yield 200
    oa = _dot(pa, vcat) * ra
    yield 40
    ob = _dot(pb, vcat_sw) * rb
    yield 40
    r = lambda o, i: o[i * BLK:(i + 1) * BLK, :]
    out = jnp.concatenate([
        jnp.where(lo, r(oa, 0), r(ob, 0)),
        jnp.where(lo, r(oa, 1), r(ob, 1)),
        jnp.where(lo, r(ob, 2), r(oa, 2)),
        jnp.where(lo, r(ob, 3), r(oa, 3)),
    ], axis=1)
    yield 30
    return out


def _hgrn_gates(fb, lb, one_m_lb, tri_bf):
    sig = _sigmoid(fb)
    kin = one_m_lb * (1.0 - sig)
    f = lb + one_m_lb * sig
    cum = _split_dot_left(tri_bf, jnp.log(f))
    return kin, f, cum


def _hgrn_span_decay(cum):
    q = BLK // 4
    ends = [cum[(n + 1) * q - 1:(n + 1) * q, :] for n in range(4)]
    d = -ends[0]
    for n in range(1, 4):
        d = jnp.maximum(d, ends[n - 1] - ends[n])
    return d


def _hgrn_state_step(qb, kin, cum, ib, st_ref):
    q_dec = (qb * jnp.exp(cum)).astype(BF)
    last = cum[BLK - 1:BLK, :]
    k_dec = (kin * jnp.exp(last - cum)).astype(BF)
    v_bf = ib.astype(BF)
    outs = []
    for h in range(N_HEADS_B):
        sl = slice(h * LANES, (h + 1) * LANES)
        st = st_ref[h]
        outs.append(_dot_nt(q_dec[:, sl], st.astype(BF)))
        st_ref[h] = st * jnp.exp(last[:, sl]) + _dot_tn(v_bf[:, sl], k_dec[:, sl])
    return jnp.concatenate(outs, axis=1)


def _hgrn_apply(amats, ib):
    v_bf = ib.astype(BF)
    return jnp.concatenate(
        [_dot(amats[h].astype(BF), v_bf[:, h * LANES:(h + 1) * LANES])
         for h in range(N_HEADS_B)], axis=1)


def _recur_block_fast(qb, kin, cum, ib, st_ref):
    base = _hgrn_state_step(qb, kin, cum, ib, st_ref)
    yield 130
    half = BLK // 2
    row = lax.broadcasted_iota(jnp.int32, (BLK, 1), 0)
    upper = row >= half
    piv = cum[half - 1:half, :]
    w_lvl = jnp.exp(jnp.concatenate([piv - cum[:half, :], cum[half:, :] - piv], axis=0))
    p_lvl = (jnp.where(upper, qb, kin) * w_lvl).astype(BF)
    mid = jnp.where(upper, cum[half + half // 2 - 1:half + half // 2, :],
                    cum[half // 2 - 1:half // 2, :])
    e_mid = cum - mid
    q_mid = (qb * jnp.exp(e_mid)).astype(BF)
    k_mid = (kin * jnp.exp(-e_mid)).astype(BF)
    yield 230

    ri = lax.broadcasted_iota(jnp.int32, (BLK, BLK), 0)
    ci = lax.broadcasted_iota(jnp.int32, (BLK, BLK), 1)
    same_half_causal = ((ri >= half) == (ci >= half)) & (ci <= ri)
    cross = (ri >= half) & (ci < half)
    amats = []
    for h in range(N_HEADS_B):
        sl = slice(h * LANES, (h + 1) * LANES)
        a_mid = _dot_nt(q_mid[:, sl], k_mid[:, sl])
        a_lvl = _dot_nt(p_lvl[:, sl], p_lvl[:, sl])
        amats.append(jnp.where(same_half_causal, a_mid, jnp.where(cross, a_lvl, 0.0)))
    yield 60
    full = base + _hgrn_apply(amats, ib)
    yield 40
    return base, full


def _hgrn_intra_robust(qb, kin, f, cum, ib, lvl_mask_ref):
    row = lax.broadcasted_iota(jnp.int32, (BLK, 1), 0)
    lvl_ops = []
    for b in LEVELS:
        pieces = []
        for r0 in range(0, BLK, 2 * b):
            piv = cum[r0 + b - 1:r0 + b, :]
            pieces.append(piv - cum[r0:r0 + b, :])
            pieces.append(cum[r0 + b:r0 + 2 * b, :] - piv)
        w = jnp.exp(jnp.concatenate(pieces, axis=0))
        second = (row & b) != 0
        lvl_ops.append((jnp.where(second, qb, kin) * w).astype(BF))

    n8 = BLK // SUB
    q3 = qb.reshape(n8, SUB, WIDTH_B)
    k3 = kin.reshape(n8, SUB, WIDTH_B)
    f3 = f.reshape(n8, SUB, WIDTH_B)
    v3 = ib.reshape(n8, SUB, WIDTH_B)
    subl = lax.broadcasted_iota(jnp.int32, (n8, SUB, 1), 1)

    def head(x, h):
        return x[..., h * LANES:(h + 1) * LANES]

    g = q3 * k3
    acc = [jnp.sum(head(g, h), axis=-1, keepdims=True) * head(v3, h) for h in range(N_HEADS_B)]
    dec = jnp.ones_like(f3)
    kd = k3
    vd = v3
    for d in range(1, SUB):
        dec = f3 * pltpu.roll(dec, 1, 1)
        kd = pltpu.roll(kd, 1, 1)
        vd = pltpu.roll(vd, 1, 1)
        g = q3 * kd * dec
        ok = subl >= d
        for h in range(N_HEADS_B):
            a = jnp.where(ok, jnp.sum(head(g, h), axis=-1, keepdims=True), 0.0)
            acc[h] = acc[h] + a * head(vd, h)

    amats = []
    for h in range(N_HEADS_B):
        sl = slice(h * LANES, (h + 1) * LANES)
        amat = jnp.zeros((BLK, BLK), F32)
        for li in range(len(LEVELS)):
            p = lvl_ops[li][:, sl]
            amat = amat + lvl_mask_ref[li] * _dot_nt(p, p)
        amats.append(amat)
    diag = jnp.concatenate([acc[h].reshape(BLK, LANES) for h in range(N_HEADS_B)], axis=1)
    return _hgrn_apply(amats, ib) + diag


def _split_dot_left(a_bf, b_f32):
    hi = b_f32.astype(BF)
    lo = (b_f32 - hi.astype(F32)).astype(BF)
    return _dot(a_bf, hi) + _dot(a_bf, lo)


def _branch_b_out(o, zb, og):
    outs = []
    for h in range(N_HEADS_B):
        sl = slice(h * LANES, (h + 1) * LANES)
        oh = o[:, sl]
        ms = jnp.mean(oh * oh, axis=-1, keepdims=True)
        outs.append(oh * lax.rsqrt(ms + EPS))
    return jnp.concatenate(outs, axis=1) * og * _silu(zb)


def _prompt_kernel(sinks_ref,
                   x0_ref, x1_ref, x2_ref, mod_ref, ng_ref,
                   w_in_ref, w_mg_ref, b_mg_ref, w_pa_ref, w_pb_ref, w_out_ref,
                   gains_ref, lbl_ref,
                   cb_ref, sb_ref, cl_ref, sl_ref, sgn_ref,
                   seg_ref, tri_ref, bias_ref, lmask_ref,
                   y_ref, kwin_ref, vwin_ref, state_ref,
                   st_ref, kprev_ref, kprev_sw_ref, vprev_ref, vprev_sw_ref,
                   p0_ref, p1_ref, g_ref, h0_ref, h1_ref, ob_ref, obase_ref,
                   *, tile, n_steps):
    s = pl.program_id(0)
    nblk = tile // BLK

    def stage_a(x_ref, h_ref, p_ref):
        def prep():
            h_ref[...] = _norm_modulate(x_ref[...], ng_ref[...], mod_ref)

        def proj_chunk(c):
            def run():
                cs = slice(c * MXU_N, (c + 1) * MXU_N)
                p_ref[:, cs] = _dot(h_ref[...], w_in_ref[:, cs])
            return run

        return [prep] + [proj_chunk(c) for c in range(D_IN // MXU_N)]

    def gate_chunks(h_ref):
        def gate_chunk(c):
            def run():
                cs = slice(c * MXU_N, (c + 1) * MXU_N)
                g_ref[:, cs] = _sigmoid(_dot(h_ref[...], w_mg_ref[:, cs]) + b_mg_ref[:, cs])
            return run

        return [gate_chunk(c) for c in range(2 * D_MODEL // MXU_N)]

    def phase(h_cur_ref, a_next, b_parts):
        gc = gate_chunks(h_cur_ref)
        interleave(gc[:2] + a_next[:1] + gc[2:] + a_next[1:], b_parts, lead=12)

    def stage_b(x_ref, p_ref, t_idx, y_rows):
        blocks = [slice(blk * BLK, (blk + 1) * BLK) for blk in range(nblk)]

        def qkv_stream():
            seg = seg_ref[...]
            rc, ru, rd = _rope_tables(cb_ref[pl.ds(t_idx, 1), :], sb_ref[pl.ds(t_idx, 1), :],
                                      cl_ref[...], sl_ref[...], sgn_ref[0:1, :], sgn_ref[1:2, :])
            qa = p_ref[:, OFF_QA:OFF_QA + WIDTH_A]
            qa = qa * _head_norm_scale(qa, seg) * (gains_ref[:, G_Q:G_K] * (HEAD_DIM_A ** -0.5))
            yield 160
            qa = jnp.concatenate(
                [_rope(qa[:, c * LANES:(c + 1) * LANES], rc, ru, rd) for c in range(4)], axis=1)
            yield 160
            ka = p_ref[:, OFF_KA:OFF_KA + KV_WIDTH]
            ka = ka * _head_norm_scale(ka, seg[:KV_WIDTH, :KV_WIDTH]) * gains_ref[:, G_K:G_O]
            ka = _rope(ka, rc, ru, rd)
            va = p_ref[:, OFF_VA:OFF_VA + KV_WIDTH]
            ka_sw = pltpu.roll(ka, HEAD_DIM_A, 1)
            va_sw = pltpu.roll(va, HEAD_DIM_A, 1)
            rows4 = lax.broadcasted_iota(jnp.int32, (4 * BLK, 1), 0) // BLK

            def sink_col(heads):
                col = jnp.zeros((4 * BLK, 1), F32)
                for n, hd in enumerate(heads):
                    col = jnp.where(rows4 == n, sinks_ref[hd], col)
                return col
            kwin_ref[...] = ka[tile - WINDOW:, :]
            vwin_ref[...] = va[tile - WINDOW:, :]
            res = (qa, ka, va, ka_sw, va_sw, sink_col((0, 2, 5, 7)), sink_col((1, 3, 4, 6)))
            yield 120
            return res

        def gates_stream():
            lb = _lower_bound(lbl_ref[...])
            tri = tri_ref[...]
            gates = []
            half_w = WIDTH_B // 2
            for rs in blocks:
                halves = []
                for c0 in (0, half_w):
                    cs = slice(c0, c0 + half_w)
                    halves.append(_hgrn_gates(p_ref[rs, OFF_FB + c0:OFF_FB + c0 + half_w],
                                              lb[:, cs], 1.0 - lb[:, cs], tri))
                    yield 240
                gates.append(tuple(jnp.concatenate(pair, axis=1) for pair in zip(*halves)))
            span = _hgrn_span_decay(gates[0][2])
            for g in gates[1:]:
                span = jnp.maximum(span, _hgrn_span_decay(g[2]))
            mild = jnp.max(span) < HGRN_FAST_MAX
            yield 10
            return gates, mild

        (qa, ka, va, ka_sw, va_sw, sink_a, sink_b), (gates, mild) = (
            yield from _round_robin(qkv_stream(), gates_stream()))

        kv_refs = (kprev_ref, kprev_sw_ref, vprev_ref, vprev_sw_ref)
        kv_new = [a.astype(BF) for a in (ka, ka_sw, va, va_sw)]
        kv_old = [r[...] for r in kv_refs]
        for r, a in zip(kv_refs, kv_new):
            r[...] = a[blocks[-1]]

        def attend(blk, rs):
            cats = [jnp.concatenate([old if blk == 0 else new[blocks[blk - 1]], new[rs]], axis=0)
                    for old, new in zip(kv_old, kv_new)]
            bias = bias_ref[jnp.where(t_idx == 0, 0, 1)] if blk == 0 else bias_ref[1]
            return (yield from _attn_block(qa[rs], *cats, bias, sink_a, sink_b))

        def recur(blk, rs):
            kin, _, cum = gates[blk]
            qb, ib = p_ref[rs, OFF_QB:OFF_QB + WIDTH_B], p_ref[rs, OFF_IB:OFF_IB + WIDTH_B]
            base, full = yield from _recur_block_fast(qb, kin, cum, ib, st_ref)
            obase_ref[rs, :] = base
            ob_ref[rs, :] = full

        ya_parts = []
        for blk, rs in enumerate(blocks):
            res = yield from _round_robin(attend(blk, rs), recur(blk, rs))
            ya_parts.append(res[0])

        @pl.when(jnp.logical_not(mild))
        def _():
            for rs, (kin, f, cum) in zip(blocks, gates):
                qb, ib = p_ref[rs, OFF_QB:OFF_QB + WIDTH_B], p_ref[rs, OFF_IB:OFF_IB + WIDTH_B]
                ob_ref[rs, :] = obase_ref[rs, :] + _hgrn_intra_robust(
                    qb, kin, f, cum, ib, lmask_ref)
        yield 0

        ya_bf = (jnp.concatenate(ya_parts, axis=0)
                 * _silu(p_ref[:, OFF_ZA:OFF_ZA + WIDTH_A])).astype(BF)
        yield 160
        yb_bf = _branch_b_out(ob_ref[...], p_ref[:, OFF_ZB:OFF_ZB + WIDTH_B],
                              gains_ref[:, G_O:G_END]).astype(BF)
        yield 330

        y = (g_ref[:, :D_MODEL] * _dot(ya_bf, w_pa_ref[...])
             + g_ref[:, D_MODEL:] * _dot(yb_bf, w_pb_ref[...]))
        y_ref[y_rows, :] = x_ref[...] + _modulation(mod_ref)[2] * _dot(y.astype(BF), w_out_ref[...])
        yield 0

    def interleave(a_thunks, b_parts, lead):
        for th in a_thunks[:lead]:
            th()
        acc, done = 0, lead
        for cost in b_parts:
            acc += cost
            upto = lead + int(round((len(a_thunks) - lead) * min(acc, STAGE_B_COST) / STAGE_B_COST))
            for th in a_thunks[done:upto]:
                th()
            done = upto
        assert done == len(a_thunks) and acc == STAGE_B_COST, (done, acc)

    @pl.when(s == 0)
    def _():
        st_ref[...] = jnp.zeros_like(st_ref)
        for r in (kprev_ref, kprev_sw_ref, vprev_ref, vprev_sw_ref):
            r[...] = jnp.zeros_like(r)
        for th in stage_a(x0_ref, h0_ref, p0_ref):
            th()

    phase(h0_ref, stage_a(x1_ref, h1_ref, p1_ref),
          stage_b(x0_ref, p0_ref, 2 * s, slice(0, tile)))
    phase(h1_ref, stage_a(x2_ref, h0_ref, p0_ref),
          stage_b(x1_ref, p1_ref, 2 * s + 1, slice(tile, 2 * tile)))

    @pl.when(s == n_steps - 1)
    def _():
        for hd in range(N_HEADS_B):
            state_ref[hd] = st_ref[hd].T


def _const_spec(shape):
    nd = len(shape)
    return pl.BlockSpec(shape, lambda i, *_: (0,) * nd, pipeline_mode=pl.Buffered(1))


def _prompt_call(x, mod, consts, w, tile):
    t = x.shape[0]
    n_tiles = t // tile
    n_steps = n_tiles // 2
    assert n_steps * 2 * tile == t
    row = lambda n: _const_spec((1, n))
    in_specs = [
        pl.BlockSpec((tile, D_MODEL), lambda i, *_: (2 * i, 0)),
        pl.BlockSpec((tile, D_MODEL), lambda i, *_: (2 * i + 1, 0)),
        pl.BlockSpec((tile, D_MODEL), lambda i, *_: (jnp.minimum(2 * i + 2, n_tiles - 1), 0)),
        row(3 * D_MODEL), row(D_MODEL),
        _const_spec((D_MODEL, D_IN)), _const_spec((D_MODEL, 2 * D_MODEL)), row(2 * D_MODEL),
        _const_spec((WIDTH_A, D_MODEL)), _const_spec((WIDTH_B, D_MODEL)),
        _const_spec((D_MODEL, D_MODEL)),
        row(G_END), _const_spec((2, WIDTH_B)),
        _const_spec((n_tiles, LANES)), _const_spec((n_tiles, LANES)),
        _const_spec((tile, LANES)), _const_spec((tile, LANES)), _const_spec((2, LANES)),
        _const_spec((WIDTH_A, WIDTH_A)), _const_spec((BLK, BLK)),
        _const_spec((2, 4 * BLK, 2 * BLK)), _const_spec((len(LEVELS), BLK, BLK)),
    ]
    out_specs = [
        pl.BlockSpec((2 * tile, D_MODEL), lambda i, *_: (i, 0)),
        pl.BlockSpec((WINDOW, KV_WIDTH), lambda i, *_: (0, 0)),
        pl.BlockSpec((WINDOW, KV_WIDTH), lambda i, *_: (0, 0)),
        pl.BlockSpec((N_HEADS_B, HEAD_DIM_B, HEAD_DIM_B), lambda i, *_: (0, 0, 0)),
    ]
    out_shape = [
        jax.ShapeDtypeStruct((t, D_MODEL), F32),
        jax.ShapeDtypeStruct((WINDOW, KV_WIDTH), F32),
        jax.ShapeDtypeStruct((WINDOW, KV_WIDTH), F32),
        jax.ShapeDtypeStruct((N_HEADS_B, HEAD_DIM_B, HEAD_DIM_B), F32),
    ]
    scratch = [
        pltpu.VMEM((N_HEADS_B, HEAD_DIM_B, HEAD_DIM_B), F32),
        pltpu.VMEM((BLK, KV_WIDTH), BF), pltpu.VMEM((BLK, KV_WIDTH), BF),
        pltpu.VMEM((BLK, KV_WIDTH), BF), pltpu.VMEM((BLK, KV_WIDTH), BF),
        pltpu.VMEM((tile, D_IN), F32), pltpu.VMEM((tile, D_IN), F32),
        pltpu.VMEM((tile, 2 * D_MODEL), F32),
        pltpu.VMEM((tile, D_MODEL), BF), pltpu.VMEM((tile, D_MODEL), BF),
        pltpu.VMEM((tile, WIDTH_B), F32), pltpu.VMEM((tile, WIDTH_B), F32),
    ]
    return pl.pallas_call(
        functools.partial(_prompt_kernel, tile=tile, n_steps=n_steps),
        grid_spec=pltpu.PrefetchScalarGridSpec(
            num_scalar_prefetch=1, grid=(n_steps,),
            in_specs=in_specs, out_specs=out_specs, scratch_shapes=scratch),
        out_shape=out_shape,
        compiler_params=pltpu.CompilerParams(
            dimension_semantics=("arbitrary",), vmem_limit_bytes=VMEM_LIMIT),
        name="prompt_layer",
    )(w["sinks"], x, x, x, mod, w["norm_g"],
      w["w_in"], w["w_merge"], w["b_merge"], w["w_proj_a"], w["w_proj_b"], w["w_out"],
      w["gains"], w["lb_logits"],
      consts["rope_cb"], consts["rope_sb"], consts["rope_cl"], consts["rope_sl"],
      consts["rope_sgn"],
      consts["seg"], consts["tri"], consts["bias"], consts["lmask"])


def _dec_in_kernel(x_ref, mod_ref, ng_ref, w_in_ref, w_mg_ref, b_mg_ref, gains_ref, lbl_ref,
                   rc_ref, ru_ref, rd_ref, seg_ref, perm_ref,
                   w_in_bf_ref, w_mg_bf_ref, g_ref,
                   qm_ref, kn_ref, vn_ref, za_ref, qb_ref, kin_ref, ib_ref, zb_ref,
                   h_ref, p_ref):
    j = pl.program_id(0)
    half_in = D_IN // 2

    @pl.when(j == 0)
    def _():
        h_ref[...] = _norm_modulate(x_ref[...], ng_ref[...], mod_ref)

    w_in_bf = w_in_ref[...].astype(BF)
    w_mg_bf = w_mg_ref[...].astype(BF)
    w_in_bf_ref[...] = w_in_bf
    w_mg_bf_ref[...] = w_mg_bf
    h_bf = h_ref[...]
    g_ref[...] = _sigmoid(_dot(h_bf, w_mg_bf) + b_mg_ref[...])
    part = _dot(h_bf, w_in_bf)

    @pl.when(j == 0)
    def _():
        p_ref[:, :half_in] = part

    @pl.when(j == 1)
    def _():
        p_ref[:, half_in:] = part
        seg = seg_ref[...]
        rc, ru, rd = rc_ref[...], ru_ref[...], rd_ref[...]
        qa = p_ref[:, OFF_QA:OFF_QA + WIDTH_A]
        qa = qa * _head_norm_scale(qa, seg) * (gains_ref[:, G_Q:G_K] * (HEAD_DIM_A ** -0.5))
        qa = jnp.concatenate(
            [_rope(qa[:, c * LANES:(c + 1) * LANES], rc, ru, rd) for c in range(4)], axis=1)
        qm_ref[...] = _dot(qa.astype(BF), perm_ref[...])
        ka = p_ref[:, OFF_KA:OFF_KA + KV_WIDTH]
        ka = ka * _head_norm_scale(ka, seg[:KV_WIDTH, :KV_WIDTH]) * gains_ref[:, G_K:G_O]
        kn_ref[...] = _rope(ka, rc, ru, rd)
        vn_ref[...] = p_ref[:, OFF_VA:OFF_VA + KV_WIDTH]
        za_ref[...] = p_ref[:, OFF_ZA:OFF_ZA + WIDTH_A]
        qb_ref[...] = p_ref[:, OFF_QB:OFF_QB + WIDTH_B]
        lb = _lower_bound(lbl_ref[...])
        kin_ref[...] = (1.0 - lb) * (1.0 - _sigmoid(p_ref[:, OFF_FB:OFF_FB + WIDTH_B]))
        ib_ref[...] = p_ref[:, OFF_IB:OFF_IB + WIDTH_B]
        zb_ref[...] = p_ref[:, OFF_ZB:OFF_ZB + WIDTH_B]


def _dec_in_call(x, mod, consts, w):
    b = x.shape[0]
    n_steps = 2
    half_in, half_mg = D_IN // n_steps, 2 * D_MODEL // n_steps
    assert half_in % LANES == 0
    const = lambda a: pl.BlockSpec(a.shape, lambda j: (0,) * a.ndim)
    cols = lambda rows, n: pl.BlockSpec((rows, n), lambda j: (0, j))
    row_out = lambda n: pl.BlockSpec((b, n), lambda j: (0, 0))
    f = lambda n: jax.ShapeDtypeStruct((b, n), F32)
    widths = [N_HEADS_A * LANES, KV_WIDTH, KV_WIDTH, WIDTH_A, WIDTH_B, WIDTH_B, WIDTH_B, WIDTH_B]
    small = [w["norm_g"]]
    tail = [w["gains"], w["lb_logits"], consts["rope_c1"], consts["rope_u1"], consts["rope_d1"],
            consts["seg"], consts["perm"]]
    return pl.pallas_call(
        _dec_in_kernel,
        grid=(n_steps,),
        in_specs=[const(x), const(mod)] + [const(a) for a in small]
        + [cols(D_MODEL, half_in), cols(D_MODEL, half_mg), cols(1, half_mg)]
        + [const(a) for a in tail],
        out_specs=[cols(D_MODEL, half_in), cols(D_MODEL, half_mg), cols(b, half_mg)]
        + [row_out(n) for n in widths],
        out_shape=[jax.ShapeDtypeStruct((D_MODEL, D_IN), BF),
                   jax.ShapeDtypeStruct((D_MODEL, 2 * D_MODEL), BF), f(2 * D_MODEL)]
        + [f(n) for n in widths],
        scratch_shapes=[pltpu.VMEM((b, D_MODEL), BF), pltpu.VMEM((b, D_IN), F32)],
        compiler_params=pltpu.CompilerParams(
            dimension_semantics=("arbitrary",), vmem_limit_bytes=VMEM_LIMIT),
        name="decode_in",
    )(x, mod, *small, w["w_in_f32"], w["w_merge_f32"], w["b_merge"], *tail)


def _dec_mix_kernel(sinks_ref, qm_ref, kn_ref, vn_ref, qb_ref, kin_ref, ib_ref,
                    ck_ref, cv_ref, st_ref,
                    att_ref, ob_ref, nk_ref, nv_ref, nst_ref, *, bt):
    nh = N_HEADS_A
    s = jnp.concatenate([_dot(qm_ref[j].astype(BF), ck_ref[j].astype(BF))
                         for j in range(bt)], axis=0)
    key = lax.broadcasted_iota(jnp.int32, (bt * nh, WINDOW), 1)
    s = jnp.where(key == 0, NEG, s)
    rep = lambda r: jnp.broadcast_to(r[...][:, None, :], (bt, nh, r.shape[-1])).reshape(
        bt * nh, r.shape[-1])
    kn_rows, vn_rows = rep(kn_ref), rep(vn_ref)
    head = lax.broadcasted_iota(jnp.int32, (bt * nh, 1), 0) % nh
    sink = jnp.zeros((bt * nh, 1), F32)
    for hd in range(nh):
        sink = jnp.where(head == hd, sinks_ref[hd], sink)
    s_new = jnp.sum(qm_ref[...].reshape(bt * nh, LANES) * kn_rows, axis=-1, keepdims=True)
    m = jnp.maximum(jnp.maximum(jnp.max(s, axis=-1, keepdims=True), s_new), sink)
    p = jnp.exp(s - m)
    p_new = jnp.exp(s_new - m)
    den = jnp.sum(p, axis=-1, keepdims=True) + p_new + jnp.exp(sink - m)
    p_bf = p.astype(BF)
    pv = jnp.concatenate([_dot_nt(p_bf[j * nh:(j + 1) * nh, :], cv_ref[j].astype(BF))
                          for j in range(bt)], axis=0)
    att_ref[...] = ((pv + p_new * vn_rows) * (1.0 / den)).reshape(bt, nh, LANES)
    kn_t, vn_t = kn_ref[...].T, vn_ref[...].T
    newest = lax.broadcasted_iota(jnp.int32, (KV_WIDTH, WINDOW), 1) == WINDOW - 1
    for j in range(bt):
        nk_ref[j] = jnp.where(newest, kn_t[:, j:j + 1], pltpu.roll(ck_ref[j], WINDOW - 1, 1))
        nv_ref[j] = jnp.where(newest, vn_t[:, j:j + 1], pltpu.roll(cv_ref[j], WINDOW - 1, 1))

    for hd in range(N_HEADS_B):
        sl = slice(hd * LANES, (hd + 1) * LANES)
        kin_t = kin_ref[:, sl].T
        q_bf = qb_ref[:, sl].astype(BF)
        for j in range(bt):
            kcol = kin_t[:, j:j + 1]
            st = st_ref[j, hd]
            new = st - kcol * (st - ib_ref[j:j + 1, sl])
            nst_ref[j, hd] = new
            ob_ref[j:j + 1, sl] = _dot(q_bf, new.astype(BF))[j:j + 1, :]


def _dec_mix_call(sinks, qm, kn, vn, qb, kin, ib, cache_k, cache_v, state, bt):
    b = kn.shape[0]
    rows = lambda n: pl.BlockSpec((bt, n), lambda i, *_: (i, 0))
    cache_spec = pl.BlockSpec((bt, WINDOW, KV_WIDTH), lambda i, *_: (i, 0, 0))
    st_spec = pl.BlockSpec((bt, N_HEADS_B, HEAD_DIM_B, HEAD_DIM_B), lambda i, *_: (i, 0, 0, 0))
    qm_spec = pl.BlockSpec((bt, N_HEADS_A, LANES), lambda i, *_: (i, 0, 0))
    return pl.pallas_call(
        functools.partial(_dec_mix_kernel, bt=bt),
        grid_spec=pltpu.PrefetchScalarGridSpec(
            num_scalar_prefetch=1, grid=(b // bt,),
            in_specs=[qm_spec, rows(KV_WIDTH), rows(KV_WIDTH), rows(WIDTH_B), rows(WIDTH_B),
                      rows(WIDTH_B), cache_spec, cache_spec, st_spec],
            out_specs=[qm_spec, rows(WIDTH_B), cache_spec, cache_spec, st_spec]),
        out_shape=[jax.ShapeDtypeStruct((b, N_HEADS_A, LANES), F32),
                   jax.ShapeDtypeStruct((b, WIDTH_B), F32),
                   jax.ShapeDtypeStruct(cache_k.shape, F32),
                   jax.ShapeDtypeStruct(cache_v.shape, F32),
                   jax.ShapeDtypeStruct(state.shape, F32)],
        compiler_params=pltpu.CompilerParams(
            dimension_semantics=("arbitrary",), vmem_limit_bytes=VMEM_LIMIT),
        name="decode_mix",
    )(sinks, qm, kn, vn, qb, kin, ib, cache_k, cache_v, state)


def _dec_out_kernel(x_ref, mod_ref, att_ref, za_ref, ob_ref, zb_ref, g_ref, gains_ref,
                    permt_ref, w_pa_ref, w_pb_ref, w_out_ref,
                    y_ref, w_pa_bf_ref, w_pb_bf_ref, w_out_bf_ref):
    w_pa, w_pb, w_out = (r[...].astype(BF) for r in (w_pa_ref, w_pb_ref, w_out_ref))
    w_pa_bf_ref[...] = w_pa
    w_pb_bf_ref[...] = w_pb
    w_out_bf_ref[...] = w_out
    att = _split_dot(att_ref[...], permt_ref[...])
    ya = att * _silu(za_ref[...])
    yb = _branch_b_out(ob_ref[...], zb_ref[...], gains_ref[:, G_O:G_END])
    g = g_ref[...]
    y = (g[:, :D_MODEL] * _dot(ya.astype(BF), w_pa) + g[:, D_MODEL:] * _dot(yb.astype(BF), w_pb))
    y_ref[...] = x_ref[...] + _modulation(mod_ref)[2] * _dot(y.astype(BF), w_out)


def _dec_out_call(x, mod, att, za, ob, zb, g, consts, w):
    bf = lambda a: jax.ShapeDtypeStruct(a.shape, BF)
    ws = (w["w_proj_a_f32"], w["w_proj_b_f32"], w["w_out_f32"])
    return pl.pallas_call(
        _dec_out_kernel,
        out_shape=[jax.ShapeDtypeStruct(x.shape, F32)] + [bf(a) for a in ws],
        compiler_params=pltpu.CompilerParams(vmem_limit_bytes=VMEM_LIMIT),
        name="decode_out",
    )(x, mod, att, za, ob, zb, g, w["gains"], consts["permt"], *ws)


def _rope_consts(n_tiles, tile, past_len):
    half = ROT_DIM // 2
    inv = ROPE_THETA ** (-np.arange(0, ROT_DIM, 2, dtype=np.float64) / ROT_DIM)
    e = np.arange(LANES) % HEAD_DIM_A
    rot = e < ROT_DIM

    def tables(pos):
        ang = np.asarray(pos, np.float64)[:, None] * inv[e % half][None, :]
        return np.where(rot, np.cos(ang), 1.0), np.where(rot, np.sin(ang), 0.0)

    sgn = np.stack([np.where(e < half, -1.0, 0.0),
                    np.where(rot & (e >= half), 1.0, 0.0)])
    cb, sb = tables(np.arange(n_tiles) * tile)
    cl, sl = tables(np.arange(tile))
    c1, s1 = tables([past_len])
    f = lambda a: jnp.asarray(a, F32)
    return {"rope_cb": f(cb), "rope_sb": f(sb), "rope_cl": f(cl), "rope_sl": f(sl),
            "rope_sgn": f(sgn), "rope_c1": f(c1), "rope_u1": f(s1 * sgn[0:1]),
            "rope_d1": f(s1 * sgn[1:2])}


def _static_consts():
    seg = np.kron(np.eye(N_HEADS_A), np.full((HEAD_DIM_A, HEAD_DIM_A), 1.0 / HEAD_DIM_A))
    tri = np.tril(np.ones((BLK, BLK)))
    r = np.arange(4 * BLK)[:, None] % BLK
    c = np.arange(2 * BLK)[None, :]
    ok_prev = (c < BLK) & (c > r)
    ok_cur = (c >= BLK) & (c - BLK <= r)
    bias = np.stack([np.where(ok_cur, 0.0, NEG), np.where(ok_prev | ok_cur, 0.0, NEG)])
    i = np.arange(BLK)[:, None]
    j = np.arange(BLK)[None, :]
    lmask = np.stack([((i // (2 * b)) == (j // (2 * b))) & ((i & b) != 0) & ((j & b) == 0)
                      for b in LEVELS]).astype(np.float32)
    perm = np.zeros((WIDTH_A, N_HEADS_A * LANES), np.float32)
    for hd in range(N_HEADS_A):
        kvh = hd // (N_HEADS_A // N_KV_A)
        for d in range(HEAD_DIM_A):
            perm[hd * HEAD_DIM_A + d, hd * LANES + kvh * HEAD_DIM_A + d] = 1.0
    return {
        "seg": jnp.asarray(seg, BF), "tri": jnp.asarray(tri, BF),
        "bias": jnp.asarray(bias, F32), "lmask": jnp.asarray(lmask, F32),
        "perm": jnp.asarray(perm, BF), "permt": jnp.asarray(perm.T, BF),
    }


def kernel(x_prompt, x_sample, cache_win_k, cache_win_v, state_hgrn, c_prompt, c_sample,
           w_ada, b_ada, norm_g, w_in, q_norm_g, k_norm_g, sinks, lb_logits, o_norm_g,
           w_merge, b_merge, w_proj_a, w_proj_b, w_out):
    depth = w_in.shape[0]
    assert depth == 1 and x_prompt.shape[0] == 1 and x_sample.shape[1] == 1
    t = x_prompt.shape[1]
    nb = x_sample.shape[0]
    past_len = t
    tile = PROMPT_TILE
    bt = DECODE_BATCH_TILE

    consts = _static_consts()
    consts.update(_rope_consts(t // tile, tile, past_len))

    w = {
        "sinks": sinks[0], "norm_g": norm_g,
        "w_in_f32": w_in[0], "w_merge_f32": w_merge[0], "b_merge": b_merge,
        "w_proj_a_f32": w_proj_a[0], "w_proj_b_f32": w_proj_b[0], "w_out_f32": w_out[0],
        "gains": jnp.concatenate([jnp.tile(q_norm_g[0], N_HEADS_A), jnp.tile(k_norm_g[0], N_KV_A),
                                  jnp.tile(o_norm_g[0], N_HEADS_B)])[None, :],
        "lb_logits": lb_logits,
    }

    mod_p, mod_s = _ada_call(c_prompt, c_sample, w_ada[0], b_ada)

    xs = x_sample[:, 0, :]
    (w["w_in"], w["w_merge"], g,
     qm, kn, vn, za, qb, kin, ib, zb) = _dec_in_call(xs, mod_s, consts, w)
    to_t = lambda c: jnp.transpose(c[0], (0, 2, 3, 1)).reshape(nb, KV_WIDTH, WINDOW)
    from_t = lambda c: jnp.transpose(
        c.reshape(nb, N_KV_A, HEAD_DIM_A, WINDOW), (0, 3, 1, 2))[None]
    att, ob, nk, nv, nst = _dec_mix_call(
        w["sinks"], qm.reshape(nb, N_HEADS_A, LANES), kn, vn, qb, kin, ib,
        to_t(cache_win_k), to_t(cache_win_v), state_hgrn[0], bt)
    y_s, w["w_proj_a"], w["w_proj_b"], w["w_out"] = _dec_out_call(
        xs, mod_s, att.reshape(nb, N_HEADS_A * LANES), za, ob, zb, g, consts, w)

    y_p, kwin, vwin, st_p = _prompt_call(x_prompt[0], mod_p, consts, w, tile)

    kv_shape = (1, 1, WINDOW, N_KV_A, HEAD_DIM_A)
    return (y_p[None], y_s[:, None, :],
            kwin.reshape(kv_shape), vwin.reshape(kv_shape), st_p[None, None],
            from_t(nk), from_t(nv), nst[None])
```

```python
import functools

import numpy as np
import jax
import jax.numpy as jnp
from jax import lax
from jax.experimental import pallas as pl
from jax.experimental.pallas import tpu as pltpu

D_MODEL = 1024
HEAD_DIM_A = 64
N_HEADS_A = 8
N_KV_A = 2
WIDTH_A = N_HEADS_A * HEAD_DIM_A
KV_WIDTH = N_KV_A * HEAD_DIM_A
WINDOW = 128
ROT_DIM = HEAD_DIM_A // 4
ROPE_THETA = 500000.0
HEAD_DIM_B = 128
N_HEADS_B = 4
WIDTH_B = N_HEADS_B * HEAD_DIM_B
EPS = 1e-6
OFF_QA = 0
OFF_KA = OFF_QA + WIDTH_A
OFF_VA = OFF_KA + KV_WIDTH
OFF_ZA = OFF_VA + KV_WIDTH
OFF_QB = OFF_ZA + WIDTH_A
OFF_FB = OFF_QB + WIDTH_B
OFF_IB = OFF_FB + WIDTH_B
OFF_ZB = OFF_IB + WIDTH_B
D_IN = OFF_ZB + WIDTH_B

LANES = 128
BLK = 128
SUB = 8
LEVELS = (64, 32, 16, 8)
HGRN_FAST_MAX = 80.0
MXU_N = 256
NEG = -1e30
VMEM_LIMIT = 56 * 1024 * 1024
PROMPT_TILE = 256
DECODE_BATCH_TILE = 16
G_Q, G_K, G_O = 0, WIDTH_A, WIDTH_A + KV_WIDTH
G_END = G_O + WIDTH_B

BF = jnp.bfloat16
F32 = jnp.float32


def _dot(a, b):
    return jnp.dot(a, b, preferred_element_type=F32)


def _dot_nt(a, b):
    return lax.dot_general(a, b, (((1,), (1,)), ((), ())), preferred_element_type=F32)


def _dot_tn(a, b):
    return lax.dot_general(a, b, (((0,), (0,)), ((), ())), preferred_element_type=F32)


def _split_dot(a_f32, b_bf):
    hi = a_f32.astype(BF)
    lo = (a_f32 - hi.astype(F32)).astype(BF)
    return _dot(hi, b_bf) + _dot(lo, b_bf)


def _sigmoid(x):
    return 1.0 / (1.0 + jnp.exp(-x))


def _silu(x):
    return x * _sigmoid(x)


def _lower_bound(lb_logits):
    l0 = lb_logits[0:1, :]
    l1 = lb_logits[1:2, :]
    m = jnp.maximum(l0, l1)
    e0 = jnp.exp(l0 - m)
    e1 = jnp.exp(l1 - m)
    return e0 / (e0 + e1)


def _rope_tables(cb, sb, cl, sl, sgn_up, sgn_dn):
    c = cb * cl - sb * sl
    s = sb * cl + cb * sl
    return c, s * sgn_up, s * sgn_dn


def _rope(x, c, s_up, s_dn):
    return x * c + pltpu.roll(x, LANES - ROT_DIM // 2, 1) * s_up + pltpu.roll(x, ROT_DIM // 2, 1) * s_dn


def _modulation(mod_ref):
    return (mod_ref[:, 0:D_MODEL], mod_ref[:, D_MODEL:2 * D_MODEL],
            mod_ref[:, 2 * D_MODEL:3 * D_MODEL])


def _norm_modulate(x, ng, mod_ref):
    shift, scale, _ = _modulation(mod_ref)
    ms = jnp.mean(x * x, axis=-1, keepdims=True)
    return (x * lax.rsqrt(ms + EPS) * (ng * (1.0 + scale)) + shift).astype(BF)


def _head_norm_scale(x, seg_mean_bf):
    ms = _dot((x * x).astype(BF), seg_mean_bf)
    return lax.rsqrt(ms + EPS)


def _ada_kernel(cp_ref, cs_ref, w_ref, b_ref, qg_ref, kg_ref, og_ref, op_ref, os_ref, gains_ref):
    gains_ref[...] = jnp.concatenate(
        [qg_ref[...]] * N_HEADS_A + [kg_ref[...]] * N_KV_A + [og_ref[...]] * N_HEADS_B, axis=1)
    w = w_ref[...]
    w_hi = w.astype(BF)
    w_lo = (w - w_hi.astype(F32)).astype(BF)
    b = b_ref[...]

    ns, n_p = cs_ref.shape[0], cp_ref.shape[0]
    c = jnp.concatenate([cs_ref[...], cp_ref[...],
                         jnp.zeros(((-n_p) % SUB, D_MODEL), F32)], axis=0)
    c_hi = c.astype(BF)
    c_lo = (c - c_hi.astype(F32)).astype(BF)
    out = _dot(c_hi, w_hi) + (_dot(c_hi, w_lo) + _dot(c_lo, w_hi)) + b
    os_ref[...] = out[:ns, :]
    op_ref[...] = out[ns:ns + n_p, :]


def _ada_call(c_p, c_s, w_ada, b_ada, q_g, k_g, o_g):
    mp, ms = c_p.shape[0], c_s.shape[0]
    n = w_ada.shape[1]
    tn = 512
    const = lambda a: pl.BlockSpec(a.shape, lambda j: (0,) * a.ndim)
    return pl.pallas_call(
        _ada_kernel,
        grid=(n // tn,),
        in_specs=[const(c_p), const(c_s),
                  pl.BlockSpec((D_MODEL, tn), lambda j: (0, j)),
                  pl.BlockSpec((1, tn), lambda j: (0, j)),
                  const(q_g), const(k_g), const(o_g)],
        out_specs=[pl.BlockSpec((mp, tn), lambda j: (0, j)),
                   pl.BlockSpec((ms, tn), lambda j: (0, j)),
                   pl.BlockSpec((1, G_END), lambda j: (0, 0))],
        out_shape=[jax.ShapeDtypeStruct((mp, n), F32), jax.ShapeDtypeStruct((ms, n), F32),
                   jax.ShapeDtypeStruct((1, G_END), F32)],
        name="ada",
    )(c_p, c_s, w_ada, b_ada, q_g, k_g, o_g)


STAGE_B_COST = 3840


def _round_robin(*gens):
    results = [None] * len(gens)
    live = list(range(len(gens)))
    while live:
        for n in list(live):
            try:
                yield next(gens[n])
            except StopIteration as stop:
                results[n] = stop.value
                live.remove(n)
    return results


def _attn_block(q_blk, kcat, kcat_sw, vcat, vcat_sw, bias, sink_a, sink_b):
    lane = lax.broadcasted_iota(jnp.int32, (BLK, LANES), 1)
    lo = lane < HEAD_DIM_A
    chunks = [q_blk[:, c * LANES:(c + 1) * LANES] for c in range(4)]
    zero = jnp.zeros((BLK, LANES), F32)
    q_lo = [jnp.where(lo, c, zero).astype(BF) for c in chunks]
    q_hi = [jnp.where(lo, zero, c).astype(BF) for c in chunks]
    qa = jnp.concatenate([q_lo[0], q_lo[1], q_hi[2], q_hi[3]], axis=0)
    qb = jnp.concatenate([q_hi[0], q_hi[1], q_lo[2], q_lo[3]], axis=0)

    def probs(qs, kc, sink):
        s = _dot_nt(qs, kc) + bias
        m = jnp.maximum(jnp.max(s, axis=-1, keepdims=True), sink)
        p = jnp.exp(s - m)
        den = jnp.sum(p, axis=-1, keepdims=True) + jnp.exp(sink - m)
        return p.astype(BF), 1.0 / den

    pa, ra = probs(qa, kcat, sink_a)
    yield 200
    pb, rb = probs(qb, kcat_sw, sink_b)
    yield 200
    oa = _dot(pa, vcat) * ra
    yield 40
    ob = _dot(pb, vcat_sw) * rb
    yield 40
    r = lambda o, i: o[i * BLK:(i + 1) * BLK, :]
    out = jnp.concatenate([
        jnp.where(lo, r(oa, 0), r(ob, 0)),
        jnp.where(lo, r(oa, 1), r(ob, 1)),
        jnp.where(lo, r(ob, 2), r(oa, 2)),
        jnp.where(lo, r(ob, 3), r(oa, 3)),
    ], axis=1)
    yield 30
    return out


def _hgrn_gates(fb, lb, one_m_lb, tri_bf):
    sig = _sigmoid(fb)
    kin = one_m_lb * (1.0 - sig)
    f = lb + one_m_lb * sig
    cum = _split_dot_left(tri_bf, jnp.log(f))
    return kin, f, cum


def _hgrn_span_decay(cum):
    q = BLK // 4
    ends = [cum[(n + 1) * q - 1:(n + 1) * q, :] for n in range(4)]
    d = -ends[0]
    for n in range(1, 4):
        d = jnp.maximum(d, ends[n - 1] - ends[n])
    return d


def _hgrn_state_step(qb, kin, cum, ib, st_ref):
    q_dec = (qb * jnp.exp(cum)).astype(BF)
    last = cum[BLK - 1:BLK, :]
    k_dec = (kin * jnp.exp(last - cum)).astype(BF)
    v_bf = ib.astype(BF)
    outs = []
    for h in range(N_HEADS_B):
        sl = slice(h * LANES, (h + 1) * LANES)
        st = st_ref[h]
        outs.append(_dot_nt(q_dec[:, sl], st.astype(BF)))
        st_ref[h] = st * jnp.exp(last[:, sl]) + _dot_tn(v_bf[:, sl], k_dec[:, sl])
    return jnp.concatenate(outs, axis=1)


def _hgrn_apply(amats, ib):
    v_bf = ib.astype(BF)
    return jnp.concatenate(
        [_dot(amats[h].astype(BF), v_bf[:, h * LANES:(h + 1) * LANES])
         for h in range(N_HEADS_B)], axis=1)


def _recur_block_fast(qb, kin, cum, ib, st_ref):
    base = _hgrn_state_step(qb, kin, cum, ib, st_ref)
    yield 130
    half = BLK // 2
    row = lax.broadcasted_iota(jnp.int32, (BLK, 1), 0)
    upper = row >= half
    piv = cum[half - 1:half, :]
    w_lvl = jnp.exp(jnp.concatenate([piv - cum[:half, :], cum[half:, :] - piv], axis=0))
    p_lvl = (jnp.where(upper, qb, kin) * w_lvl).astype(BF)
    mid = jnp.where(upper, cum[half + half // 2 - 1:half + half // 2, :],
                    cum[half // 2 - 1:half // 2, :])
    e_mid = cum - mid
    q_mid = (qb * jnp.exp(e_mid)).astype(BF)
    k_mid = (kin * jnp.exp(-e_mid)).astype(BF)
    yield 230

    ri = lax.broadcasted_iota(jnp.int32, (BLK, BLK), 0)
    ci = lax.broadcasted_iota(jnp.int32, (BLK, BLK), 1)
    same_half_causal = ((ri >= half) == (ci >= half)) & (ci <= ri)
    cross = (ri >= half) & (ci < half)
    amats = []
    for h in range(N_HEADS_B):
        sl = slice(h * LANES, (h + 1) * LANES)
        a_mid = _dot_nt(q_mid[:, sl], k_mid[:, sl])
        a_lvl = _dot_nt(p_lvl[:, sl], p_lvl[:, sl])
        amats.append(jnp.where(same_half_causal, a_mid, jnp.where(cross, a_lvl, 0.0)))
    yield 60
    full = base + _hgrn_apply(amats, ib)
    yield 40
    return base, full


def _hgrn_intra_robust(qb, kin, f, cum, ib, lvl_mask_ref):
    row = lax.broadcasted_iota(jnp.int32, (BLK, 1), 0)
    lvl_ops = []
    for b in LEVELS:
        pieces = []
        for r0 in range(0, BLK, 2 * b):
            piv = cum[r0 + b - 1:r0 + b, :]
            pieces.append(piv - cum[r0:r0 + b, :])
            pieces.append(cum[r0 + b:r0 + 2 * b, :] - piv)
        w = jnp.exp(jnp.concatenate(pieces, axis=0))
        second = (row & b) != 0
        lvl_ops.append((jnp.where(second, qb, kin) * w).astype(BF))

    n8 = BLK // SUB
    q3 = qb.reshape(n8, SUB, WIDTH_B)
    k3 = kin.reshape(n8, SUB, WIDTH_B)
    f3 = f.reshape(n8, SUB, WIDTH_B)
    v3 = ib.reshape(n8, SUB, WIDTH_B)
    subl = lax.broadcasted_iota(jnp.int32, (n8, SUB, 1), 1)

    def head(x, h):
        return x[..., h * LANES:(h + 1) * LANES]

    g = q3 * k3
    acc = [jnp.sum(head(g, h), axis=-1, keepdims=True) * head(v3, h) for h in range(N_HEADS_B)]
    dec = jnp.ones_like(f3)
    kd = k3
    vd = v3
    for d in range(1, SUB):
        dec = f3 * pltpu.roll(dec, 1, 1)
        kd = pltpu.roll(kd, 1, 1)
        vd = pltpu.roll(vd, 1, 1)
        g = q3 * kd * dec
        ok = subl >= d
        for h in range(N_HEADS_B):
            a = jnp.where(ok, jnp.sum(head(g, h), axis=-1, keepdims=True), 0.0)
            acc[h] = acc[h] + a * head(vd, h)

    amats = []
    for h in range(N_HEADS_B):
        sl = slice(h * LANES, (h + 1) * LANES)
        amat = jnp.zeros((BLK, BLK), F32)
        for li in range(len(LEVELS)):
            p = lvl_ops[li][:, sl]
            amat = amat + lvl_mask_ref[li] * _dot_nt(p, p)
        amats.append(amat)
    diag = jnp.concatenate([acc[h].reshape(BLK, LANES) for h in range(N_HEADS_B)], axis=1)
    return _hgrn_apply(amats, ib) + diag


def _split_dot_left(a_bf, b_f32):
    hi = b_f32.astype(BF)
    lo = (b_f32 - hi.astype(F32)).astype(BF)
    return _dot(a_bf, hi) + _dot(a_bf, lo)


def _branch_b_out(o, zb, og):
    outs = []
    for h in range(N_HEADS_B):
        sl = slice(h * LANES, (h + 1) * LANES)
        oh = o[:, sl]
        ms = jnp.mean(oh * oh, axis=-1, keepdims=True)
        outs.append(oh * lax.rsqrt(ms + EPS))
    return jnp.concatenate(outs, axis=1) * og * _silu(zb)


def _prompt_kernel(sinks_ref,
                   x0_ref, x1_ref, x2_ref, mod_ref, ng_ref,
                   w_in_ref, w_mg_ref, b_mg_ref, w_pa_ref, w_pb_ref, w_out_ref,
                   gains_ref, lbl_ref,
                   cb_ref, sb_ref, cl_ref, sl_ref, sgn_ref,
                   seg_ref, tri_ref, bias_ref, lmask_ref,
                   y_ref, kwin_ref, vwin_ref, state_ref,
                   st_ref, kprev_ref, kprev_sw_ref, vprev_ref, vprev_sw_ref,
                   p0_ref, p1_ref, g_ref, h0_ref, h1_ref, ob_ref, obase_ref,
                   *, tile, n_steps):
    s = pl.program_id(0)
    nblk = tile // BLK

    def stage_a(x_ref, h_ref, p_ref):
        def prep():
            h_ref[...] = _norm_modulate(x_ref[...], ng_ref[...], mod_ref)

        def proj_chunk(c):
            def run():
                cs = slice(c * MXU_N, (c + 1) * MXU_N)
                p_ref[:, cs] = _dot(h_ref[...], w_in_ref[:, cs])
            return run

        return [prep] + [proj_chunk(c) for c in range(D_IN // MXU_N)]

    def gate_chunks(h_ref):
        def gate_chunk(c):
            def run():
                cs = slice(c * MXU_N, (c + 1) * MXU_N)
                g_ref[:, cs] = _sigmoid(_dot(h_ref[...], w_mg_ref[:, cs]) + b_mg_ref[:, cs])
            return run

        return [gate_chunk(c) for c in range(2 * D_MODEL // MXU_N)]

    def phase(h_cur_ref, a_next, b_parts):
        gc = gate_chunks(h_cur_ref)
        interleave(gc[:2] + a_next[:1] + gc[2:] + a_next[1:], b_parts, lead=12)

    def stage_b(x_ref, p_ref, t_idx, y_rows):
        blocks = [slice(blk * BLK, (blk + 1) * BLK) for blk in range(nblk)]

        def qkv_stream():
            seg = seg_ref[...]
            rc, ru, rd = _rope_tables(cb_ref[pl.ds(t_idx, 1), :], sb_ref[pl.ds(t_idx, 1), :],
                                      cl_ref[...], sl_ref[...], sgn_ref[0:1, :], sgn_ref[1:2, :])
            qa = p_ref[:, OFF_QA:OFF_QA + WIDTH_A]
            qa = qa * _head_norm_scale(qa, seg) * (gains_ref[:, G_Q:G_K] * (HEAD_DIM_A ** -0.5))
            yield 160
            qa = jnp.concatenate(
                [_rope(qa[:, c * LANES:(c + 1) * LANES], rc, ru, rd) for c in range(4)], axis=1)
            yield 160
            ka = p_ref[:, OFF_KA:OFF_KA + KV_WIDTH]
            ka = ka * _head_norm_scale(ka, seg[:KV_WIDTH, :KV_WIDTH]) * gains_ref[:, G_K:G_O]
            ka = _rope(ka, rc, ru, rd)
            va = p_ref[:, OFF_VA:OFF_VA + KV_WIDTH]
            ka_sw = pltpu.roll(ka, HEAD_DIM_A, 1)
            va_sw = pltpu.roll(va, HEAD_DIM_A, 1)
            rows4 = lax.broadcasted_iota(jnp.int32, (4 * BLK, 1), 0) // BLK

            def sink_col(heads):
                col = jnp.zeros((4 * BLK, 1), F32)
                for n, hd in enumerate(heads):
                    col = jnp.where(rows4 == n, sinks_ref[hd], col)
                return col
            kwin_ref[...] = ka[tile - WINDOW:, :]
            vwin_ref[...] = va[tile - WINDOW:, :]
            res = (qa, ka, va, ka_sw, va_sw, sink_col((0, 2, 5, 7)), sink_col((1, 3, 4, 6)))
            yield 120
            return res

        def gates_stream():
            lb = _lower_bound(lbl_ref[...])
            tri = tri_ref[...]
            gates = []
            half_w = WIDTH_B // 2
            for rs in blocks:
                halves = []
                for c0 in (0, half_w):
                    cs = slice(c0, c0 + half_w)
                    halves.append(_hgrn_gates(p_ref[rs, OFF_FB + c0:OFF_FB + c0 + half_w],
                                              lb[:, cs], 1.0 - lb[:, cs], tri))
                    yield 240
                gates.append(tuple(jnp.concatenate(pair, axis=1) for pair in zip(*halves)))
            span = _hgrn_span_decay(gates[0][2])
            for g in gates[1:]:
                span = jnp.maximum(span, _hgrn_span_decay(g[2]))
            mild = jnp.max(span) < HGRN_FAST_MAX
            yield 10
            return gates, mild

        (qa, ka, va, ka_sw, va_sw, sink_a, sink_b), (gates, mild) = (
            yield from _round_robin(qkv_stream(), gates_stream()))

        kv_refs = (kprev_ref, kprev_sw_ref, vprev_ref, vprev_sw_ref)
        kv_new = [a.astype(BF) for a in (ka, ka_sw, va, va_sw)]
        kv_old = [r[...] for r in kv_refs]
        for r, a in zip(kv_refs, kv_new):
            r[...] = a[blocks[-1]]

        def attend(blk, rs):
            cats = [jnp.concatenate([old if blk == 0 else new[blocks[blk - 1]], new[rs]], axis=0)
                    for old, new in zip(kv_old, kv_new)]
            bias = bias_ref[jnp.where(t_idx == 0, 0, 1)] if blk == 0 else bias_ref[1]
            return (yield from _attn_block(qa[rs], *cats, bias, sink_a, sink_b))

        def recur(blk, rs):
            kin, _, cum = gates[blk]
            qb, ib = p_ref[rs, OFF_QB:OFF_QB + WIDTH_B], p_ref[rs, OFF_IB:OFF_IB + WIDTH_B]
            base, full = yield from _recur_block_fast(qb, kin, cum, ib, st_ref)
            obase_ref[rs, :] = base
            ob_ref[rs, :] = full

        ya_parts = []
        for blk, rs in enumerate(blocks):
            res = yield from _round_robin(attend(blk, rs), recur(blk, rs))
            ya_parts.append(res[0])

        @pl.when(jnp.logical_not(mild))
        def _():
            for rs, (kin, f, cum) in zip(blocks, gates):
                qb, ib = p_ref[rs, OFF_QB:OFF_QB + WIDTH_B], p_ref[rs, OFF_IB:OFF_IB + WIDTH_B]
                ob_ref[rs, :] = obase_ref[rs, :] + _hgrn_intra_robust(
                    qb, kin, f, cum, ib, lmask_ref)
        yield 0

        ya_bf = (jnp.concatenate(ya_parts, axis=0)
                 * _silu(p_ref[:, OFF_ZA:OFF_ZA + WIDTH_A])).astype(BF)
        yield 160
        yb_bf = _branch_b_out(ob_ref[...], p_ref[:, OFF_ZB:OFF_ZB + WIDTH_B],
                              gains_ref[:, G_O:G_END]).astype(BF)
        yield 330

        y = (g_ref[:, :D_MODEL] * _dot(ya_bf, w_pa_ref[...])
             + g_ref[:, D_MODEL:] * _dot(yb_bf, w_pb_ref[...]))
        y_ref[y_rows, :] = x_ref[...] + _modulation(mod_ref)[2] * _dot(y.astype(BF), w_out_ref[...])
        yield 0

    def interleave(a_thunks, b_parts, lead):
        for th in a_thunks[:lead]:
            th()
        acc, done = 0, lead
        for cost in b_parts:
            acc += cost
            upto = lead + int(round((len(a_thunks) - lead) * min(acc, STAGE_B_COST) / STAGE_B_COST))
            for th in a_thunks[done:upto]:
                th()
            done = upto
        assert done == len(a_thunks) and acc == STAGE_B_COST, (done, acc)

    @pl.when(s == 0)
    def _():
        st_ref[...] = jnp.zeros_like(st_ref)
        for r in (kprev_ref, kprev_sw_ref, vprev_ref, vprev_sw_ref):
            r[...] = jnp.zeros_like(r)
        for th in stage_a(x0_ref, h0_ref, p0_ref):
            th()

    phase(h0_ref, stage_a(x1_ref, h1_ref, p1_ref),
          stage_b(x0_ref, p0_ref, 2 * s, slice(0, tile)))
    phase(h1_ref, stage_a(x2_ref, h0_ref, p0_ref),
          stage_b(x1_ref, p1_ref, 2 * s + 1, slice(tile, 2 * tile)))

    @pl.when(s == n_steps - 1)
    def _():
        for hd in range(N_HEADS_B):
            state_ref[hd] = st_ref[hd].T
        kwin_ref[...] = kwin_ref[...].T
        vwin_ref[...] = vwin_ref[...].T


def _const_spec(shape):
    nd = len(shape)
    return pl.BlockSpec(shape, lambda i, *_: (0,) * nd, pipeline_mode=pl.Buffered(1))


def _prompt_call(x, mod, consts, w, tile):
    t = x.shape[0]
    n_tiles = t // tile
    n_steps = n_tiles // 2
    assert n_steps * 2 * tile == t
    row = lambda n: _const_spec((1, n))
    in_specs = [
        pl.BlockSpec((tile, D_MODEL), lambda i, *_: (2 * i, 0)),
        pl.BlockSpec((tile, D_MODEL), lambda i, *_: (2 * i + 1, 0)),
        pl.BlockSpec((tile, D_MODEL), lambda i, *_: (jnp.minimum(2 * i + 2, n_tiles - 1), 0)),
        row(3 * D_MODEL), row(D_MODEL),
        _const_spec((D_MODEL, D_IN)), _const_spec((D_MODEL, 2 * D_MODEL)), row(2 * D_MODEL),
        _const_spec((WIDTH_A, D_MODEL)), _const_spec((WIDTH_B, D_MODEL)),
        _const_spec((D_MODEL, D_MODEL)),
        row(G_END), _const_spec((2, WIDTH_B)),
        _const_spec((n_tiles, LANES)), _const_spec((n_tiles, LANES)),
        _const_spec((tile, LANES)), _const_spec((tile, LANES)), _const_spec((2, LANES)),
        _const_spec((WIDTH_A, WIDTH_A)), _const_spec((BLK, BLK)),
        _const_spec((2, 4 * BLK, 2 * BLK)), _const_spec((len(LEVELS), BLK, BLK)),
    ]
    out_specs = [
        pl.BlockSpec((2 * tile, D_MODEL), lambda i, *_: (i, 0)),
        pl.BlockSpec((WINDOW, KV_WIDTH), lambda i, *_: (0, 0)),
        pl.BlockSpec((WINDOW, KV_WIDTH), lambda i, *_: (0, 0)),
        pl.BlockSpec((N_HEADS_B, HEAD_DIM_B, HEAD_DIM_B), lambda i, *_: (0, 0, 0)),
    ]
    out_shape = [
        jax.ShapeDtypeStruct((t, D_MODEL), F32),
        jax.ShapeDtypeStruct((WINDOW, KV_WIDTH), F32),
        jax.ShapeDtypeStruct((WINDOW, KV_WIDTH), F32),
        jax.ShapeDtypeStruct((N_HEADS_B, HEAD_DIM_B, HEAD_DIM_B), F32),
    ]
    scratch = [
        pltpu.VMEM((N_HEADS_B, HEAD_DIM_B, HEAD_DIM_B), F32),
        pltpu.VMEM((BLK, KV_WIDTH), BF), pltpu.VMEM((BLK, KV_WIDTH), BF),
        pltpu.VMEM((BLK, KV_WIDTH), BF), pltpu.VMEM((BLK, KV_WIDTH), BF),
        pltpu.VMEM((tile, D_IN), F32), pltpu.VMEM((tile, D_IN), F32),
        pltpu.VMEM((tile, 2 * D_MODEL), F32),
        pltpu.VMEM((tile, D_MODEL), BF), pltpu.VMEM((tile, D_MODEL), BF),
        pltpu.VMEM((tile, WIDTH_B), F32), pltpu.VMEM((tile, WIDTH_B), F32),
    ]
    return pl.pallas_call(
        functools.partial(_prompt_kernel, tile=tile, n_steps=n_steps),
        grid_spec=pltpu.PrefetchScalarGridSpec(
            num_scalar_prefetch=1, grid=(n_steps,),
            in_specs=in_specs, out_specs=out_specs, scratch_shapes=scratch),
        out_shape=out_shape,
        compiler_params=pltpu.CompilerParams(
            dimension_semantics=("arbitrary",), vmem_limit_bytes=VMEM_LIMIT),
        name="prompt_layer",
    )(w["sinks"], x, x, x, mod, w["norm_g"],
      w["w_in"], w["w_merge"], w["b_merge"], w["w_proj_a"], w["w_proj_b"], w["w_out"],
      w["gains"], w["lb_logits"],
      consts["rope_cb"], consts["rope_sb"], consts["rope_cl"], consts["rope_sl"],
      consts["rope_sgn"],
      consts["seg"], consts["tri"], consts["bias"], consts["lmask"])


def _dec_in_kernel(x_ref, mod_ref, ng_ref, w_in_ref, w_mg_ref, b_mg_ref, gains_ref, lbl_ref,
                   rc_ref, ru_ref, rd_ref, seg_ref, perm_ref,
                   w_in_bf_ref, w_mg_bf_ref, g_ref,
                   qm_ref, kn_ref, vn_ref, za_ref, qb_ref, kin_ref, ib_ref, zb_ref,
                   h_ref, p_ref):
    j = pl.program_id(0)
    half_in = D_IN // 2

    @pl.when(j == 0)
    def _():
        h_ref[...] = _norm_modulate(x_ref[:, 0, :], ng_ref[...], mod_ref)

    w_in_bf = w_in_ref[...].astype(BF)
    w_mg_bf = w_mg_ref[...].astype(BF)
    w_in_bf_ref[...] = w_in_bf
    w_mg_bf_ref[...] = w_mg_bf
    h_bf = h_ref[...]
    g_ref[...] = _sigmoid(_dot(h_bf, w_mg_bf) + b_mg_ref[...])
    part = _dot(h_bf, w_in_bf)

    @pl.when(j == 0)
    def _():
        p_ref[:, :half_in] = part

    @pl.when(j == 1)
    def _():
        p_ref[:, half_in:] = part
        seg = seg_ref[...]
        rc, ru, rd = rc_ref[...], ru_ref[...], rd_ref[...]
        qa = p_ref[:, OFF_QA:OFF_QA + WIDTH_A]
        qa = qa * _head_norm_scale(qa, seg) * (gains_ref[:, G_Q:G_K] * (HEAD_DIM_A ** -0.5))
        qa = jnp.concatenate(
            [_rope(qa[:, c * LANES:(c + 1) * LANES], rc, ru, rd) for c in range(4)], axis=1)
        qm = _dot(qa.astype(BF), perm_ref[...])
        for hd in range(N_HEADS_A):
            qm_ref[:, hd, :] = qm[:, hd * LANES:(hd + 1) * LANES]
        ka = p_ref[:, OFF_KA:OFF_KA + KV_WIDTH]
        ka = ka * _head_norm_scale(ka, seg[:KV_WIDTH, :KV_WIDTH]) * gains_ref[:, G_K:G_O]
        kn_ref[...] = _rope(ka, rc, ru, rd)
        vn_ref[...] = p_ref[:, OFF_VA:OFF_VA + KV_WIDTH]
        za_ref[...] = p_ref[:, OFF_ZA:OFF_ZA + WIDTH_A]
        qb_ref[...] = p_ref[:, OFF_QB:OFF_QB + WIDTH_B]
        lb = _lower_bound(lbl_ref[...])
        kin_ref[...] = (1.0 - lb) * (1.0 - _sigmoid(p_ref[:, OFF_FB:OFF_FB + WIDTH_B]))
        ib_ref[...] = p_ref[:, OFF_IB:OFF_IB + WIDTH_B]
        zb_ref[...] = p_ref[:, OFF_ZB:OFF_ZB + WIDTH_B]


def _dec_in_call(x, mod, consts, w):
    b = x.shape[0]
    n_steps = 2
    half_in, half_mg = D_IN // n_steps, 2 * D_MODEL // n_steps
    assert half_in % LANES == 0
    const = lambda a: pl.BlockSpec(a.shape, lambda j: (0,) * a.ndim)
    cols = lambda rows, n: pl.BlockSpec((rows, n), lambda j: (0, j))
    row_out = lambda n: pl.BlockSpec((b, n), lambda j: (0, 0))
    f = lambda n: jax.ShapeDtypeStruct((b, n), F32)
    widths = [KV_WIDTH, KV_WIDTH, WIDTH_A, WIDTH_B, WIDTH_B, WIDTH_B, WIDTH_B]
    qm_shape = (b, N_HEADS_A, LANES)
    small = [w["norm_g"]]
    tail = [w["gains"], w["lb_logits"], consts["rope_c1"], consts["rope_u1"], consts["rope_d1"],
            consts["seg"], consts["perm"]]
    return pl.pallas_call(
        _dec_in_kernel,
        grid=(n_steps,),
        in_specs=[const(x), const(mod)] + [const(a) for a in small]
        + [cols(D_MODEL, half_in), cols(D_MODEL, half_mg), cols(1, half_mg)]
        + [const(a) for a in tail],
        out_specs=[cols(D_MODEL, half_in), cols(D_MODEL, half_mg), cols(b, half_mg),
                   pl.BlockSpec(qm_shape, lambda j: (0, 0, 0))]
        + [row_out(n) for n in widths],
        out_shape=[jax.ShapeDtypeStruct((D_MODEL, D_IN), BF),
                   jax.ShapeDtypeStruct((D_MODEL, 2 * D_MODEL), BF), f(2 * D_MODEL),
                   jax.ShapeDtypeStruct(qm_shape, F32)]
        + [f(n) for n in widths],
        scratch_shapes=[pltpu.VMEM((b, D_MODEL), BF), pltpu.VMEM((b, D_IN), F32)],
        compiler_params=pltpu.CompilerParams(
            dimension_semantics=("arbitrary",), vmem_limit_bytes=VMEM_LIMIT),
        name="decode_in",
    )(x, mod, *small, w["w_in_f32"], w["w_merge_f32"], w["b_merge"], *tail)


def _dec_mix_kernel(sinks_ref, qm_ref, kn_ref, vn_ref, qb_ref, kin_ref, ib_ref,
                    ck_ref, cv_ref, st_ref,
                    att_ref, ob_ref, nk_ref, nv_ref, nst_ref, *, bt):
    nh = N_HEADS_A
    s = jnp.concatenate([_dot(qm_ref[j].astype(BF), ck_ref[j].astype(BF))
                         for j in range(bt)], axis=0)
    key = lax.broadcasted_iota(jnp.int32, (bt * nh, WINDOW), 1)
    s = jnp.where(key == 0, NEG, s)
    rep = lambda r: jnp.broadcast_to(r[...][:, None, :], (bt, nh, r.shape[-1])).reshape(
        bt * nh, r.shape[-1])
    kn_rows, vn_rows = rep(kn_ref), rep(vn_ref)
    head = lax.broadcasted_iota(jnp.int32, (bt * nh, 1), 0) % nh
    sink = jnp.zeros((bt * nh, 1), F32)
    for hd in range(nh):
        sink = jnp.where(head == hd, sinks_ref[hd], sink)
    s_new = jnp.sum(qm_ref[...].reshape(bt * nh, LANES) * kn_rows, axis=-1, keepdims=True)
    m = jnp.maximum(jnp.maximum(jnp.max(s, axis=-1, keepdims=True), s_new), sink)
    p = jnp.exp(s - m)
    p_new = jnp.exp(s_new - m)
    den = jnp.sum(p, axis=-1, keepdims=True) + p_new + jnp.exp(sink - m)
    p_bf = p.astype(BF)
    pv = jnp.concatenate([_dot_nt(p_bf[j * nh:(j + 1) * nh, :], cv_ref[j].astype(BF))
                          for j in range(bt)], axis=0)
    att_ref[...] = ((pv + p_new * vn_rows) * (1.0 / den)).reshape(bt, nh, LANES)
    kn_t, vn_t = kn_ref[...].T, vn_ref[...].T
    newest = lax.broadcasted_iota(jnp.int32, (KV_WIDTH, WINDOW), 1) == WINDOW - 1
    for j in range(bt):
        nk_ref[j] = jnp.where(newest, kn_t[:, j:j + 1], pltpu.roll(ck_ref[j], WINDOW - 1, 1))
        nv_ref[j] = jnp.where(newest, vn_t[:, j:j + 1], pltpu.roll(cv_ref[j], WINDOW - 1, 1))

    for hd in range(N_HEADS_B):
        sl = slice(hd * LANES, (hd + 1) * LANES)
        kin_t = kin_ref[:, sl].T
        q_bf = qb_ref[:, sl].astype(BF)
        for j in range(bt):
            kcol = kin_t[:, j:j + 1]
            st = st_ref[j, hd]
            new = st - kcol * (st - ib_ref[j:j + 1, sl])
            nst_ref[j, hd] = new
            ob_ref[j:j + 1, sl] = _dot(q_bf, new.astype(BF))[j:j + 1, :]


def _dec_mix_call(sinks, qm, kn, vn, qb, kin, ib, cache_k, cache_v, state, bt):
    b = kn.shape[0]
    rows = lambda n: pl.BlockSpec((bt, n), lambda i, *_: (i, 0))
    cache_spec = pl.BlockSpec((bt, WINDOW, KV_WIDTH), lambda i, *_: (i, 0, 0))
    st_spec = pl.BlockSpec((bt, N_HEADS_B, HEAD_DIM_B, HEAD_DIM_B), lambda i, *_: (i, 0, 0, 0))
    qm_spec = pl.BlockSpec((bt, N_HEADS_A, LANES), lambda i, *_: (i, 0, 0))
    return pl.pallas_call(
        functools.partial(_dec_mix_kernel, bt=bt),
        grid_spec=pltpu.PrefetchScalarGridSpec(
            num_scalar_prefetch=1, grid=(b // bt,),
            in_specs=[qm_spec, rows(KV_WIDTH), rows(KV_WIDTH), rows(WIDTH_B), rows(WIDTH_B),
                      rows(WIDTH_B), cache_spec, cache_spec, st_spec],
            out_specs=[qm_spec, rows(WIDTH_B), cache_spec, cache_spec, st_spec]),
        out_shape=[jax.ShapeDtypeStruct((b, N_HEADS_A, LANES), F32),
                   jax.ShapeDtypeStruct((b, WIDTH_B), F32),
                   jax.ShapeDtypeStruct(cache_k.shape, F32),
                   jax.ShapeDtypeStruct(cache_v.shape, F32),
                   jax.ShapeDtypeStruct(state.shape, F32)],
        compiler_params=pltpu.CompilerParams(
            dimension_semantics=("arbitrary",), vmem_limit_bytes=VMEM_LIMIT),
        name="decode_mix",
    )(sinks, qm, kn, vn, qb, kin, ib, cache_k, cache_v, state)


def _dec_out_kernel(x_ref, mod_ref, att_ref, za_ref, ob_ref, zb_ref, g_ref, gains_ref,
                    permt_ref, w_pa_ref, w_pb_ref, w_out_ref,
                    y_ref, w_pa_bf_ref, w_pb_bf_ref, w_out_bf_ref):
    w_pa, w_pb, w_out = (r[...].astype(BF) for r in (w_pa_ref, w_pb_ref, w_out_ref))
    w_pa_bf_ref[...] = w_pa
    w_pb_bf_ref[...] = w_pb
    w_out_bf_ref[...] = w_out
    att = sum(_split_dot(att_ref[:, hd, :], permt_ref[hd * LANES:(hd + 1) * LANES, :])
              for hd in range(N_HEADS_A))
    ya = att * _silu(za_ref[...])
    yb = _branch_b_out(ob_ref[...], zb_ref[...], gains_ref[:, G_O:G_END])
    g = g_ref[...]
    y = (g[:, :D_MODEL] * _dot(ya.astype(BF), w_pa) + g[:, D_MODEL:] * _dot(yb.astype(BF), w_pb))
    y_ref[:, 0, :] = x_ref[:, 0, :] + _modulation(mod_ref)[2] * _dot(y.astype(BF), w_out)


def _dec_out_call(x, mod, att, za, ob, zb, g, consts, w):
    bf = lambda a: jax.ShapeDtypeStruct(a.shape, BF)
    ws = (w["w_proj_a_f32"], w["w_proj_b_f32"], w["w_out_f32"])
    return pl.pallas_call(
        _dec_out_kernel,
        out_shape=[jax.ShapeDtypeStruct(x.shape, F32)] + [bf(a) for a in ws],
        compiler_params=pltpu.CompilerParams(vmem_limit_bytes=VMEM_LIMIT),
        name="decode_out",
    )(x, mod, att, za, ob, zb, g, w["gains"], consts["permt"], *ws)


def _rope_consts(n_tiles, tile, past_len):
    half = ROT_DIM // 2
    inv = ROPE_THETA ** (-np.arange(0, ROT_DIM, 2, dtype=np.float64) / ROT_DIM)
    e = np.arange(LANES) % HEAD_DIM_A
    rot = e < ROT_DIM

    def tables(pos):
        ang = np.asarray(pos, np.float64)[:, None] * inv[e % half][None, :]
        return np.where(rot, np.cos(ang), 1.0), np.where(rot, np.sin(ang), 0.0)

    sgn = np.stack([np.where(e < half, -1.0, 0.0),
                    np.where(rot & (e >= half), 1.0, 0.0)])
    cb, sb = tables(np.arange(n_tiles) * tile)
    cl, sl = tables(np.arange(tile))
    c1, s1 = tables([past_len])
    f = lambda a: jnp.asarray(a, F32)
    return {"rope_cb": f(cb), "rope_sb": f(sb), "rope_cl": f(cl), "rope_sl": f(sl),
            "rope_sgn": f(sgn), "rope_c1": f(c1), "rope_u1": f(s1 * sgn[0:1]),
            "rope_d1": f(s1 * sgn[1:2])}


def _static_consts():
    seg = np.kron(np.eye(N_HEADS_A), np.full((HEAD_DIM_A, HEAD_DIM_A), 1.0 / HEAD_DIM_A))
    tri = np.tril(np.ones((BLK, BLK)))
    r = np.arange(4 * BLK)[:, None] % BLK
    c = np.arange(2 * BLK)[None, :]
    ok_prev = (c < BLK) & (c > r)
    ok_cur = (c >= BLK) & (c - BLK <= r)
    bias = np.stack([np.where(ok_cur, 0.0, NEG), np.where(ok_prev | ok_cur, 0.0, NEG)])
    i = np.arange(BLK)[:, None]
    j = np.arange(BLK)[None, :]
    lmask = np.stack([((i // (2 * b)) == (j // (2 * b))) & ((i & b) != 0) & ((j & b) == 0)
                      for b in LEVELS]).astype(np.float32)
    perm = np.zeros((WIDTH_A, N_HEADS_A * LANES), np.float32)
    for hd in range(N_HEADS_A):
        kvh = hd // (N_HEADS_A // N_KV_A)
        for d in range(HEAD_DIM_A):
            perm[hd * HEAD_DIM_A + d, hd * LANES + kvh * HEAD_DIM_A + d] = 1.0
    return {
        "seg": jnp.asarray(seg, BF), "tri": jnp.asarray(tri, BF),
        "bias": jnp.asarray(bias, F32), "lmask": jnp.asarray(lmask, F32),
        "perm": jnp.asarray(perm, BF), "permt": jnp.asarray(perm.T, BF),
    }


def kernel(x_prompt, x_sample, cache_win_k, cache_win_v, state_hgrn, c_prompt, c_sample,
           w_ada, b_ada, norm_g, w_in, q_norm_g, k_norm_g, sinks, lb_logits, o_norm_g,
           w_merge, b_merge, w_proj_a, w_proj_b, w_out):
    depth = w_in.shape[0]
    assert depth == 1 and x_prompt.shape[0] == 1 and x_sample.shape[1] == 1
    t = x_prompt.shape[1]
    nb = x_sample.shape[0]
    past_len = t
    tile = PROMPT_TILE
    bt = DECODE_BATCH_TILE

    consts = _static_consts()
    consts.update(_rope_consts(t // tile, tile, past_len))

    w = {
        "sinks": sinks[0], "norm_g": norm_g,
        "w_in_f32": w_in[0], "w_merge_f32": w_merge[0], "b_merge": b_merge,
        "w_proj_a_f32": w_proj_a[0], "w_proj_b_f32": w_proj_b[0], "w_out_f32": w_out[0],
        "lb_logits": lb_logits,
    }

    mod_p, mod_s, w["gains"] = _ada_call(c_prompt, c_sample, w_ada[0], b_ada,
                                         q_norm_g, k_norm_g, o_norm_g)

    xs = x_sample
    (w["w_in"], w["w_merge"], g,
     qm, kn, vn, za, qb, kin, ib, zb) = _dec_in_call(xs, mod_s, consts, w)
    to_t = lambda c: jnp.transpose(c[0], (0, 2, 3, 1)).reshape(nb, KV_WIDTH, WINDOW)
    from_t = lambda c: jnp.transpose(
        c.reshape(-1, N_KV_A, HEAD_DIM_A, WINDOW), (0, 3, 1, 2))[None]
    att, ob, nk, nv, nst = _dec_mix_call(
        w["sinks"], qm, kn, vn, qb, kin, ib,
        to_t(cache_win_k), to_t(cache_win_v), state_hgrn[0], bt)
    y_s, w["w_proj_a"], w["w_proj_b"], w["w_out"] = _dec_out_call(
        xs, mod_s, att, za, ob, zb, g, consts, w)

    y_p, kwin, vwin, st_p = _prompt_call(x_prompt[0], mod_p, consts, w, tile)

    return (y_p[None], y_s,
            from_t(kwin[None]), from_t(vwin[None]), st_p[None, None],
            from_t(nk), from_t(nv), nst[None])
```

```python
import functools

import numpy as np
import jax
import jax.numpy as jnp
from jax import lax
from jax.experimental import pallas as pl
from jax.experimental.pallas import tpu as pltpu

D_MODEL = 1024
HEAD_DIM_A = 64
N_HEADS_A = 8
N_KV_A = 2
WIDTH_A = N_HEADS_A * HEAD_DIM_A
KV_WIDTH = N_KV_A * HEAD_DIM_A
WINDOW = 128
ROT_DIM = HEAD_DIM_A // 4
ROPE_THETA = 500000.0
HEAD_DIM_B = 128
N_HEADS_B = 4
WIDTH_B = N_HEADS_B * HEAD_DIM_B
EPS = 1e-6
OFF_QA = 0
OFF_KA = OFF_QA + WIDTH_A
OFF_VA = OFF_KA + KV_WIDTH
OFF_ZA = OFF_VA + KV_WIDTH
OFF_QB = OFF_ZA + WIDTH_A
OFF_FB = OFF_QB + WIDTH_B
OFF_IB = OFF_FB + WIDTH_B
OFF_ZB = OFF_IB + WIDTH_B
D_IN = OFF_ZB + WIDTH_B

LANES = 128
BLK = 128
SUB = 8
LEVELS = (64, 32, 16, 8)
HGRN_FAST_MAX = 80.0
MXU_N = 256
NEG = -1e30
VMEM_LIMIT = 56 * 1024 * 1024
PROMPT_TILE = 256
PROMPT_TILES_PER_STEP = 4
DECODE_BATCH_TILE = 16
G_Q, G_K, G_O = 0, WIDTH_A, WIDTH_A + KV_WIDTH
G_END = G_O + WIDTH_B

BF = jnp.bfloat16
F32 = jnp.float32


def _dot(a, b):
    return jnp.dot(a, b, preferred_element_type=F32)


def _dot_nt(a, b):
    return lax.dot_general(a, b, (((1,), (1,)), ((), ())), preferred_element_type=F32)


def _dot_tn(a, b):
    return lax.dot_general(a, b, (((0,), (0,)), ((), ())), preferred_element_type=F32)


def _split_dot(a_f32, b_bf):
    hi = a_f32.astype(BF)
    lo = (a_f32 - hi.astype(F32)).astype(BF)
    return _dot(hi, b_bf) + _dot(lo, b_bf)


def _sigmoid(x):
    return 1.0 / (1.0 + jnp.exp(-x))


def _silu(x):
    return x * _sigmoid(x)


def _lower_bound(lb_logits):
    l0 = lb_logits[0:1, :]
    l1 = lb_logits[1:2, :]
    m = jnp.maximum(l0, l1)
    e0 = jnp.exp(l0 - m)
    e1 = jnp.exp(l1 - m)
    return e0 / (e0 + e1)


def _rope_tables(cb, sb, cl, sl, sgn_up, sgn_dn):
    c = cb * cl - sb * sl
    s = sb * cl + cb * sl
    return c, s * sgn_up, s * sgn_dn


def _rope(x, c, s_up, s_dn):
    return x * c + pltpu.roll(x, LANES - ROT_DIM // 2, 1) * s_up + pltpu.roll(x, ROT_DIM // 2, 1) * s_dn


def _modulation(mod_ref):
    return (mod_ref[:, 0:D_MODEL], mod_ref[:, D_MODEL:2 * D_MODEL],
            mod_ref[:, 2 * D_MODEL:3 * D_MODEL])


def _norm_modulate(x, ng, mod_ref):
    shift, scale, _ = _modulation(mod_ref)
    ms = jnp.mean(x * x, axis=-1, keepdims=True)
    return (x * lax.rsqrt(ms + EPS) * (ng * (1.0 + scale)) + shift).astype(BF)


def _head_norm_scale(x, seg_mean_bf):
    ms = _dot((x * x).astype(BF), seg_mean_bf)
    return lax.rsqrt(ms + EPS)


def _ada_kernel(cp_ref, cs_ref, w_ref, b_ref, qg_ref, kg_ref, og_ref, op_ref, os_ref, gains_ref):
    gains_ref[...] = jnp.concatenate(
        [qg_ref[...]] * N_HEADS_A + [kg_ref[...]] * N_KV_A + [og_ref[...]] * N_HEADS_B, axis=1)
    w = w_ref[...]
    w_hi = w.astype(BF)
    w_lo = (w - w_hi.astype(F32)).astype(BF)
    b = b_ref[...]

    ns, n_p = cs_ref.shape[0], cp_ref.shape[0]
    c = jnp.concatenate([cs_ref[...], cp_ref[...],
                         jnp.zeros(((-n_p) % SUB, D_MODEL), F32)], axis=0)
    c_hi = c.astype(BF)
    c_lo = (c - c_hi.astype(F32)).astype(BF)
    out = _dot(c_hi, w_hi) + (_dot(c_hi, w_lo) + _dot(c_lo, w_hi)) + b
    os_ref[...] = out[:ns, :]
    op_ref[...] = out[ns:ns + n_p, :]


def _ada_call(c_p, c_s, w_ada, b_ada, q_g, k_g, o_g):
    mp, ms = c_p.shape[0], c_s.shape[0]
    n = w_ada.shape[1]
    tn = 512
    const = lambda a: pl.BlockSpec(a.shape, lambda j: (0,) * a.ndim)
    return pl.pallas_call(
        _ada_kernel,
        grid=(n // tn,),
        in_specs=[const(c_p), const(c_s),
                  pl.BlockSpec((D_MODEL, tn), lambda j: (0, j)),
                  pl.BlockSpec((1, tn), lambda j: (0, j)),
                  const(q_g), const(k_g), const(o_g)],
        out_specs=[pl.BlockSpec((mp, tn), lambda j: (0, j)),
                   pl.BlockSpec((ms, tn), lambda j: (0, j)),
                   pl.BlockSpec((1, G_END), lambda j: (0, 0))],
        out_shape=[jax.ShapeDtypeStruct((mp, n), F32), jax.ShapeDtypeStruct((ms, n), F32),
                   jax.ShapeDtypeStruct((1, G_END), F32)],
        name="ada",
    )(c_p, c_s, w_ada, b_ada, q_g, k_g, o_g)


STAGE_B_COST = 3840


def _round_robin(*gens):
    results = [None] * len(gens)
    live = list(range(len(gens)))
    while live:
        for n in list(live):
            try:
                yield next(gens[n])
            except StopIteration as stop:
                results[n] = stop.value
                live.remove(n)
    return results


def _attn_block(q_blk, kcat, kcat_sw, vcat, vcat_sw, bias, sink_a, sink_b):
    lane = lax.broadcasted_iota(jnp.int32, (BLK, LANES), 1)
    lo = lane < HEAD_DIM_A
    chunks = [q_blk[:, c * LANES:(c + 1) * LANES] for c in range(4)]
    zero = jnp.zeros((BLK, LANES), F32)
    q_lo = [jnp.where(lo, c, zero).astype(BF) for c in chunks]
    q_hi = [jnp.where(lo, zero, c).astype(BF) for c in chunks]
    qa = jnp.concatenate([q_lo[0], q_lo[1], q_hi[2], q_hi[3]], axis=0)
    qb = jnp.concatenate([q_hi[0], q_hi[1], q_lo[2], q_lo[3]], axis=0)

    def probs(qs, kc, sink):
        s = _dot_nt(qs, kc) + bias
        m = jnp.maximum(jnp.max(s, axis=-1, keepdims=True), sink)
        p = jnp.exp(s - m)
        den = jnp.sum(p, axis=-1, keepdims=True) + jnp.exp(sink - m)
        return p.astype(BF), 1.0 / den

    pa, ra = probs(qa, kcat, sink_a)
    yield 200
    pb, rb = probs(qb, kcat_sw, sink_b)
    yield 200
    oa = _dot(pa, vcat) * ra
    yield 40
    ob = _dot(pb, vcat_sw) * rb
    yield 40
    r = lambda o, i: o[i * BLK:(i + 1) * BLK, :]
    out = jnp.concatenate([
        jnp.where(lo, r(oa, 0), r(ob, 0)),
        jnp.where(lo, r(oa, 1), r(ob, 1)),
        jnp.where(lo, r(ob, 2), r(oa, 2)),
        jnp.where(lo, r(ob, 3), r(oa, 3)),
    ], axis=1)
    yield 30
    return out


def _hgrn_gates(fb, lb, one_m_lb, tri_bf):
    sig = _sigmoid(fb)
    kin = one_m_lb * (1.0 - sig)
    f = lb + one_m_lb * sig
    cum = _split_dot_left(tri_bf, jnp.log(f))
    return kin, f, cum


def _hgrn_span_decay(cum):
    q = BLK // 4
    ends = [cum[(n + 1) * q - 1:(n + 1) * q, :] for n in range(4)]
    d = -ends[0]
    for n in range(1, 4):
        d = jnp.maximum(d, ends[n - 1] - ends[n])
    return d


def _hgrn_state_step(qb, kin, cum, ib, st_ref):
    q_dec = (qb * jnp.exp(cum)).astype(BF)
    last = cum[BLK - 1:BLK, :]
    k_dec = (kin * jnp.exp(last - cum)).astype(BF)
    v_bf = ib.astype(BF)
    outs = []
    for h in range(N_HEADS_B):
        sl = slice(h * LANES, (h + 1) * LANES)
        st = st_ref[h]
        outs.append(_dot_nt(q_dec[:, sl], st.astype(BF)))
        st_ref[h] = st * jnp.exp(last[:, sl]) + _dot_tn(v_bf[:, sl], k_dec[:, sl])
    return jnp.concatenate(outs, axis=1)


def _hgrn_apply(amats, ib):
    v_bf = ib.astype(BF)
    return jnp.concatenate(
        [_dot(amats[h].astype(BF), v_bf[:, h * LANES:(h + 1) * LANES])
         for h in range(N_HEADS_B)], axis=1)


def _recur_block_fast(qb, kin, cum, ib, st_ref):
    base = _hgrn_state_step(qb, kin, cum, ib, st_ref)
    yield 130
    half = BLK // 2
    row = lax.broadcasted_iota(jnp.int32, (BLK, 1), 0)
    upper = row >= half
    piv = cum[half - 1:half, :]
    w_lvl = jnp.exp(jnp.concatenate([piv - cum[:half, :], cum[half:, :] - piv], axis=0))
    p_lvl = (jnp.where(upper, qb, kin) * w_lvl).astype(BF)
    mid = jnp.where(upper, cum[half + half // 2 - 1:half + half // 2, :],
                    cum[half // 2 - 1:half // 2, :])
    e_mid = cum - mid
    q_mid = (qb * jnp.exp(e_mid)).astype(BF)
    k_mid = (kin * jnp.exp(-e_mid)).astype(BF)
    yield 230

    ri = lax.broadcasted_iota(jnp.int32, (BLK, BLK), 0)
    ci = lax.broadcasted_iota(jnp.int32, (BLK, BLK), 1)
    same_half_causal = ((ri >= half) == (ci >= half)) & (ci <= ri)
    cross = (ri >= half) & (ci < half)
    amats = []
    for h in range(N_HEADS_B):
        sl = slice(h * LANES, (h + 1) * LANES)
        a_mid = _dot_nt(q_mid[:, sl], k_mid[:, sl])
        a_lvl = _dot_nt(p_lvl[:, sl], p_lvl[:, sl])
        amats.append(jnp.where(same_half_causal, a_mid, jnp.where(cross, a_lvl, 0.0)))
    yield 60
    full = base + _hgrn_apply(amats, ib)
    yield 40
    return base, full


def _hgrn_intra_robust(qb, kin, f, cum, ib, lvl_mask_ref):
    row = lax.broadcasted_iota(jnp.int32, (BLK, 1), 0)
    lvl_ops = []
    for b in LEVELS:
        pieces = []
        for r0 in range(0, BLK, 2 * b):
            piv = cum[r0 + b - 1:r0 + b, :]
            pieces.append(piv - cum[r0:r0 + b, :])
            pieces.append(cum[r0 + b:r0 + 2 * b, :] - piv)
        w = jnp.exp(jnp.concatenate(pieces, axis=0))
        second = (row & b) != 0
        lvl_ops.append((jnp.where(second, qb, kin) * w).astype(BF))

    n8 = BLK // SUB
    q3 = qb.reshape(n8, SUB, WIDTH_B)
    k3 = kin.reshape(n8, SUB, WIDTH_B)
    f3 = f.reshape(n8, SUB, WIDTH_B)
    v3 = ib.reshape(n8, SUB, WIDTH_B)
    subl = lax.broadcasted_iota(jnp.int32, (n8, SUB, 1), 1)

    def head(x, h):
        return x[..., h * LANES:(h + 1) * LANES]

    g = q3 * k3
    acc = [jnp.sum(head(g, h), axis=-1, keepdims=True) * head(v3, h) for h in range(N_HEADS_B)]
    dec = jnp.ones_like(f3)
    kd = k3
    vd = v3
    for d in range(1, SUB):
        dec = f3 * pltpu.roll(dec, 1, 1)
        kd = pltpu.roll(kd, 1, 1)
        vd = pltpu.roll(vd, 1, 1)
        g = q3 * kd * dec
        ok = subl >= d
        for h in range(N_HEADS_B):
            a = jnp.where(ok, jnp.sum(head(g, h), axis=-1, keepdims=True), 0.0)
            acc[h] = acc[h] + a * head(vd, h)

    amats = []
    for h in range(N_HEADS_B):
        sl = slice(h * LANES, (h + 1) * LANES)
        amat = jnp.zeros((BLK, BLK), F32)
        for li in range(len(LEVELS)):
            p = lvl_ops[li][:, sl]
            amat = amat + lvl_mask_ref[li] * _dot_nt(p, p)
        amats.append(amat)
    diag = jnp.concatenate([acc[h].reshape(BLK, LANES) for h in range(N_HEADS_B)], axis=1)
    return _hgrn_apply(amats, ib) + diag


def _split_dot_left(a_bf, b_f32):
    hi = b_f32.astype(BF)
    lo = (b_f32 - hi.astype(F32)).astype(BF)
    return _dot(a_bf, hi) + _dot(a_bf, lo)


def _branch_b_out(o, zb, og):
    outs = []
    for h in range(N_HEADS_B):
        sl = slice(h * LANES, (h + 1) * LANES)
        oh = o[:, sl]
        ms = jnp.mean(oh * oh, axis=-1, keepdims=True)
        outs.append(oh * lax.rsqrt(ms + EPS))
    return jnp.concatenate(outs, axis=1) * og * _silu(zb)


def _prompt_kernel(sinks_ref, *refs, tile, n_steps, tps):
    x_refs = refs[:tps + 1]
    (mod_ref, ng_ref,
     w_in_ref, w_mg_ref, b_mg_ref, w_pa_ref, w_pb_ref, w_out_ref,
     gains_ref, lbl_ref,
     cb_ref, sb_ref, cl_ref, sl_ref, sgn_ref,
     seg_ref, tri_ref, bias_ref, lmask_ref,
     y_ref, kwin_ref, vwin_ref, state_ref,
     st_ref, kprev_ref, kprev_sw_ref, vprev_ref, vprev_sw_ref,
     p0_ref, p1_ref, g_ref, h0_ref, h1_ref, ob_ref, obase_ref) = refs[tps + 1:]
    p_refs, h_refs = (p0_ref, p1_ref), (h0_ref, h1_ref)
    s = pl.program_id(0)
    nblk = tile // BLK

    def stage_a(x_ref, h_ref, p_ref):
        def prep():
            h_ref[...] = _norm_modulate(x_ref[...], ng_ref[...], mod_ref)

        def proj_chunk(c):
            def run():
                cs = slice(c * MXU_N, (c + 1) * MXU_N)
                p_ref[:, cs] = _dot(h_ref[...], w_in_ref[:, cs])
            return run

        return [prep] + [proj_chunk(c) for c in range(D_IN // MXU_N)]

    def gate_chunks(h_ref):
        def gate_chunk(c):
            def run():
                cs = slice(c * MXU_N, (c + 1) * MXU_N)
                g_ref[:, cs] = _sigmoid(_dot(h_ref[...], w_mg_ref[:, cs]) + b_mg_ref[:, cs])
            return run

        return [gate_chunk(c) for c in range(2 * D_MODEL // MXU_N)]

    def phase(h_cur_ref, a_next, b_parts):
        gc = gate_chunks(h_cur_ref)
        interleave(gc[:2] + a_next[:1] + gc[2:] + a_next[1:], b_parts, lead=12)

    def stage_b(x_ref, p_ref, t_idx, y_rows):
        blocks = [slice(blk * BLK, (blk + 1) * BLK) for blk in range(nblk)]

        def qkv_stream():
            seg = seg_ref[...]
            rc, ru, rd = _rope_tables(cb_ref[pl.ds(t_idx, 1), :], sb_ref[pl.ds(t_idx, 1), :],
                                      cl_ref[...], sl_ref[...], sgn_ref[0:1, :], sgn_ref[1:2, :])
            qa = p_ref[:, OFF_QA:OFF_QA + WIDTH_A]
            qa = qa * _head_norm_scale(qa, seg) * (gains_ref[:, G_Q:G_K] * (HEAD_DIM_A ** -0.5))
            yield 160
            qa = jnp.concatenate(
                [_rope(qa[:, c * LANES:(c + 1) * LANES], rc, ru, rd) for c in range(4)], axis=1)
            yield 160
            ka = p_ref[:, OFF_KA:OFF_KA + KV_WIDTH]
            ka = ka * _head_norm_scale(ka, seg[:KV_WIDTH, :KV_WIDTH]) * gains_ref[:, G_K:G_O]
            ka = _rope(ka, rc, ru, rd)
            va = p_ref[:, OFF_VA:OFF_VA + KV_WIDTH]
            ka_sw = pltpu.roll(ka, HEAD_DIM_A, 1)
            va_sw = pltpu.roll(va, HEAD_DIM_A, 1)
            rows4 = lax.broadcasted_iota(jnp.int32, (4 * BLK, 1), 0) // BLK

            def sink_col(heads):
                col = jnp.zeros((4 * BLK, 1), F32)
                for n, hd in enumerate(heads):
                    col = jnp.where(rows4 == n, sinks_ref[hd], col)
                return col
            kwin_ref[...] = ka[tile - WINDOW:, :]
            vwin_ref[...] = va[tile - WINDOW:, :]
            res = (qa, ka, va, ka_sw, va_sw, sink_col((0, 2, 5, 7)), sink_col((1, 3, 4, 6)))
            yield 120
            return res

        def gates_stream():
            lb = _lower_bound(lbl_ref[...])
            tri = tri_ref[...]
            gates = []
            half_w = WIDTH_B // 2
            for rs in blocks:
                halves = []
                for c0 in (0, half_w):
                    cs = slice(c0, c0 + half_w)
                    halves.append(_hgrn_gates(p_ref[rs, OFF_FB + c0:OFF_FB + c0 + half_w],
                                              lb[:, cs], 1.0 - lb[:, cs], tri))
                    yield 240
                gates.append(tuple(jnp.concatenate(pair, axis=1) for pair in zip(*halves)))
            span = _hgrn_span_decay(gates[0][2])
            for g in gates[1:]:
                span = jnp.maximum(span, _hgrn_span_decay(g[2]))
            mild = jnp.max(span) < HGRN_FAST_MAX
            yield 10
            return gates, mild

        (qa, ka, va, ka_sw, va_sw, sink_a, sink_b), (gates, mild) = (
            yield from _round_robin(qkv_stream(), gates_stream()))

        kv_refs = (kprev_ref, kprev_sw_ref, vprev_ref, vprev_sw_ref)
        kv_new = [a.astype(BF) for a in (ka, ka_sw, va, va_sw)]
        kv_old = [r[...] for r in kv_refs]
        for r, a in zip(kv_refs, kv_new):
            r[...] = a[blocks[-1]]

        def attend(blk, rs):
            cats = [jnp.concatenate([old if blk == 0 else new[blocks[blk - 1]], new[rs]], axis=0)
                    for old, new in zip(kv_old, kv_new)]
            bias = bias_ref[jnp.where(t_idx == 0, 0, 1)] if blk == 0 else bias_ref[1]
            return (yield from _attn_block(qa[rs], *cats, bias, sink_a, sink_b))

        def recur(blk, rs):
            kin, _, cum = gates[blk]
            qb, ib = p_ref[rs, OFF_QB:OFF_QB + WIDTH_B], p_ref[rs, OFF_IB:OFF_IB + WIDTH_B]
            base, full = yield from _recur_block_fast(qb, kin, cum, ib, st_ref)
            obase_ref[rs, :] = base
            ob_ref[rs, :] = full

        ya_parts = []
        for blk, rs in enumerate(blocks):
            res = yield from _round_robin(attend(blk, rs), recur(blk, rs))
            ya_parts.append(res[0])

        @pl.when(jnp.logical_not(mild))
        def _():
            for rs, (kin, f, cum) in zip(blocks, gates):
                qb, ib = p_ref[rs, OFF_QB:OFF_QB + WIDTH_B], p_ref[rs, OFF_IB:OFF_IB + WIDTH_B]
                ob_ref[rs, :] = obase_ref[rs, :] + _hgrn_intra_robust(
                    qb, kin, f, cum, ib, lmask_ref)
        yield 0

        ya_bf = (jnp.concatenate(ya_parts, axis=0)
                 * _silu(p_ref[:, OFF_ZA:OFF_ZA + WIDTH_A])).astype(BF)
        yield 160
        yb_bf = _branch_b_out(ob_ref[...], p_ref[:, OFF_ZB:OFF_ZB + WIDTH_B],
                              gains_ref[:, G_O:G_END]).astype(BF)
        yield 330

        y = (g_ref[:, :D_MODEL] * _dot(ya_bf, w_pa_ref[...])
             + g_ref[:, D_MODEL:] * _dot(yb_bf, w_pb_ref[...]))
        y_ref[y_rows, :] = x_ref[...] + _modulation(mod_ref)[2] * _dot(y.astype(BF), w_out_ref[...])
        yield 0

    def interleave(a_thunks, b_parts, lead):
        for th in a_thunks[:lead]:
            th()
        acc, done = 0, lead
        for cost in b_parts:
            acc += cost
            upto = lead + int(round((len(a_thunks) - lead) * min(acc, STAGE_B_COST) / STAGE_B_COST))
            for th in a_thunks[done:upto]:
                th()
            done = upto
        assert done == len(a_thunks) and acc == STAGE_B_COST, (done, acc)

    @pl.when(s == 0)
    def _():
        st_ref[...] = jnp.zeros_like(st_ref)
        for r in (kprev_ref, kprev_sw_ref, vprev_ref, vprev_sw_ref):
            r[...] = jnp.zeros_like(r)
        for th in stage_a(x_refs[0], h0_ref, p0_ref):
            th()

    for k in range(tps):
        cur, nxt = k % 2, (k + 1) % 2
        phase(h_refs[cur], stage_a(x_refs[k + 1], h_refs[nxt], p_refs[nxt]),
              stage_b(x_refs[k], p_refs[cur], tps * s + k, slice(k * tile, (k + 1) * tile)))

    @pl.when(s == n_steps - 1)
    def _():
        for hd in range(N_HEADS_B):
            state_ref[hd] = st_ref[hd].T
        kwin_ref[...] = kwin_ref[...].T
        vwin_ref[...] = vwin_ref[...].T


def _const_spec(shape):
    nd = len(shape)
    return pl.BlockSpec(shape, lambda i, *_: (0,) * nd, pipeline_mode=pl.Buffered(1))


def _prompt_call(x, mod, consts, w, tile, tps):
    t = x.shape[0]
    n_tiles = t // tile
    n_steps = n_tiles // tps
    assert n_steps * tps * tile == t and tps % 2 == 0
    row = lambda n: _const_spec((1, n))
    x_spec = lambda k: pl.BlockSpec(
        (tile, D_MODEL), lambda i, *_: (jnp.minimum(tps * i + k, n_tiles - 1), 0))
    in_specs = [x_spec(k) for k in range(tps + 1)] + [
        row(3 * D_MODEL), row(D_MODEL),
        _const_spec((D_MODEL, D_IN)), _const_spec((D_MODEL, 2 * D_MODEL)), row(2 * D_MODEL),
        _const_spec((WIDTH_A, D_MODEL)), _const_spec((WIDTH_B, D_MODEL)),
        _const_spec((D_MODEL, D_MODEL)),
        row(G_END), _const_spec((2, WIDTH_B)),
        _const_spec((n_tiles, LANES)), _const_spec((n_tiles, LANES)),
        _const_spec((tile, LANES)), _const_spec((tile, LANES)), _const_spec((2, LANES)),
        _const_spec((WIDTH_A, WIDTH_A)), _const_spec((BLK, BLK)),
        _const_spec((2, 4 * BLK, 2 * BLK)), _const_spec((len(LEVELS), BLK, BLK)),
    ]
    out_specs = [
        pl.BlockSpec((tps * tile, D_MODEL), lambda i, *_: (i, 0)),
        pl.BlockSpec((WINDOW, KV_WIDTH), lambda i, *_: (0, 0)),
        pl.BlockSpec((WINDOW, KV_WIDTH), lambda i, *_: (0, 0)),
        pl.BlockSpec((N_HEADS_B, HEAD_DIM_B, HEAD_DIM_B), lambda i, *_: (0, 0, 0)),
    ]
    out_shape = [
        jax.ShapeDtypeStruct((t, D_MODEL), F32),
        jax.ShapeDtypeStruct((WINDOW, KV_WIDTH), F32),
        jax.ShapeDtypeStruct((WINDOW, KV_WIDTH), F32),
        jax.ShapeDtypeStruct((N_HEADS_B, HEAD_DIM_B, HEAD_DIM_B), F32),
    ]
    scratch = [
        pltpu.VMEM((N_HEADS_B, HEAD_DIM_B, HEAD_DIM_B), F32),
        pltpu.VMEM((BLK, KV_WIDTH), BF), pltpu.VMEM((BLK, KV_WIDTH), BF),
        pltpu.VMEM((BLK, KV_WIDTH), BF), pltpu.VMEM((BLK, KV_WIDTH), BF),
        pltpu.VMEM((tile, D_IN), F32), pltpu.VMEM((tile, D_IN), F32),
        pltpu.VMEM((tile, 2 * D_MODEL), F32),
        pltpu.VMEM((tile, D_MODEL), BF), pltpu.VMEM((tile, D_MODEL), BF),
        pltpu.VMEM((tile, WIDTH_B), F32), pltpu.VMEM((tile, WIDTH_B), F32),
    ]
    return pl.pallas_call(
        functools.partial(_prompt_kernel, tile=tile, n_steps=n_steps, tps=tps),
        grid_spec=pltpu.PrefetchScalarGridSpec(
            num_scalar_prefetch=1, grid=(n_steps,),
            in_specs=in_specs, out_specs=out_specs, scratch_shapes=scratch),
        out_shape=out_shape,
        compiler_params=pltpu.CompilerParams(
            dimension_semantics=("arbitrary",), vmem_limit_bytes=VMEM_LIMIT),
        name="prompt_layer",
    )(w["sinks"], *([x] * (tps + 1)), mod, w["norm_g"],
      w["w_in"], w["w_merge"], w["b_merge"], w["w_proj_a"], w["w_proj_b"], w["w_out"],
      w["gains"], w["lb_logits"],
      consts["rope_cb"], consts["rope_sb"], consts["rope_cl"], consts["rope_sl"],
      consts["rope_sgn"],
      consts["seg"], consts["tri"], consts["bias"], consts["lmask"])


def _dec_in_kernel(x_ref, mod_ref, ng_ref, w_in_ref, w_mg_ref, b_mg_ref, gains_ref, lbl_ref,
                   rc_ref, ru_ref, rd_ref, seg_ref, perm_ref,
                   w_in_bf_ref, w_mg_bf_ref, g_ref,
                   qm_ref, kn_ref, vn_ref, za_ref, qb_ref, kin_ref, ib_ref, zb_ref,
                   h_ref, p_ref):
    j = pl.program_id(0)
    half_in = D_IN // 2

    @pl.when(j == 0)
    def _():
        h_ref[...] = _norm_modulate(x_ref[:, 0, :], ng_ref[...], mod_ref)

    w_in_bf = w_in_ref[...].astype(BF)
    w_mg_bf = w_mg_ref[...].astype(BF)
    w_in_bf_ref[...] = w_in_bf
    w_mg_bf_ref[...] = w_mg_bf
    h_bf = h_ref[...]
    g_ref[...] = _sigmoid(_dot(h_bf, w_mg_bf) + b_mg_ref[...])
    part = _dot(h_bf, w_in_bf)

    @pl.when(j == 0)
    def _():
        p_ref[:, :half_in] = part

    @pl.when(j == 1)
    def _():
        p_ref[:, half_in:] = part
        seg = seg_ref[...]
        rc, ru, rd = rc_ref[...], ru_ref[...], rd_ref[...]
        qa = p_ref[:, OFF_QA:OFF_QA + WIDTH_A]
        qa = qa * _head_norm_scale(qa, seg) * (gains_ref[:, G_Q:G_K] * (HEAD_DIM_A ** -0.5))
        qa = jnp.concatenate(
            [_rope(qa[:, c * LANES:(c + 1) * LANES], rc, ru, rd) for c in range(4)], axis=1)
        qm = _dot(qa.astype(BF), perm_ref[...])
        for hd in range(N_HEADS_A):
            qm_ref[:, hd, :] = qm[:, hd * LANES:(hd + 1) * LANES]
        ka = p_ref[:, OFF_KA:OFF_KA + KV_WIDTH]
        ka = ka * _head_norm_scale(ka, seg[:KV_WIDTH, :KV_WIDTH]) * gains_ref[:, G_K:G_O]
        kn_ref[...] = _rope(ka, rc, ru, rd)
        vn_ref[...] = p_ref[:, OFF_VA:OFF_VA + KV_WIDTH]
        za_ref[...] = p_ref[:, OFF_ZA:OFF_ZA + WIDTH_A]
        qb_ref[...] = p_ref[:, OFF_QB:OFF_QB + WIDTH_B]
        lb = _lower_bound(lbl_ref[...])
        kin_ref[...] = (1.0 - lb) * (1.0 - _sigmoid(p_ref[:, OFF_FB:OFF_FB + WIDTH_B]))
        ib_ref[...] = p_ref[:, OFF_IB:OFF_IB + WIDTH_B]
        zb_ref[...] = p_ref[:, OFF_ZB:OFF_ZB + WIDTH_B]


def _dec_in_call(x, mod, consts, w):
    b = x.shape[0]
    n_steps = 2
    half_in, half_mg = D_IN // n_steps, 2 * D_MODEL // n_steps
    assert half_in % LANES == 0
    const = lambda a: pl.BlockSpec(a.shape, lambda j: (0,) * a.ndim)
    cols = lambda rows, n: pl.BlockSpec((rows, n), lambda j: (0, j))
    row_out = lambda n: pl.BlockSpec((b, n), lambda j: (0, 0))
    f = lambda n: jax.ShapeDtypeStruct((b, n), F32)
    widths = [KV_WIDTH, KV_WIDTH, WIDTH_A, WIDTH_B, WIDTH_B, WIDTH_B, WIDTH_B]
    qm_shape = (b, N_HEADS_A, LANES)
    small = [w["norm_g"]]
    tail = [w["gains"], w["lb_logits"], consts["rope_c1"], consts["rope_u1"], consts["rope_d1"],
            consts["seg"], consts["perm"]]
    return pl.pallas_call(
        _dec_in_kernel,
        grid=(n_steps,),
        in_specs=[const(x), const(mod)] + [const(a) for a in small]
        + [cols(D_MODEL, half_in), cols(D_MODEL, half_mg), cols(1, half_mg)]
        + [const(a) for a in tail],
        out_specs=[cols(D_MODEL, half_in), cols(D_MODEL, half_mg), cols(b, half_mg),
                   pl.BlockSpec(qm_shape, lambda j: (0, 0, 0))]
        + [row_out(n) for n in widths],
        out_shape=[jax.ShapeDtypeStruct((D_MODEL, D_IN), BF),
                   jax.ShapeDtypeStruct((D_MODEL, 2 * D_MODEL), BF), f(2 * D_MODEL),
                   jax.ShapeDtypeStruct(qm_shape, F32)]
        + [f(n) for n in widths],
        scratch_shapes=[pltpu.VMEM((b, D_MODEL), BF), pltpu.VMEM((b, D_IN), F32)],
        compiler_params=pltpu.CompilerParams(
            dimension_semantics=("arbitrary",), vmem_limit_bytes=VMEM_LIMIT),
        name="decode_in",
    )(x, mod, *small, w["w_in_f32"], w["w_merge_f32"], w["b_merge"], *tail)


def _dec_mix_kernel(sinks_ref, qm_ref, kn_ref, vn_ref, qb_ref, kin_ref, ib_ref,
                    ck_ref, cv_ref, st_ref,
                    att_ref, ob_ref, nk_ref, nv_ref, nst_ref, *, bt):
    nh = N_HEADS_A
    s = jnp.concatenate([_dot(qm_ref[j].astype(BF), ck_ref[j].astype(BF))
                         for j in range(bt)], axis=0)
    key = lax.broadcasted_iota(jnp.int32, (bt * nh, WINDOW), 1)
    s = jnp.where(key == 0, NEG, s)
    rep = lambda r: jnp.broadcast_to(r[...][:, None, :], (bt, nh, r.shape[-1])).reshape(
        bt * nh, r.shape[-1])
    kn_rows, vn_rows = rep(kn_ref), rep(vn_ref)
    head = lax.broadcasted_iota(jnp.int32, (bt * nh, 1), 0) % nh
    sink = jnp.zeros((bt * nh, 1), F32)
    for hd in range(nh):
        sink = jnp.where(head == hd, sinks_ref[hd], sink)
    s_new = jnp.sum(qm_ref[...].reshape(bt * nh, LANES) * kn_rows, axis=-1, keepdims=True)
    m = jnp.maximum(jnp.maximum(jnp.max(s, axis=-1, keepdims=True), s_new), sink)
    p = jnp.exp(s - m)
    p_new = jnp.exp(s_new - m)
    den = jnp.sum(p, axis=-1, keepdims=True) + p_new + jnp.exp(sink - m)
    p_bf = p.astype(BF)
    pv = jnp.concatenate([_dot_nt(p_bf[j * nh:(j + 1) * nh, :], cv_ref[j].astype(BF))
                          for j in range(bt)], axis=0)
    att_ref[...] = ((pv + p_new * vn_rows) * (1.0 / den)).reshape(bt, nh, LANES)
    kn_t, vn_t = kn_ref[...].T, vn_ref[...].T
    newest = lax.broadcasted_iota(jnp.int32, (KV_WIDTH, WINDOW), 1) == WINDOW - 1
    for j in range(bt):
        nk_ref[j] = jnp.where(newest, kn_t[:, j:j + 1], pltpu.roll(ck_ref[j], WINDOW - 1, 1))
        nv_ref[j] = jnp.where(newest, vn_t[:, j:j + 1], pltpu.roll(cv_ref[j], WINDOW - 1, 1))

    for hd in range(N_HEADS_B):
        sl = slice(hd * LANES, (hd + 1) * LANES)
        kin_t = kin_ref[:, sl].T
        q_bf = qb_ref[:, sl].astype(BF)
        for j in range(bt):
            kcol = kin_t[:, j:j + 1]
            st = st_ref[j, hd]
            new = st - kcol * (st - ib_ref[j:j + 1, sl])
            nst_ref[j, hd] = new
            ob_ref[j:j + 1, sl] = _dot(q_bf, new.astype(BF))[j:j + 1, :]


def _dec_mix_call(sinks, qm, kn, vn, qb, kin, ib, cache_k, cache_v, state, bt):
    b = kn.shape[0]
    rows = lambda n: pl.BlockSpec((bt, n), lambda i, *_: (i, 0))
    cache_spec = pl.BlockSpec((bt, WINDOW, KV_WIDTH), lambda i, *_: (i, 0, 0))
    st_spec = pl.BlockSpec((bt, N_HEADS_B, HEAD_DIM_B, HEAD_DIM_B), lambda i, *_: (i, 0, 0, 0))
    qm_spec = pl.BlockSpec((bt, N_HEADS_A, LANES), lambda i, *_: (i, 0, 0))
    return pl.pallas_call(
        functools.partial(_dec_mix_kernel, bt=bt),
        grid_spec=pltpu.PrefetchScalarGridSpec(
            num_scalar_prefetch=1, grid=(b // bt,),
            in_specs=[qm_spec, rows(KV_WIDTH), rows(KV_WIDTH), rows(WIDTH_B), rows(WIDTH_B),
                      rows(WIDTH_B), cache_spec, cache_spec, st_spec],
            out_specs=[qm_spec, rows(WIDTH_B), cache_spec, cache_spec, st_spec]),
        out_shape=[jax.ShapeDtypeStruct((b, N_HEADS_A, LANES), F32),
                   jax.ShapeDtypeStruct((b, WIDTH_B), F32),
                   jax.ShapeDtypeStruct(cache_k.shape, F32),
                   jax.ShapeDtypeStruct(cache_v.shape, F32),
                   jax.ShapeDtypeStruct(state.shape, F32)],
        compiler_params=pltpu.CompilerParams(
            dimension_semantics=("arbitrary",), vmem_limit_bytes=VMEM_LIMIT),
        name="decode_mix",
    )(sinks, qm, kn, vn, qb, kin, ib, cache_k, cache_v, state)


def _dec_out_kernel(x_ref, mod_ref, att_ref, za_ref, ob_ref, zb_ref, g_ref, gains_ref,
                    permt_ref, w_pa_ref, w_pb_ref, w_out_ref,
                    y_ref, w_pa_bf_ref, w_pb_bf_ref, w_out_bf_ref):
    w_pa, w_pb, w_out = (r[...].astype(BF) for r in (w_pa_ref, w_pb_ref, w_out_ref))
    w_pa_bf_ref[...] = w_pa
    w_pb_bf_ref[...] = w_pb
    w_out_bf_ref[...] = w_out
    att = sum(_split_dot(att_ref[:, hd, :], permt_ref[hd * LANES:(hd + 1) * LANES, :])
              for hd in range(N_HEADS_A))
    ya = att * _silu(za_ref[...])
    yb = _branch_b_out(ob_ref[...], zb_ref[...], gains_ref[:, G_O:G_END])
    g = g_ref[...]
    y = (g[:, :D_MODEL] * _dot(ya.astype(BF), w_pa) + g[:, D_MODEL:] * _dot(yb.astype(BF), w_pb))
    y_ref[:, 0, :] = x_ref[:, 0, :] + _modulation(mod_ref)[2] * _dot(y.astype(BF), w_out)


def _dec_out_call(x, mod, att, za, ob, zb, g, consts, w):
    bf = lambda a: jax.ShapeDtypeStruct(a.shape, BF)
    ws = (w["w_proj_a_f32"], w["w_proj_b_f32"], w["w_out_f32"])
    return pl.pallas_call(
        _dec_out_kernel,
        out_shape=[jax.ShapeDtypeStruct(x.shape, F32)] + [bf(a) for a in ws],
        compiler_params=pltpu.CompilerParams(vmem_limit_bytes=VMEM_LIMIT),
        name="decode_out",
    )(x, mod, att, za, ob, zb, g, w["gains"], consts["permt"], *ws)


def _rope_consts(n_tiles, tile, past_len):
    half = ROT_DIM // 2
    inv = ROPE_THETA ** (-np.arange(0, ROT_DIM, 2, dtype=np.float64) / ROT_DIM)
    e = np.arange(LANES) % HEAD_DIM_A
    rot = e < ROT_DIM

    def tables(pos):
        ang = np.asarray(pos, np.float64)[:, None] * inv[e % half][None, :]
        return np.where(rot, np.cos(ang), 1.0), np.where(rot, np.sin(ang), 0.0)

    sgn = np.stack([np.where(e < half, -1.0, 0.0),
                    np.where(rot & (e >= half), 1.0, 0.0)])
    cb, sb = tables(np.arange(n_tiles) * tile)
    cl, sl = tables(np.arange(tile))
    c1, s1 = tables([past_len])
    f = lambda a: jnp.asarray(a, F32)
    return {"rope_cb": f(cb), "rope_sb": f(sb), "rope_cl": f(cl), "rope_sl": f(sl),
            "rope_sgn": f(sgn), "rope_c1": f(c1), "rope_u1": f(s1 * sgn[0:1]),
            "rope_d1": f(s1 * sgn[1:2])}


def _static_consts():
    seg = np.kron(np.eye(N_HEADS_A), np.full((HEAD_DIM_A, HEAD_DIM_A), 1.0 / HEAD_DIM_A))
    tri = np.tril(np.ones((BLK, BLK)))
    r = np.arange(4 * BLK)[:, None] % BLK
    c = np.arange(2 * BLK)[None, :]
    ok_prev = (c < BLK) & (c > r)
    ok_cur = (c >= BLK) & (c - BLK <= r)
    bias = np.stack([np.where(ok_cur, 0.0, NEG), np.where(ok_prev | ok_cur, 0.0, NEG)])
    i = np.arange(BLK)[:, None]
    j = np.arange(BLK)[None, :]
    lmask = np.stack([((i // (2 * b)) == (j // (2 * b))) & ((i & b) != 0) & ((j & b) == 0)
                      for b in LEVELS]).astype(np.float32)
    perm = np.zeros((WIDTH_A, N_HEADS_A * LANES), np.float32)
    for hd in range(N_HEADS_A):
        kvh = hd // (N_HEADS_A // N_KV_A)
        for d in range(HEAD_DIM_A):
            perm[hd * HEAD_DIM_A + d, hd * LANES + kvh * HEAD_DIM_A + d] = 1.0
    return {
        "seg": jnp.asarray(seg, BF), "tri": jnp.asarray(tri, BF),
        "bias": jnp.asarray(bias, F32), "lmask": jnp.asarray(lmask, F32),
        "perm": jnp.asarray(perm, BF), "permt": jnp.asarray(perm.T, BF),
    }


def kernel(x_prompt, x_sample, cache_win_k, cache_win_v, state_hgrn, c_prompt, c_sample,
           w_ada, b_ada, norm_g, w_in, q_norm_g, k_norm_g, sinks, lb_logits, o_norm_g,
           w_merge, b_merge, w_proj_a, w_proj_b, w_out):
    depth = w_in.shape[0]
    assert depth == 1 and x_prompt.shape[0] == 1 and x_sample.shape[1] == 1
    t = x_prompt.shape[1]
    nb = x_sample.shape[0]
    past_len = t
    tile = PROMPT_TILE
    bt = DECODE_BATCH_TILE

    consts = _static_consts()
    consts.update(_rope_consts(t // tile, tile, past_len))

    w = {
        "sinks": sinks[0], "norm_g": norm_g,
        "w_in_f32": w_in[0], "w_merge_f32": w_merge[0], "b_merge": b_merge,
        "w_proj_a_f32": w_proj_a[0], "w_proj_b_f32": w_proj_b[0], "w_out_f32": w_out[0],
        "lb_logits": lb_logits,
    }

    mod_p, mod_s, w["gains"] = _ada_call(c_prompt, c_sample, w_ada[0], b_ada,
                                         q_norm_g, k_norm_g, o_norm_g)

    xs = x_sample
    (w["w_in"], w["w_merge"], g,
     qm, kn, vn, za, qb, kin, ib, zb) = _dec_in_call(xs, mod_s, consts, w)
    to_t = lambda c: jnp.transpose(c[0], (0, 2, 3, 1)).reshape(nb, KV_WIDTH, WINDOW)
    from_t = lambda c: jnp.transpose(
        c.reshape(-1, N_KV_A, HEAD_DIM_A, WINDOW), (0, 3, 1, 2))[None]
    att, ob, nk, nv, nst = _dec_mix_call(
        w["sinks"], qm, kn, vn, qb, kin, ib,
        to_t(cache_win_k), to_t(cache_win_v), state_hgrn[0], bt)
    y_s, w["w_proj_a"], w["w_proj_b"], w["w_out"] = _dec_out_call(
        xs, mod_s, att, za, ob, zb, g, consts, w)

    y_p, kwin, vwin, st_p = _prompt_call(x_prompt[0], mod_p, consts, w, tile,
                                         PROMPT_TILES_PER_STEP)

    return (y_p[None], y_s,
            from_t(kwin[None]), from_t(vwin[None]), st_p[None, None],
            from_t(nk), from_t(nv), nst[None])
```

```python
import functools

import numpy as np
import jax
import jax.numpy as jnp
from jax import lax
from jax.experimental import pallas as pl
from jax.experimental.pallas import tpu as pltpu

D_MODEL = 1024
HEAD_DIM_A = 64
N_HEADS_A = 8
N_KV_A = 2
WIDTH_A = N_HEADS_A * HEAD_DIM_A
KV_WIDTH = N_KV_A * HEAD_DIM_A
WINDOW = 128
ROT_DIM = HEAD_DIM_A // 4
ROPE_THETA = 500000.0
HEAD_DIM_B = 128
N_HEADS_B = 4
WIDTH_B = N_HEADS_B * HEAD_DIM_B
EPS = 1e-6
OFF_QA = 0
OFF_KA = OFF_QA + WIDTH_A
OFF_VA = OFF_KA + KV_WIDTH
OFF_ZA = OFF_VA + KV_WIDTH
OFF_QB = OFF_ZA + WIDTH_A
OFF_FB = OFF_QB + WIDTH_B
OFF_IB = OFF_FB + WIDTH_B
OFF_ZB = OFF_IB + WIDTH_B
D_IN = OFF_ZB + WIDTH_B

LANES = 128
BLK = 128
SUB = 8
LEVELS = (64, 32, 16, 8)
HGRN_FAST_MAX = 80.0
MXU_N = 256
NEG = -1e30
VMEM_LIMIT = 56 * 1024 * 1024
PROMPT_TILE = 256
PROMPT_TILES_PER_STEP = 2
DECODE_BATCH_TILE = 16
G_Q, G_K, G_O = 0, WIDTH_A, WIDTH_A + KV_WIDTH
G_END = G_O + WIDTH_B

BF = jnp.bfloat16
F32 = jnp.float32


def _dot(a, b):
    return jnp.dot(a, b, preferred_element_type=F32)


def _dot_nt(a, b):
    return lax.dot_general(a, b, (((1,), (1,)), ((), ())), preferred_element_type=F32)


def _dot_tn(a, b):
    return lax.dot_general(a, b, (((0,), (0,)), ((), ())), preferred_element_type=F32)


def _split_dot(a_f32, b_bf):
    hi = a_f32.astype(BF)
    lo = (a_f32 - hi.astype(F32)).astype(BF)
    return _dot(hi, b_bf) + _dot(lo, b_bf)


def _sigmoid(x):
    return 1.0 / (1.0 + jnp.exp(-x))


def _silu(x):
    return x * _sigmoid(x)


def _lower_bound(lb_logits):
    l0 = lb_logits[0:1, :]
    l1 = lb_logits[1:2, :]
    m = jnp.maximum(l0, l1)
    e0 = jnp.exp(l0 - m)
    e1 = jnp.exp(l1 - m)
    return e0 / (e0 + e1)


def _rope_tables(cb, sb, cl, sl, sgn_up, sgn_dn):
    c = cb * cl - sb * sl
    s = sb * cl + cb * sl
    return c, s * sgn_up, s * sgn_dn


def _rope(x, c, s_up, s_dn):
    return x * c + pltpu.roll(x, LANES - ROT_DIM // 2, 1) * s_up + pltpu.roll(x, ROT_DIM // 2, 1) * s_dn


def _modulation(mod_ref):
    return (mod_ref[:, 0:D_MODEL], mod_ref[:, D_MODEL:2 * D_MODEL],
            mod_ref[:, 2 * D_MODEL:3 * D_MODEL])


def _norm_modulate(x, ng, mod_ref):
    shift, scale, _ = _modulation(mod_ref)
    ms = jnp.mean(x * x, axis=-1, keepdims=True)
    return (x * lax.rsqrt(ms + EPS) * (ng * (1.0 + scale)) + shift).astype(BF)


def _head_norm_scale(x, seg_mean_bf):
    ms = _dot((x * x).astype(BF), seg_mean_bf)
    return lax.rsqrt(ms + EPS)


def _ada_kernel(cp_ref, cs_ref, w_ref, b_ref, qg_ref, kg_ref, og_ref, op_ref, os_ref, gains_ref):
    gains_ref[...] = jnp.concatenate(
        [qg_ref[...]] * N_HEADS_A + [kg_ref[...]] * N_KV_A + [og_ref[...]] * N_HEADS_B, axis=1)
    w = w_ref[...]
    w_hi = w.astype(BF)
    w_lo = (w - w_hi.astype(F32)).astype(BF)
    b = b_ref[...]

    ns, n_p = cs_ref.shape[0], cp_ref.shape[0]
    c = jnp.concatenate([cs_ref[...], cp_ref[...],
                         jnp.zeros(((-n_p) % SUB, D_MODEL), F32)], axis=0)
    c_hi = c.astype(BF)
    c_lo = (c - c_hi.astype(F32)).astype(BF)
    out = _dot(c_hi, w_hi) + (_dot(c_hi, w_lo) + _dot(c_lo, w_hi)) + b
    os_ref[...] = out[:ns, :]
    op_ref[...] = out[ns:ns + n_p, :]


def _ada_call(c_p, c_s, w_ada, b_ada, q_g, k_g, o_g):
    mp, ms = c_p.shape[0], c_s.shape[0]
    n = w_ada.shape[1]
    tn = 512
    const = lambda a: pl.BlockSpec(a.shape, lambda j: (0,) * a.ndim)
    return pl.pallas_call(
        _ada_kernel,
        grid=(n // tn,),
        in_specs=[const(c_p), const(c_s),
                  pl.BlockSpec((D_MODEL, tn), lambda j: (0, j)),
                  pl.BlockSpec((1, tn), lambda j: (0, j)),
                  const(q_g), const(k_g), const(o_g)],
        out_specs=[pl.BlockSpec((mp, tn), lambda j: (0, j)),
                   pl.BlockSpec((ms, tn), lambda j: (0, j)),
                   pl.BlockSpec((1, G_END), lambda j: (0, 0))],
        out_shape=[jax.ShapeDtypeStruct((mp, n), F32), jax.ShapeDtypeStruct((ms, n), F32),
                   jax.ShapeDtypeStruct((1, G_END), F32)],
        name="ada",
    )(c_p, c_s, w_ada, b_ada, q_g, k_g, o_g)


STAGE_B_COST = 3840


def _round_robin(*gens):
    results = [None] * len(gens)
    live = list(range(len(gens)))
    while live:
        for n in list(live):
            try:
                yield next(gens[n])
            except StopIteration as stop:
                results[n] = stop.value
                live.remove(n)
    return results


def _attn_block(q_blk, kcat, kcat_sw, vcat, vcat_sw, bias, sink_a, sink_b):
    lane = lax.broadcasted_iota(jnp.int32, (BLK, LANES), 1)
    lo = lane < HEAD_DIM_A
    chunks = [q_blk[:, c * LANES:(c + 1) * LANES] for c in range(4)]
    zero = jnp.zeros((BLK, LANES), F32)
    q_lo = [jnp.where(lo, c, zero).astype(BF) for c in chunks]
    q_hi = [jnp.where(lo, zero, c).astype(BF) for c in chunks]
    qa = jnp.concatenate([q_lo[0], q_lo[1], q_hi[2], q_hi[3]], axis=0)
    qb = jnp.concatenate([q_hi[0], q_hi[1], q_lo[2], q_lo[3]], axis=0)

    def probs(qs, kc, sink):
        s = _dot_nt(qs, kc) + bias
        m = jnp.maximum(jnp.max(s, axis=-1, keepdims=True), sink)
        p = jnp.exp(s - m)
        den = jnp.sum(p, axis=-1, keepdims=True) + jnp.exp(sink - m)
        return p.astype(BF), 1.0 / den

    pa, ra = probs(qa, kcat, sink_a)
    yield 200
    pb, rb = probs(qb, kcat_sw, sink_b)
    yield 200
    oa = _dot(pa, vcat) * ra
    yield 40
    ob = _dot(pb, vcat_sw) * rb
    yield 40
    r = lambda o, i: o[i * BLK:(i + 1) * BLK, :]
    out = jnp.concatenate([
        jnp.where(lo, r(oa, 0), r(ob, 0)),
        jnp.where(lo, r(oa, 1), r(ob, 1)),
        jnp.where(lo, r(ob, 2), r(oa, 2)),
        jnp.where(lo, r(ob, 3), r(oa, 3)),
    ], axis=1)
    yield 30
    return out


def _hgrn_gates(fb, lb, one_m_lb, tri_bf):
    sig = _sigmoid(fb)
    kin = one_m_lb * (1.0 - sig)
    f = lb + one_m_lb * sig
    cum = _split_dot_left(tri_bf, jnp.log(f))
    return kin, f, cum


def _hgrn_span_decay(cum):
    q = BLK // 4
    ends = [cum[(n + 1) * q - 1:(n + 1) * q, :] for n in range(4)]
    d = -ends[0]
    for n in range(1, 4):
        d = jnp.maximum(d, ends[n - 1] - ends[n])
    return d


def _hgrn_state_step(qb, kin, cum, ib, st_ref):
    q_dec = (qb * jnp.exp(cum)).astype(BF)
    last = cum[BLK - 1:BLK, :]
    k_dec = (kin * jnp.exp(last - cum)).astype(BF)
    v_bf = ib.astype(BF)
    outs = []
    for h in range(N_HEADS_B):
        sl = slice(h * LANES, (h + 1) * LANES)
        st = st_ref[h]
        outs.append(_dot_nt(q_dec[:, sl], st.astype(BF)))
        st_ref[h] = st * jnp.exp(last[:, sl]) + _dot_tn(v_bf[:, sl], k_dec[:, sl])
    return jnp.concatenate(outs, axis=1)


def _hgrn_apply(amats, ib):
    v_bf = ib.astype(BF)
    return jnp.concatenate(
        [_dot(amats[h].astype(BF), v_bf[:, h * LANES:(h + 1) * LANES])
         for h in range(N_HEADS_B)], axis=1)


def _recur_block_fast(qb, kin, cum, ib, st_ref):
    base = _hgrn_state_step(qb, kin, cum, ib, st_ref)
    yield 130
    half = BLK // 2
    row = lax.broadcasted_iota(jnp.int32, (BLK, 1), 0)
    upper = row >= half
    piv = cum[half - 1:half, :]
    w_lvl = jnp.exp(jnp.concatenate([piv - cum[:half, :], cum[half:, :] - piv], axis=0))
    p_lvl = (jnp.where(upper, qb, kin) * w_lvl).astype(BF)
    mid = jnp.where(upper, cum[half + half // 2 - 1:half + half // 2, :],
                    cum[half // 2 - 1:half // 2, :])
    e_mid = cum - mid
    q_mid = (qb * jnp.exp(e_mid)).astype(BF)
    k_mid = (kin * jnp.exp(-e_mid)).astype(BF)
    yield 230

    ri = lax.broadcasted_iota(jnp.int32, (BLK, BLK), 0)
    ci = lax.broadcasted_iota(jnp.int32, (BLK, BLK), 1)
    same_half_causal = ((ri >= half) == (ci >= half)) & (ci <= ri)
    cross = (ri >= half) & (ci < half)
    amats = []
    for h in range(N_HEADS_B):
        sl = slice(h * LANES, (h + 1) * LANES)
        a_mid = _dot_nt(q_mid[:, sl], k_mid[:, sl])
        a_lvl = _dot_nt(p_lvl[:, sl], p_lvl[:, sl])
        amats.append(jnp.where(same_half_causal, a_mid, jnp.where(cross, a_lvl, 0.0)))
    yield 60
    full = base + _hgrn_apply(amats, ib)
    yield 40
    return base, full


def _hgrn_intra_robust(qb, kin, f, cum, ib, lvl_mask_ref):
    row = lax.broadcasted_iota(jnp.int32, (BLK, 1), 0)
    lvl_ops = []
    for b in LEVELS:
        pieces = []
        for r0 in range(0, BLK, 2 * b):
            piv = cum[r0 + b - 1:r0 + b, :]
            pieces.append(piv - cum[r0:r0 + b, :])
            pieces.append(cum[r0 + b:r0 + 2 * b, :] - piv)
        w = jnp.exp(jnp.concatenate(pieces, axis=0))
        second = (row & b) != 0
        lvl_ops.append((jnp.where(second, qb, kin) * w).astype(BF))

    n8 = BLK // SUB
    q3 = qb.reshape(n8, SUB, WIDTH_B)
    k3 = kin.reshape(n8, SUB, WIDTH_B)
    f3 = f.reshape(n8, SUB, WIDTH_B)
    v3 = ib.reshape(n8, SUB, WIDTH_B)
    subl = lax.broadcasted_iota(jnp.int32, (n8, SUB, 1), 1)

    def head(x, h):
        return x[..., h * LANES:(h + 1) * LANES]

    g = q3 * k3
    acc = [jnp.sum(head(g, h), axis=-1, keepdims=True) * head(v3, h) for h in range(N_HEADS_B)]
    dec = jnp.ones_like(f3)
    kd = k3
    vd = v3
    for d in range(1, SUB):
        dec = f3 * pltpu.roll(dec, 1, 1)
        kd = pltpu.roll(kd, 1, 1)
        vd = pltpu.roll(vd, 1, 1)
        g = q3 * kd * dec
        ok = subl >= d
        for h in range(N_HEADS_B):
            a = jnp.where(ok, jnp.sum(head(g, h), axis=-1, keepdims=True), 0.0)
            acc[h] = acc[h] + a * head(vd, h)

    amats = []
    for h in range(N_HEADS_B):
        sl = slice(h * LANES, (h + 1) * LANES)
        amat = jnp.zeros((BLK, BLK), F32)
        for li in range(len(LEVELS)):
            p = lvl_ops[li][:, sl]
            amat = amat + lvl_mask_ref[li] * _dot_nt(p, p)
        amats.append(amat)
    diag = jnp.concatenate([acc[h].reshape(BLK, LANES) for h in range(N_HEADS_B)], axis=1)
    return _hgrn_apply(amats, ib) + diag


def _split_dot_left(a_bf, b_f32):
    hi = b_f32.astype(BF)
    lo = (b_f32 - hi.astype(F32)).astype(BF)
    return _dot(a_bf, hi) + _dot(a_bf, lo)


def _branch_b_out(o, zb, og):
    outs = []
    for h in range(N_HEADS_B):
        sl = slice(h * LANES, (h + 1) * LANES)
        oh = o[:, sl]
        ms = jnp.mean(oh * oh, axis=-1, keepdims=True)
        outs.append(oh * lax.rsqrt(ms + EPS))
    return jnp.concatenate(outs, axis=1) * og * _silu(zb)


def _prompt_kernel(sinks_ref, *refs, tile, n_steps, tps):
    x_refs = refs[:tps + 1]
    (mod_ref, ng_ref,
     w_in_ref, w_mg_ref, b_mg_ref, w_pa_ref, w_pb_ref, w_out_ref,
     gains_ref, lbl_ref,
     cb_ref, sb_ref, cl_ref, sl_ref, sgn_ref,
     seg_ref, tri_ref, bias_ref, lmask_ref,
     y_ref, kwin_ref, vwin_ref, state_ref,
     st_ref, kprev_ref, kprev_sw_ref, vprev_ref, vprev_sw_ref,
     p0_ref, p1_ref, g_ref, h0_ref, h1_ref, ob_ref, obase_ref) = refs[tps + 1:]
    p_refs, h_refs = (p0_ref, p1_ref), (h0_ref, h1_ref)
    s = pl.program_id(0)
    nblk = tile // BLK

    def stage_a(x_ref, h_ref, p_ref):
        def prep():
            h_ref[...] = _norm_modulate(x_ref[...], ng_ref[...], mod_ref)

        def proj_chunk(c):
            def run():
                cs = slice(c * MXU_N, (c + 1) * MXU_N)
                p_ref[:, cs] = _dot(h_ref[...], w_in_ref[:, cs])
            return run

        return [prep] + [proj_chunk(c) for c in range(D_IN // MXU_N)]

    def gate_chunks(h_ref):
        def gate_chunk(c):
            def run():
                cs = slice(c * MXU_N, (c + 1) * MXU_N)
                g_ref[:, cs] = _sigmoid(_dot(h_ref[...], w_mg_ref[:, cs]) + b_mg_ref[:, cs])
            return run

        return [gate_chunk(c) for c in range(2 * D_MODEL // MXU_N)]

    def phase(h_cur_ref, a_next, b_parts):
        gc = gate_chunks(h_cur_ref)
        interleave(gc[:2] + a_next[:1] + gc[2:] + a_next[1:], b_parts, lead=12)

    def stage_b(x_ref, p_ref, t_idx, y_rows):
        blocks = [slice(blk * BLK, (blk + 1) * BLK) for blk in range(nblk)]

        def qkv_stream():
            seg = seg_ref[...]
            rc, ru, rd = _rope_tables(cb_ref[pl.ds(t_idx, 1), :], sb_ref[pl.ds(t_idx, 1), :],
                                      cl_ref[...], sl_ref[...], sgn_ref[0:1, :], sgn_ref[1:2, :])
            qa = p_ref[:, OFF_QA:OFF_QA + WIDTH_A]
            qa = qa * _head_norm_scale(qa, seg) * (gains_ref[:, G_Q:G_K] * (HEAD_DIM_A ** -0.5))
            yield 160
            qa = jnp.concatenate(
                [_rope(qa[:, c * LANES:(c + 1) * LANES], rc, ru, rd) for c in range(4)], axis=1)
            yield 160
            ka = p_ref[:, OFF_KA:OFF_KA + KV_WIDTH]
            ka = ka * _head_norm_scale(ka, seg[:KV_WIDTH, :KV_WIDTH]) * gains_ref[:, G_K:G_O]
            ka = _rope(ka, rc, ru, rd)
            va = p_ref[:, OFF_VA:OFF_VA + KV_WIDTH]
            ka_sw = pltpu.roll(ka, HEAD_DIM_A, 1)
            va_sw = pltpu.roll(va, HEAD_DIM_A, 1)
            rows4 = lax.broadcasted_iota(jnp.int32, (4 * BLK, 1), 0) // BLK

            def sink_col(heads):
                col = jnp.zeros((4 * BLK, 1), F32)
                for n, hd in enumerate(heads):
                    col = jnp.where(rows4 == n, sinks_ref[hd], col)
                return col
            kwin_ref[...] = ka[tile - WINDOW:, :]
            vwin_ref[...] = va[tile - WINDOW:, :]
            res = (qa, ka, va, ka_sw, va_sw, sink_col((0, 2, 5, 7)), sink_col((1, 3, 4, 6)))
            yield 120
            return res

        def gates_stream():
            lb = _lower_bound(lbl_ref[...])
            tri = tri_ref[...]
            gates = []
            half_w = WIDTH_B // 2
            for rs in blocks:
                halves = []
                for c0 in (0, half_w):
                    cs = slice(c0, c0 + half_w)
                    halves.append(_hgrn_gates(p_ref[rs, OFF_FB + c0:OFF_FB + c0 + half_w],
                                              lb[:, cs], 1.0 - lb[:, cs], tri))
                    yield 240
                gates.append(tuple(jnp.concatenate(pair, axis=1) for pair in zip(*halves)))
            span = _hgrn_span_decay(gates[0][2])
            for g in gates[1:]:
                span = jnp.maximum(span, _hgrn_span_decay(g[2]))
            mild = jnp.max(span) < HGRN_FAST_MAX
            yield 10
            return gates, mild

        (qa, ka, va, ka_sw, va_sw, sink_a, sink_b), (gates, mild) = (
            yield from _round_robin(qkv_stream(), gates_stream()))

        kv_refs = (kprev_ref, kprev_sw_ref, vprev_ref, vprev_sw_ref)
        kv_new = [a.astype(BF) for a in (ka, ka_sw, va, va_sw)]
        kv_old = [r[...] for r in kv_refs]
        for r, a in zip(kv_refs, kv_new):
            r[...] = a[blocks[-1]]

        def attend(blk, rs):
            cats = [jnp.concatenate([old if blk == 0 else new[blocks[blk - 1]], new[rs]], axis=0)
                    for old, new in zip(kv_old, kv_new)]
            bias = bias_ref[jnp.where(t_idx == 0, 0, 1)] if blk == 0 else bias_ref[1]
            return (yield from _attn_block(qa[rs], *cats, bias, sink_a, sink_b))

        def recur(blk, rs):
            kin, _, cum = gates[blk]
            qb, ib = p_ref[rs, OFF_QB:OFF_QB + WIDTH_B], p_ref[rs, OFF_IB:OFF_IB + WIDTH_B]
            base, full = yield from _recur_block_fast(qb, kin, cum, ib, st_ref)
            obase_ref[rs, :] = base
            ob_ref[rs, :] = full

        ya_parts = []
        for blk, rs in enumerate(blocks):
            res = yield from _round_robin(attend(blk, rs), recur(blk, rs))
            ya_parts.append(res[0])

        def _probe_unused():
            for rs, (kin, f, cum) in zip(blocks, gates):
                qb, ib = p_ref[rs, OFF_QB:OFF_QB + WIDTH_B], p_ref[rs, OFF_IB:OFF_IB + WIDTH_B]
                ob_ref[rs, :] = obase_ref[rs, :] + _hgrn_intra_robust(
                    qb, kin, f, cum, ib, lmask_ref)
        yield 0

        ya_bf = (jnp.concatenate(ya_parts, axis=0)
                 * _silu(p_ref[:, OFF_ZA:OFF_ZA + WIDTH_A])).astype(BF)
        yield 160
        yb_bf = _branch_b_out(ob_ref[...], p_ref[:, OFF_ZB:OFF_ZB + WIDTH_B],
                              gains_ref[:, G_O:G_END]).astype(BF)
        yield 330

        y = (g_ref[:, :D_MODEL] * _dot(ya_bf, w_pa_ref[...])
             + g_ref[:, D_MODEL:] * _dot(yb_bf, w_pb_ref[...]))
        y_ref[y_rows, :] = x_ref[...] + _modulation(mod_ref)[2] * _dot(y.astype(BF), w_out_ref[...])
        yield 0

    def interleave(a_thunks, b_parts, lead):
        for th in a_thunks[:lead]:
            th()
        acc, done = 0, lead
        for cost in b_parts:
            acc += cost
            upto = lead + int(round((len(a_thunks) - lead) * min(acc, STAGE_B_COST) / STAGE_B_COST))
            for th in a_thunks[done:upto]:
                th()
            done = upto
        assert done == len(a_thunks) and acc == STAGE_B_COST, (done, acc)

    @pl.when(s == 0)
    def _():
        st_ref[...] = jnp.zeros_like(st_ref)
        for r in (kprev_ref, kprev_sw_ref, vprev_ref, vprev_sw_ref):
            r[...] = jnp.zeros_like(r)
        for th in stage_a(x_refs[0], h0_ref, p0_ref):
            th()

    for k in range(tps):
        cur, nxt = k % 2, (k + 1) % 2
        phase(h_refs[cur], stage_a(x_refs[k + 1], h_refs[nxt], p_refs[nxt]),
              stage_b(x_refs[k], p_refs[cur], tps * s + k, slice(k * tile, (k + 1) * tile)))

    @pl.when(s == n_steps - 1)
    def _():
        for hd in range(N_HEADS_B):
            state_ref[hd] = st_ref[hd].T
        kwin_ref[...] = kwin_ref[...].T
        vwin_ref[...] = vwin_ref[...].T


def _const_spec(shape):
    nd = len(shape)
    return pl.BlockSpec(shape, lambda i, *_: (0,) * nd, pipeline_mode=pl.Buffered(1))


def _prompt_call(x, mod, consts, w, tile, tps):
    t = x.shape[0]
    n_tiles = t // tile
    n_steps = n_tiles // tps
    assert n_steps * tps * tile == t and tps % 2 == 0
    row = lambda n: _const_spec((1, n))
    x_spec = lambda k: pl.BlockSpec(
        (tile, D_MODEL), lambda i, *_: (jnp.minimum(tps * i + k, n_tiles - 1), 0))
    in_specs = [x_spec(k) for k in range(tps + 1)] + [
        row(3 * D_MODEL), row(D_MODEL),
        _const_spec((D_MODEL, D_IN)), _const_spec((D_MODEL, 2 * D_MODEL)), row(2 * D_MODEL),
        _const_spec((WIDTH_A, D_MODEL)), _const_spec((WIDTH_B, D_MODEL)),
        _const_spec((D_MODEL, D_MODEL)),
        row(G_END), _const_spec((2, WIDTH_B)),
        _const_spec((n_tiles, LANES)), _const_spec((n_tiles, LANES)),
        _const_spec((tile, LANES)), _const_spec((tile, LANES)), _const_spec((2, LANES)),
        _const_spec((WIDTH_A, WIDTH_A)), _const_spec((BLK, BLK)),
        _const_spec((2, 4 * BLK, 2 * BLK)), _const_spec((len(LEVELS), BLK, BLK)),
    ]
    out_specs = [
        pl.BlockSpec((tps * tile, D_MODEL), lambda i, *_: (i, 0)),
        pl.BlockSpec((WINDOW, KV_WIDTH), lambda i, *_: (0, 0)),
        pl.BlockSpec((WINDOW, KV_WIDTH), lambda i, *_: (0, 0)),
        pl.BlockSpec((N_HEADS_B, HEAD_DIM_B, HEAD_DIM_B), lambda i, *_: (0, 0, 0)),
    ]
    out_shape = [
        jax.ShapeDtypeStruct((t, D_MODEL), F32),
        jax.ShapeDtypeStruct((WINDOW, KV_WIDTH), F32),
        jax.ShapeDtypeStruct((WINDOW, KV_WIDTH), F32),
        jax.ShapeDtypeStruct((N_HEADS_B, HEAD_DIM_B, HEAD_DIM_B), F32),
    ]
    scratch = [
        pltpu.VMEM((N_HEADS_B, HEAD_DIM_B, HEAD_DIM_B), F32),
        pltpu.VMEM((BLK, KV_WIDTH), BF), pltpu.VMEM((BLK, KV_WIDTH), BF),
        pltpu.VMEM((BLK, KV_WIDTH), BF), pltpu.VMEM((BLK, KV_WIDTH), BF),
        pltpu.VMEM((tile, D_IN), F32), pltpu.VMEM((tile, D_IN), F32),
        pltpu.VMEM((tile, 2 * D_MODEL), F32),
        pltpu.VMEM((tile, D_MODEL), BF), pltpu.VMEM((tile, D_MODEL), BF),
        pltpu.VMEM((tile, WIDTH_B), F32), pltpu.VMEM((tile, WIDTH_B), F32),
    ]
    return pl.pallas_call(
        functools.partial(_prompt_kernel, tile=tile, n_steps=n_steps, tps=tps),
        grid_spec=pltpu.PrefetchScalarGridSpec(
            num_scalar_prefetch=1, grid=(n_steps,),
            in_specs=in_specs, out_specs=out_specs, scratch_shapes=scratch),
        out_shape=out_shape,
        compiler_params=pltpu.CompilerParams(
            dimension_semantics=("arbitrary",), vmem_limit_bytes=VMEM_LIMIT),
        name="prompt_layer",
    )(w["sinks"], *([x] * (tps + 1)), mod, w["norm_g"],
      w["w_in"], w["w_merge"], w["b_merge"], w["w_proj_a"], w["w_proj_b"], w["w_out"],
      w["gains"], w["lb_logits"],
      consts["rope_cb"], consts["rope_sb"], consts["rope_cl"], consts["rope_sl"],
      consts["rope_sgn"],
      consts["seg"], consts["tri"], consts["bias"], consts["lmask"])


def _dec_in_kernel(x_ref, mod_ref, ng_ref, w_in_ref, w_mg_ref, b_mg_ref, gains_ref, lbl_ref,
                   rc_ref, ru_ref, rd_ref, seg_ref, perm_ref,
                   w_in_bf_ref, w_mg_bf_ref, g_ref,
                   qm_ref, kn_ref, vn_ref, za_ref, qb_ref, kin_ref, ib_ref, zb_ref,
                   h_ref, p_ref):
    j = pl.program_id(0)
    half_in = D_IN // 2

    @pl.when(j == 0)
    def _():
        h_ref[...] = _norm_modulate(x_ref[:, 0, :], ng_ref[...], mod_ref)

    w_in_bf = w_in_ref[...].astype(BF)
    w_mg_bf = w_mg_ref[...].astype(BF)
    w_in_bf_ref[...] = w_in_bf
    w_mg_bf_ref[...] = w_mg_bf
    h_bf = h_ref[...]
    g_ref[...] = _sigmoid(_dot(h_bf, w_mg_bf) + b_mg_ref[...])
    part = _dot(h_bf, w_in_bf)

    @pl.when(j == 0)
    def _():
        p_ref[:, :half_in] = part

    @pl.when(j == 1)
    def _():
        p_ref[:, half_in:] = part
        seg = seg_ref[...]
        rc, ru, rd = rc_ref[...], ru_ref[...], rd_ref[...]
        qa = p_ref[:, OFF_QA:OFF_QA + WIDTH_A]
        qa = qa * _head_norm_scale(qa, seg) * (gains_ref[:, G_Q:G_K] * (HEAD_DIM_A ** -0.5))
        qa = jnp.concatenate(
            [_rope(qa[:, c * LANES:(c + 1) * LANES], rc, ru, rd) for c in range(4)], axis=1)
        qm = _dot(qa.astype(BF), perm_ref[...])
        for hd in range(N_HEADS_A):
            qm_ref[:, hd, :] = qm[:, hd * LANES:(hd + 1) * LANES]
        ka = p_ref[:, OFF_KA:OFF_KA + KV_WIDTH]
        ka = ka * _head_norm_scale(ka, seg[:KV_WIDTH, :KV_WIDTH]) * gains_ref[:, G_K:G_O]
        kn_ref[...] = _rope(ka, rc, ru, rd)
        vn_ref[...] = p_ref[:, OFF_VA:OFF_VA + KV_WIDTH]
        za_ref[...] = p_ref[:, OFF_ZA:OFF_ZA + WIDTH_A]
        qb_ref[...] = p_ref[:, OFF_QB:OFF_QB + WIDTH_B]
        lb = _lower_bound(lbl_ref[...])
        kin_ref[...] = (1.0 - lb) * (1.0 - _sigmoid(p_ref[:, OFF_FB:OFF_FB + WIDTH_B]))
        ib_ref[...] = p_ref[:, OFF_IB:OFF_IB + WIDTH_B]
        zb_ref[...] = p_ref[:, OFF_ZB:OFF_ZB + WIDTH_B]


def _dec_in_call(x, mod, consts, w):
    b = x.shape[0]
    n_steps = 2
    half_in, half_mg = D_IN // n_steps, 2 * D_MODEL // n_steps
    assert half_in % LANES == 0
    const = lambda a: pl.BlockSpec(a.shape, lambda j: (0,) * a.ndim)
    cols = lambda rows, n: pl.BlockSpec((rows, n), lambda j: (0, j))
    row_out = lambda n: pl.BlockSpec((b, n), lambda j: (0, 0))
    f = lambda n: jax.ShapeDtypeStruct((b, n), F32)
    widths = [KV_WIDTH, KV_WIDTH, WIDTH_A, WIDTH_B, WIDTH_B, WIDTH_B, WIDTH_B]
    qm_shape = (b, N_HEADS_A, LANES)
    small = [w["norm_g"]]
    tail = [w["gains"], w["lb_logits"], consts["rope_c1"], consts["rope_u1"], consts["rope_d1"],
            consts["seg"], consts["perm"]]
    return pl.pallas_call(
        _dec_in_kernel,
        grid=(n_steps,),
        in_specs=[const(x), const(mod)] + [const(a) for a in small]
        + [cols(D_MODEL, half_in), cols(D_MODEL, half_mg), cols(1, half_mg)]
        + [const(a) for a in tail],
        out_specs=[cols(D_MODEL, half_in), cols(D_MODEL, half_mg), cols(b, half_mg),
                   pl.BlockSpec(qm_shape, lambda j: (0, 0, 0))]
        + [row_out(n) for n in widths],
        out_shape=[jax.ShapeDtypeStruct((D_MODEL, D_IN), BF),
                   jax.ShapeDtypeStruct((D_MODEL, 2 * D_MODEL), BF), f(2 * D_MODEL),
                   jax.ShapeDtypeStruct(qm_shape, F32)]
        + [f(n) for n in widths],
        scratch_shapes=[pltpu.VMEM((b, D_MODEL), BF), pltpu.VMEM((b, D_IN), F32)],
        compiler_params=pltpu.CompilerParams(
            dimension_semantics=("arbitrary",), vmem_limit_bytes=VMEM_LIMIT),
        name="decode_in",
    )(x, mod, *small, w["w_in_f32"], w["w_merge_f32"], w["b_merge"], *tail)


def _dec_mix_kernel(sinks_ref, qm_ref, kn_ref, vn_ref, qb_ref, kin_ref, ib_ref,
                    ck_ref, cv_ref, st_ref,
                    att_ref, ob_ref, nk_ref, nv_ref, nst_ref, *, bt):
    nh = N_HEADS_A
    s = jnp.concatenate([_dot(qm_ref[j].astype(BF), ck_ref[j].astype(BF))
                         for j in range(bt)], axis=0)
    key = lax.broadcasted_iota(jnp.int32, (bt * nh, WINDOW), 1)
    s = jnp.where(key == 0, NEG, s)
    rep = lambda r: jnp.broadcast_to(r[...][:, None, :], (bt, nh, r.shape[-1])).reshape(
        bt * nh, r.shape[-1])
    kn_rows, vn_rows = rep(kn_ref), rep(vn_ref)
    head = lax.broadcasted_iota(jnp.int32, (bt * nh, 1), 0) % nh
    sink = jnp.zeros((bt * nh, 1), F32)
    for hd in range(nh):
        sink = jnp.where(head == hd, sinks_ref[hd], sink)
    s_new = jnp.sum(qm_ref[...].reshape(bt * nh, LANES) * kn_rows, axis=-1, keepdims=True)
    m = jnp.maximum(jnp.maximum(jnp.max(s, axis=-1, keepdims=True), s_new), sink)
    p = jnp.exp(s - m)
    p_new = jnp.exp(s_new - m)
    den = jnp.sum(p, axis=-1, keepdims=True) + p_new + jnp.exp(sink - m)
    p_bf = p.astype(BF)
    pv = jnp.concatenate([_dot_nt(p_bf[j * nh:(j + 1) * nh, :], cv_ref[j].astype(BF))
                          for j in range(bt)], axis=0)
    att_ref[...] = ((pv + p_new * vn_rows) * (1.0 / den)).reshape(bt, nh, LANES)
    kn_t, vn_t = kn_ref[...].T, vn_ref[...].T
    newest = lax.broadcasted_iota(jnp.int32, (KV_WIDTH, WINDOW), 1) == WINDOW - 1
    for j in range(bt):
        nk_ref[j] = jnp.where(newest, kn_t[:, j:j + 1], pltpu.roll(ck_ref[j], WINDOW - 1, 1))
        nv_ref[j] = jnp.where(newest, vn_t[:, j:j + 1], pltpu.roll(cv_ref[j], WINDOW - 1, 1))

    for hd in range(N_HEADS_B):
        sl = slice(hd * LANES, (hd + 1) * LANES)
        kin_t = kin_ref[:, sl].T
        q_bf = qb_ref[:, sl].astype(BF)
        for j in range(bt):
            kcol = kin_t[:, j:j + 1]
            st = st_ref[j, hd]
            new = st - kcol * (st - ib_ref[j:j + 1, sl])
            nst_ref[j, hd] = new
            ob_ref[j:j + 1, sl] = _dot(q_bf, new.astype(BF))[j:j + 1, :]


def _dec_mix_call(sinks, qm, kn, vn, qb, kin, ib, cache_k, cache_v, state, bt):
    b = kn.shape[0]
    rows = lambda n: pl.BlockSpec((bt, n), lambda i, *_: (i, 0))
    cache_spec = pl.BlockSpec((bt, WINDOW, KV_WIDTH), lambda i, *_: (i, 0, 0))
    st_spec = pl.BlockSpec((bt, N_HEADS_B, HEAD_DIM_B, HEAD_DIM_B), lambda i, *_: (i, 0, 0, 0))
    qm_spec = pl.BlockSpec((bt, N_HEADS_A, LANES), lambda i, *_: (i, 0, 0))
    return pl.pallas_call(
        functools.partial(_dec_mix_kernel, bt=bt),
        grid_spec=pltpu.PrefetchScalarGridSpec(
            num_scalar_prefetch=1, grid=(b // bt,),
            in_specs=[qm_spec, rows(KV_WIDTH), rows(KV_WIDTH), rows(WIDTH_B), rows(WIDTH_B),
                      rows(WIDTH_B), cache_spec, cache_spec, st_spec],
            out_specs=[qm_spec, rows(WIDTH_B), cache_spec, cache_spec, st_spec]),
        out_shape=[jax.ShapeDtypeStruct((b, N_HEADS_A, LANES), F32),
                   jax.ShapeDtypeStruct((b, WIDTH_B), F32),
                   jax.ShapeDtypeStruct(cache_k.shape, F32),
                   jax.ShapeDtypeStruct(cache_v.shape, F32),
                   jax.ShapeDtypeStruct(state.shape, F32)],
        compiler_params=pltpu.CompilerParams(
            dimension_semantics=("arbitrary",), vmem_limit_bytes=VMEM_LIMIT),
        name="decode_mix",
    )(sinks, qm, kn, vn, qb, kin, ib, cache_k, cache_v, state)


def _dec_out_kernel(x_ref, mod_ref, att_ref, za_ref, ob_ref, zb_ref, g_ref, gains_ref,
                    permt_ref, w_pa_ref, w_pb_ref, w_out_ref,
                    y_ref, w_pa_bf_ref, w_pb_bf_ref, w_out_bf_ref):
    w_pa, w_pb, w_out = (r[...].astype(BF) for r in (w_pa_ref, w_pb_ref, w_out_ref))
    w_pa_bf_ref[...] = w_pa
    w_pb_bf_ref[...] = w_pb
    w_out_bf_ref[...] = w_out
    att = sum(_split_dot(att_ref[:, hd, :], permt_ref[hd * LANES:(hd + 1) * LANES, :])
              for hd in range(N_HEADS_A))
    ya = att * _silu(za_ref[...])
    yb = _branch_b_out(ob_ref[...], zb_ref[...], gains_ref[:, G_O:G_END])
    g = g_ref[...]
    y = (g[:, :D_MODEL] * _dot(ya.astype(BF), w_pa) + g[:, D_MODEL:] * _dot(yb.astype(BF), w_pb))
    y_ref[:, 0, :] = x_ref[:, 0, :] + _modulation(mod_ref)[2] * _dot(y.astype(BF), w_out)


def _dec_out_call(x, mod, att, za, ob, zb, g, consts, w):
    bf = lambda a: jax.ShapeDtypeStruct(a.shape, BF)
    ws = (w["w_proj_a_f32"], w["w_proj_b_f32"], w["w_out_f32"])
    return pl.pallas_call(
        _dec_out_kernel,
        out_shape=[jax.ShapeDtypeStruct(x.shape, F32)] + [bf(a) for a in ws],
        compiler_params=pltpu.CompilerParams(vmem_limit_bytes=VMEM_LIMIT),
        name="decode_out",
    )(x, mod, att, za, ob, zb, g, w["gains"], consts["permt"], *ws)


def _rope_consts(n_tiles, tile, past_len):
    half = ROT_DIM // 2
    inv = ROPE_THETA ** (-np.arange(0, ROT_DIM, 2, dtype=np.float64) / ROT_DIM)
    e = np.arange(LANES) % HEAD_DIM_A
    rot = e < ROT_DIM

    def tables(pos):
        ang = np.asarray(pos, np.float64)[:, None] * inv[e % half][None, :]
        return np.where(rot, np.cos(ang), 1.0), np.where(rot, np.sin(ang), 0.0)

    sgn = np.stack([np.where(e < half, -1.0, 0.0),
                    np.where(rot & (e >= half), 1.0, 0.0)])
    cb, sb = tables(np.arange(n_tiles) * tile)
    cl, sl = tables(np.arange(tile))
    c1, s1 = tables([past_len])
    f = lambda a: jnp.asarray(a, F32)
    return {"rope_cb": f(cb), "rope_sb": f(sb), "rope_cl": f(cl), "rope_sl": f(sl),
            "rope_sgn": f(sgn), "rope_c1": f(c1), "rope_u1": f(s1 * sgn[0:1]),
            "rope_d1": f(s1 * sgn[1:2])}


def _static_consts():
    seg = np.kron(np.eye(N_HEADS_A), np.full((HEAD_DIM_A, HEAD_DIM_A), 1.0 / HEAD_DIM_A))
    tri = np.tril(np.ones((BLK, BLK)))
    r = np.arange(4 * BLK)[:, None] % BLK
    c = np.arange(2 * BLK)[None, :]
    ok_prev = (c < BLK) & (c > r)
    ok_cur = (c >= BLK) & (c - BLK <= r)
    bias = np.stack([np.where(ok_cur, 0.0, NEG), np.where(ok_prev | ok_cur, 0.0, NEG)])
    i = np.arange(BLK)[:, None]
    j = np.arange(BLK)[None, :]
    lmask = np.stack([((i // (2 * b)) == (j // (2 * b))) & ((i & b) != 0) & ((j & b) == 0)
                      for b in LEVELS]).astype(np.float32)
    perm = np.zeros((WIDTH_A, N_HEADS_A * LANES), np.float32)
    for hd in range(N_HEADS_A):
        kvh = hd // (N_HEADS_A // N_KV_A)
        for d in range(HEAD_DIM_A):
            perm[hd * HEAD_DIM_A + d, hd * LANES + kvh * HEAD_DIM_A + d] = 1.0
    return {
        "seg": jnp.asarray(seg, BF), "tri": jnp.asarray(tri, BF),
        "bias": jnp.asarray(bias, F32), "lmask": jnp.asarray(lmask, F32),
        "perm": jnp.asarray(perm, BF), "permt": jnp.asarray(perm.T, BF),
    }


def kernel(x_prompt, x_sample, cache_win_k, cache_win_v, state_hgrn, c_prompt, c_sample,
           w_ada, b_ada, norm_g, w_in, q_norm_g, k_norm_g, sinks, lb_logits, o_norm_g,
           w_merge, b_merge, w_proj_a, w_proj_b, w_out):
    depth = w_in.shape[0]
    assert depth == 1 and x_prompt.shape[0] == 1 and x_sample.shape[1] == 1
    t = x_prompt.shape[1]
    nb = x_sample.shape[0]
    past_len = t
    tile = PROMPT_TILE
    bt = DECODE_BATCH_TILE

    consts = _static_consts()
    consts.update(_rope_consts(t // tile, tile, past_len))

    w = {
        "sinks": sinks[0], "norm_g": norm_g,
        "w_in_f32": w_in[0], "w_merge_f32": w_merge[0], "b_merge": b_merge,
        "w_proj_a_f32": w_proj_a[0], "w_proj_b_f32": w_proj_b[0], "w_out_f32": w_out[0],
        "lb_logits": lb_logits,
    }

    mod_p, mod_s, w["gains"] = _ada_call(c_prompt, c_sample, w_ada[0], b_ada,
                                         q_norm_g, k_norm_g, o_norm_g)

    xs = x_sample
    (w["w_in"], w["w_merge"], g,
     qm, kn, vn, za, qb, kin, ib, zb) = _dec_in_call(xs, mod_s, consts, w)
    to_t = lambda c: jnp.transpose(c[0], (0, 2, 3, 1)).reshape(nb, KV_WIDTH, WINDOW)
    from_t = lambda c: jnp.transpose(
        c.reshape(-1, N_KV_A, HEAD_DIM_A, WINDOW), (0, 3, 1, 2))[None]
    att, ob, nk, nv, nst = _dec_mix_call(
        w["sinks"], qm, kn, vn, qb, kin, ib,
        to_t(cache_win_k), to_t(cache_win_v), state_hgrn[0], bt)
    y_s, w["w_proj_a"], w["w_proj_b"], w["w_out"] = _dec_out_call(
        xs, mod_s, att, za, ob, zb, g, consts, w)

    y_p, kwin, vwin, st_p = _prompt_call(x_prompt[0], mod_p, consts, w, tile,
                                         PROMPT_TILES_PER_STEP)

    return (y_p[None], y_s,
            from_t(kwin[None]), from_t(vwin[None]), st_p[None, None],
            from_t(nk), from_t(nv), nst[None])
```

```python
import functools

import numpy as np
import jax
import jax.numpy as jnp
from jax import lax
from jax.experimental import pallas as pl
from jax.experimental.pallas import tpu as pltpu

D_MODEL = 1024
HEAD_DIM_A = 64
N_HEADS_A = 8
N_KV_A = 2
WIDTH_A = N_HEADS_A * HEAD_DIM_A
KV_WIDTH = N_KV_A * HEAD_DIM_A
WINDOW = 128
ROT_DIM = HEAD_DIM_A // 4
ROPE_THETA = 500000.0
HEAD_DIM_B = 128
N_HEADS_B = 4
WIDTH_B = N_HEADS_B * HEAD_DIM_B
EPS = 1e-6
OFF_QA = 0
OFF_KA = OFF_QA + WIDTH_A
OFF_VA = OFF_KA + KV_WIDTH
OFF_ZA = OFF_VA + KV_WIDTH
OFF_QB = OFF_ZA + WIDTH_A
OFF_FB = OFF_QB + WIDTH_B
OFF_IB = OFF_FB + WIDTH_B
OFF_ZB = OFF_IB + WIDTH_B
D_IN = OFF_ZB + WIDTH_B

LANES = 128
BLK = 128
SUB = 8
LEVELS = (64, 32, 16, 8)
HGRN_FAST_MAX = 80.0
MXU_N = 256
NEG = -1e30
VMEM_LIMIT = 56 * 1024 * 1024
PROMPT_TILE = 256
PROMPT_TILES_PER_STEP = 2
DECODE_BATCH_TILE = 16
G_Q, G_K, G_O = 0, WIDTH_A, WIDTH_A + KV_WIDTH
G_END = G_O + WIDTH_B

BF = jnp.bfloat16
F32 = jnp.float32


def _dot(a, b):
    return jnp.dot(a, b, preferred_element_type=F32)


def _dot_nt(a, b):
    return lax.dot_general(a, b, (((1,), (1,)), ((), ())), preferred_element_type=F32)


def _dot_tn(a, b):
    return lax.dot_general(a, b, (((0,), (0,)), ((), ())), preferred_element_type=F32)


def _split_dot(a_f32, b_bf):
    hi = a_f32.astype(BF)
    lo = (a_f32 - hi.astype(F32)).astype(BF)
    return _dot(hi, b_bf) + _dot(lo, b_bf)


def _sigmoid(x):
    return 1.0 / (1.0 + jnp.exp(-x))


def _silu(x):
    return x * _sigmoid(x)


def _lower_bound(lb_logits):
    l0 = lb_logits[0:1, :]
    l1 = lb_logits[1:2, :]
    m = jnp.maximum(l0, l1)
    e0 = jnp.exp(l0 - m)
    e1 = jnp.exp(l1 - m)
    return e0 / (e0 + e1)


def _rope_tables(cb, sb, cl, sl, sgn_up, sgn_dn):
    c = cb * cl - sb * sl
    s = sb * cl + cb * sl
    return c, s * sgn_up, s * sgn_dn


def _rope(x, c, s_up, s_dn):
    return x * c + pltpu.roll(x, LANES - ROT_DIM // 2, 1) * s_up + pltpu.roll(x, ROT_DIM // 2, 1) * s_dn


def _modulation(mod_ref):
    return (mod_ref[:, 0:D_MODEL], mod_ref[:, D_MODEL:2 * D_MODEL],
            mod_ref[:, 2 * D_MODEL:3 * D_MODEL])


def _norm_modulate(x, ng, mod_ref):
    shift, scale, _ = _modulation(mod_ref)
    ms = jnp.mean(x * x, axis=-1, keepdims=True)
    return (x * lax.rsqrt(ms + EPS) * (ng * (1.0 + scale)) + shift).astype(BF)


def _head_norm_scale(x, seg_mean_bf):
    ms = _dot((x * x).astype(BF), seg_mean_bf)
    return lax.rsqrt(ms + EPS)


def _ada_kernel(cp_ref, cs_ref, w_ref, b_ref, qg_ref, kg_ref, og_ref, op_ref, os_ref, gains_ref):
    gains_ref[...] = jnp.concatenate(
        [qg_ref[...]] * N_HEADS_A + [kg_ref[...]] * N_KV_A + [og_ref[...]] * N_HEADS_B, axis=1)
    w = w_ref[...]
    w_hi = w.astype(BF)
    w_lo = (w - w_hi.astype(F32)).astype(BF)
    b = b_ref[...]

    ns, n_p = cs_ref.shape[0], cp_ref.shape[0]
    c = jnp.concatenate([cs_ref[...], cp_ref[...],
                         jnp.zeros(((-n_p) % SUB, D_MODEL), F32)], axis=0)
    c_hi = c.astype(BF)
    c_lo = (c - c_hi.astype(F32)).astype(BF)
    out = _dot(c_hi, w_hi) + (_dot(c_hi, w_lo) + _dot(c_lo, w_hi)) + b
    os_ref[...] = out[:ns, :]
    op_ref[...] = out[ns:ns + n_p, :]


def _ada_call(c_p, c_s, w_ada, b_ada, q_g, k_g, o_g):
    mp, ms = c_p.shape[0], c_s.shape[0]
    n = w_ada.shape[1]
    tn = 512
    const = lambda a: pl.BlockSpec(a.shape, lambda j: (0,) * a.ndim)
    return pl.pallas_call(
        _ada_kernel,
        grid=(n // tn,),
        in_specs=[const(c_p), const(c_s),
                  pl.BlockSpec((D_MODEL, tn), lambda j: (0, j)),
                  pl.BlockSpec((1, tn), lambda j: (0, j)),
                  const(q_g), const(k_g), const(o_g)],
        out_specs=[pl.BlockSpec((mp, tn), lambda j: (0, j)),
                   pl.BlockSpec((ms, tn), lambda j: (0, j)),
                   pl.BlockSpec((1, G_END), lambda j: (0, 0))],
        out_shape=[jax.ShapeDtypeStruct((mp, n), F32), jax.ShapeDtypeStruct((ms, n), F32),
                   jax.ShapeDtypeStruct((1, G_END), F32)],
        name="ada",
    )(c_p, c_s, w_ada, b_ada, q_g, k_g, o_g)


STAGE_B_COST = 3840


def _round_robin(*gens):
    results = [None] * len(gens)
    live = list(range(len(gens)))
    while live:
        for n in list(live):
            try:
                yield next(gens[n])
            except StopIteration as stop:
                results[n] = stop.value
                live.remove(n)
    return results


def _attn_block(q_blk, kcat, kcat_sw, vcat, vcat_sw, bias, sink_a, sink_b):
    lane = lax.broadcasted_iota(jnp.int32, (BLK, LANES), 1)
    lo = lane < HEAD_DIM_A
    chunks = [q_blk[:, c * LANES:(c + 1) * LANES] for c in range(4)]
    zero = jnp.zeros((BLK, LANES), F32)
    q_lo = [jnp.where(lo, c, zero).astype(BF) for c in chunks]
    q_hi = [jnp.where(lo, zero, c).astype(BF) for c in chunks]
    qa = jnp.concatenate([q_lo[0], q_lo[1], q_hi[2], q_hi[3]], axis=0)
    qb = jnp.concatenate([q_hi[0], q_hi[1], q_lo[2], q_lo[3]], axis=0)

    def probs(qs, kc, sink):
        s = _dot_nt(qs, kc) + bias
        m = jnp.maximum(jnp.max(s, axis=-1, keepdims=True), sink)
        p = jnp.exp(s - m)
        den = jnp.sum(p, axis=-1, keepdims=True) + jnp.exp(sink - m)
        return p.astype(BF), 1.0 / den

    pa, ra = probs(qa, kcat, sink_a)
    yield 200
    pb, rb = probs(qb, kcat_sw, sink_b)
    yield 200
    oa = _dot(pa, vcat) * ra
    yield 40
    ob = _dot(pb, vcat_sw) * rb
    yield 40
    r = lambda o, i: o[i * BLK:(i + 1) * BLK, :]
    out = jnp.concatenate([
        jnp.where(lo, r(oa, 0), r(ob, 0)),
        jnp.where(lo, r(oa, 1), r(ob, 1)),
        jnp.where(lo, r(ob, 2), r(oa, 2)),
        jnp.where(lo, r(ob, 3), r(oa, 3)),
    ], axis=1)
    yield 30
    return out


def _hgrn_gates(fb, lb, one_m_lb, tri_bf):
    sig = _sigmoid(fb)
    kin = one_m_lb * (1.0 - sig)
    f = lb + one_m_lb * sig
    cum = _split_dot_left(tri_bf, jnp.log(f))
    return kin, f, cum


def _hgrn_span_decay(cum):
    q = BLK // 4
    ends = [cum[(n + 1) * q - 1:(n + 1) * q, :] for n in range(4)]
    d = -ends[0]
    for n in range(1, 4):
        d = jnp.maximum(d, ends[n - 1] - ends[n])
    return d


def _hgrn_state_step(qb, kin, cum, ib, st_ref):
    q_dec = (qb * jnp.exp(cum)).astype(BF)
    last = cum[BLK - 1:BLK, :]
    k_dec = (kin * jnp.exp(last - cum)).astype(BF)
    v_bf = ib.astype(BF)
    outs = []
    for h in range(N_HEADS_B):
        sl = slice(h * LANES, (h + 1) * LANES)
        st = st_ref[h]
        outs.append(_dot_nt(q_dec[:, sl], st.astype(BF)))
        st_ref[h] = st * jnp.exp(last[:, sl]) + _dot_tn(v_bf[:, sl], k_dec[:, sl])
    return jnp.concatenate(outs, axis=1)


def _hgrn_apply(amats, ib):
    v_bf = ib.astype(BF)
    return jnp.concatenate(
        [_dot(amats[h].astype(BF), v_bf[:, h * LANES:(h + 1) * LANES])
         for h in range(N_HEADS_B)], axis=1)


def _recur_block_fast(qb, kin, cum, ib, st_ref):
    base = _hgrn_state_step(qb, kin, cum, ib, st_ref)
    yield 130
    half = BLK // 2
    row = lax.broadcasted_iota(jnp.int32, (BLK, 1), 0)
    upper = row >= half
    piv = cum[half - 1:half, :]
    w_lvl = jnp.exp(jnp.concatenate([piv - cum[:half, :], cum[half:, :] - piv], axis=0))
    p_lvl = (jnp.where(upper, qb, kin) * w_lvl).astype(BF)
    mid = jnp.where(upper, cum[half + half // 2 - 1:half + half // 2, :],
                    cum[half // 2 - 1:half // 2, :])
    e_mid = cum - mid
    q_mid = (qb * jnp.exp(e_mid)).astype(BF)
    k_mid = (kin * jnp.exp(-e_mid)).astype(BF)
    yield 230

    ri = lax.broadcasted_iota(jnp.int32, (BLK, BLK), 0)
    ci = lax.broadcasted_iota(jnp.int32, (BLK, BLK), 1)
    same_half_causal = ((ri >= half) == (ci >= half)) & (ci <= ri)
    cross = (ri >= half) & (ci < half)
    amats = []
    for h in range(N_HEADS_B):
        sl = slice(h * LANES, (h + 1) * LANES)
        a_mid = _dot_nt(q_mid[:, sl], k_mid[:, sl])
        a_lvl = _dot_nt(p_lvl[:, sl], p_lvl[:, sl])
        amats.append(jnp.where(same_half_causal, a_mid, jnp.where(cross, a_lvl, 0.0)))
    yield 60
    full = base + _hgrn_apply(amats, ib)
    yield 40
    return full


def _hgrn_intra_robust(qb, kin, f, cum, ib, lvl_mask_ref):
    row = lax.broadcasted_iota(jnp.int32, (BLK, 1), 0)
    lvl_ops = []
    for b in LEVELS:
        pieces = []
        for r0 in range(0, BLK, 2 * b):
            piv = cum[r0 + b - 1:r0 + b, :]
            pieces.append(piv - cum[r0:r0 + b, :])
            pieces.append(cum[r0 + b:r0 + 2 * b, :] - piv)
        w = jnp.exp(jnp.concatenate(pieces, axis=0))
        second = (row & b) != 0
        lvl_ops.append((jnp.where(second, qb, kin) * w).astype(BF))

    n8 = BLK // SUB
    q3 = qb.reshape(n8, SUB, WIDTH_B)
    k3 = kin.reshape(n8, SUB, WIDTH_B)
    f3 = f.reshape(n8, SUB, WIDTH_B)
    v3 = ib.reshape(n8, SUB, WIDTH_B)
    subl = lax.broadcasted_iota(jnp.int32, (n8, SUB, 1), 1)

    def head(x, h):
        return x[..., h * LANES:(h + 1) * LANES]

    g = q3 * k3
    acc = [jnp.sum(head(g, h), axis=-1, keepdims=True) * head(v3, h) for h in range(N_HEADS_B)]
    dec = jnp.ones_like(f3)
    kd = k3
    vd = v3
    for d in range(1, SUB):
        dec = f3 * pltpu.roll(dec, 1, 1)
        kd = pltpu.roll(kd, 1, 1)
        vd = pltpu.roll(vd, 1, 1)
        g = q3 * kd * dec
        ok = subl >= d
        for h in range(N_HEADS_B):
            a = jnp.where(ok, jnp.sum(head(g, h), axis=-1, keepdims=True), 0.0)
            acc[h] = acc[h] + a * head(vd, h)

    amats = []
    for h in range(N_HEADS_B):
        sl = slice(h * LANES, (h + 1) * LANES)
        amat = jnp.zeros((BLK, BLK), F32)
        for li in range(len(LEVELS)):
            p = lvl_ops[li][:, sl]
            amat = amat + lvl_mask_ref[li] * _dot_nt(p, p)
        amats.append(amat)
    diag = jnp.concatenate([acc[h].reshape(BLK, LANES) for h in range(N_HEADS_B)], axis=1)
    return _hgrn_apply(amats, ib) + diag


def _split_dot_left(a_bf, b_f32):
    hi = b_f32.astype(BF)
    lo = (b_f32 - hi.astype(F32)).astype(BF)
    return _dot(a_bf, hi) + _dot(a_bf, lo)


def _branch_b_out(o, zb, og):
    outs = []
    for h in range(N_HEADS_B):
        sl = slice(h * LANES, (h + 1) * LANES)
        oh = o[:, sl]
        ms = jnp.mean(oh * oh, axis=-1, keepdims=True)
        outs.append(oh * lax.rsqrt(ms + EPS))
    return jnp.concatenate(outs, axis=1) * og * _silu(zb)


def _prompt_kernel(sinks_ref, *refs, tile, n_steps, tps, robust):
    x_refs = refs[:tps + 1]
    (mod_ref, ng_ref,
     w_in_ref, w_mg_ref, b_mg_ref, w_pa_ref, w_pb_ref, w_out_ref,
     gains_ref, lbl_ref,
     cb_ref, sb_ref, cl_ref, sl_ref, sgn_ref,
     seg_ref, tri_ref, bias_ref, lmask_ref,
     y_ref, kwin_ref, vwin_ref, state_ref, span_ref,
     st_ref, kprev_ref, kprev_sw_ref, vprev_ref, vprev_sw_ref,
     p0_ref, p1_ref, g_ref, h0_ref, h1_ref, ob_ref) = refs[tps + 1:]
    p_refs, h_refs = (p0_ref, p1_ref), (h0_ref, h1_ref)
    s = pl.program_id(0)
    nblk = tile // BLK

    def stage_a(x_ref, h_ref, p_ref):
        def prep():
            h_ref[...] = _norm_modulate(x_ref[...], ng_ref[...], mod_ref)

        def proj_chunk(c):
            def run():
                cs = slice(c * MXU_N, (c + 1) * MXU_N)
                p_ref[:, cs] = _dot(h_ref[...], w_in_ref[:, cs])
            return run

        return [prep] + [proj_chunk(c) for c in range(D_IN // MXU_N)]

    def gate_chunks(h_ref):
        def gate_chunk(c):
            def run():
                cs = slice(c * MXU_N, (c + 1) * MXU_N)
                g_ref[:, cs] = _sigmoid(_dot(h_ref[...], w_mg_ref[:, cs]) + b_mg_ref[:, cs])
            return run

        return [gate_chunk(c) for c in range(2 * D_MODEL // MXU_N)]

    def phase(h_cur_ref, a_next, b_parts):
        gc = gate_chunks(h_cur_ref)
        interleave(gc[:2] + a_next[:1] + gc[2:] + a_next[1:], b_parts, lead=12)

    def stage_b(x_ref, p_ref, t_idx, y_rows):
        blocks = [slice(blk * BLK, (blk + 1) * BLK) for blk in range(nblk)]

        def qkv_stream():
            seg = seg_ref[...]
            rc, ru, rd = _rope_tables(cb_ref[pl.ds(t_idx, 1), :], sb_ref[pl.ds(t_idx, 1), :],
                                      cl_ref[...], sl_ref[...], sgn_ref[0:1, :], sgn_ref[1:2, :])
            qa = p_ref[:, OFF_QA:OFF_QA + WIDTH_A]
            qa = qa * _head_norm_scale(qa, seg) * (gains_ref[:, G_Q:G_K] * (HEAD_DIM_A ** -0.5))
            yield 160
            qa = jnp.concatenate(
                [_rope(qa[:, c * LANES:(c + 1) * LANES], rc, ru, rd) for c in range(4)], axis=1)
            yield 160
            ka = p_ref[:, OFF_KA:OFF_KA + KV_WIDTH]
            ka = ka * _head_norm_scale(ka, seg[:KV_WIDTH, :KV_WIDTH]) * gains_ref[:, G_K:G_O]
            ka = _rope(ka, rc, ru, rd)
            va = p_ref[:, OFF_VA:OFF_VA + KV_WIDTH]
            ka_sw = pltpu.roll(ka, HEAD_DIM_A, 1)
            va_sw = pltpu.roll(va, HEAD_DIM_A, 1)
            rows4 = lax.broadcasted_iota(jnp.int32, (4 * BLK, 1), 0) // BLK

            def sink_col(heads):
                col = jnp.zeros((4 * BLK, 1), F32)
                for n, hd in enumerate(heads):
                    col = jnp.where(rows4 == n, sinks_ref[hd], col)
                return col
            kwin_ref[...] = ka[tile - WINDOW:, :]
            vwin_ref[...] = va[tile - WINDOW:, :]
            res = (qa, ka, va, ka_sw, va_sw, sink_col((0, 2, 5, 7)), sink_col((1, 3, 4, 6)))
            yield 120
            return res

        def gates_stream():
            lb = _lower_bound(lbl_ref[...])
            tri = tri_ref[...]
            gates = []
            half_w = WIDTH_B // 2
            for rs in blocks:
                halves = []
                for c0 in (0, half_w):
                    cs = slice(c0, c0 + half_w)
                    halves.append(_hgrn_gates(p_ref[rs, OFF_FB + c0:OFF_FB + c0 + half_w],
                                              lb[:, cs], 1.0 - lb[:, cs], tri))
                    yield 240
                gates.append(tuple(jnp.concatenate(pair, axis=1) for pair in zip(*halves)))
            span = span_ref[...]
            for g in gates:
                span = jnp.maximum(span, _hgrn_span_decay(g[2]))
            span_ref[...] = span
            yield 10
            return gates

        (qa, ka, va, ka_sw, va_sw, sink_a, sink_b), gates = (
            yield from _round_robin(qkv_stream(), gates_stream()))

        kv_refs = (kprev_ref, kprev_sw_ref, vprev_ref, vprev_sw_ref)
        kv_new = [a.astype(BF) for a in (ka, ka_sw, va, va_sw)]
        kv_old = [r[...] for r in kv_refs]
        for r, a in zip(kv_refs, kv_new):
            r[...] = a[blocks[-1]]

        def attend(blk, rs):
            cats = [jnp.concatenate([old if blk == 0 else new[blocks[blk - 1]], new[rs]], axis=0)
                    for old, new in zip(kv_old, kv_new)]
            bias = bias_ref[jnp.where(t_idx == 0, 0, 1)] if blk == 0 else bias_ref[1]
            return (yield from _attn_block(qa[rs], *cats, bias, sink_a, sink_b))

        def recur(blk, rs):
            kin, f, cum = gates[blk]
            qb, ib = p_ref[rs, OFF_QB:OFF_QB + WIDTH_B], p_ref[rs, OFF_IB:OFF_IB + WIDTH_B]
            if robust:
                base = _hgrn_state_step(qb, kin, cum, ib, st_ref)
                yield 130
                full = base + _hgrn_intra_robust(qb, kin, f, cum, ib, lmask_ref)
                yield 230 + 60 + 40
            else:
                full = yield from _recur_block_fast(qb, kin, cum, ib, st_ref)
            ob_ref[rs, :] = full

        ya_parts = []
        for blk, rs in enumerate(blocks):
            res = yield from _round_robin(attend(blk, rs), recur(blk, rs))
            ya_parts.append(res[0])

        ya_bf = (jnp.concatenate(ya_parts, axis=0)
                 * _silu(p_ref[:, OFF_ZA:OFF_ZA + WIDTH_A])).astype(BF)
        yield 160
        yb_bf = _branch_b_out(ob_ref[...], p_ref[:, OFF_ZB:OFF_ZB + WIDTH_B],
                              gains_ref[:, G_O:G_END]).astype(BF)
        yield 330

        y = (g_ref[:, :D_MODEL] * _dot(ya_bf, w_pa_ref[...])
             + g_ref[:, D_MODEL:] * _dot(yb_bf, w_pb_ref[...]))
        y_ref[y_rows, :] = x_ref[...] + _modulation(mod_ref)[2] * _dot(y.astype(BF), w_out_ref[...])
        yield 0

    def interleave(a_thunks, b_parts, lead):
        for th in a_thunks[:lead]:
            th()
        acc, done = 0, lead
        for cost in b_parts:
            acc += cost
            upto = lead + int(round((len(a_thunks) - lead) * min(acc, STAGE_B_COST) / STAGE_B_COST))
            for th in a_thunks[done:upto]:
                th()
            done = upto
        assert done == len(a_thunks) and acc == STAGE_B_COST, (done, acc)

    @pl.when(s == 0)
    def _():
        st_ref[...] = jnp.zeros_like(st_ref)
        span_ref[...] = jnp.zeros_like(span_ref)
        for r in (kprev_ref, kprev_sw_ref, vprev_ref, vprev_sw_ref):
            r[...] = jnp.zeros_like(r)
        for th in stage_a(x_refs[0], h0_ref, p0_ref):
            th()

    for k in range(tps):
        cur, nxt = k % 2, (k + 1) % 2
        phase(h_refs[cur], stage_a(x_refs[k + 1], h_refs[nxt], p_refs[nxt]),
              stage_b(x_refs[k], p_refs[cur], tps * s + k, slice(k * tile, (k + 1) * tile)))

    @pl.when(s == n_steps - 1)
    def _():
        for hd in range(N_HEADS_B):
            state_ref[hd] = st_ref[hd].T
        kwin_ref[...] = kwin_ref[...].T
        vwin_ref[...] = vwin_ref[...].T


def _const_spec(shape):
    nd = len(shape)
    return pl.BlockSpec(shape, lambda i, *_: (0,) * nd, pipeline_mode=pl.Buffered(1))


def _prompt_call(x, mod, consts, w, tile, tps, robust):
    t = x.shape[0]
    n_tiles = t // tile
    n_steps = n_tiles // tps
    assert n_steps * tps * tile == t and tps % 2 == 0
    row = lambda n: _const_spec((1, n))
    x_spec = lambda k: pl.BlockSpec(
        (tile, D_MODEL), lambda i, *_: (jnp.minimum(tps * i + k, n_tiles - 1), 0))
    in_specs = [x_spec(k) for k in range(tps + 1)] + [
        row(3 * D_MODEL), row(D_MODEL),
        _const_spec((D_MODEL, D_IN)), _const_spec((D_MODEL, 2 * D_MODEL)), row(2 * D_MODEL),
        _const_spec((WIDTH_A, D_MODEL)), _const_spec((WIDTH_B, D_MODEL)),
        _const_spec((D_MODEL, D_MODEL)),
        row(G_END), _const_spec((2, WIDTH_B)),
        _const_spec((n_tiles, LANES)), _const_spec((n_tiles, LANES)),
        _const_spec((tile, LANES)), _const_spec((tile, LANES)), _const_spec((2, LANES)),
        _const_spec((WIDTH_A, WIDTH_A)), _const_spec((BLK, BLK)),
        _const_spec((2, 4 * BLK, 2 * BLK)), _const_spec((len(LEVELS), BLK, BLK)),
    ]
    out_specs = [
        pl.BlockSpec((tps * tile, D_MODEL), lambda i, *_: (i, 0)),
        pl.BlockSpec((WINDOW, KV_WIDTH), lambda i, *_: (0, 0)),
        pl.BlockSpec((WINDOW, KV_WIDTH), lambda i, *_: (0, 0)),
        pl.BlockSpec((N_HEADS_B, HEAD_DIM_B, HEAD_DIM_B), lambda i, *_: (0, 0, 0)),
        pl.BlockSpec((1, WIDTH_B), lambda i, *_: (0, 0)),
    ]
    out_shape = [
        jax.ShapeDtypeStruct((t, D_MODEL), F32),
        jax.ShapeDtypeStruct((WINDOW, KV_WIDTH), F32),
        jax.ShapeDtypeStruct((WINDOW, KV_WIDTH), F32),
        jax.ShapeDtypeStruct((N_HEADS_B, HEAD_DIM_B, HEAD_DIM_B), F32),
        jax.ShapeDtypeStruct((1, WIDTH_B), F32),
    ]
    scratch = [
        pltpu.VMEM((N_HEADS_B, HEAD_DIM_B, HEAD_DIM_B), F32),
        pltpu.VMEM((BLK, KV_WIDTH), BF), pltpu.VMEM((BLK, KV_WIDTH), BF),
        pltpu.VMEM((BLK, KV_WIDTH), BF), pltpu.VMEM((BLK, KV_WIDTH), BF),
        pltpu.VMEM((tile, D_IN), F32), pltpu.VMEM((tile, D_IN), F32),
        pltpu.VMEM((tile, 2 * D_MODEL), F32),
        pltpu.VMEM((tile, D_MODEL), BF), pltpu.VMEM((tile, D_MODEL), BF),
        pltpu.VMEM((tile, WIDTH_B), F32),
    ]
    return pl.pallas_call(
        functools.partial(_prompt_kernel, tile=tile, n_steps=n_steps, tps=tps, robust=robust),
        grid_spec=pltpu.PrefetchScalarGridSpec(
            num_scalar_prefetch=1, grid=(n_steps,),
            in_specs=in_specs, out_specs=out_specs, scratch_shapes=scratch),
        out_shape=out_shape,
        compiler_params=pltpu.CompilerParams(
            dimension_semantics=("arbitrary",), vmem_limit_bytes=VMEM_LIMIT),
        name="prompt_layer_robust" if robust else "prompt_layer",
    )(w["sinks"], *([x] * (tps + 1)), mod, w["norm_g"],
      w["w_in"], w["w_merge"], w["b_merge"], w["w_proj_a"], w["w_proj_b"], w["w_out"],
      w["gains"], w["lb_logits"],
      consts["rope_cb"], consts["rope_sb"], consts["rope_cl"], consts["rope_sl"],
      consts["rope_sgn"],
      consts["seg"], consts["tri"], consts["bias"], consts["lmask"])


def _dec_in_kernel(x_ref, mod_ref, ng_ref, w_in_ref, w_mg_ref, b_mg_ref, gains_ref, lbl_ref,
                   rc_ref, ru_ref, rd_ref, seg_ref, perm_ref,
                   w_in_bf_ref, w_mg_bf_ref, g_ref,
                   qm_ref, kn_ref, vn_ref, za_ref, qb_ref, kin_ref, ib_ref, zb_ref,
                   h_ref, p_ref):
    j = pl.program_id(0)
    half_in = D_IN // 2

    @pl.when(j == 0)
    def _():
        h_ref[...] = _norm_modulate(x_ref[:, 0, :], ng_ref[...], mod_ref)

    w_in_bf = w_in_ref[...].astype(BF)
    w_mg_bf = w_mg_ref[...].astype(BF)
    w_in_bf_ref[...] = w_in_bf
    w_mg_bf_ref[...] = w_mg_bf
    h_bf = h_ref[...]
    g_ref[...] = _sigmoid(_dot(h_bf, w_mg_bf) + b_mg_ref[...])
    part = _dot(h_bf, w_in_bf)

    @pl.when(j == 0)
    def _():
        p_ref[:, :half_in] = part

    @pl.when(j == 1)
    def _():
        p_ref[:, half_in:] = part
        seg = seg_ref[...]
        rc, ru, rd = rc_ref[...], ru_ref[...], rd_ref[...]
        qa = p_ref[:, OFF_QA:OFF_QA + WIDTH_A]
        qa = qa * _head_norm_scale(qa, seg) * (gains_ref[:, G_Q:G_K] * (HEAD_DIM_A ** -0.5))
        qa = jnp.concatenate(
            [_rope(qa[:, c * LANES:(c + 1) * LANES], rc, ru, rd) for c in range(4)], axis=1)
        qm = _dot(qa.astype(BF), perm_ref[...])
        for hd in range(N_HEADS_A):
            qm_ref[:, hd, :] = qm[:, hd * LANES:(hd + 1) * LANES]
        ka = p_ref[:, OFF_KA:OFF_KA + KV_WIDTH]
        ka = ka * _head_norm_scale(ka, seg[:KV_WIDTH, :KV_WIDTH]) * gains_ref[:, G_K:G_O]
        kn_ref[...] = _rope(ka, rc, ru, rd)
        vn_ref[...] = p_ref[:, OFF_VA:OFF_VA + KV_WIDTH]
        za_ref[...] = p_ref[:, OFF_ZA:OFF_ZA + WIDTH_A]
        qb_ref[...] = p_ref[:, OFF_QB:OFF_QB + WIDTH_B]
        lb = _lower_bound(lbl_ref[...])
        kin_ref[...] = (1.0 - lb) * (1.0 - _sigmoid(p_ref[:, OFF_FB:OFF_FB + WIDTH_B]))
        ib_ref[...] = p_ref[:, OFF_IB:OFF_IB + WIDTH_B]
        zb_ref[...] = p_ref[:, OFF_ZB:OFF_ZB + WIDTH_B]


def _dec_in_call(x, mod, consts, w):
    b = x.shape[0]
    n_steps = 2
    half_in, half_mg = D_IN // n_steps, 2 * D_MODEL // n_steps
    assert half_in % LANES == 0
    const = lambda a: pl.BlockSpec(a.shape, lambda j: (0,) * a.ndim)
    cols = lambda rows, n: pl.BlockSpec((rows, n), lambda j: (0, j))
    row_out = lambda n: pl.BlockSpec((b, n), lambda j: (0, 0))
    f = lambda n: jax.ShapeDtypeStruct((b, n), F32)
    widths = [KV_WIDTH, KV_WIDTH, WIDTH_A, WIDTH_B, WIDTH_B, WIDTH_B, WIDTH_B]
    qm_shape = (b, N_HEADS_A, LANES)
    small = [w["norm_g"]]
    tail = [w["gains"], w["lb_logits"], consts["rope_c1"], consts["rope_u1"], consts["rope_d1"],
            consts["seg"], consts["perm"]]
    return pl.pallas_call(
        _dec_in_kernel,
        grid=(n_steps,),
        in_specs=[const(x), const(mod)] + [const(a) for a in small]
        + [cols(D_MODEL, half_in), cols(D_MODEL, half_mg), cols(1, half_mg)]
        + [const(a) for a in tail],
        out_specs=[cols(D_MODEL, half_in), cols(D_MODEL, half_mg), cols(b, half_mg),
                   pl.BlockSpec(qm_shape, lambda j: (0, 0, 0))]
        + [row_out(n) for n in widths],
        out_shape=[jax.ShapeDtypeStruct((D_MODEL, D_IN), BF),
                   jax.ShapeDtypeStruct((D_MODEL, 2 * D_MODEL), BF), f(2 * D_MODEL),
                   jax.ShapeDtypeStruct(qm_shape, F32)]
        + [f(n) for n in widths],
        scratch_shapes=[pltpu.VMEM((b, D_MODEL), BF), pltpu.VMEM((b, D_IN), F32)],
        compiler_params=pltpu.CompilerParams(
            dimension_semantics=("arbitrary",), vmem_limit_bytes=VMEM_LIMIT),
        name="decode_in",
    )(x, mod, *small, w["w_in_f32"], w["w_merge_f32"], w["b_merge"], *tail)


def _dec_mix_kernel(sinks_ref, qm_ref, kn_ref, vn_ref, qb_ref, kin_ref, ib_ref,
                    ck_ref, cv_ref, st_ref,
                    att_ref, ob_ref, nk_ref, nv_ref, nst_ref, *, bt):
    nh = N_HEADS_A
    s = jnp.concatenate([_dot(qm_ref[j].astype(BF), ck_ref[j].astype(BF))
                         for j in range(bt)], axis=0)
    key = lax.broadcasted_iota(jnp.int32, (bt * nh, WINDOW), 1)
    s = jnp.where(key == 0, NEG, s)
    rep = lambda r: jnp.broadcast_to(r[...][:, None, :], (bt, nh, r.shape[-1])).reshape(
        bt * nh, r.shape[-1])
    kn_rows, vn_rows = rep(kn_ref), rep(vn_ref)
    head = lax.broadcasted_iota(jnp.int32, (bt * nh, 1), 0) % nh
    sink = jnp.zeros((bt * nh, 1), F32)
    for hd in range(nh):
        sink = jnp.where(head == hd, sinks_ref[hd], sink)
    s_new = jnp.sum(qm_ref[...].reshape(bt * nh, LANES) * kn_rows, axis=-1, keepdims=True)
    m = jnp.maximum(jnp.maximum(jnp.max(s, axis=-1, keepdims=True), s_new), sink)
    p = jnp.exp(s - m)
    p_new = jnp.exp(s_new - m)
    den = jnp.sum(p, axis=-1, keepdims=True) + p_new + jnp.exp(sink - m)
    p_bf = p.astype(BF)
    pv = jnp.concatenate([_dot_nt(p_bf[j * nh:(j + 1) * nh, :], cv_ref[j].astype(BF))
                          for j in range(bt)], axis=0)
    att_ref[...] = ((pv + p_new * vn_rows) * (1.0 / den)).reshape(bt, nh, LANES)
    kn_t, vn_t = kn_ref[...].T, vn_ref[...].T
    newest = lax.broadcasted_iota(jnp.int32, (KV_WIDTH, WINDOW), 1) == WINDOW - 1
    for j in range(bt):
        nk_ref[j] = jnp.where(newest, kn_t[:, j:j + 1], pltpu.roll(ck_ref[j], WINDOW - 1, 1))
        nv_ref[j] = jnp.where(newest, vn_t[:, j:j + 1], pltpu.roll(cv_ref[j], WINDOW - 1, 1))

    for hd in range(N_HEADS_B):
        sl = slice(hd * LANES, (hd + 1) * LANES)
        kin_t = kin_ref[:, sl].T
        q_bf = qb_ref[:, sl].astype(BF)
        for j in range(bt):
            kcol = kin_t[:, j:j + 1]
            st = st_ref[j, hd]
            new = st - kcol * (st - ib_ref[j:j + 1, sl])
            nst_ref[j, hd] = new
            ob_ref[j:j + 1, sl] = _dot(q_bf, new.astype(BF))[j:j + 1, :]


def _dec_mix_call(sinks, qm, kn, vn, qb, kin, ib, cache_k, cache_v, state, bt):
    b = kn.shape[0]
    rows = lambda n: pl.BlockSpec((bt, n), lambda i, *_: (i, 0))
    cache_spec = pl.BlockSpec((bt, WINDOW, KV_WIDTH), lambda i, *_: (i, 0, 0))
    st_spec = pl.BlockSpec((bt, N_HEADS_B, HEAD_DIM_B, HEAD_DIM_B), lambda i, *_: (i, 0, 0, 0))
    qm_spec = pl.BlockSpec((bt, N_HEADS_A, LANES), lambda i, *_: (i, 0, 0))
    return pl.pallas_call(
        functools.partial(_dec_mix_kernel, bt=bt),
        grid_spec=pltpu.PrefetchScalarGridSpec(
            num_scalar_prefetch=1, grid=(b // bt,),
            in_specs=[qm_spec, rows(KV_WIDTH), rows(KV_WIDTH), rows(WIDTH_B), rows(WIDTH_B),
                      rows(WIDTH_B), cache_spec, cache_spec, st_spec],
            out_specs=[qm_spec, rows(WIDTH_B), cache_spec, cache_spec, st_spec]),
        out_shape=[jax.ShapeDtypeStruct((b, N_HEADS_A, LANES), F32),
                   jax.ShapeDtypeStruct((b, WIDTH_B), F32),
                   jax.ShapeDtypeStruct(cache_k.shape, F32),
                   jax.ShapeDtypeStruct(cache_v.shape, F32),
                   jax.ShapeDtypeStruct(state.shape, F32)],
        compiler_params=pltpu.CompilerParams(
            dimension_semantics=("arbitrary",), vmem_limit_bytes=VMEM_LIMIT),
        name="decode_mix",
    )(sinks, qm, kn, vn, qb, kin, ib, cache_k, cache_v, state)


def _dec_out_kernel(x_ref, mod_ref, att_ref, za_ref, ob_ref, zb_ref, g_ref, gains_ref,
                    permt_ref, w_pa_ref, w_pb_ref, w_out_ref,
                    y_ref, w_pa_bf_ref, w_pb_bf_ref, w_out_bf_ref):
    w_pa, w_pb, w_out = (r[...].astype(BF) for r in (w_pa_ref, w_pb_ref, w_out_ref))
    w_pa_bf_ref[...] = w_pa
    w_pb_bf_ref[...] = w_pb
    w_out_bf_ref[...] = w_out
    att = sum(_split_dot(att_ref[:, hd, :], permt_ref[hd * LANES:(hd + 1) * LANES, :])
              for hd in range(N_HEADS_A))
    ya = att * _silu(za_ref[...])
    yb = _branch_b_out(ob_ref[...], zb_ref[...], gains_ref[:, G_O:G_END])
    g = g_ref[...]
    y = (g[:, :D_MODEL] * _dot(ya.astype(BF), w_pa) + g[:, D_MODEL:] * _dot(yb.astype(BF), w_pb))
    y_ref[:, 0, :] = x_ref[:, 0, :] + _modulation(mod_ref)[2] * _dot(y.astype(BF), w_out)


def _dec_out_call(x, mod, att, za, ob, zb, g, consts, w):
    bf = lambda a: jax.ShapeDtypeStruct(a.shape, BF)
    ws = (w["w_proj_a_f32"], w["w_proj_b_f32"], w["w_out_f32"])
    return pl.pallas_call(
        _dec_out_kernel,
        out_shape=[jax.ShapeDtypeStruct(x.shape, F32)] + [bf(a) for a in ws],
        compiler_params=pltpu.CompilerParams(vmem_limit_bytes=VMEM_LIMIT),
        name="decode_out",
    )(x, mod, att, za, ob, zb, g, w["gains"], consts["permt"], *ws)


def _rope_consts(n_tiles, tile, past_len):
    half = ROT_DIM // 2
    inv = ROPE_THETA ** (-np.arange(0, ROT_DIM, 2, dtype=np.float64) / ROT_DIM)
    e = np.arange(LANES) % HEAD_DIM_A
    rot = e < ROT_DIM

    def tables(pos):
        ang = np.asarray(pos, np.float64)[:, None] * inv[e % half][None, :]
        return np.where(rot, np.cos(ang), 1.0), np.where(rot, np.sin(ang), 0.0)

    sgn = np.stack([np.where(e < half, -1.0, 0.0),
                    np.where(rot & (e >= half), 1.0, 0.0)])
    cb, sb = tables(np.arange(n_tiles) * tile)
    cl, sl = tables(np.arange(tile))
    c1, s1 = tables([past_len])
    f = lambda a: jnp.asarray(a, F32)
    return {"rope_cb": f(cb), "rope_sb": f(sb), "rope_cl": f(cl), "rope_sl": f(sl),
            "rope_sgn": f(sgn), "rope_c1": f(c1), "rope_u1": f(s1 * sgn[0:1]),
            "rope_d1": f(s1 * sgn[1:2])}


def _static_consts():
    seg = np.kron(np.eye(N_HEADS_A), np.full((HEAD_DIM_A, HEAD_DIM_A), 1.0 / HEAD_DIM_A))
    tri = np.tril(np.ones((BLK, BLK)))
    r = np.arange(4 * BLK)[:, None] % BLK
    c = np.arange(2 * BLK)[None, :]
    ok_prev = (c < BLK) & (c > r)
    ok_cur = (c >= BLK) & (c - BLK <= r)
    bias = np.stack([np.where(ok_cur, 0.0, NEG), np.where(ok_prev | ok_cur, 0.0, NEG)])
    i = np.arange(BLK)[:, None]
    j = np.arange(BLK)[None, :]
    lmask = np.stack([((i // (2 * b)) == (j // (2 * b))) & ((i & b) != 0) & ((j & b) == 0)
                      for b in LEVELS]).astype(np.float32)
    perm = np.zeros((WIDTH_A, N_HEADS_A * LANES), np.float32)
    for hd in range(N_HEADS_A):
        kvh = hd // (N_HEADS_A // N_KV_A)
        for d in range(HEAD_DIM_A):
            perm[hd * HEAD_DIM_A + d, hd * LANES + kvh * HEAD_DIM_A + d] = 1.0
    return {
        "seg": jnp.asarray(seg, BF), "tri": jnp.asarray(tri, BF),
        "bias": jnp.asarray(bias, F32), "lmask": jnp.asarray(lmask, F32),
        "perm": jnp.asarray(perm, BF), "permt": jnp.asarray(perm.T, BF),
    }


def kernel(x_prompt, x_sample, cache_win_k, cache_win_v, state_hgrn, c_prompt, c_sample,
           w_ada, b_ada, norm_g, w_in, q_norm_g, k_norm_g, sinks, lb_logits, o_norm_g,
           w_merge, b_merge, w_proj_a, w_proj_b, w_out):
    depth = w_in.shape[0]
    assert depth == 1 and x_prompt.shape[0] == 1 and x_sample.shape[1] == 1
    t = x_prompt.shape[1]
    nb = x_sample.shape[0]
    past_len = t
    tile = PROMPT_TILE
    bt = DECODE_BATCH_TILE

    consts = _static_consts()
    consts.update(_rope_consts(t // tile, tile, past_len))

    w = {
        "sinks": sinks[0], "norm_g": norm_g,
        "w_in_f32": w_in[0], "w_merge_f32": w_merge[0], "b_merge": b_merge,
        "w_proj_a_f32": w_proj_a[0], "w_proj_b_f32": w_proj_b[0], "w_out_f32": w_out[0],
        "lb_logits": lb_logits,
    }

    mod_p, mod_s, w["gains"] = _ada_call(c_prompt, c_sample, w_ada[0], b_ada,
                                         q_norm_g, k_norm_g, o_norm_g)

    xs = x_sample
    (w["w_in"], w["w_merge"], g,
     qm, kn, vn, za, qb, kin, ib, zb) = _dec_in_call(xs, mod_s, consts, w)
    to_t = lambda c: jnp.transpose(c[0], (0, 2, 3, 1)).reshape(nb, KV_WIDTH, WINDOW)
    from_t = lambda c: jnp.transpose(
        c.reshape(-1, N_KV_A, HEAD_DIM_A, WINDOW), (0, 3, 1, 2))[None]
    att, ob, nk, nv, nst = _dec_mix_call(
        w["sinks"], qm, kn, vn, qb, kin, ib,
        to_t(cache_win_k), to_t(cache_win_v), state_hgrn[0], bt)
    y_s, w["w_proj_a"], w["w_proj_b"], w["w_out"] = _dec_out_call(
        xs, mod_s, att, za, ob, zb, g, consts, w)

    run = functools.partial(_prompt_call, x_prompt[0], mod_p, consts, w, tile,
                            PROMPT_TILES_PER_STEP)
    *fast, span = run(robust=False)
    y_p, kwin, vwin, st_p = lax.cond(jnp.max(span) < HGRN_FAST_MAX,
                                     lambda: tuple(fast), lambda: tuple(run(robust=True)[:4]))

    return (y_p[None], y_s,
            from_t(kwin[None]), from_t(vwin[None]), st_p[None, None],
            from_t(nk), from_t(nv), nst[None])
```

```python
import functools

import numpy as np
import jax
import jax.numpy as jnp
from jax import lax
from jax.experimental import pallas as pl
from jax.experimental.pallas import tpu as pltpu

D_MODEL = 1024
HEAD_DIM_A = 64
N_HEADS_A = 8
N_KV_A = 2
WIDTH_A = N_HEADS_A * HEAD_DIM_A
KV_WIDTH = N_KV_A * HEAD_DIM_A
WINDOW = 128
ROT_DIM = HEAD_DIM_A // 4
ROPE_THETA = 500000.0
HEAD_DIM_B = 128
N_HEADS_B = 4
WIDTH_B = N_HEADS_B * HEAD_DIM_B
EPS = 1e-6
OFF_QA = 0
OFF_KA = OFF_QA + WIDTH_A
OFF_VA = OFF_KA + KV_WIDTH
OFF_ZA = OFF_VA + KV_WIDTH
OFF_QB = OFF_ZA + WIDTH_A
OFF_FB = OFF_QB + WIDTH_B
OFF_IB = OFF_FB + WIDTH_B
OFF_ZB = OFF_IB + WIDTH_B
D_IN = OFF_ZB + WIDTH_B

LANES = 128
BLK = 128
SUB = 8
LEVELS = (64, 32, 16, 8)
HGRN_FAST_MAX = 80.0
MXU_N = 256
NEG = -1e30
VMEM_LIMIT = 56 * 1024 * 1024
PROMPT_TILE = 256
PROMPT_TILES_PER_STEP = 2
DECODE_BATCH_TILE = 16
G_Q, G_K, G_O = 0, WIDTH_A, WIDTH_A + KV_WIDTH
G_END = G_O + WIDTH_B

BF = jnp.bfloat16
F32 = jnp.float32


def _dot(a, b):
    return jnp.dot(a, b, preferred_element_type=F32)


def _dot_nt(a, b):
    return lax.dot_general(a, b, (((1,), (1,)), ((), ())), preferred_element_type=F32)


def _dot_tn(a, b):
    return lax.dot_general(a, b, (((0,), (0,)), ((), ())), preferred_element_type=F32)


def _split_dot(a_f32, b_bf):
    hi = a_f32.astype(BF)
    lo = (a_f32 - hi.astype(F32)).astype(BF)
    return _dot(hi, b_bf) + _dot(lo, b_bf)


def _sigmoid(x):
    return 1.0 / (1.0 + jnp.exp(-x))


def _silu(x):
    return x * _sigmoid(x)


def _lower_bound(lb_logits):
    l0 = lb_logits[0:1, :]
    l1 = lb_logits[1:2, :]
    m = jnp.maximum(l0, l1)
    e0 = jnp.exp(l0 - m)
    e1 = jnp.exp(l1 - m)
    return e0 / (e0 + e1)


def _rope_tables(cb, sb, cl, sl, sgn_up, sgn_dn):
    c = cb * cl - sb * sl
    s = sb * cl + cb * sl
    return c, s * sgn_up, s * sgn_dn


def _rope(x, c, s_up, s_dn):
    return x * c + pltpu.roll(x, LANES - ROT_DIM // 2, 1) * s_up + pltpu.roll(x, ROT_DIM // 2, 1) * s_dn


def _modulation(mod_ref):
    return (mod_ref[:, 0:D_MODEL], mod_ref[:, D_MODEL:2 * D_MODEL],
            mod_ref[:, 2 * D_MODEL:3 * D_MODEL])


def _norm_modulate(x, ng, mod_ref):
    shift, scale, _ = _modulation(mod_ref)
    ms = jnp.mean(x * x, axis=-1, keepdims=True)
    return (x * lax.rsqrt(ms + EPS) * (ng * (1.0 + scale)) + shift).astype(BF)


def _head_norm_scale(x, seg_mean_bf):
    ms = _dot((x * x).astype(BF), seg_mean_bf)
    return lax.rsqrt(ms + EPS)


def _ada_kernel(cp_ref, cs_ref, w_ref, b_ref, qg_ref, kg_ref, og_ref, op_ref, os_ref, gains_ref):
    gains_ref[...] = jnp.concatenate(
        [qg_ref[...]] * N_HEADS_A + [kg_ref[...]] * N_KV_A + [og_ref[...]] * N_HEADS_B, axis=1)
    w = w_ref[...]
    w_hi = w.astype(BF)
    w_lo = (w - w_hi.astype(F32)).astype(BF)
    b = b_ref[...]

    ns, n_p = cs_ref.shape[0], cp_ref.shape[0]
    c = jnp.concatenate([cs_ref[...], cp_ref[...],
                         jnp.zeros(((-n_p) % SUB, D_MODEL), F32)], axis=0)
    c_hi = c.astype(BF)
    c_lo = (c - c_hi.astype(F32)).astype(BF)
    out = _dot(c_hi, w_hi) + (_dot(c_hi, w_lo) + _dot(c_lo, w_hi)) + b
    os_ref[...] = out[:ns, :]
    op_ref[...] = out[ns:ns + n_p, :]


def _ada_call(c_p, c_s, w_ada, b_ada, q_g, k_g, o_g):
    mp, ms = c_p.shape[0], c_s.shape[0]
    n = w_ada.shape[1]
    tn = 512
    const = lambda a: pl.BlockSpec(a.shape, lambda j: (0,) * a.ndim)
    return pl.pallas_call(
        _ada_kernel,
        grid=(n // tn,),
        in_specs=[const(c_p), const(c_s),
                  pl.BlockSpec((D_MODEL, tn), lambda j: (0, j)),
                  pl.BlockSpec((1, tn), lambda j: (0, j)),
                  const(q_g), const(k_g), const(o_g)],
        out_specs=[pl.BlockSpec((mp, tn), lambda j: (0, j)),
                   pl.BlockSpec((ms, tn), lambda j: (0, j)),
                   pl.BlockSpec((1, G_END), lambda j: (0, 0))],
        out_shape=[jax.ShapeDtypeStruct((mp, n), F32), jax.ShapeDtypeStruct((ms, n), F32),
                   jax.ShapeDtypeStruct((1, G_END), F32)],
        name="ada",
    )(c_p, c_s, w_ada, b_ada, q_g, k_g, o_g)


STAGE_B_COST = 3840


def _round_robin(*gens):
    results = [None] * len(gens)
    live = list(range(len(gens)))
    while live:
        for n in list(live):
            try:
                yield next(gens[n])
            except StopIteration as stop:
                results[n] = stop.value
                live.remove(n)
    return results


def _attn_block(q_blk, kcat, kcat_sw, vcat, vcat_sw, bias, sink_a, sink_b):
    lane = lax.broadcasted_iota(jnp.int32, (BLK, LANES), 1)
    lo = lane < HEAD_DIM_A
    chunks = [q_blk[:, c * LANES:(c + 1) * LANES] for c in range(4)]
    zero = jnp.zeros((BLK, LANES), F32)
    q_lo = [jnp.where(lo, c, zero).astype(BF) for c in chunks]
    q_hi = [jnp.where(lo, zero, c).astype(BF) for c in chunks]
    qa = jnp.concatenate([q_lo[0], q_lo[1], q_hi[2], q_hi[3]], axis=0)
    qb = jnp.concatenate([q_hi[0], q_hi[1], q_lo[2], q_lo[3]], axis=0)

    def probs(qs, kc, sink):
        s = _dot_nt(qs, kc) + bias
        m = jnp.maximum(jnp.max(s, axis=-1, keepdims=True), sink)
        p = jnp.exp(s - m)
        den = jnp.sum(p, axis=-1, keepdims=True) + jnp.exp(sink - m)
        return p.astype(BF), 1.0 / den

    pa, ra = probs(qa, kcat, sink_a)
    yield 200
    pb, rb = probs(qb, kcat_sw, sink_b)
    yield 200
    oa = _dot(pa, vcat) * ra
    yield 40
    ob = _dot(pb, vcat_sw) * rb
    yield 40
    r = lambda o, i: o[i * BLK:(i + 1) * BLK, :]
    out = jnp.concatenate([
        jnp.where(lo, r(oa, 0), r(ob, 0)),
        jnp.where(lo, r(oa, 1), r(ob, 1)),
        jnp.where(lo, r(ob, 2), r(oa, 2)),
        jnp.where(lo, r(ob, 3), r(oa, 3)),
    ], axis=1)
    yield 30
    return out


def _hgrn_gates(fb, lb, one_m_lb, tri_bf):
    sig = _sigmoid(fb)
    kin = one_m_lb * (1.0 - sig)
    f = lb + one_m_lb * sig
    cum = _dot(tri_bf, jnp.log(f).astype(BF))
    return kin, f, cum


def _hgrn_span_decay(cum):
    q = BLK // 4
    ends = [cum[(n + 1) * q - 1:(n + 1) * q, :] for n in range(4)]
    d = -ends[0]
    for n in range(1, 4):
        d = jnp.maximum(d, ends[n - 1] - ends[n])
    return d


def _hgrn_state_step(qb, kin, cum, ib, st_ref):
    q_dec = (qb * jnp.exp(cum)).astype(BF)
    last = cum[BLK - 1:BLK, :]
    k_dec = (kin * jnp.exp(last - cum)).astype(BF)
    v_bf = ib.astype(BF)
    outs = []
    for h in range(N_HEADS_B):
        sl = slice(h * LANES, (h + 1) * LANES)
        st = st_ref[h]
        outs.append(_dot_nt(q_dec[:, sl], st.astype(BF)))
        st_ref[h] = st * jnp.exp(last[:, sl]) + _dot_tn(v_bf[:, sl], k_dec[:, sl])
    return jnp.concatenate(outs, axis=1)


def _hgrn_apply(amats, ib):
    v_bf = ib.astype(BF)
    return jnp.concatenate(
        [_dot(amats[h].astype(BF), v_bf[:, h * LANES:(h + 1) * LANES])
         for h in range(N_HEADS_B)], axis=1)


def _recur_block_fast(qb, kin, cum, ib, st_ref):
    base = _hgrn_state_step(qb, kin, cum, ib, st_ref)
    yield 130
    half = BLK // 2
    row = lax.broadcasted_iota(jnp.int32, (BLK, 1), 0)
    upper = row >= half
    piv = cum[half - 1:half, :]
    w_lvl = jnp.exp(jnp.concatenate([piv - cum[:half, :], cum[half:, :] - piv], axis=0))
    p_lvl = (jnp.where(upper, qb, kin) * w_lvl).astype(BF)
    mid = jnp.where(upper, cum[half + half // 2 - 1:half + half // 2, :],
                    cum[half // 2 - 1:half // 2, :])
    e_mid = cum - mid
    q_mid = (qb * jnp.exp(e_mid)).astype(BF)
    k_mid = (kin * jnp.exp(-e_mid)).astype(BF)
    yield 230

    ri = lax.broadcasted_iota(jnp.int32, (BLK, BLK), 0)
    ci = lax.broadcasted_iota(jnp.int32, (BLK, BLK), 1)
    same_half_causal = ((ri >= half) == (ci >= half)) & (ci <= ri)
    cross = (ri >= half) & (ci < half)
    amats = []
    for h in range(N_HEADS_B):
        sl = slice(h * LANES, (h + 1) * LANES)
        a_mid = _dot_nt(q_mid[:, sl], k_mid[:, sl])
        a_lvl = _dot_nt(p_lvl[:, sl], p_lvl[:, sl])
        amats.append(jnp.where(same_half_causal, a_mid, jnp.where(cross, a_lvl, 0.0)))
    yield 60
    full = base + _hgrn_apply(amats, ib)
    yield 40
    return full


def _hgrn_intra_robust(qb, kin, f, cum, ib, lvl_mask_ref):
    row = lax.broadcasted_iota(jnp.int32, (BLK, 1), 0)
    lvl_ops = []
    for b in LEVELS:
        pieces = []
        for r0 in range(0, BLK, 2 * b):
            piv = cum[r0 + b - 1:r0 + b, :]
            pieces.append(piv - cum[r0:r0 + b, :])
            pieces.append(cum[r0 + b:r0 + 2 * b, :] - piv)
        w = jnp.exp(jnp.concatenate(pieces, axis=0))
        second = (row & b) != 0
        lvl_ops.append((jnp.where(second, qb, kin) * w).astype(BF))

    n8 = BLK // SUB
    q3 = qb.reshape(n8, SUB, WIDTH_B)
    k3 = kin.reshape(n8, SUB, WIDTH_B)
    f3 = f.reshape(n8, SUB, WIDTH_B)
    v3 = ib.reshape(n8, SUB, WIDTH_B)
    subl = lax.broadcasted_iota(jnp.int32, (n8, SUB, 1), 1)

    def head(x, h):
        return x[..., h * LANES:(h + 1) * LANES]

    g = q3 * k3
    acc = [jnp.sum(head(g, h), axis=-1, keepdims=True) * head(v3, h) for h in range(N_HEADS_B)]
    dec = jnp.ones_like(f3)
    kd = k3
    vd = v3
    for d in range(1, SUB):
        dec = f3 * pltpu.roll(dec, 1, 1)
        kd = pltpu.roll(kd, 1, 1)
        vd = pltpu.roll(vd, 1, 1)
        g = q3 * kd * dec
        ok = subl >= d
        for h in range(N_HEADS_B):
            a = jnp.where(ok, jnp.sum(head(g, h), axis=-1, keepdims=True), 0.0)
            acc[h] = acc[h] + a * head(vd, h)

    amats = []
    for h in range(N_HEADS_B):
        sl = slice(h * LANES, (h + 1) * LANES)
        amat = jnp.zeros((BLK, BLK), F32)
        for li in range(len(LEVELS)):
            p = lvl_ops[li][:, sl]
            amat = amat + lvl_mask_ref[li] * _dot_nt(p, p)
        amats.append(amat)
    diag = jnp.concatenate([acc[h].reshape(BLK, LANES) for h in range(N_HEADS_B)], axis=1)
    return _hgrn_apply(amats, ib) + diag


def _branch_b_out(o, zb, og):
    outs = []
    for h in range(N_HEADS_B):
        sl = slice(h * LANES, (h + 1) * LANES)
        oh = o[:, sl]
        ms = jnp.mean(oh * oh, axis=-1, keepdims=True)
        outs.append(oh * lax.rsqrt(ms + EPS))
    return jnp.concatenate(outs, axis=1) * og * _silu(zb)


def _prompt_kernel(sinks_ref, *refs, tile, n_steps, tps, robust):
    x_refs = refs[:tps + 1]
    (mod_ref, ng_ref,
     w_in_ref, w_mg_ref, b_mg_ref, w_pa_ref, w_pb_ref, w_out_ref,
     gains_ref, lbl_ref,
     cb_ref, sb_ref, cl_ref, sl_ref, sgn_ref,
     seg_ref, tri_ref, bias_ref, lmask_ref,
     y_ref, kwin_ref, vwin_ref, state_ref, span_ref,
     st_ref, kprev_ref, kprev_sw_ref, vprev_ref, vprev_sw_ref,
     p0_ref, p1_ref, g_ref, h0_ref, h1_ref, ob_ref) = refs[tps + 1:]
    p_refs, h_refs = (p0_ref, p1_ref), (h0_ref, h1_ref)
    s = pl.program_id(0)
    nblk = tile // BLK

    def stage_a(x_ref, h_ref, p_ref):
        def prep():
            h_ref[...] = _norm_modulate(x_ref[...], ng_ref[...], mod_ref)

        def proj_chunk(c):
            def run():
                cs = slice(c * MXU_N, (c + 1) * MXU_N)
                p_ref[:, cs] = _dot(h_ref[...], w_in_ref[:, cs])
            return run

        return [prep] + [proj_chunk(c) for c in range(D_IN // MXU_N)]

    def gate_chunks(h_ref):
        def gate_chunk(c):
            def run():
                cs = slice(c * MXU_N, (c + 1) * MXU_N)
                g_ref[:, cs] = _sigmoid(_dot(h_ref[...], w_mg_ref[:, cs]) + b_mg_ref[:, cs])
            return run

        return [gate_chunk(c) for c in range(2 * D_MODEL // MXU_N)]

    def phase(h_cur_ref, a_next, b_parts):
        gc = gate_chunks(h_cur_ref)
        interleave(gc[:2] + a_next[:1] + gc[2:] + a_next[1:], b_parts, lead=12)

    def stage_b(x_ref, p_ref, t_idx, y_rows):
        blocks = [slice(blk * BLK, (blk + 1) * BLK) for blk in range(nblk)]

        def qkv_stream():
            seg = seg_ref[...]
            rc, ru, rd = _rope_tables(cb_ref[pl.ds(t_idx, 1), :], sb_ref[pl.ds(t_idx, 1), :],
                                      cl_ref[...], sl_ref[...], sgn_ref[0:1, :], sgn_ref[1:2, :])
            qa = p_ref[:, OFF_QA:OFF_QA + WIDTH_A]
            qa = qa * _head_norm_scale(qa, seg) * (gains_ref[:, G_Q:G_K] * (HEAD_DIM_A ** -0.5))
            yield 160
            qa = jnp.concatenate(
                [_rope(qa[:, c * LANES:(c + 1) * LANES], rc, ru, rd) for c in range(4)], axis=1)
            yield 160
            ka = p_ref[:, OFF_KA:OFF_KA + KV_WIDTH]
            ka = ka * _head_norm_scale(ka, seg[:KV_WIDTH, :KV_WIDTH]) * gains_ref[:, G_K:G_O]
            ka = _rope(ka, rc, ru, rd)
            va = p_ref[:, OFF_VA:OFF_VA + KV_WIDTH]
            ka_sw = pltpu.roll(ka, HEAD_DIM_A, 1)
            va_sw = pltpu.roll(va, HEAD_DIM_A, 1)
            rows4 = lax.broadcasted_iota(jnp.int32, (4 * BLK, 1), 0) // BLK

            def sink_col(heads):
                col = jnp.zeros((4 * BLK, 1), F32)
                for n, hd in enumerate(heads):
                    col = jnp.where(rows4 == n, sinks_ref[hd], col)
                return col
            kwin_ref[...] = ka[tile - WINDOW:, :]
            vwin_ref[...] = va[tile - WINDOW:, :]
            res = (qa, ka, va, ka_sw, va_sw, sink_col((0, 2, 5, 7)), sink_col((1, 3, 4, 6)))
            yield 120
            return res

        def gates_stream():
            lb = _lower_bound(lbl_ref[...])
            tri = tri_ref[...]
            gates = []
            half_w = WIDTH_B // 2
            for rs in blocks:
                halves = []
                for c0 in (0, half_w):
                    cs = slice(c0, c0 + half_w)
                    halves.append(_hgrn_gates(p_ref[rs, OFF_FB + c0:OFF_FB + c0 + half_w],
                                              lb[:, cs], 1.0 - lb[:, cs], tri))
                    yield 240
                gates.append(tuple(jnp.concatenate(pair, axis=1) for pair in zip(*halves)))
            span = span_ref[...]
            for g in gates:
                span = jnp.maximum(span, _hgrn_span_decay(g[2]))
            span_ref[...] = span
            yield 10
            return gates

        (qa, ka, va, ka_sw, va_sw, sink_a, sink_b), gates = (
            yield from _round_robin(qkv_stream(), gates_stream()))

        kv_refs = (kprev_ref, kprev_sw_ref, vprev_ref, vprev_sw_ref)
        kv_new = [a.astype(BF) for a in (ka, ka_sw, va, va_sw)]
        kv_old = [r[...] for r in kv_refs]
        for r, a in zip(kv_refs, kv_new):
            r[...] = a[blocks[-1]]

        def attend(blk, rs):
            cats = [jnp.concatenate([old if blk == 0 else new[blocks[blk - 1]], new[rs]], axis=0)
                    for old, new in zip(kv_old, kv_new)]
            bias = bias_ref[jnp.where(t_idx == 0, 0, 1)] if blk == 0 else bias_ref[1]
            return (yield from _attn_block(qa[rs], *cats, bias, sink_a, sink_b))

        def recur(blk, rs):
            kin, f, cum = gates[blk]
            qb, ib = p_ref[rs, OFF_QB:OFF_QB + WIDTH_B], p_ref[rs, OFF_IB:OFF_IB + WIDTH_B]
            if robust:
                base = _hgrn_state_step(qb, kin, cum, ib, st_ref)
                yield 130
                full = base + _hgrn_intra_robust(qb, kin, f, cum, ib, lmask_ref)
                yield 230 + 60 + 40
            else:
                full = yield from _recur_block_fast(qb, kin, cum, ib, st_ref)
            ob_ref[rs, :] = full

        ya_parts = []
        for blk, rs in enumerate(blocks):
            res = yield from _round_robin(attend(blk, rs), recur(blk, rs))
            ya_parts.append(res[0])

        ya_bf = (jnp.concatenate(ya_parts, axis=0)
                 * _silu(p_ref[:, OFF_ZA:OFF_ZA + WIDTH_A])).astype(BF)
        yield 160
        yb_bf = _branch_b_out(ob_ref[...], p_ref[:, OFF_ZB:OFF_ZB + WIDTH_B],
                              gains_ref[:, G_O:G_END]).astype(BF)
        yield 330

        y = (g_ref[:, :D_MODEL] * _dot(ya_bf, w_pa_ref[...])
             + g_ref[:, D_MODEL:] * _dot(yb_bf, w_pb_ref[...]))
        y_ref[y_rows, :] = x_ref[...] + _modulation(mod_ref)[2] * _dot(y.astype(BF), w_out_ref[...])
        yield 0

    def interleave(a_thunks, b_parts, lead):
        for th in a_thunks[:lead]:
            th()
        acc, done = 0, lead
        for cost in b_parts:
            acc += cost
            upto = lead + int(round((len(a_thunks) - lead) * min(acc, STAGE_B_COST) / STAGE_B_COST))
            for th in a_thunks[done:upto]:
                th()
            done = upto
        assert done == len(a_thunks) and acc == STAGE_B_COST, (done, acc)

    @pl.when(s == 0)
    def _():
        st_ref[...] = jnp.zeros_like(st_ref)
        span_ref[...] = jnp.zeros_like(span_ref)
        for r in (kprev_ref, kprev_sw_ref, vprev_ref, vprev_sw_ref):
            r[...] = jnp.zeros_like(r)
        for th in stage_a(x_refs[0], h0_ref, p0_ref):
            th()

    for k in range(tps):
        cur, nxt = k % 2, (k + 1) % 2
        phase(h_refs[cur], stage_a(x_refs[k + 1], h_refs[nxt], p_refs[nxt]),
              stage_b(x_refs[k], p_refs[cur], tps * s + k, slice(k * tile, (k + 1) * tile)))

    @pl.when(s == n_steps - 1)
    def _():
        for hd in range(N_HEADS_B):
            state_ref[hd] = st_ref[hd].T
        kwin_ref[...] = kwin_ref[...].T
        vwin_ref[...] = vwin_ref[...].T


def _const_spec(shape):
    nd = len(shape)
    return pl.BlockSpec(shape, lambda i, *_: (0,) * nd, pipeline_mode=pl.Buffered(1))


def _prompt_call(x, mod, consts, w, tile, tps, robust):
    t = x.shape[0]
    n_tiles = t // tile
    n_steps = n_tiles // tps
    assert n_steps * tps * tile == t and tps % 2 == 0
    row = lambda n: _const_spec((1, n))
    x_spec = lambda k: pl.BlockSpec(
        (tile, D_MODEL), lambda i, *_: (jnp.minimum(tps * i + k, n_tiles - 1), 0))
    in_specs = [x_spec(k) for k in range(tps + 1)] + [
        row(3 * D_MODEL), row(D_MODEL),
        _const_spec((D_MODEL, D_IN)), _const_spec((D_MODEL, 2 * D_MODEL)), row(2 * D_MODEL),
        _const_spec((WIDTH_A, D_MODEL)), _const_spec((WIDTH_B, D_MODEL)),
        _const_spec((D_MODEL, D_MODEL)),
        row(G_END), _const_spec((2, WIDTH_B)),
        _const_spec((n_tiles, LANES)), _const_spec((n_tiles, LANES)),
        _const_spec((tile, LANES)), _const_spec((tile, LANES)), _const_spec((2, LANES)),
        _const_spec((WIDTH_A, WIDTH_A)), _const_spec((BLK, BLK)),
        _const_spec((2, 4 * BLK, 2 * BLK)), _const_spec((len(LEVELS), BLK, BLK)),
    ]
    out_specs = [
        pl.BlockSpec((tps * tile, D_MODEL), lambda i, *_: (i, 0)),
        pl.BlockSpec((WINDOW, KV_WIDTH), lambda i, *_: (0, 0)),
        pl.BlockSpec((WINDOW, KV_WIDTH), lambda i, *_: (0, 0)),
        pl.BlockSpec((N_HEADS_B, HEAD_DIM_B, HEAD_DIM_B), lambda i, *_: (0, 0, 0)),
        pl.BlockSpec((1, WIDTH_B), lambda i, *_: (0, 0)),
    ]
    out_shape = [
        jax.ShapeDtypeStruct((t, D_MODEL), F32),
        jax.ShapeDtypeStruct((WINDOW, KV_WIDTH), F32),
        jax.ShapeDtypeStruct((WINDOW, KV_WIDTH), F32),
        jax.ShapeDtypeStruct((N_HEADS_B, HEAD_DIM_B, HEAD_DIM_B), F32),
        jax.ShapeDtypeStruct((1, WIDTH_B), F32),
    ]
    scratch = [
        pltpu.VMEM((N_HEADS_B, HEAD_DIM_B, HEAD_DIM_B), F32),
        pltpu.VMEM((BLK, KV_WIDTH), BF), pltpu.VMEM((BLK, KV_WIDTH), BF),
        pltpu.VMEM((BLK, KV_WIDTH), BF), pltpu.VMEM((BLK, KV_WIDTH), BF),
        pltpu.VMEM((tile, D_IN), F32), pltpu.VMEM((tile, D_IN), F32),
        pltpu.VMEM((tile, 2 * D_MODEL), F32),
        pltpu.VMEM((tile, D_MODEL), BF), pltpu.VMEM((tile, D_MODEL), BF),
        pltpu.VMEM((tile, WIDTH_B), F32),
    ]
    return pl.pallas_call(
        functools.partial(_prompt_kernel, tile=tile, n_steps=n_steps, tps=tps, robust=robust),
        grid_spec=pltpu.PrefetchScalarGridSpec(
            num_scalar_prefetch=1, grid=(n_steps,),
            in_specs=in_specs, out_specs=out_specs, scratch_shapes=scratch),
        out_shape=out_shape,
        compiler_params=pltpu.CompilerParams(
            dimension_semantics=("arbitrary",), vmem_limit_bytes=VMEM_LIMIT),
        name="prompt_layer_robust" if robust else "prompt_layer",
    )(w["sinks"], *([x] * (tps + 1)), mod, w["norm_g"],
      w["w_in"], w["w_merge"], w["b_merge"], w["w_proj_a"], w["w_proj_b"], w["w_out"],
      w["gains"], w["lb_logits"],
      consts["rope_cb"], consts["rope_sb"], consts["rope_cl"], consts["rope_sl"],
      consts["rope_sgn"],
      consts["seg"], consts["tri"], consts["bias"], consts["lmask"])


def _dec_in_kernel(x_ref, mod_ref, ng_ref, w_in_ref, w_mg_ref, b_mg_ref, gains_ref, lbl_ref,
                   rc_ref, ru_ref, rd_ref, seg_ref, perm_ref,
                   w_in_bf_ref, w_mg_bf_ref, g_ref,
                   qm_ref, kn_ref, vn_ref, za_ref, qb_ref, kin_ref, ib_ref, zb_ref,
                   h_ref, p_ref):
    j = pl.program_id(0)
    half_in = D_IN // 2

    @pl.when(j == 0)
    def _():
        h_ref[...] = _norm_modulate(x_ref[:, 0, :], ng_ref[...], mod_ref)

    w_in_bf = w_in_ref[...].astype(BF)
    w_mg_bf = w_mg_ref[...].astype(BF)
    w_in_bf_ref[...] = w_in_bf
    w_mg_bf_ref[...] = w_mg_bf
    h_bf = h_ref[...]
    g_ref[...] = _sigmoid(_dot(h_bf, w_mg_bf) + b_mg_ref[...])
    part = _dot(h_bf, w_in_bf)

    @pl.when(j == 0)
    def _():
        p_ref[:, :half_in] = part

    @pl.when(j == 1)
    def _():
        p_ref[:, half_in:] = part
        seg = seg_ref[...]
        rc, ru, rd = rc_ref[...], ru_ref[...], rd_ref[...]
        qa = p_ref[:, OFF_QA:OFF_QA + WIDTH_A]
        qa = qa * _head_norm_scale(qa, seg) * (gains_ref[:, G_Q:G_K] * (HEAD_DIM_A ** -0.5))
        qa = jnp.concatenate(
            [_rope(qa[:, c * LANES:(c + 1) * LANES], rc, ru, rd) for c in range(4)], axis=1)
        qm = _dot(qa.astype(BF), perm_ref[...])
        for hd in range(N_HEADS_A):
            qm_ref[:, hd, :] = qm[:, hd * LANES:(hd + 1) * LANES]
        ka = p_ref[:, OFF_KA:OFF_KA + KV_WIDTH]
        ka = ka * _head_norm_scale(ka, seg[:KV_WIDTH, :KV_WIDTH]) * gains_ref[:, G_K:G_O]
        kn_ref[...] = _rope(ka, rc, ru, rd)
        vn_ref[...] = p_ref[:, OFF_VA:OFF_VA + KV_WIDTH]
        za_ref[...] = p_ref[:, OFF_ZA:OFF_ZA + WIDTH_A]
        qb_ref[...] = p_ref[:, OFF_QB:OFF_QB + WIDTH_B]
        lb = _lower_bound(lbl_ref[...])
        kin_ref[...] = (1.0 - lb) * (1.0 - _sigmoid(p_ref[:, OFF_FB:OFF_FB + WIDTH_B]))
        ib_ref[...] = p_ref[:, OFF_IB:OFF_IB + WIDTH_B]
        zb_ref[...] = p_ref[:, OFF_ZB:OFF_ZB + WIDTH_B]


def _dec_in_call(x, mod, consts, w):
    b = x.shape[0]
    n_steps = 2
    half_in, half_mg = D_IN // n_steps, 2 * D_MODEL // n_steps
    assert half_in % LANES == 0
    const = lambda a: pl.BlockSpec(a.shape, lambda j: (0,) * a.ndim)
    cols = lambda rows, n: pl.BlockSpec((rows, n), lambda j: (0, j))
    row_out = lambda n: pl.BlockSpec((b, n), lambda j: (0, 0))
    f = lambda n: jax.ShapeDtypeStruct((b, n), F32)
    widths = [KV_WIDTH, KV_WIDTH, WIDTH_A, WIDTH_B, WIDTH_B, WIDTH_B, WIDTH_B]
    qm_shape = (b, N_HEADS_A, LANES)
    small = [w["norm_g"]]
    tail = [w["gains"], w["lb_logits"], consts["rope_c1"], consts["rope_u1"], consts["rope_d1"],
            consts["seg"], consts["perm"]]
    return pl.pallas_call(
        _dec_in_kernel,
        grid=(n_steps,),
        in_specs=[const(x), const(mod)] + [const(a) for a in small]
        + [cols(D_MODEL, half_in), cols(D_MODEL, half_mg), cols(1, half_mg)]
        + [const(a) for a in tail],
        out_specs=[cols(D_MODEL, half_in), cols(D_MODEL, half_mg), cols(b, half_mg),
                   pl.BlockSpec(qm_shape, lambda j: (0, 0, 0))]
        + [row_out(n) for n in widths],
        out_shape=[jax.ShapeDtypeStruct((D_MODEL, D_IN), BF),
                   jax.ShapeDtypeStruct((D_MODEL, 2 * D_MODEL), BF), f(2 * D_MODEL),
                   jax.ShapeDtypeStruct(qm_shape, F32)]
        + [f(n) for n in widths],
        scratch_shapes=[pltpu.VMEM((b, D_MODEL), BF), pltpu.VMEM((b, D_IN), F32)],
        compiler_params=pltpu.CompilerParams(
            dimension_semantics=("arbitrary",), vmem_limit_bytes=VMEM_LIMIT),
        name="decode_in",
    )(x, mod, *small, w["w_in_f32"], w["w_merge_f32"], w["b_merge"], *tail)


def _dec_mix_kernel(sinks_ref, qm_ref, kn_ref, vn_ref, qb_ref, kin_ref, ib_ref,
                    ck_ref, cv_ref, st_ref,
                    att_ref, ob_ref, nk_ref, nv_ref, nst_ref, *, bt):
    nh = N_HEADS_A
    s = jnp.concatenate([_dot(qm_ref[j].astype(BF), ck_ref[j].astype(BF))
                         for j in range(bt)], axis=0)
    key = lax.broadcasted_iota(jnp.int32, (bt * nh, WINDOW), 1)
    s = jnp.where(key == 0, NEG, s)
    rep = lambda r: jnp.broadcast_to(r[...][:, None, :], (bt, nh, r.shape[-1])).reshape(
        bt * nh, r.shape[-1])
    kn_rows, vn_rows = rep(kn_ref), rep(vn_ref)
    head = lax.broadcasted_iota(jnp.int32, (bt * nh, 1), 0) % nh
    sink = jnp.zeros((bt * nh, 1), F32)
    for hd in range(nh):
        sink = jnp.where(head == hd, sinks_ref[hd], sink)
    s_new = jnp.sum(qm_ref[...].reshape(bt * nh, LANES) * kn_rows, axis=-1, keepdims=True)
    m = jnp.maximum(jnp.maximum(jnp.max(s, axis=-1, keepdims=True), s_new), sink)
    p = jnp.exp(s - m)
    p_new = jnp.exp(s_new - m)
    den = jnp.sum(p, axis=-1, keepdims=True) + p_new + jnp.exp(sink - m)
    p_bf = p.astype(BF)
    pv = jnp.concatenate([_dot_nt(p_bf[j * nh:(j + 1) * nh, :], cv_ref[j].astype(BF))
                          for j in range(bt)], axis=0)
    att_ref[...] = ((pv + p_new * vn_rows) * (1.0 / den)).reshape(bt, nh, LANES)
    kn_t, vn_t = kn_ref[...].T, vn_ref[...].T
    newest = lax.broadcasted_iota(jnp.int32, (KV_WIDTH, WINDOW), 1) == WINDOW - 1
    for j in range(bt):
        nk_ref[j] = jnp.where(newest, kn_t[:, j:j + 1], pltpu.roll(ck_ref[j], WINDOW - 1, 1))
        nv_ref[j] = jnp.where(newest, vn_t[:, j:j + 1], pltpu.roll(cv_ref[j], WINDOW - 1, 1))

    for hd in range(N_HEADS_B):
        sl = slice(hd * LANES, (hd + 1) * LANES)
        kin_t = kin_ref[:, sl].T
        q_bf = qb_ref[:, sl].astype(BF)
        for j in range(bt):
            kcol = kin_t[:, j:j + 1]
            st = st_ref[j, hd]
            new = st - kcol * (st - ib_ref[j:j + 1, sl])
            nst_ref[j, hd] = new
            ob_ref[j:j + 1, sl] = _dot(q_bf, new.astype(BF))[j:j + 1, :]


def _dec_mix_call(sinks, qm, kn, vn, qb, kin, ib, cache_k, cache_v, state, bt):
    b = kn.shape[0]
    rows = lambda n: pl.BlockSpec((bt, n), lambda i, *_: (i, 0))
    cache_spec = pl.BlockSpec((bt, WINDOW, KV_WIDTH), lambda i, *_: (i, 0, 0))
    st_spec = pl.BlockSpec((bt, N_HEADS_B, HEAD_DIM_B, HEAD_DIM_B), lambda i, *_: (i, 0, 0, 0))
    qm_spec = pl.BlockSpec((bt, N_HEADS_A, LANES), lambda i, *_: (i, 0, 0))
    return pl.pallas_call(
        functools.partial(_dec_mix_kernel, bt=bt),
        grid_spec=pltpu.PrefetchScalarGridSpec(
            num_scalar_prefetch=1, grid=(b // bt,),
            in_specs=[qm_spec, rows(KV_WIDTH), rows(KV_WIDTH), rows(WIDTH_B), rows(WIDTH_B),
                      rows(WIDTH_B), cache_spec, cache_spec, st_spec],
            out_specs=[qm_spec, rows(WIDTH_B), cache_spec, cache_spec, st_spec]),
        out_shape=[jax.ShapeDtypeStruct((b, N_HEADS_A, LANES), F32),
                   jax.ShapeDtypeStruct((b, WIDTH_B), F32),
                   jax.ShapeDtypeStruct(cache_k.shape, F32),
                   jax.ShapeDtypeStruct(cache_v.shape, F32),
                   jax.ShapeDtypeStruct(state.shape, F32)],
        compiler_params=pltpu.CompilerParams(
            dimension_semantics=("arbitrary",), vmem_limit_bytes=VMEM_LIMIT),
        name="decode_mix",
    )(sinks, qm, kn, vn, qb, kin, ib, cache_k, cache_v, state)


def _dec_out_kernel(x_ref, mod_ref, att_ref, za_ref, ob_ref, zb_ref, g_ref, gains_ref,
                    permt_ref, w_pa_ref, w_pb_ref, w_out_ref,
                    y_ref, w_pa_bf_ref, w_pb_bf_ref, w_out_bf_ref):
    w_pa, w_pb, w_out = (r[...].astype(BF) for r in (w_pa_ref, w_pb_ref, w_out_ref))
    w_pa_bf_ref[...] = w_pa
    w_pb_bf_ref[...] = w_pb
    w_out_bf_ref[...] = w_out
    att = sum(_split_dot(att_ref[:, hd, :], permt_ref[hd * LANES:(hd + 1) * LANES, :])
              for hd in range(N_HEADS_A))
    ya = att * _silu(za_ref[...])
    yb = _branch_b_out(ob_ref[...], zb_ref[...], gains_ref[:, G_O:G_END])
    g = g_ref[...]
    y = (g[:, :D_MODEL] * _dot(ya.astype(BF), w_pa) + g[:, D_MODEL:] * _dot(yb.astype(BF), w_pb))
    y_ref[:, 0, :] = x_ref[:, 0, :] + _modulation(mod_ref)[2] * _dot(y.astype(BF), w_out)


def _dec_out_call(x, mod, att, za, ob, zb, g, consts, w):
    bf = lambda a: jax.ShapeDtypeStruct(a.shape, BF)
    ws = (w["w_proj_a_f32"], w["w_proj_b_f32"], w["w_out_f32"])
    return pl.pallas_call(
        _dec_out_kernel,
        out_shape=[jax.ShapeDtypeStruct(x.shape, F32)] + [bf(a) for a in ws],
        compiler_params=pltpu.CompilerParams(vmem_limit_bytes=VMEM_LIMIT),
        name="decode_out",
    )(x, mod, att, za, ob, zb, g, w["gains"], consts["permt"], *ws)


def _rope_consts(n_tiles, tile, past_len):
    half = ROT_DIM // 2
    inv = ROPE_THETA ** (-np.arange(0, ROT_DIM, 2, dtype=np.float64) / ROT_DIM)
    e = np.arange(LANES) % HEAD_DIM_A
    rot = e < ROT_DIM

    def tables(pos):
        ang = np.asarray(pos, np.float64)[:, None] * inv[e % half][None, :]
        return np.where(rot, np.cos(ang), 1.0), np.where(rot, np.sin(ang), 0.0)

    sgn = np.stack([np.where(e < half, -1.0, 0.0),
                    np.where(rot & (e >= half), 1.0, 0.0)])
    cb, sb = tables(np.arange(n_tiles) * tile)
    cl, sl = tables(np.arange(tile))
    c1, s1 = tables([past_len])
    f = lambda a: jnp.asarray(a, F32)
    return {"rope_cb": f(cb), "rope_sb": f(sb), "rope_cl": f(cl), "rope_sl": f(sl),
            "rope_sgn": f(sgn), "rope_c1": f(c1), "rope_u1": f(s1 * sgn[0:1]),
            "rope_d1": f(s1 * sgn[1:2])}


def _static_consts():
    seg = np.kron(np.eye(N_HEADS_A), np.full((HEAD_DIM_A, HEAD_DIM_A), 1.0 / HEAD_DIM_A))
    tri = np.tril(np.ones((BLK, BLK)))
    r = np.arange(4 * BLK)[:, None] % BLK
    c = np.arange(2 * BLK)[None, :]
    ok_prev = (c < BLK) & (c > r)
    ok_cur = (c >= BLK) & (c - BLK <= r)
    bias = np.stack([np.where(ok_cur, 0.0, NEG), np.where(ok_prev | ok_cur, 0.0, NEG)])
    i = np.arange(BLK)[:, None]
    j = np.arange(BLK)[None, :]
    lmask = np.stack([((i // (2 * b)) == (j // (2 * b))) & ((i & b) != 0) & ((j & b) == 0)
                      for b in LEVELS]).astype(np.float32)
    perm = np.zeros((WIDTH_A, N_HEADS_A * LANES), np.float32)
    for hd in range(N_HEADS_A):
        kvh = hd // (N_HEADS_A // N_KV_A)
        for d in range(HEAD_DIM_A):
            perm[hd * HEAD_DIM_A + d, hd * LANES + kvh * HEAD_DIM_A + d] = 1.0
    return {
        "seg": jnp.asarray(seg, BF), "tri": jnp.asarray(tri, BF),
        "bias": jnp.asarray(bias, F32), "lmask": jnp.asarray(lmask, F32),
        "perm": jnp.asarray(perm, BF), "permt": jnp.asarray(perm.T, BF),
    }


def kernel(x_prompt, x_sample, cache_win_k, cache_win_v, state_hgrn, c_prompt, c_sample,
           w_ada, b_ada, norm_g, w_in, q_norm_g, k_norm_g, sinks, lb_logits, o_norm_g,
           w_merge, b_merge, w_proj_a, w_proj_b, w_out):
    depth = w_in.shape[0]
    assert depth == 1 and x_prompt.shape[0] == 1 and x_sample.shape[1] == 1
    t = x_prompt.shape[1]
    nb = x_sample.shape[0]
    past_len = t
    tile = PROMPT_TILE
    bt = DECODE_BATCH_TILE

    consts = _static_consts()
    consts.update(_rope_consts(t // tile, tile, past_len))

    w = {
        "sinks": sinks[0], "norm_g": norm_g,
        "w_in_f32": w_in[0], "w_merge_f32": w_merge[0], "b_merge": b_merge,
        "w_proj_a_f32": w_proj_a[0], "w_proj_b_f32": w_proj_b[0], "w_out_f32": w_out[0],
        "lb_logits": lb_logits,
    }

    mod_p, mod_s, w["gains"] = _ada_call(c_prompt, c_sample, w_ada[0], b_ada,
                                         q_norm_g, k_norm_g, o_norm_g)

    xs = x_sample
    (w["w_in"], w["w_merge"], g,
     qm, kn, vn, za, qb, kin, ib, zb) = _dec_in_call(xs, mod_s, consts, w)
    to_t = lambda c: jnp.transpose(c[0], (0, 2, 3, 1)).reshape(nb, KV_WIDTH, WINDOW)
    from_t = lambda c: jnp.transpose(
        c.reshape(-1, N_KV_A, HEAD_DIM_A, WINDOW), (0, 3, 1, 2))[None]
    att, ob, nk, nv, nst = _dec_mix_call(
        w["sinks"], qm, kn, vn, qb, kin, ib,
        to_t(cache_win_k), to_t(cache_win_v), state_hgrn[0], bt)
    y_s, w["w_proj_a"], w["w_proj_b"], w["w_out"] = _dec_out_call(
        xs, mod_s, att, za, ob, zb, g, consts, w)

    run = functools.partial(_prompt_call, x_prompt[0], mod_p, consts, w, tile,
                            PROMPT_TILES_PER_STEP)
    *fast, span = run(robust=False)
    y_p, kwin, vwin, st_p = lax.cond(jnp.max(span) < HGRN_FAST_MAX,
                                     lambda: tuple(fast), lambda: tuple(run(robust=True)[:4]))

    return (y_p[None], y_s,
            from_t(kwin[None]), from_t(vwin[None]), st_p[None, None],
            from_t(nk), from_t(nv), nst[None])
```

```python
import functools

import numpy as np
import jax
import jax.numpy as jnp
from jax import lax
from jax.experimental import pallas as pl
from jax.experimental.pallas import tpu as pltpu

D_MODEL = 1024
HEAD_DIM_A = 64
N_HEADS_A = 8
N_KV_A = 2
WIDTH_A = N_HEADS_A * HEAD_DIM_A
KV_WIDTH = N_KV_A * HEAD_DIM_A
WINDOW = 128
ROT_DIM = HEAD_DIM_A // 4
ROPE_THETA = 500000.0
HEAD_DIM_B = 128
N_HEADS_B = 4
WIDTH_B = N_HEADS_B * HEAD_DIM_B
EPS = 1e-6
OFF_QA = 0
OFF_KA = OFF_QA + WIDTH_A
OFF_VA = OFF_KA + KV_WIDTH
OFF_ZA = OFF_VA + KV_WIDTH
OFF_QB = OFF_ZA + WIDTH_A
OFF_FB = OFF_QB + WIDTH_B
OFF_IB = OFF_FB + WIDTH_B
OFF_ZB = OFF_IB + WIDTH_B
D_IN = OFF_ZB + WIDTH_B

LANES = 128
BLK = 128
SUB = 8
LEVELS = (64, 32, 16, 8)
HGRN_FAST_MAX = 80.0
MXU_N = 256
NEG = -1e30
VMEM_LIMIT = 56 * 1024 * 1024
PROMPT_TILE = 256
PROMPT_TILES_PER_STEP = 2
DECODE_BATCH_TILE = 16
G_Q, G_K, G_O = 0, WIDTH_A, WIDTH_A + KV_WIDTH
G_END = G_O + WIDTH_B

BF = jnp.bfloat16
F32 = jnp.float32


def _dot(a, b):
    return jnp.dot(a, b, preferred_element_type=F32)


def _dot_nt(a, b):
    return lax.dot_general(a, b, (((1,), (1,)), ((), ())), preferred_element_type=F32)


def _dot_tn(a, b):
    return lax.dot_general(a, b, (((0,), (0,)), ((), ())), preferred_element_type=F32)


def _split_dot(a_f32, b_bf):
    hi = a_f32.astype(BF)
    lo = (a_f32 - hi.astype(F32)).astype(BF)
    return _dot(hi, b_bf) + _dot(lo, b_bf)


def _sigmoid(x):
    return 1.0 / (1.0 + jnp.exp(-x))


def _silu(x):
    return x * _sigmoid(x)


def _lower_bound(lb_logits):
    l0 = lb_logits[0:1, :]
    l1 = lb_logits[1:2, :]
    m = jnp.maximum(l0, l1)
    e0 = jnp.exp(l0 - m)
    e1 = jnp.exp(l1 - m)
    return e0 / (e0 + e1)


def _rope_tables(cb, sb, cl, sl, sgn_up, sgn_dn):
    c = cb * cl - sb * sl
    s = sb * cl + cb * sl
    return c, s * sgn_up, s * sgn_dn


def _rope(x, c, s_up, s_dn):
    return x * c + pltpu.roll(x, LANES - ROT_DIM // 2, 1) * s_up + pltpu.roll(x, ROT_DIM // 2, 1) * s_dn


def _modulation(mod_ref):
    return (mod_ref[:, 0:D_MODEL], mod_ref[:, D_MODEL:2 * D_MODEL],
            mod_ref[:, 2 * D_MODEL:3 * D_MODEL])


def _norm_modulate(x, ng, mod_ref):
    shift, scale, _ = _modulation(mod_ref)
    ms = jnp.mean(x * x, axis=-1, keepdims=True)
    return (x * lax.rsqrt(ms + EPS) * (ng * (1.0 + scale)) + shift).astype(BF)


def _head_norm_scale(x, seg_mean_bf):
    ms = _dot((x * x).astype(BF), seg_mean_bf)
    return lax.rsqrt(ms + EPS)


def _ada_kernel(cp_ref, cs_ref, w_ref, b_ref, qg_ref, kg_ref, og_ref, op_ref, os_ref, gains_ref):
    gains_ref[...] = jnp.concatenate(
        [qg_ref[...]] * N_HEADS_A + [kg_ref[...]] * N_KV_A + [og_ref[...]] * N_HEADS_B, axis=1)
    w = w_ref[...]
    w_hi = w.astype(BF)
    w_lo = (w - w_hi.astype(F32)).astype(BF)
    b = b_ref[...]

    ns, n_p = cs_ref.shape[0], cp_ref.shape[0]
    c = jnp.concatenate([cs_ref[...], cp_ref[...],
                         jnp.zeros(((-n_p) % SUB, D_MODEL), F32)], axis=0)
    c_hi = c.astype(BF)
    c_lo = (c - c_hi.astype(F32)).astype(BF)
    out = _dot(c_hi, w_hi) + (_dot(c_hi, w_lo) + _dot(c_lo, w_hi)) + b
    os_ref[...] = out[:ns, :]
    op_ref[...] = out[ns:ns + n_p, :]


def _ada_call(c_p, c_s, w_ada, b_ada, q_g, k_g, o_g):
    mp, ms = c_p.shape[0], c_s.shape[0]
    n = w_ada.shape[1]
    tn = 512
    const = lambda a: pl.BlockSpec(a.shape, lambda j: (0,) * a.ndim)
    return pl.pallas_call(
        _ada_kernel,
        grid=(n // tn,),
        in_specs=[const(c_p), const(c_s),
                  pl.BlockSpec((D_MODEL, tn), lambda j: (0, j)),
                  pl.BlockSpec((1, tn), lambda j: (0, j)),
                  const(q_g), const(k_g), const(o_g)],
        out_specs=[pl.BlockSpec((mp, tn), lambda j: (0, j)),
                   pl.BlockSpec((ms, tn), lambda j: (0, j)),
                   pl.BlockSpec((1, G_END), lambda j: (0, 0))],
        out_shape=[jax.ShapeDtypeStruct((mp, n), F32), jax.ShapeDtypeStruct((ms, n), F32),
                   jax.ShapeDtypeStruct((1, G_END), F32)],
        name="ada",
    )(c_p, c_s, w_ada, b_ada, q_g, k_g, o_g)


STAGE_B_COST = 3840


def _round_robin(*gens):
    results = [None] * len(gens)
    live = list(range(len(gens)))
    while live:
        for n in list(live):
            try:
                yield next(gens[n])
            except StopIteration as stop:
                results[n] = stop.value
                live.remove(n)
    return results


def _attn_block(q_blk, kcat, kcat_sw, vcat, vcat_sw, bias, sink_a, sink_b):
    lane = lax.broadcasted_iota(jnp.int32, (BLK, LANES), 1)
    lo = lane < HEAD_DIM_A
    chunks = [q_blk[:, c * LANES:(c + 1) * LANES] for c in range(4)]
    zero = jnp.zeros((BLK, LANES), F32)
    q_lo = [jnp.where(lo, c, zero).astype(BF) for c in chunks]
    q_hi = [jnp.where(lo, zero, c).astype(BF) for c in chunks]
    qa = jnp.concatenate([q_lo[0], q_lo[1], q_hi[2], q_hi[3]], axis=0)
    qb = jnp.concatenate([q_hi[0], q_hi[1], q_lo[2], q_lo[3]], axis=0)

    def probs(qs, kc, sink):
        s = _dot_nt(qs, kc) + bias
        m = jnp.maximum(jnp.max(s, axis=-1, keepdims=True), sink)
        p = jnp.exp(s - m)
        den = jnp.sum(p, axis=-1, keepdims=True) + jnp.exp(sink - m)
        return p.astype(BF), 1.0 / den

    pa, ra = probs(qa, kcat, sink_a)
    yield 200
    pb, rb = probs(qb, kcat_sw, sink_b)
    yield 200
    oa = _dot(pa, vcat) * ra
    yield 40
    ob = _dot(pb, vcat_sw) * rb
    yield 40
    r = lambda o, i: o[i * BLK:(i + 1) * BLK, :]
    out = jnp.concatenate([
        jnp.where(lo, r(oa, 0), r(ob, 0)),
        jnp.where(lo, r(oa, 1), r(ob, 1)),
        jnp.where(lo, r(ob, 2), r(oa, 2)),
        jnp.where(lo, r(ob, 3), r(oa, 3)),
    ], axis=1)
    yield 30
    return out


def _hgrn_gates(fb, lb, one_m_lb, tri_bf):
    sig = _sigmoid(fb)
    kin = one_m_lb * (1.0 - sig)
    f = lb + one_m_lb * sig
    cum = _dot(tri_bf, jnp.log(f).astype(BF))
    return kin, f, cum


def _hgrn_span_decay(cum):
    q = BLK // 4
    ends = [cum[(n + 1) * q - 1:(n + 1) * q, :] for n in range(4)]
    d = -ends[0]
    for n in range(1, 4):
        d = jnp.maximum(d, ends[n - 1] - ends[n])
    return d


def _hgrn_state_step(qb, kin, cum, ib, st_ref):
    q_dec = (qb * jnp.exp(cum)).astype(BF)
    last = cum[BLK - 1:BLK, :]
    k_dec = (kin * jnp.exp(last - cum)).astype(BF)
    v_bf = ib.astype(BF)
    outs = []
    for h in range(N_HEADS_B):
        sl = slice(h * LANES, (h + 1) * LANES)
        st = st_ref[h]
        outs.append(_dot_nt(q_dec[:, sl], st.astype(BF)))
        st_ref[h] = st * jnp.exp(last[:, sl]) + _dot_tn(v_bf[:, sl], k_dec[:, sl])
    return jnp.concatenate(outs, axis=1)


def _hgrn_apply(amats, ib):
    v_bf = ib.astype(BF)
    return jnp.concatenate(
        [_dot(amats[h].astype(BF), v_bf[:, h * LANES:(h + 1) * LANES])
         for h in range(N_HEADS_B)], axis=1)


def _recur_block_fast(qb, kin, cum, ib, st_ref):
    base = _hgrn_state_step(qb, kin, cum, ib, st_ref)
    yield 130
    half = BLK // 2
    row = lax.broadcasted_iota(jnp.int32, (BLK, 1), 0)
    upper = row >= half
    piv = cum[half - 1:half, :]
    w_lvl = jnp.exp(jnp.concatenate([piv - cum[:half, :], cum[half:, :] - piv], axis=0))
    p_lvl = (jnp.where(upper, qb, kin) * w_lvl).astype(BF)
    mid = jnp.where(upper, cum[half + half // 2 - 1:half + half // 2, :],
                    cum[half // 2 - 1:half // 2, :])
    e_mid = cum - mid
    q_mid = (qb * jnp.exp(e_mid)).astype(BF)
    k_mid = (kin * jnp.exp(-e_mid)).astype(BF)
    yield 230

    ri = lax.broadcasted_iota(jnp.int32, (BLK, BLK), 0)
    ci = lax.broadcasted_iota(jnp.int32, (BLK, BLK), 1)
    same_half_causal = ((ri >= half) == (ci >= half)) & (ci <= ri)
    cross = (ri >= half) & (ci < half)
    amats = []
    for h in range(N_HEADS_B):
        sl = slice(h * LANES, (h + 1) * LANES)
        a_mid = _dot_nt(q_mid[:, sl], k_mid[:, sl])
        a_lvl = _dot_nt(p_lvl[:, sl], p_lvl[:, sl])
        amats.append(jnp.where(same_half_causal, a_mid, jnp.where(cross, a_lvl, 0.0)))
    yield 60
    full = base + _hgrn_apply(amats, ib)
    yield 40
    return full


def _hgrn_intra_robust(qb, kin, f, cum, ib, lvl_mask_ref):
    row = lax.broadcasted_iota(jnp.int32, (BLK, 1), 0)
    lvl_ops = []
    for b in LEVELS:
        pieces = []
        for r0 in range(0, BLK, 2 * b):
            piv = cum[r0 + b - 1:r0 + b, :]
            pieces.append(piv - cum[r0:r0 + b, :])
            pieces.append(cum[r0 + b:r0 + 2 * b, :] - piv)
        w = jnp.exp(jnp.concatenate(pieces, axis=0))
        second = (row & b) != 0
        lvl_ops.append((jnp.where(second, qb, kin) * w).astype(BF))

    n8 = BLK // SUB
    q3 = qb.reshape(n8, SUB, WIDTH_B)
    k3 = kin.reshape(n8, SUB, WIDTH_B)
    f3 = f.reshape(n8, SUB, WIDTH_B)
    v3 = ib.reshape(n8, SUB, WIDTH_B)
    subl = lax.broadcasted_iota(jnp.int32, (n8, SUB, 1), 1)

    def head(x, h):
        return x[..., h * LANES:(h + 1) * LANES]

    g = q3 * k3
    acc = [jnp.sum(head(g, h), axis=-1, keepdims=True) * head(v3, h) for h in range(N_HEADS_B)]
    dec = jnp.ones_like(f3)
    kd = k3
    vd = v3
    for d in range(1, SUB):
        dec = f3 * pltpu.roll(dec, 1, 1)
        kd = pltpu.roll(kd, 1, 1)
        vd = pltpu.roll(vd, 1, 1)
        g = q3 * kd * dec
        ok = subl >= d
        for h in range(N_HEADS_B):
            a = jnp.where(ok, jnp.sum(head(g, h), axis=-1, keepdims=True), 0.0)
            acc[h] = acc[h] + a * head(vd, h)

    amats = []
    for h in range(N_HEADS_B):
        sl = slice(h * LANES, (h + 1) * LANES)
        amat = jnp.zeros((BLK, BLK), F32)
        for li in range(len(LEVELS)):
            p = lvl_ops[li][:, sl]
            amat = amat + lvl_mask_ref[li] * _dot_nt(p, p)
        amats.append(amat)
    diag = jnp.concatenate([acc[h].reshape(BLK, LANES) for h in range(N_HEADS_B)], axis=1)
    return _hgrn_apply(amats, ib) + diag


def _branch_b_out(o, zb, og):
    outs = []
    for h in range(N_HEADS_B):
        sl = slice(h * LANES, (h + 1) * LANES)
        oh = o[:, sl]
        ms = jnp.mean(oh * oh, axis=-1, keepdims=True)
        outs.append(oh * lax.rsqrt(ms + EPS))
    return jnp.concatenate(outs, axis=1) * og * _silu(zb)


def _prompt_kernel(sinks_ref, *refs, tile, n_steps, tps, robust):
    x_refs = refs[:tps + 1]
    (mod_ref, ng_ref,
     w_in_ref, w_mg_ref, b_mg_ref, w_pa_ref, w_pb_ref, w_out_ref,
     gains_ref, lbl_ref,
     cb_ref, sb_ref, cl_ref, sl_ref, sgn_ref,
     seg_ref, tri_ref, bias_ref, lmask_ref,
     y_ref, kwin_ref, vwin_ref, state_ref, span_ref,
     st_ref, kprev_ref, kprev_sw_ref, vprev_ref, vprev_sw_ref,
     p0_ref, p1_ref, g_ref, h0_ref, h1_ref, ob_ref) = refs[tps + 1:]
    p_refs, h_refs = (p0_ref, p1_ref), (h0_ref, h1_ref)
    s = pl.program_id(0)
    nblk = tile // BLK

    def stage_a(x_ref, h_ref, p_ref):
        def prep():
            h_ref[...] = _norm_modulate(x_ref[...], ng_ref[...], mod_ref)

        def proj_chunk(c):
            def run():
                cs = slice(c * MXU_N, (c + 1) * MXU_N)
                p_ref[:, cs] = _dot(h_ref[...], w_in_ref[:, cs])
            return run

        return [prep] + [proj_chunk(c) for c in range(D_IN // MXU_N)]

    def gate_chunks(h_ref):
        def gate_chunk(c):
            def run():
                cs = slice(c * MXU_N, (c + 1) * MXU_N)
                g_ref[:, cs] = _sigmoid(_dot(h_ref[...], w_mg_ref[:, cs]) + b_mg_ref[:, cs])
            return run

        return [gate_chunk(c) for c in range(2 * D_MODEL // MXU_N)]

    def phase(h_cur_ref, a_next, b_parts):
        gc = gate_chunks(h_cur_ref)
        interleave(gc[:2] + a_next[:1] + gc[2:] + a_next[1:], b_parts, lead=10)

    def stage_b(x_ref, p_ref, t_idx, y_rows):
        blocks = [slice(blk * BLK, (blk + 1) * BLK) for blk in range(nblk)]

        def qkv_stream():
            seg = seg_ref[...]
            rc, ru, rd = _rope_tables(cb_ref[pl.ds(t_idx, 1), :], sb_ref[pl.ds(t_idx, 1), :],
                                      cl_ref[...], sl_ref[...], sgn_ref[0:1, :], sgn_ref[1:2, :])
            qa = p_ref[:, OFF_QA:OFF_QA + WIDTH_A]
            qa = qa * _head_norm_scale(qa, seg) * (gains_ref[:, G_Q:G_K] * (HEAD_DIM_A ** -0.5))
            yield 160
            qa = jnp.concatenate(
                [_rope(qa[:, c * LANES:(c + 1) * LANES], rc, ru, rd) for c in range(4)], axis=1)
            yield 160
            ka = p_ref[:, OFF_KA:OFF_KA + KV_WIDTH]
            ka = ka * _head_norm_scale(ka, seg[:KV_WIDTH, :KV_WIDTH]) * gains_ref[:, G_K:G_O]
            ka = _rope(ka, rc, ru, rd)
            va = p_ref[:, OFF_VA:OFF_VA + KV_WIDTH]
            ka_sw = pltpu.roll(ka, HEAD_DIM_A, 1)
            va_sw = pltpu.roll(va, HEAD_DIM_A, 1)
            rows4 = lax.broadcasted_iota(jnp.int32, (4 * BLK, 1), 0) // BLK

            def sink_col(heads):
                col = jnp.zeros((4 * BLK, 1), F32)
                for n, hd in enumerate(heads):
                    col = jnp.where(rows4 == n, sinks_ref[hd], col)
                return col
            kwin_ref[...] = ka[tile - WINDOW:, :]
            vwin_ref[...] = va[tile - WINDOW:, :]
            res = (qa, ka, va, ka_sw, va_sw, sink_col((0, 2, 5, 7)), sink_col((1, 3, 4, 6)))
            yield 120
            return res

        def gates_stream():
            lb = _lower_bound(lbl_ref[...])
            tri = tri_ref[...]
            gates = []
            half_w = WIDTH_B // 2
            for rs in blocks:
                halves = []
                for c0 in (0, half_w):
                    cs = slice(c0, c0 + half_w)
                    halves.append(_hgrn_gates(p_ref[rs, OFF_FB + c0:OFF_FB + c0 + half_w],
                                              lb[:, cs], 1.0 - lb[:, cs], tri))
                    yield 240
                gates.append(tuple(jnp.concatenate(pair, axis=1) for pair in zip(*halves)))
            span = span_ref[...]
            for g in gates:
                span = jnp.maximum(span, _hgrn_span_decay(g[2]))
            span_ref[...] = span
            yield 10
            return gates

        (qa, ka, va, ka_sw, va_sw, sink_a, sink_b), gates = (
            yield from _round_robin(qkv_stream(), gates_stream()))

        kv_refs = (kprev_ref, kprev_sw_ref, vprev_ref, vprev_sw_ref)
        kv_new = [a.astype(BF) for a in (ka, ka_sw, va, va_sw)]
        kv_old = [r[...] for r in kv_refs]
        for r, a in zip(kv_refs, kv_new):
            r[...] = a[blocks[-1]]

        def attend(blk, rs):
            cats = [jnp.concatenate([old if blk == 0 else new[blocks[blk - 1]], new[rs]], axis=0)
                    for old, new in zip(kv_old, kv_new)]
            bias = bias_ref[jnp.where(t_idx == 0, 0, 1)] if blk == 0 else bias_ref[1]
            return (yield from _attn_block(qa[rs], *cats, bias, sink_a, sink_b))

        def recur(blk, rs):
            kin, f, cum = gates[blk]
            qb, ib = p_ref[rs, OFF_QB:OFF_QB + WIDTH_B], p_ref[rs, OFF_IB:OFF_IB + WIDTH_B]
            if robust:
                base = _hgrn_state_step(qb, kin, cum, ib, st_ref)
                yield 130
                full = base + _hgrn_intra_robust(qb, kin, f, cum, ib, lmask_ref)
                yield 230 + 60 + 40
            else:
                full = yield from _recur_block_fast(qb, kin, cum, ib, st_ref)
            ob_ref[rs, :] = full

        ya_parts = []
        for blk, rs in enumerate(blocks):
            res = yield from _round_robin(attend(blk, rs), recur(blk, rs))
            ya_parts.append(res[0])

        ya_bf = (jnp.concatenate(ya_parts, axis=0)
                 * _silu(p_ref[:, OFF_ZA:OFF_ZA + WIDTH_A])).astype(BF)
        yield 160
        yb_bf = _branch_b_out(ob_ref[...], p_ref[:, OFF_ZB:OFF_ZB + WIDTH_B],
                              gains_ref[:, G_O:G_END]).astype(BF)
        yield 330

        y = (g_ref[:, :D_MODEL] * _dot(ya_bf, w_pa_ref[...])
             + g_ref[:, D_MODEL:] * _dot(yb_bf, w_pb_ref[...]))
        y_ref[y_rows, :] = x_ref[...] + _modulation(mod_ref)[2] * _dot(y.astype(BF), w_out_ref[...])
        yield 0

    def interleave(a_thunks, b_parts, lead):
        for th in a_thunks[:lead]:
            th()
        acc, done = 0, lead
        for cost in b_parts:
            acc += cost
            upto = lead + int(round((len(a_thunks) - lead) * min(acc, STAGE_B_COST) / STAGE_B_COST))
            for th in a_thunks[done:upto]:
                th()
            done = upto
        assert done == len(a_thunks) and acc == STAGE_B_COST, (done, acc)

    @pl.when(s == 0)
    def _():
        st_ref[...] = jnp.zeros_like(st_ref)
        span_ref[...] = jnp.zeros_like(span_ref)
        for r in (kprev_ref, kprev_sw_ref, vprev_ref, vprev_sw_ref):
            r[...] = jnp.zeros_like(r)
        for th in stage_a(x_refs[0], h0_ref, p0_ref):
            th()

    for k in range(tps):
        cur, nxt = k % 2, (k + 1) % 2
        phase(h_refs[cur], stage_a(x_refs[k + 1], h_refs[nxt], p_refs[nxt]),
              stage_b(x_refs[k], p_refs[cur], tps * s + k, slice(k * tile, (k + 1) * tile)))

    @pl.when(s == n_steps - 1)
    def _():
        for hd in range(N_HEADS_B):
            state_ref[hd] = st_ref[hd].T
        kwin_ref[...] = kwin_ref[...].T
        vwin_ref[...] = vwin_ref[...].T


def _const_spec(shape):
    nd = len(shape)
    return pl.BlockSpec(shape, lambda i, *_: (0,) * nd, pipeline_mode=pl.Buffered(1))


def _prompt_call(x, mod, consts, w, tile, tps, robust):
    t = x.shape[0]
    n_tiles = t // tile
    n_steps = n_tiles // tps
    assert n_steps * tps * tile == t and tps % 2 == 0
    row = lambda n: _const_spec((1, n))
    x_spec = lambda k: pl.BlockSpec(
        (tile, D_MODEL), lambda i, *_: (jnp.minimum(tps * i + k, n_tiles - 1), 0))
    in_specs = [x_spec(k) for k in range(tps + 1)] + [
        row(3 * D_MODEL), row(D_MODEL),
        _const_spec((D_MODEL, D_IN)), _const_spec((D_MODEL, 2 * D_MODEL)), row(2 * D_MODEL),
        _const_spec((WIDTH_A, D_MODEL)), _const_spec((WIDTH_B, D_MODEL)),
        _const_spec((D_MODEL, D_MODEL)),
        row(G_END), _const_spec((2, WIDTH_B)),
        _const_spec((n_tiles, LANES)), _const_spec((n_tiles, LANES)),
        _const_spec((tile, LANES)), _const_spec((tile, LANES)), _const_spec((2, LANES)),
        _const_spec((WIDTH_A, WIDTH_A)), _const_spec((BLK, BLK)),
        _const_spec((2, 4 * BLK, 2 * BLK)), _const_spec((len(LEVELS), BLK, BLK)),
    ]
    out_specs = [
        pl.BlockSpec((tps * tile, D_MODEL), lambda i, *_: (i, 0)),
        pl.BlockSpec((WINDOW, KV_WIDTH), lambda i, *_: (0, 0)),
        pl.BlockSpec((WINDOW, KV_WIDTH), lambda i, *_: (0, 0)),
        pl.BlockSpec((N_HEADS_B, HEAD_DIM_B, HEAD_DIM_B), lambda i, *_: (0, 0, 0)),
        pl.BlockSpec((1, WIDTH_B), lambda i, *_: (0, 0)),
    ]
    out_shape = [
        jax.ShapeDtypeStruct((t, D_MODEL), F32),
        jax.ShapeDtypeStruct((WINDOW, KV_WIDTH), F32),
        jax.ShapeDtypeStruct((WINDOW, KV_WIDTH), F32),
        jax.ShapeDtypeStruct((N_HEADS_B, HEAD_DIM_B, HEAD_DIM_B), F32),
        jax.ShapeDtypeStruct((1, WIDTH_B), F32),
    ]
    scratch = [
        pltpu.VMEM((N_HEADS_B, HEAD_DIM_B, HEAD_DIM_B), F32),
        pltpu.VMEM((BLK, KV_WIDTH), BF), pltpu.VMEM((BLK, KV_WIDTH), BF),
        pltpu.VMEM((BLK, KV_WIDTH), BF), pltpu.VMEM((BLK, KV_WIDTH), BF),
        pltpu.VMEM((tile, D_IN), F32), pltpu.VMEM((tile, D_IN), F32),
        pltpu.VMEM((tile, 2 * D_MODEL), F32),
        pltpu.VMEM((tile, D_MODEL), BF), pltpu.VMEM((tile, D_MODEL), BF),
        pltpu.VMEM((tile, WIDTH_B), F32),
    ]
    return pl.pallas_call(
        functools.partial(_prompt_kernel, tile=tile, n_steps=n_steps, tps=tps, robust=robust),
        grid_spec=pltpu.PrefetchScalarGridSpec(
            num_scalar_prefetch=1, grid=(n_steps,),
            in_specs=in_specs, out_specs=out_specs, scratch_shapes=scratch),
        out_shape=out_shape,
        compiler_params=pltpu.CompilerParams(
            dimension_semantics=("arbitrary",), vmem_limit_bytes=VMEM_LIMIT),
        name="prompt_layer_robust" if robust else "prompt_layer",
    )(w["sinks"], *([x] * (tps + 1)), mod, w["norm_g"],
      w["w_in"], w["w_merge"], w["b_merge"], w["w_proj_a"], w["w_proj_b"], w["w_out"],
      w["gains"], w["lb_logits"],
      consts["rope_cb"], consts["rope_sb"], consts["rope_cl"], consts["rope_sl"],
      consts["rope_sgn"],
      consts["seg"], consts["tri"], consts["bias"], consts["lmask"])


def _dec_in_kernel(x_ref, mod_ref, ng_ref, w_in_ref, w_mg_ref, b_mg_ref, gains_ref, lbl_ref,
                   rc_ref, ru_ref, rd_ref, seg_ref, perm_ref,
                   w_in_bf_ref, w_mg_bf_ref, g_ref,
                   qm_ref, kn_ref, vn_ref, za_ref, qb_ref, kin_ref, ib_ref, zb_ref,
                   h_ref, p_ref):
    j = pl.program_id(0)
    half_in = D_IN // 2

    @pl.when(j == 0)
    def _():
        h_ref[...] = _norm_modulate(x_ref[:, 0, :], ng_ref[...], mod_ref)

    w_in_bf = w_in_ref[...].astype(BF)
    w_mg_bf = w_mg_ref[...].astype(BF)
    w_in_bf_ref[...] = w_in_bf
    w_mg_bf_ref[...] = w_mg_bf
    h_bf = h_ref[...]
    g_ref[...] = _sigmoid(_dot(h_bf, w_mg_bf) + b_mg_ref[...])
    part = _dot(h_bf, w_in_bf)

    @pl.when(j == 0)
    def _():
        p_ref[:, :half_in] = part

    @pl.when(j == 1)
    def _():
        p_ref[:, half_in:] = part
        seg = seg_ref[...]
        rc, ru, rd = rc_ref[...], ru_ref[...], rd_ref[...]
        qa = p_ref[:, OFF_QA:OFF_QA + WIDTH_A]
        qa = qa * _head_norm_scale(qa, seg) * (gains_ref[:, G_Q:G_K] * (HEAD_DIM_A ** -0.5))
        qa = jnp.concatenate(
            [_rope(qa[:, c * LANES:(c + 1) * LANES], rc, ru, rd) for c in range(4)], axis=1)
        qm = _dot(qa.astype(BF), perm_ref[...])
        for hd in range(N_HEADS_A):
            qm_ref[:, hd, :] = qm[:, hd * LANES:(hd + 1) * LANES]
        ka = p_ref[:, OFF_KA:OFF_KA + KV_WIDTH]
        ka = ka * _head_norm_scale(ka, seg[:KV_WIDTH, :KV_WIDTH]) * gains_ref[:, G_K:G_O]
        kn_ref[...] = _rope(ka, rc, ru, rd)
        vn_ref[...] = p_ref[:, OFF_VA:OFF_VA + KV_WIDTH]
        za_ref[...] = p_ref[:, OFF_ZA:OFF_ZA + WIDTH_A]
        qb_ref[...] = p_ref[:, OFF_QB:OFF_QB + WIDTH_B]
        lb = _lower_bound(lbl_ref[...])
        kin_ref[...] = (1.0 - lb) * (1.0 - _sigmoid(p_ref[:, OFF_FB:OFF_FB + WIDTH_B]))
        ib_ref[...] = p_ref[:, OFF_IB:OFF_IB + WIDTH_B]
        zb_ref[...] = p_ref[:, OFF_ZB:OFF_ZB + WIDTH_B]


def _dec_in_call(x, mod, consts, w):
    b = x.shape[0]
    n_steps = 2
    half_in, half_mg = D_IN // n_steps, 2 * D_MODEL // n_steps
    assert half_in % LANES == 0
    const = lambda a: pl.BlockSpec(a.shape, lambda j: (0,) * a.ndim)
    cols = lambda rows, n: pl.BlockSpec((rows, n), lambda j: (0, j))
    row_out = lambda n: pl.BlockSpec((b, n), lambda j: (0, 0))
    f = lambda n: jax.ShapeDtypeStruct((b, n), F32)
    widths = [KV_WIDTH, KV_WIDTH, WIDTH_A, WIDTH_B, WIDTH_B, WIDTH_B, WIDTH_B]
    qm_shape = (b, N_HEADS_A, LANES)
    small = [w["norm_g"]]
    tail = [w["gains"], w["lb_logits"], consts["rope_c1"], consts["rope_u1"], consts["rope_d1"],
            consts["seg"], consts["perm"]]
    return pl.pallas_call(
        _dec_in_kernel,
        grid=(n_steps,),
        in_specs=[const(x), const(mod)] + [const(a) for a in small]
        + [cols(D_MODEL, half_in), cols(D_MODEL, half_mg), cols(1, half_mg)]
        + [const(a) for a in tail],
        out_specs=[cols(D_MODEL, half_in), cols(D_MODEL, half_mg), cols(b, half_mg),
                   pl.BlockSpec(qm_shape, lambda j: (0, 0, 0))]
        + [row_out(n) for n in widths],
        out_shape=[jax.ShapeDtypeStruct((D_MODEL, D_IN), BF),
                   jax.ShapeDtypeStruct((D_MODEL, 2 * D_MODEL), BF), f(2 * D_MODEL),
                   jax.ShapeDtypeStruct(qm_shape, F32)]
        + [f(n) for n in widths],
        scratch_shapes=[pltpu.VMEM((b, D_MODEL), BF), pltpu.VMEM((b, D_IN), F32)],
        compiler_params=pltpu.CompilerParams(
            dimension_semantics=("arbitrary",), vmem_limit_bytes=VMEM_LIMIT),
        name="decode_in",
    )(x, mod, *small, w["w_in_f32"], w["w_merge_f32"], w["b_merge"], *tail)


def _dec_mix_kernel(sinks_ref, qm_ref, kn_ref, vn_ref, qb_ref, kin_ref, ib_ref,
                    ck_ref, cv_ref, st_ref,
                    att_ref, ob_ref, nk_ref, nv_ref, nst_ref, *, bt):
    nh = N_HEADS_A
    s = jnp.concatenate([_dot(qm_ref[j].astype(BF), ck_ref[j].astype(BF))
                         for j in range(bt)], axis=0)
    key = lax.broadcasted_iota(jnp.int32, (bt * nh, WINDOW), 1)
    s = jnp.where(key == 0, NEG, s)
    rep = lambda r: jnp.broadcast_to(r[...][:, None, :], (bt, nh, r.shape[-1])).reshape(
        bt * nh, r.shape[-1])
    kn_rows, vn_rows = rep(kn_ref), rep(vn_ref)
    head = lax.broadcasted_iota(jnp.int32, (bt * nh, 1), 0) % nh
    sink = jnp.zeros((bt * nh, 1), F32)
    for hd in range(nh):
        sink = jnp.where(head == hd, sinks_ref[hd], sink)
    s_new = jnp.sum(qm_ref[...].reshape(bt * nh, LANES) * kn_rows, axis=-1, keepdims=True)
    m = jnp.maximum(jnp.maximum(jnp.max(s, axis=-1, keepdims=True), s_new), sink)
    p = jnp.exp(s - m)
    p_new = jnp.exp(s_new - m)
    den = jnp.sum(p, axis=-1, keepdims=True) + p_new + jnp.exp(sink - m)
    p_bf = p.astype(BF)
    pv = jnp.concatenate([_dot_nt(p_bf[j * nh:(j + 1) * nh, :], cv_ref[j].astype(BF))
                          for j in range(bt)], axis=0)
    att_ref[...] = ((pv + p_new * vn_rows) * (1.0 / den)).reshape(bt, nh, LANES)
    kn_t, vn_t = kn_ref[...].T, vn_ref[...].T
    newest = lax.broadcasted_iota(jnp.int32, (KV_WIDTH, WINDOW), 1) == WINDOW - 1
    for j in range(bt):
        nk_ref[j] = jnp.where(newest, kn_t[:, j:j + 1], pltpu.roll(ck_ref[j], WINDOW - 1, 1))
        nv_ref[j] = jnp.where(newest, vn_t[:, j:j + 1], pltpu.roll(cv_ref[j], WINDOW - 1, 1))

    for hd in range(N_HEADS_B):
        sl = slice(hd * LANES, (hd + 1) * LANES)
        kin_t = kin_ref[:, sl].T
        q_bf = qb_ref[:, sl].astype(BF)
        for j in range(bt):
            kcol = kin_t[:, j:j + 1]
            st = st_ref[j, hd]
            new = st - kcol * (st - ib_ref[j:j + 1, sl])
            nst_ref[j, hd] = new
            ob_ref[j:j + 1, sl] = _dot(q_bf, new.astype(BF))[j:j + 1, :]


def _dec_mix_call(sinks, qm, kn, vn, qb, kin, ib, cache_k, cache_v, state, bt):
    b = kn.shape[0]
    rows = lambda n: pl.BlockSpec((bt, n), lambda i, *_: (i, 0))
    cache_spec = pl.BlockSpec((bt, WINDOW, KV_WIDTH), lambda i, *_: (i, 0, 0))
    st_spec = pl.BlockSpec((bt, N_HEADS_B, HEAD_DIM_B, HEAD_DIM_B), lambda i, *_: (i, 0, 0, 0))
    qm_spec = pl.BlockSpec((bt, N_HEADS_A, LANES), lambda i, *_: (i, 0, 0))
    return pl.pallas_call(
        functools.partial(_dec_mix_kernel, bt=bt),
        grid_spec=pltpu.PrefetchScalarGridSpec(
            num_scalar_prefetch=1, grid=(b // bt,),
            in_specs=[qm_spec, rows(KV_WIDTH), rows(KV_WIDTH), rows(WIDTH_B), rows(WIDTH_B),
                      rows(WIDTH_B), cache_spec, cache_spec, st_spec],
            out_specs=[qm_spec, rows(WIDTH_B), cache_spec, cache_spec, st_spec]),
        out_shape=[jax.ShapeDtypeStruct((b, N_HEADS_A, LANES), F32),
                   jax.ShapeDtypeStruct((b, WIDTH_B), F32),
                   jax.ShapeDtypeStruct(cache_k.shape, F32),
                   jax.ShapeDtypeStruct(cache_v.shape, F32),
                   jax.ShapeDtypeStruct(state.shape, F32)],
        compiler_params=pltpu.CompilerParams(
            dimension_semantics=("arbitrary",), vmem_limit_bytes=VMEM_LIMIT),
        name="decode_mix",
    )(sinks, qm, kn, vn, qb, kin, ib, cache_k, cache_v, state)


def _dec_out_kernel(x_ref, mod_ref, att_ref, za_ref, ob_ref, zb_ref, g_ref, gains_ref,
                    permt_ref, w_pa_ref, w_pb_ref, w_out_ref,
                    y_ref, w_pa_bf_ref, w_pb_bf_ref, w_out_bf_ref):
    w_pa, w_pb, w_out = (r[...].astype(BF) for r in (w_pa_ref, w_pb_ref, w_out_ref))
    w_pa_bf_ref[...] = w_pa
    w_pb_bf_ref[...] = w_pb
    w_out_bf_ref[...] = w_out
    att = sum(_split_dot(att_ref[:, hd, :], permt_ref[hd * LANES:(hd + 1) * LANES, :])
              for hd in range(N_HEADS_A))
    ya = att * _silu(za_ref[...])
    yb = _branch_b_out(ob_ref[...], zb_ref[...], gains_ref[:, G_O:G_END])
    g = g_ref[...]
    y = (g[:, :D_MODEL] * _dot(ya.astype(BF), w_pa) + g[:, D_MODEL:] * _dot(yb.astype(BF), w_pb))
    y_ref[:, 0, :] = x_ref[:, 0, :] + _modulation(mod_ref)[2] * _dot(y.astype(BF), w_out)


def _dec_out_call(x, mod, att, za, ob, zb, g, consts, w):
    bf = lambda a: jax.ShapeDtypeStruct(a.shape, BF)
    ws = (w["w_proj_a_f32"], w["w_proj_b_f32"], w["w_out_f32"])
    return pl.pallas_call(
        _dec_out_kernel,
        out_shape=[jax.ShapeDtypeStruct(x.shape, F32)] + [bf(a) for a in ws],
        compiler_params=pltpu.CompilerParams(vmem_limit_bytes=VMEM_LIMIT),
        name="decode_out",
    )(x, mod, att, za, ob, zb, g, w["gains"], consts["permt"], *ws)


def _rope_consts(n_tiles, tile, past_len):
    half = ROT_DIM // 2
    inv = ROPE_THETA ** (-np.arange(0, ROT_DIM, 2, dtype=np.float64) / ROT_DIM)
    e = np.arange(LANES) % HEAD_DIM_A
    rot = e < ROT_DIM

    def tables(pos):
        ang = np.asarray(pos, np.float64)[:, None] * inv[e % half][None, :]
        return np.where(rot, np.cos(ang), 1.0), np.where(rot, np.sin(ang), 0.0)

    sgn = np.stack([np.where(e < half, -1.0, 0.0),
                    np.where(rot & (e >= half), 1.0, 0.0)])
    cb, sb = tables(np.arange(n_tiles) * tile)
    cl, sl = tables(np.arange(tile))
    c1, s1 = tables([past_len])
    f = lambda a: jnp.asarray(a, F32)
    return {"rope_cb": f(cb), "rope_sb": f(sb), "rope_cl": f(cl), "rope_sl": f(sl),
            "rope_sgn": f(sgn), "rope_c1": f(c1), "rope_u1": f(s1 * sgn[0:1]),
            "rope_d1": f(s1 * sgn[1:2])}


def _static_consts():
    seg = np.kron(np.eye(N_HEADS_A), np.full((HEAD_DIM_A, HEAD_DIM_A), 1.0 / HEAD_DIM_A))
    tri = np.tril(np.ones((BLK, BLK)))
    r = np.arange(4 * BLK)[:, None] % BLK
    c = np.arange(2 * BLK)[None, :]
    ok_prev = (c < BLK) & (c > r)
    ok_cur = (c >= BLK) & (c - BLK <= r)
    bias = np.stack([np.where(ok_cur, 0.0, NEG), np.where(ok_prev | ok_cur, 0.0, NEG)])
    i = np.arange(BLK)[:, None]
    j = np.arange(BLK)[None, :]
    lmask = np.stack([((i // (2 * b)) == (j // (2 * b))) & ((i & b) != 0) & ((j & b) == 0)
                      for b in LEVELS]).astype(np.float32)
    perm = np.zeros((WIDTH_A, N_HEADS_A * LANES), np.float32)
    for hd in range(N_HEADS_A):
        kvh = hd // (N_HEADS_A // N_KV_A)
        for d in range(HEAD_DIM_A):
            perm[hd * HEAD_DIM_A + d, hd * LANES + kvh * HEAD_DIM_A + d] = 1.0
    return {
        "seg": jnp.asarray(seg, BF), "tri": jnp.asarray(tri, BF),
        "bias": jnp.asarray(bias, F32), "lmask": jnp.asarray(lmask, F32),
        "perm": jnp.asarray(perm, BF), "permt": jnp.asarray(perm.T, BF),
    }


def kernel(x_prompt, x_sample, cache_win_k, cache_win_v, state_hgrn, c_prompt, c_sample,
           w_ada, b_ada, norm_g, w_in, q_norm_g, k_norm_g, sinks, lb_logits, o_norm_g,
           w_merge, b_merge, w_proj_a, w_proj_b, w_out):
    depth = w_in.shape[0]
    assert depth == 1 and x_prompt.shape[0] == 1 and x_sample.shape[1] == 1
    t = x_prompt.shape[1]
    nb = x_sample.shape[0]
    past_len = t
    tile = PROMPT_TILE
    bt = DECODE_BATCH_TILE

    consts = _static_consts()
    consts.update(_rope_consts(t // tile, tile, past_len))

    w = {
        "sinks": sinks[0], "norm_g": norm_g,
        "w_in_f32": w_in[0], "w_merge_f32": w_merge[0], "b_merge": b_merge,
        "w_proj_a_f32": w_proj_a[0], "w_proj_b_f32": w_proj_b[0], "w_out_f32": w_out[0],
        "lb_logits": lb_logits,
    }

    mod_p, mod_s, w["gains"] = _ada_call(c_prompt, c_sample, w_ada[0], b_ada,
                                         q_norm_g, k_norm_g, o_norm_g)

    xs = x_sample
    (w["w_in"], w["w_merge"], g,
     qm, kn, vn, za, qb, kin, ib, zb) = _dec_in_call(xs, mod_s, consts, w)
    to_t = lambda c: jnp.transpose(c[0], (0, 2, 3, 1)).reshape(nb, KV_WIDTH, WINDOW)
    from_t = lambda c: jnp.transpose(
        c.reshape(-1, N_KV_A, HEAD_DIM_A, WINDOW), (0, 3, 1, 2))[None]
    att, ob, nk, nv, nst = _dec_mix_call(
        w["sinks"], qm, kn, vn, qb, kin, ib,
        to_t(cache_win_k), to_t(cache_win_v), state_hgrn[0], bt)
    y_s, w["w_proj_a"], w["w_proj_b"], w["w_out"] = _dec_out_call(
        xs, mod_s, att, za, ob, zb, g, consts, w)

    run = functools.partial(_prompt_call, x_prompt[0], mod_p, consts, w, tile,
                            PROMPT_TILES_PER_STEP)
    *fast, span = run(robust=False)
    y_p, kwin, vwin, st_p = lax.cond(jnp.max(span) < HGRN_FAST_MAX,
                                     lambda: tuple(fast), lambda: tuple(run(robust=True)[:4]))

    return (y_p[None], y_s,
            from_t(kwin[None]), from_t(vwin[None]), st_p[None, None],
            from_t(nk), from_t(nv), nst[None])
```

```python
import functools

import numpy as np
import jax
import jax.numpy as jnp
from jax import lax
from jax.experimental import pallas as pl
from jax.experimental.pallas import tpu as pltpu

D_MODEL = 1024
HEAD_DIM_A = 64
N_HEADS_A = 8
N_KV_A = 2
WIDTH_A = N_HEADS_A * HEAD_DIM_A
KV_WIDTH = N_KV_A * HEAD_DIM_A
WINDOW = 128
ROT_DIM = HEAD_DIM_A // 4
ROPE_THETA = 500000.0
HEAD_DIM_B = 128
N_HEADS_B = 4
WIDTH_B = N_HEADS_B * HEAD_DIM_B
EPS = 1e-6
OFF_QA = 0
OFF_KA = OFF_QA + WIDTH_A
OFF_VA = OFF_KA + KV_WIDTH
OFF_ZA = OFF_VA + KV_WIDTH
OFF_QB = OFF_ZA + WIDTH_A
OFF_FB = OFF_QB + WIDTH_B
OFF_IB = OFF_FB + WIDTH_B
OFF_ZB = OFF_IB + WIDTH_B
D_IN = OFF_ZB + WIDTH_B

LANES = 128
BLK = 128
SUB = 8
LEVELS = (64, 32, 16, 8)
HGRN_FAST_MAX = 80.0
MXU_N = 256
NEG = -1e30
VMEM_LIMIT = 56 * 1024 * 1024
PROMPT_TILE = 256
PROMPT_TILES_PER_STEP = 2
DECODE_BATCH_TILE = 16
G_Q, G_K, G_O = 0, WIDTH_A, WIDTH_A + KV_WIDTH
G_END = G_O + WIDTH_B

BF = jnp.bfloat16
F32 = jnp.float32


def _dot(a, b):
    return jnp.dot(a, b, preferred_element_type=F32)


def _dot_nt(a, b):
    return lax.dot_general(a, b, (((1,), (1,)), ((), ())), preferred_element_type=F32)


def _dot_tn(a, b):
    return lax.dot_general(a, b, (((0,), (0,)), ((), ())), preferred_element_type=F32)


def _split_dot(a_f32, b_bf):
    hi = a_f32.astype(BF)
    lo = (a_f32 - hi.astype(F32)).astype(BF)
    return _dot(hi, b_bf) + _dot(lo, b_bf)


def _sigmoid(x):
    return 1.0 / (1.0 + jnp.exp(-x))


def _silu(x):
    return x * _sigmoid(x)


def _lower_bound(lb_logits):
    l0 = lb_logits[0:1, :]
    l1 = lb_logits[1:2, :]
    m = jnp.maximum(l0, l1)
    e0 = jnp.exp(l0 - m)
    e1 = jnp.exp(l1 - m)
    return e0 / (e0 + e1)


def _rope_tables(cb, sb, cl, sl, sgn_up, sgn_dn):
    c = cb * cl - sb * sl
    s = sb * cl + cb * sl
    return c, s * sgn_up, s * sgn_dn


def _rope(x, c, s_up, s_dn):
    return x * c + pltpu.roll(x, LANES - ROT_DIM // 2, 1) * s_up + pltpu.roll(x, ROT_DIM // 2, 1) * s_dn


def _modulation(mod_ref):
    return (mod_ref[:, 0:D_MODEL], mod_ref[:, D_MODEL:2 * D_MODEL],
            mod_ref[:, 2 * D_MODEL:3 * D_MODEL])


def _norm_modulate(x, ng, mod_ref):
    shift, scale, _ = _modulation(mod_ref)
    ms = jnp.mean(x * x, axis=-1, keepdims=True)
    return (x * lax.rsqrt(ms + EPS) * (ng * (1.0 + scale)) + shift).astype(BF)


def _head_norm_scale(x, seg_mean_bf):
    sq = (x * x).astype(BF)
    w = min(x.shape[1], MXU_N)
    ms = jnp.concatenate([_dot(sq[:, c:c + w], seg_mean_bf[:w, :w])
                          for c in range(0, x.shape[1], w)], axis=1)
    return lax.rsqrt(ms + EPS)


def _ada_kernel(cp_ref, cs_ref, w_ref, b_ref, qg_ref, kg_ref, og_ref, op_ref, os_ref, gains_ref):
    gains_ref[...] = jnp.concatenate(
        [qg_ref[...]] * N_HEADS_A + [kg_ref[...]] * N_KV_A + [og_ref[...]] * N_HEADS_B, axis=1)
    w = w_ref[...]
    w_hi = w.astype(BF)
    w_lo = (w - w_hi.astype(F32)).astype(BF)
    b = b_ref[...]

    ns, n_p = cs_ref.shape[0], cp_ref.shape[0]
    c = jnp.concatenate([cs_ref[...], cp_ref[...],
                         jnp.zeros(((-n_p) % SUB, D_MODEL), F32)], axis=0)
    c_hi = c.astype(BF)
    c_lo = (c - c_hi.astype(F32)).astype(BF)
    out = _dot(c_hi, w_hi) + (_dot(c_hi, w_lo) + _dot(c_lo, w_hi)) + b
    os_ref[...] = out[:ns, :]
    op_ref[...] = out[ns:ns + n_p, :]


def _ada_call(c_p, c_s, w_ada, b_ada, q_g, k_g, o_g):
    mp, ms = c_p.shape[0], c_s.shape[0]
    n = w_ada.shape[1]
    tn = 512
    const = lambda a: pl.BlockSpec(a.shape, lambda j: (0,) * a.ndim)
    return pl.pallas_call(
        _ada_kernel,
        grid=(n // tn,),
        in_specs=[const(c_p), const(c_s),
                  pl.BlockSpec((D_MODEL, tn), lambda j: (0, j)),
                  pl.BlockSpec((1, tn), lambda j: (0, j)),
                  const(q_g), const(k_g), const(o_g)],
        out_specs=[pl.BlockSpec((mp, tn), lambda j: (0, j)),
                   pl.BlockSpec((ms, tn), lambda j: (0, j)),
                   pl.BlockSpec((1, G_END), lambda j: (0, 0))],
        out_shape=[jax.ShapeDtypeStruct((mp, n), F32), jax.ShapeDtypeStruct((ms, n), F32),
                   jax.ShapeDtypeStruct((1, G_END), F32)],
        name="ada",
    )(c_p, c_s, w_ada, b_ada, q_g, k_g, o_g)


STAGE_B_COST = 3840


def _round_robin(*gens):
    results = [None] * len(gens)
    live = list(range(len(gens)))
    while live:
        for n in list(live):
            try:
                yield next(gens[n])
            except StopIteration as stop:
                results[n] = stop.value
                live.remove(n)
    return results


def _attn_block(q_blk, kcat, kcat_sw, vcat, vcat_sw, bias, sink_a, sink_b):
    lane = lax.broadcasted_iota(jnp.int32, (BLK, LANES), 1)
    lo = lane < HEAD_DIM_A
    chunks = [q_blk[:, c * LANES:(c + 1) * LANES] for c in range(4)]
    zero = jnp.zeros((BLK, LANES), F32)
    q_lo = [jnp.where(lo, c, zero).astype(BF) for c in chunks]
    q_hi = [jnp.where(lo, zero, c).astype(BF) for c in chunks]
    qa = jnp.concatenate([q_lo[0], q_lo[1], q_hi[2], q_hi[3]], axis=0)
    qb = jnp.concatenate([q_hi[0], q_hi[1], q_lo[2], q_lo[3]], axis=0)

    def probs(qs, kc, sink):
        s = _dot_nt(qs, kc) + bias
        m = jnp.maximum(jnp.max(s, axis=-1, keepdims=True), sink)
        p = jnp.exp(s - m)
        den = jnp.sum(p, axis=-1, keepdims=True) + jnp.exp(sink - m)
        return p.astype(BF), 1.0 / den

    pa, ra = probs(qa, kcat, sink_a)
    yield 200
    pb, rb = probs(qb, kcat_sw, sink_b)
    yield 200
    oa = _dot(pa, vcat) * ra
    yield 40
    ob = _dot(pb, vcat_sw) * rb
    yield 40
    r = lambda o, i: o[i * BLK:(i + 1) * BLK, :]
    out = jnp.concatenate([
        jnp.where(lo, r(oa, 0), r(ob, 0)),
        jnp.where(lo, r(oa, 1), r(ob, 1)),
        jnp.where(lo, r(ob, 2), r(oa, 2)),
        jnp.where(lo, r(ob, 3), r(oa, 3)),
    ], axis=1)
    yield 30
    return out


def _hgrn_gates(fb, lb, one_m_lb, tri_bf):
    sig = _sigmoid(fb)
    kin = one_m_lb * (1.0 - sig)
    f = lb + one_m_lb * sig
    cum = _dot(tri_bf, jnp.log(f).astype(BF))
    return kin, f, cum


def _hgrn_span_decay(cum):
    q = BLK // 4
    ends = [cum[(n + 1) * q - 1:(n + 1) * q, :] for n in range(4)]
    d = -ends[0]
    for n in range(1, 4):
        d = jnp.maximum(d, ends[n - 1] - ends[n])
    return d


def _hgrn_state_step(qb, kin, cum, ib, st_ref):
    q_dec = (qb * jnp.exp(cum)).astype(BF)
    last = cum[BLK - 1:BLK, :]
    k_dec = (kin * jnp.exp(last - cum)).astype(BF)
    v_bf = ib.astype(BF)
    outs = []
    for h in range(N_HEADS_B):
        sl = slice(h * LANES, (h + 1) * LANES)
        st = st_ref[h]
        outs.append(_dot_nt(q_dec[:, sl], st.astype(BF)))
        st_ref[h] = st * jnp.exp(last[:, sl]) + _dot_tn(v_bf[:, sl], k_dec[:, sl])
    return jnp.concatenate(outs, axis=1)


def _hgrn_apply(amats, ib):
    v_bf = ib.astype(BF)
    return jnp.concatenate(
        [_dot(amats[h].astype(BF), v_bf[:, h * LANES:(h + 1) * LANES])
         for h in range(N_HEADS_B)], axis=1)


def _recur_block_fast(qb, kin, cum, ib, st_ref):
    base = _hgrn_state_step(qb, kin, cum, ib, st_ref)
    yield 130
    half = BLK // 2
    row = lax.broadcasted_iota(jnp.int32, (BLK, 1), 0)
    upper = row >= half
    piv = cum[half - 1:half, :]
    w_lvl = jnp.exp(jnp.concatenate([piv - cum[:half, :], cum[half:, :] - piv], axis=0))
    p_lvl = (jnp.where(upper, qb, kin) * w_lvl).astype(BF)
    mid = jnp.where(upper, cum[half + half // 2 - 1:half + half // 2, :],
                    cum[half // 2 - 1:half // 2, :])
    e_mid = cum - mid
    q_mid = (qb * jnp.exp(e_mid)).astype(BF)
    k_mid = (kin * jnp.exp(-e_mid)).astype(BF)
    yield 230

    ri = lax.broadcasted_iota(jnp.int32, (BLK, BLK), 0)
    ci = lax.broadcasted_iota(jnp.int32, (BLK, BLK), 1)
    same_half_causal = ((ri >= half) == (ci >= half)) & (ci <= ri)
    cross = (ri >= half) & (ci < half)
    amats = []
    for h in range(N_HEADS_B):
        sl = slice(h * LANES, (h + 1) * LANES)
        a_mid = _dot_nt(q_mid[:, sl], k_mid[:, sl])
        a_lvl = _dot_nt(p_lvl[:, sl], p_lvl[:, sl])
        amats.append(jnp.where(same_half_causal, a_mid, jnp.where(cross, a_lvl, 0.0)))
    yield 60
    full = base + _hgrn_apply(amats, ib)
    yield 40
    return full


def _hgrn_intra_robust(qb, kin, f, cum, ib, lvl_mask_ref):
    row = lax.broadcasted_iota(jnp.int32, (BLK, 1), 0)
    lvl_ops = []
    for b in LEVELS:
        pieces = []
        for r0 in range(0, BLK, 2 * b):
            piv = cum[r0 + b - 1:r0 + b, :]
            pieces.append(piv - cum[r0:r0 + b, :])
            pieces.append(cum[r0 + b:r0 + 2 * b, :] - piv)
        w = jnp.exp(jnp.concatenate(pieces, axis=0))
        second = (row & b) != 0
        lvl_ops.append((jnp.where(second, qb, kin) * w).astype(BF))

    n8 = BLK // SUB
    q3 = qb.reshape(n8, SUB, WIDTH_B)
    k3 = kin.reshape(n8, SUB, WIDTH_B)
    f3 = f.reshape(n8, SUB, WIDTH_B)
    v3 = ib.reshape(n8, SUB, WIDTH_B)
    subl = lax.broadcasted_iota(jnp.int32, (n8, SUB, 1), 1)

    def head(x, h):
        return x[..., h * LANES:(h + 1) * LANES]

    g = q3 * k3
    acc = [jnp.sum(head(g, h), axis=-1, keepdims=True) * head(v3, h) for h in range(N_HEADS_B)]
    dec = jnp.ones_like(f3)
    kd = k3
    vd = v3
    for d in range(1, SUB):
        dec = f3 * pltpu.roll(dec, 1, 1)
        kd = pltpu.roll(kd, 1, 1)
        vd = pltpu.roll(vd, 1, 1)
        g = q3 * kd * dec
        ok = subl >= d
        for h in range(N_HEADS_B):
            a = jnp.where(ok, jnp.sum(head(g, h), axis=-1, keepdims=True), 0.0)
            acc[h] = acc[h] + a * head(vd, h)

    amats = []
    for h in range(N_HEADS_B):
        sl = slice(h * LANES, (h + 1) * LANES)
        amat = jnp.zeros((BLK, BLK), F32)
        for li in range(len(LEVELS)):
            p = lvl_ops[li][:, sl]
            amat = amat + lvl_mask_ref[li] * _dot_nt(p, p)
        amats.append(amat)
    diag = jnp.concatenate([acc[h].reshape(BLK, LANES) for h in range(N_HEADS_B)], axis=1)
    return _hgrn_apply(amats, ib) + diag


def _branch_b_out(o, zb, og):
    outs = []
    for h in range(N_HEADS_B):
        sl = slice(h * LANES, (h + 1) * LANES)
        oh = o[:, sl]
        ms = jnp.mean(oh * oh, axis=-1, keepdims=True)
        outs.append(oh * lax.rsqrt(ms + EPS))
    return jnp.concatenate(outs, axis=1) * og * _silu(zb)


def _prompt_kernel(sinks_ref, *refs, tile, n_steps, tps, robust):
    x_refs = refs[:tps + 1]
    (mod_ref, ng_ref,
     w_in_ref, w_mg_ref, b_mg_ref, w_pa_ref, w_pb_ref, w_out_ref,
     gains_ref, lbl_ref,
     cb_ref, sb_ref, cl_ref, sl_ref, sgn_ref,
     seg_ref, tri_ref, bias_ref, lmask_ref,
     y_ref, kwin_ref, vwin_ref, state_ref, span_ref,
     st_ref, kprev_ref, kprev_sw_ref, vprev_ref, vprev_sw_ref,
     p0_ref, p1_ref, g_ref, h0_ref, h1_ref, ob_ref) = refs[tps + 1:]
    p_refs, h_refs = (p0_ref, p1_ref), (h0_ref, h1_ref)
    s = pl.program_id(0)
    nblk = tile // BLK

    def stage_a(x_ref, h_ref, p_ref):
        def prep():
            h_ref[...] = _norm_modulate(x_ref[...], ng_ref[...], mod_ref)

        def proj_chunk(c):
            def run():
                cs = slice(c * MXU_N, (c + 1) * MXU_N)
                p_ref[:, cs] = _dot(h_ref[...], w_in_ref[:, cs])
            return run

        return [prep] + [proj_chunk(c) for c in range(D_IN // MXU_N)]

    def gate_chunks(h_ref):
        def gate_chunk(c):
            def run():
                cs = slice(c * MXU_N, (c + 1) * MXU_N)
                g_ref[:, cs] = _sigmoid(_dot(h_ref[...], w_mg_ref[:, cs]) + b_mg_ref[:, cs])
            return run

        return [gate_chunk(c) for c in range(2 * D_MODEL // MXU_N)]

    def phase(h_cur_ref, a_next, b_parts):
        gc = gate_chunks(h_cur_ref)
        interleave(gc[:2] + a_next[:1] + gc[2:] + a_next[1:], b_parts, lead=10)

    def stage_b(x_ref, p_ref, t_idx, y_rows):
        blocks = [slice(blk * BLK, (blk + 1) * BLK) for blk in range(nblk)]

        def qkv_stream():
            seg = seg_ref[...]
            rc, ru, rd = _rope_tables(cb_ref[pl.ds(t_idx, 1), :], sb_ref[pl.ds(t_idx, 1), :],
                                      cl_ref[...], sl_ref[...], sgn_ref[0:1, :], sgn_ref[1:2, :])
            qa = p_ref[:, OFF_QA:OFF_QA + WIDTH_A]
            qa = qa * _head_norm_scale(qa, seg) * (gains_ref[:, G_Q:G_K] * (HEAD_DIM_A ** -0.5))
            yield 160
            qa = jnp.concatenate(
                [_rope(qa[:, c * LANES:(c + 1) * LANES], rc, ru, rd) for c in range(4)], axis=1)
            yield 160
            ka = p_ref[:, OFF_KA:OFF_KA + KV_WIDTH]
            ka = ka * _head_norm_scale(ka, seg[:KV_WIDTH, :KV_WIDTH]) * gains_ref[:, G_K:G_O]
            ka = _rope(ka, rc, ru, rd)
            va = p_ref[:, OFF_VA:OFF_VA + KV_WIDTH]
            ka_sw = pltpu.roll(ka, HEAD_DIM_A, 1)
            va_sw = pltpu.roll(va, HEAD_DIM_A, 1)
            rows4 = lax.broadcasted_iota(jnp.int32, (4 * BLK, 1), 0) // BLK

            def sink_col(heads):
                col = jnp.zeros((4 * BLK, 1), F32)
                for n, hd in enumerate(heads):
                    col = jnp.where(rows4 == n, sinks_ref[hd], col)
                return col
            kwin_ref[...] = ka[tile - WINDOW:, :]
            vwin_ref[...] = va[tile - WINDOW:, :]
            res = (qa, ka, va, ka_sw, va_sw, sink_col((0, 2, 5, 7)), sink_col((1, 3, 4, 6)))
            yield 120
            return res

        def gates_stream():
            lb = _lower_bound(lbl_ref[...])
            tri = tri_ref[...]
            gates = []
            half_w = WIDTH_B // 2
            for rs in blocks:
                halves = []
                for c0 in (0, half_w):
                    cs = slice(c0, c0 + half_w)
                    halves.append(_hgrn_gates(p_ref[rs, OFF_FB + c0:OFF_FB + c0 + half_w],
                                              lb[:, cs], 1.0 - lb[:, cs], tri))
                    yield 240
                gates.append(tuple(jnp.concatenate(pair, axis=1) for pair in zip(*halves)))
            span = span_ref[...]
            for g in gates:
                span = jnp.maximum(span, _hgrn_span_decay(g[2]))
            span_ref[...] = span
            yield 10
            return gates

        (qa, ka, va, ka_sw, va_sw, sink_a, sink_b), gates = (
            yield from _round_robin(qkv_stream(), gates_stream()))

        kv_refs = (kprev_ref, kprev_sw_ref, vprev_ref, vprev_sw_ref)
        kv_new = [a.astype(BF) for a in (ka, ka_sw, va, va_sw)]
        kv_old = [r[...] for r in kv_refs]
        for r, a in zip(kv_refs, kv_new):
            r[...] = a[blocks[-1]]

        def attend(blk, rs):
            cats = [jnp.concatenate([old if blk == 0 else new[blocks[blk - 1]], new[rs]], axis=0)
                    for old, new in zip(kv_old, kv_new)]
            bias = bias_ref[jnp.where(t_idx == 0, 0, 1)] if blk == 0 else bias_ref[1]
            return (yield from _attn_block(qa[rs], *cats, bias, sink_a, sink_b))

        def recur(blk, rs):
            kin, f, cum = gates[blk]
            qb, ib = p_ref[rs, OFF_QB:OFF_QB + WIDTH_B], p_ref[rs, OFF_IB:OFF_IB + WIDTH_B]
            if robust:
                base = _hgrn_state_step(qb, kin, cum, ib, st_ref)
                yield 130
                full = base + _hgrn_intra_robust(qb, kin, f, cum, ib, lmask_ref)
                yield 230 + 60 + 40
            else:
                full = yield from _recur_block_fast(qb, kin, cum, ib, st_ref)
            ob_ref[rs, :] = full

        ya_parts = []
        for blk, rs in enumerate(blocks):
            res = yield from _round_robin(attend(blk, rs), recur(blk, rs))
            ya_parts.append(res[0])

        ya_bf = (jnp.concatenate(ya_parts, axis=0)
                 * _silu(p_ref[:, OFF_ZA:OFF_ZA + WIDTH_A])).astype(BF)
        yield 160
        yb_bf = _branch_b_out(ob_ref[...], p_ref[:, OFF_ZB:OFF_ZB + WIDTH_B],
                              gains_ref[:, G_O:G_END]).astype(BF)
        yield 330

        y = (g_ref[:, :D_MODEL] * _dot(ya_bf, w_pa_ref[...])
             + g_ref[:, D_MODEL:] * _dot(yb_bf, w_pb_ref[...]))
        y_ref[y_rows, :] = x_ref[...] + _modulation(mod_ref)[2] * _dot(y.astype(BF), w_out_ref[...])
        yield 0

    def interleave(a_thunks, b_parts, lead):
        for th in a_thunks[:lead]:
            th()
        acc, done = 0, lead
        for cost in b_parts:
            acc += cost
            upto = lead + int(round((len(a_thunks) - lead) * min(acc, STAGE_B_COST) / STAGE_B_COST))
            for th in a_thunks[done:upto]:
                th()
            done = upto
        assert done == len(a_thunks) and acc == STAGE_B_COST, (done, acc)

    @pl.when(s == 0)
    def _():
        st_ref[...] = jnp.zeros_like(st_ref)
        span_ref[...] = jnp.zeros_like(span_ref)
        for r in (kprev_ref, kprev_sw_ref, vprev_ref, vprev_sw_ref):
            r[...] = jnp.zeros_like(r)
        for th in stage_a(x_refs[0], h0_ref, p0_ref):
            th()

    for k in range(tps):
        cur, nxt = k % 2, (k + 1) % 2
        phase(h_refs[cur], stage_a(x_refs[k + 1], h_refs[nxt], p_refs[nxt]),
              stage_b(x_refs[k], p_refs[cur], tps * s + k, slice(k * tile, (k + 1) * tile)))

    @pl.when(s == n_steps - 1)
    def _():
        for hd in range(N_HEADS_B):
            state_ref[hd] = st_ref[hd].T
        kwin_ref[...] = kwin_ref[...].T
        vwin_ref[...] = vwin_ref[...].T


def _const_spec(shape):
    nd = len(shape)
    return pl.BlockSpec(shape, lambda i, *_: (0,) * nd, pipeline_mode=pl.Buffered(1))


def _prompt_call(x, mod, consts, w, tile, tps, robust):
    t = x.shape[0]
    n_tiles = t // tile
    n_steps = n_tiles // tps
    assert n_steps * tps * tile == t and tps % 2 == 0
    row = lambda n: _const_spec((1, n))
    x_spec = lambda k: pl.BlockSpec(
        (tile, D_MODEL), lambda i, *_: (jnp.minimum(tps * i + k, n_tiles - 1), 0))
    in_specs = [x_spec(k) for k in range(tps + 1)] + [
        row(3 * D_MODEL), row(D_MODEL),
        _const_spec((D_MODEL, D_IN)), _const_spec((D_MODEL, 2 * D_MODEL)), row(2 * D_MODEL),
        _const_spec((WIDTH_A, D_MODEL)), _const_spec((WIDTH_B, D_MODEL)),
        _const_spec((D_MODEL, D_MODEL)),
        row(G_END), _const_spec((2, WIDTH_B)),
        _const_spec((n_tiles, LANES)), _const_spec((n_tiles, LANES)),
        _const_spec((tile, LANES)), _const_spec((tile, LANES)), _const_spec((2, LANES)),
        _const_spec((MXU_N, MXU_N)), _const_spec((BLK, BLK)),
        _const_spec((2, 4 * BLK, 2 * BLK)), _const_spec((len(LEVELS), BLK, BLK)),
    ]
    out_specs = [
        pl.BlockSpec((tps * tile, D_MODEL), lambda i, *_: (i, 0)),
        pl.BlockSpec((WINDOW, KV_WIDTH), lambda i, *_: (0, 0)),
        pl.BlockSpec((WINDOW, KV_WIDTH), lambda i, *_: (0, 0)),
        pl.BlockSpec((N_HEADS_B, HEAD_DIM_B, HEAD_DIM_B), lambda i, *_: (0, 0, 0)),
        pl.BlockSpec((1, WIDTH_B), lambda i, *_: (0, 0)),
    ]
    out_shape = [
        jax.ShapeDtypeStruct((t, D_MODEL), F32),
        jax.ShapeDtypeStruct((WINDOW, KV_WIDTH), F32),
        jax.ShapeDtypeStruct((WINDOW, KV_WIDTH), F32),
        jax.ShapeDtypeStruct((N_HEADS_B, HEAD_DIM_B, HEAD_DIM_B), F32),
        jax.ShapeDtypeStruct((1, WIDTH_B), F32),
    ]
    scratch = [
        pltpu.VMEM((N_HEADS_B, HEAD_DIM_B, HEAD_DIM_B), F32),
        pltpu.VMEM((BLK, KV_WIDTH), BF), pltpu.VMEM((BLK, KV_WIDTH), BF),
        pltpu.VMEM((BLK, KV_WIDTH), BF), pltpu.VMEM((BLK, KV_WIDTH), BF),
        pltpu.VMEM((tile, D_IN), F32), pltpu.VMEM((tile, D_IN), F32),
        pltpu.VMEM((tile, 2 * D_MODEL), F32),
        pltpu.VMEM((tile, D_MODEL), BF), pltpu.VMEM((tile, D_MODEL), BF),
        pltpu.VMEM((tile, WIDTH_B), F32),
    ]
    return pl.pallas_call(
        functools.partial(_prompt_kernel, tile=tile, n_steps=n_steps, tps=tps, robust=robust),
        grid_spec=pltpu.PrefetchScalarGridSpec(
            num_scalar_prefetch=1, grid=(n_steps,),
            in_specs=in_specs, out_specs=out_specs, scratch_shapes=scratch),
        out_shape=out_shape,
        compiler_params=pltpu.CompilerParams(
            dimension_semantics=("arbitrary",), vmem_limit_bytes=VMEM_LIMIT),
        name="prompt_layer_robust" if robust else "prompt_layer",
    )(w["sinks"], *([x] * (tps + 1)), mod, w["norm_g"],
      w["w_in"], w["w_merge"], w["b_merge"], w["w_proj_a"], w["w_proj_b"], w["w_out"],
      w["gains"], w["lb_logits"],
      consts["rope_cb"], consts["rope_sb"], consts["rope_cl"], consts["rope_sl"],
      consts["rope_sgn"],
      consts["seg"], consts["tri"], consts["bias"], consts["lmask"])


def _dec_in_kernel(x_ref, mod_ref, ng_ref, w_in_ref, w_mg_ref, b_mg_ref, gains_ref, lbl_ref,
                   rc_ref, ru_ref, rd_ref, seg_ref, perm_ref,
                   w_in_bf_ref, w_mg_bf_ref, g_ref,
                   qm_ref, kn_ref, vn_ref, za_ref, qb_ref, kin_ref, ib_ref, zb_ref,
                   h_ref, p_ref):
    j = pl.program_id(0)
    half_in = D_IN // 2

    @pl.when(j == 0)
    def _():
        h_ref[...] = _norm_modulate(x_ref[:, 0, :], ng_ref[...], mod_ref)

    w_in_bf = w_in_ref[...].astype(BF)
    w_mg_bf = w_mg_ref[...].astype(BF)
    w_in_bf_ref[...] = w_in_bf
    w_mg_bf_ref[...] = w_mg_bf
    h_bf = h_ref[...]
    g_ref[...] = _sigmoid(_dot(h_bf, w_mg_bf) + b_mg_ref[...])
    part = _dot(h_bf, w_in_bf)

    @pl.when(j == 0)
    def _():
        p_ref[:, :half_in] = part

    @pl.when(j == 1)
    def _():
        p_ref[:, half_in:] = part
        seg = seg_ref[...]
        rc, ru, rd = rc_ref[...], ru_ref[...], rd_ref[...]
        qa = p_ref[:, OFF_QA:OFF_QA + WIDTH_A]
        qa = qa * _head_norm_scale(qa, seg) * (gains_ref[:, G_Q:G_K] * (HEAD_DIM_A ** -0.5))
        qa = jnp.concatenate(
            [_rope(qa[:, c * LANES:(c + 1) * LANES], rc, ru, rd) for c in range(4)], axis=1)
        qm = _dot(qa.astype(BF), perm_ref[...])
        for hd in range(N_HEADS_A):
            qm_ref[:, hd, :] = qm[:, hd * LANES:(hd + 1) * LANES]
        ka = p_ref[:, OFF_KA:OFF_KA + KV_WIDTH]
        ka = ka * _head_norm_scale(ka, seg[:KV_WIDTH, :KV_WIDTH]) * gains_ref[:, G_K:G_O]
        kn_ref[...] = _rope(ka, rc, ru, rd)
        vn_ref[...] = p_ref[:, OFF_VA:OFF_VA + KV_WIDTH]
        za_ref[...] = p_ref[:, OFF_ZA:OFF_ZA + WIDTH_A]
        qb_ref[...] = p_ref[:, OFF_QB:OFF_QB + WIDTH_B]
        lb = _lower_bound(lbl_ref[...])
        kin_ref[...] = (1.0 - lb) * (1.0 - _sigmoid(p_ref[:, OFF_FB:OFF_FB + WIDTH_B]))
        ib_ref[...] = p_ref[:, OFF_IB:OFF_IB + WIDTH_B]
        zb_ref[...] = p_ref[:, OFF_ZB:OFF_ZB + WIDTH_B]


def _dec_in_call(x, mod, consts, w):
    b = x.shape[0]
    n_steps = 2
    half_in, half_mg = D_IN // n_steps, 2 * D_MODEL // n_steps
    assert half_in % LANES == 0
    const = lambda a: pl.BlockSpec(a.shape, lambda j: (0,) * a.ndim)
    cols = lambda rows, n: pl.BlockSpec((rows, n), lambda j: (0, j))
    row_out = lambda n: pl.BlockSpec((b, n), lambda j: (0, 0))
    f = lambda n: jax.ShapeDtypeStruct((b, n), F32)
    widths = [KV_WIDTH, KV_WIDTH, WIDTH_A, WIDTH_B, WIDTH_B, WIDTH_B, WIDTH_B]
    qm_shape = (b, N_HEADS_A, LANES)
    small = [w["norm_g"]]
    tail = [w["gains"], w["lb_logits"], consts["rope_c1"], consts["rope_u1"], consts["rope_d1"],
            consts["seg"], consts["perm"]]
    return pl.pallas_call(
        _dec_in_kernel,
        grid=(n_steps,),
        in_specs=[const(x), const(mod)] + [const(a) for a in small]
        + [cols(D_MODEL, half_in), cols(D_MODEL, half_mg), cols(1, half_mg)]
        + [const(a) for a in tail],
        out_specs=[cols(D_MODEL, half_in), cols(D_MODEL, half_mg), cols(b, half_mg),
                   pl.BlockSpec(qm_shape, lambda j: (0, 0, 0))]
        + [row_out(n) for n in widths],
        out_shape=[jax.ShapeDtypeStruct((D_MODEL, D_IN), BF),
                   jax.ShapeDtypeStruct((D_MODEL, 2 * D_MODEL), BF), f(2 * D_MODEL),
                   jax.ShapeDtypeStruct(qm_shape, F32)]
        + [f(n) for n in widths],
        scratch_shapes=[pltpu.VMEM((b, D_MODEL), BF), pltpu.VMEM((b, D_IN), F32)],
        compiler_params=pltpu.CompilerParams(
            dimension_semantics=("arbitrary",), vmem_limit_bytes=VMEM_LIMIT),
        name="decode_in",
    )(x, mod, *small, w["w_in_f32"], w["w_merge_f32"], w["b_merge"], *tail)


def _dec_mix_kernel(sinks_ref, qm_ref, kn_ref, vn_ref, qb_ref, kin_ref, ib_ref,
                    ck_ref, cv_ref, st_ref,
                    att_ref, ob_ref, nk_ref, nv_ref, nst_ref, *, bt):
    nh = N_HEADS_A
    s = jnp.concatenate([_dot(qm_ref[j].astype(BF), ck_ref[j].astype(BF))
                         for j in range(bt)], axis=0)
    key = lax.broadcasted_iota(jnp.int32, (bt * nh, WINDOW), 1)
    s = jnp.where(key == 0, NEG, s)
    rep = lambda r: jnp.broadcast_to(r[...][:, None, :], (bt, nh, r.shape[-1])).reshape(
        bt * nh, r.shape[-1])
    kn_rows, vn_rows = rep(kn_ref), rep(vn_ref)
    head = lax.broadcasted_iota(jnp.int32, (bt * nh, 1), 0) % nh
    sink = jnp.zeros((bt * nh, 1), F32)
    for hd in range(nh):
        sink = jnp.where(head == hd, sinks_ref[hd], sink)
    s_new = jnp.sum(qm_ref[...].reshape(bt * nh, LANES) * kn_rows, axis=-1, keepdims=True)
    m = jnp.maximum(jnp.maximum(jnp.max(s, axis=-1, keepdims=True), s_new), sink)
    p = jnp.exp(s - m)
    p_new = jnp.exp(s_new - m)
    den = jnp.sum(p, axis=-1, keepdims=True) + p_new + jnp.exp(sink - m)
    p_bf = p.astype(BF)
    pv = jnp.concatenate([_dot_nt(p_bf[j * nh:(j + 1) * nh, :], cv_ref[j].astype(BF))
                          for j in range(bt)], axis=0)
    att_ref[...] = ((pv + p_new * vn_rows) * (1.0 / den)).reshape(bt, nh, LANES)
    kn_t, vn_t = kn_ref[...].T, vn_ref[...].T
    newest = lax.broadcasted_iota(jnp.int32, (KV_WIDTH, WINDOW), 1) == WINDOW - 1
    for j in range(bt):
        nk_ref[j] = jnp.where(newest, kn_t[:, j:j + 1], pltpu.roll(ck_ref[j], WINDOW - 1, 1))
        nv_ref[j] = jnp.where(newest, vn_t[:, j:j + 1], pltpu.roll(cv_ref[j], WINDOW - 1, 1))

    for hd in range(N_HEADS_B):
        sl = slice(hd * LANES, (hd + 1) * LANES)
        kin_t = kin_ref[:, sl].T
        q_bf = qb_ref[:, sl].astype(BF)
        for j in range(bt):
            kcol = kin_t[:, j:j + 1]
            st = st_ref[j, hd]
            new = st - kcol * (st - ib_ref[j:j + 1, sl])
            nst_ref[j, hd] = new
            ob_ref[j:j + 1, sl] = _dot(q_bf, new.astype(BF))[j:j + 1, :]


def _dec_mix_call(sinks, qm, kn, vn, qb, kin, ib, cache_k, cache_v, state, bt):
    b = kn.shape[0]
    rows = lambda n: pl.BlockSpec((bt, n), lambda i, *_: (i, 0))
    cache_spec = pl.BlockSpec((bt, WINDOW, KV_WIDTH), lambda i, *_: (i, 0, 0))
    st_spec = pl.BlockSpec((bt, N_HEADS_B, HEAD_DIM_B, HEAD_DIM_B), lambda i, *_: (i, 0, 0, 0))
    qm_spec = pl.BlockSpec((bt, N_HEADS_A, LANES), lambda i, *_: (i, 0, 0))
    return pl.pallas_call(
        functools.partial(_dec_mix_kernel, bt=bt),
        grid_spec=pltpu.PrefetchScalarGridSpec(
            num_scalar_prefetch=1, grid=(b // bt,),
            in_specs=[qm_spec, rows(KV_WIDTH), rows(KV_WIDTH), rows(WIDTH_B), rows(WIDTH_B),
                      rows(WIDTH_B), cache_spec, cache_spec, st_spec],
            out_specs=[qm_spec, rows(WIDTH_B), cache_spec, cache_spec, st_spec]),
        out_shape=[jax.ShapeDtypeStruct((b, N_HEADS_A, LANES), F32),
                   jax.ShapeDtypeStruct((b, WIDTH_B), F32),
                   jax.ShapeDtypeStruct(cache_k.shape, F32),
                   jax.ShapeDtypeStruct(cache_v.shape, F32),
                   jax.ShapeDtypeStruct(state.shape, F32)],
        compiler_params=pltpu.CompilerParams(
            dimension_semantics=("arbitrary",), vmem_limit_bytes=VMEM_LIMIT),
        name="decode_mix",
    )(sinks, qm, kn, vn, qb, kin, ib, cache_k, cache_v, state)


def _dec_out_kernel(x_ref, mod_ref, att_ref, za_ref, ob_ref, zb_ref, g_ref, gains_ref,
                    permt_ref, w_pa_ref, w_pb_ref, w_out_ref,
                    y_ref, w_pa_bf_ref, w_pb_bf_ref, w_out_bf_ref):
    w_pa, w_pb, w_out = (r[...].astype(BF) for r in (w_pa_ref, w_pb_ref, w_out_ref))
    w_pa_bf_ref[...] = w_pa
    w_pb_bf_ref[...] = w_pb
    w_out_bf_ref[...] = w_out
    att = sum(_split_dot(att_ref[:, hd, :], permt_ref[hd * LANES:(hd + 1) * LANES, :])
              for hd in range(N_HEADS_A))
    ya = att * _silu(za_ref[...])
    yb = _branch_b_out(ob_ref[...], zb_ref[...], gains_ref[:, G_O:G_END])
    g = g_ref[...]
    y = (g[:, :D_MODEL] * _dot(ya.astype(BF), w_pa) + g[:, D_MODEL:] * _dot(yb.astype(BF), w_pb))
    y_ref[:, 0, :] = x_ref[:, 0, :] + _modulation(mod_ref)[2] * _dot(y.astype(BF), w_out)


def _dec_out_call(x, mod, att, za, ob, zb, g, consts, w):
    bf = lambda a: jax.ShapeDtypeStruct(a.shape, BF)
    ws = (w["w_proj_a_f32"], w["w_proj_b_f32"], w["w_out_f32"])
    return pl.pallas_call(
        _dec_out_kernel,
        out_shape=[jax.ShapeDtypeStruct(x.shape, F32)] + [bf(a) for a in ws],
        compiler_params=pltpu.CompilerParams(vmem_limit_bytes=VMEM_LIMIT),
        name="decode_out",
    )(x, mod, att, za, ob, zb, g, w["gains"], consts["permt"], *ws)


def _rope_consts(n_tiles, tile, past_len):
    half = ROT_DIM // 2
    inv = ROPE_THETA ** (-np.arange(0, ROT_DIM, 2, dtype=np.float64) / ROT_DIM)
    e = np.arange(LANES) % HEAD_DIM_A
    rot = e < ROT_DIM

    def tables(pos):
        ang = np.asarray(pos, np.float64)[:, None] * inv[e % half][None, :]
        return np.where(rot, np.cos(ang), 1.0), np.where(rot, np.sin(ang), 0.0)

    sgn = np.stack([np.where(e < half, -1.0, 0.0),
                    np.where(rot & (e >= half), 1.0, 0.0)])
    cb, sb = tables(np.arange(n_tiles) * tile)
    cl, sl = tables(np.arange(tile))
    c1, s1 = tables([past_len])
    f = lambda a: jnp.asarray(a, F32)
    return {"rope_cb": f(cb), "rope_sb": f(sb), "rope_cl": f(cl), "rope_sl": f(sl),
            "rope_sgn": f(sgn), "rope_c1": f(c1), "rope_u1": f(s1 * sgn[0:1]),
            "rope_d1": f(s1 * sgn[1:2])}


def _static_consts():
    seg = np.kron(np.eye(MXU_N // HEAD_DIM_A),
                  np.full((HEAD_DIM_A, HEAD_DIM_A), 1.0 / HEAD_DIM_A))
    tri = np.tril(np.ones((BLK, BLK)))
    r = np.arange(4 * BLK)[:, None] % BLK
    c = np.arange(2 * BLK)[None, :]
    ok_prev = (c < BLK) & (c > r)
    ok_cur = (c >= BLK) & (c - BLK <= r)
    bias = np.stack([np.where(ok_cur, 0.0, NEG), np.where(ok_prev | ok_cur, 0.0, NEG)])
    i = np.arange(BLK)[:, None]
    j = np.arange(BLK)[None, :]
    lmask = np.stack([((i // (2 * b)) == (j // (2 * b))) & ((i & b) != 0) & ((j & b) == 0)
                      for b in LEVELS]).astype(np.float32)
    perm = np.zeros((WIDTH_A, N_HEADS_A * LANES), np.float32)
    for hd in range(N_HEADS_A):
        kvh = hd // (N_HEADS_A // N_KV_A)
        for d in range(HEAD_DIM_A):
            perm[hd * HEAD_DIM_A + d, hd * LANES + kvh * HEAD_DIM_A + d] = 1.0
    return {
        "seg": jnp.asarray(seg, BF), "tri": jnp.asarray(tri, BF),
        "bias": jnp.asarray(bias, F32), "lmask": jnp.asarray(lmask, F32),
        "perm": jnp.asarray(perm, BF), "permt": jnp.asarray(perm.T, BF),
    }


def kernel(x_prompt, x_sample, cache_win_k, cache_win_v, state_hgrn, c_prompt, c_sample,
           w_ada, b_ada, norm_g, w_in, q_norm_g, k_norm_g, sinks, lb_logits, o_norm_g,
           w_merge, b_merge, w_proj_a, w_proj_b, w_out):
    depth = w_in.shape[0]
    assert depth == 1 and x_prompt.shape[0] == 1 and x_sample.shape[1] == 1
    t = x_prompt.shape[1]
    nb = x_sample.shape[0]
    past_len = t
    tile = PROMPT_TILE
    bt = DECODE_BATCH_TILE

    consts = _static_consts()
    consts.update(_rope_consts(t // tile, tile, past_len))

    w = {
        "sinks": sinks[0], "norm_g": norm_g,
        "w_in_f32": w_in[0], "w_merge_f32": w_merge[0], "b_merge": b_merge,
        "w_proj_a_f32": w_proj_a[0], "w_proj_b_f32": w_proj_b[0], "w_out_f32": w_out[0],
        "lb_logits": lb_logits,
    }

    mod_p, mod_s, w["gains"] = _ada_call(c_prompt, c_sample, w_ada[0], b_ada,
                                         q_norm_g, k_norm_g, o_norm_g)

    xs = x_sample
    (w["w_in"], w["w_merge"], g,
     qm, kn, vn, za, qb, kin, ib, zb) = _dec_in_call(xs, mod_s, consts, w)
    to_t = lambda c: jnp.transpose(c[0], (0, 2, 3, 1)).reshape(nb, KV_WIDTH, WINDOW)
    from_t = lambda c: jnp.transpose(
        c.reshape(-1, N_KV_A, HEAD_DIM_A, WINDOW), (0, 3, 1, 2))[None]
    att, ob, nk, nv, nst = _dec_mix_call(
        w["sinks"], qm, kn, vn, qb, kin, ib,
        to_t(cache_win_k), to_t(cache_win_v), state_hgrn[0], bt)
    y_s, w["w_proj_a"], w["w_proj_b"], w["w_out"] = _dec_out_call(
        xs, mod_s, att, za, ob, zb, g, consts, w)

    run = functools.partial(_prompt_call, x_prompt[0], mod_p, consts, w, tile,
                            PROMPT_TILES_PER_STEP)
    *fast, span = run(robust=False)
    y_p, kwin, vwin, st_p = lax.cond(jnp.max(span) < HGRN_FAST_MAX,
                                     lambda: tuple(fast), lambda: tuple(run(robust=True)[:4]))

    return (y_p[None], y_s,
            from_t(kwin[None]), from_t(vwin[None]), st_p[None, None],
            from_t(nk), from_t(nv), nst[None])
```

```python
import functools

import numpy as np
import jax
import jax.numpy as jnp
from jax import lax
from jax.experimental import pallas as pl
from jax.experimental.pallas import tpu as pltpu

D_MODEL = 1024
HEAD_DIM_A = 64
N_HEADS_A = 8
N_KV_A = 2
WIDTH_A = N_HEADS_A * HEAD_DIM_A
KV_WIDTH = N_KV_A * HEAD_DIM_A
WINDOW = 128
ROT_DIM = HEAD_DIM_A // 4
ROPE_THETA = 500000.0
HEAD_DIM_B = 128
N_HEADS_B = 4
WIDTH_B = N_HEADS_B * HEAD_DIM_B
EPS = 1e-6
OFF_QA = 0
OFF_KA = OFF_QA + WIDTH_A
OFF_VA = OFF_KA + KV_WIDTH
OFF_ZA = OFF_VA + KV_WIDTH
OFF_QB = OFF_ZA + WIDTH_A
OFF_FB = OFF_QB + WIDTH_B
OFF_IB = OFF_FB + WIDTH_B
OFF_ZB = OFF_IB + WIDTH_B
D_IN = OFF_ZB + WIDTH_B

LANES = 128
BLK = 128
SUB = 8
LEVELS = (64, 32, 16, 8)
HGRN_FAST_MAX = 80.0
MXU_N = 256
NEG = -1e30
VMEM_LIMIT = 56 * 1024 * 1024
PROMPT_TILE = 256
PROMPT_TILES_PER_STEP = 2
DECODE_BATCH_TILE = 8
G_Q, G_K, G_O = 0, WIDTH_A, WIDTH_A + KV_WIDTH
G_END = G_O + WIDTH_B

BF = jnp.bfloat16
F32 = jnp.float32


def _dot(a, b):
    return jnp.dot(a, b, preferred_element_type=F32)


def _dot_nt(a, b):
    return lax.dot_general(a, b, (((1,), (1,)), ((), ())), preferred_element_type=F32)


def _dot_tn(a, b):
    return lax.dot_general(a, b, (((0,), (0,)), ((), ())), preferred_element_type=F32)


def _split_dot(a_f32, b_bf):
    hi = a_f32.astype(BF)
    lo = (a_f32 - hi.astype(F32)).astype(BF)
    return _dot(hi, b_bf) + _dot(lo, b_bf)


def _sigmoid(x):
    return 1.0 / (1.0 + jnp.exp(-x))


def _silu(x):
    return x * _sigmoid(x)


def _lower_bound(lb_logits):
    l0 = lb_logits[0:1, :]
    l1 = lb_logits[1:2, :]
    m = jnp.maximum(l0, l1)
    e0 = jnp.exp(l0 - m)
    e1 = jnp.exp(l1 - m)
    return e0 / (e0 + e1)


def _rope_tables(cb, sb, cl, sl, sgn_up, sgn_dn):
    c = cb * cl - sb * sl
    s = sb * cl + cb * sl
    return c, s * sgn_up, s * sgn_dn


def _rope(x, c, s_up, s_dn):
    return x * c + pltpu.roll(x, LANES - ROT_DIM // 2, 1) * s_up + pltpu.roll(x, ROT_DIM // 2, 1) * s_dn


def _modulation(mod_ref):
    return (mod_ref[:, 0:D_MODEL], mod_ref[:, D_MODEL:2 * D_MODEL],
            mod_ref[:, 2 * D_MODEL:3 * D_MODEL])


def _norm_modulate(x, ng, mod_ref):
    shift, scale, _ = _modulation(mod_ref)
    ms = jnp.mean(x * x, axis=-1, keepdims=True)
    return (x * lax.rsqrt(ms + EPS) * (ng * (1.0 + scale)) + shift).astype(BF)


def _head_norm_scale(x, seg_mean_bf):
    sq = (x * x).astype(BF)
    w = min(x.shape[1], MXU_N)
    ms = jnp.concatenate([_dot(sq[:, c:c + w], seg_mean_bf[:w, :w])
                          for c in range(0, x.shape[1], w)], axis=1)
    return lax.rsqrt(ms + EPS)


def _ada_kernel(cp_ref, cs_ref, w_ref, b_ref, qg_ref, kg_ref, og_ref, op_ref, os_ref, gains_ref):
    gains_ref[...] = jnp.concatenate(
        [qg_ref[...]] * N_HEADS_A + [kg_ref[...]] * N_KV_A + [og_ref[...]] * N_HEADS_B, axis=1)
    w = w_ref[...]
    w_hi = w.astype(BF)
    w_lo = (w - w_hi.astype(F32)).astype(BF)
    b = b_ref[...]

    ns, n_p = cs_ref.shape[0], cp_ref.shape[0]
    c = jnp.concatenate([cs_ref[...], cp_ref[...],
                         jnp.zeros(((-n_p) % SUB, D_MODEL), F32)], axis=0)
    c_hi = c.astype(BF)
    c_lo = (c - c_hi.astype(F32)).astype(BF)
    out = _dot(c_hi, w_hi) + (_dot(c_hi, w_lo) + _dot(c_lo, w_hi)) + b
    os_ref[...] = out[:ns, :]
    op_ref[...] = out[ns:ns + n_p, :]


def _ada_call(c_p, c_s, w_ada, b_ada, q_g, k_g, o_g):
    mp, ms = c_p.shape[0], c_s.shape[0]
    n = w_ada.shape[1]
    tn = 512
    const = lambda a: pl.BlockSpec(a.shape, lambda j: (0,) * a.ndim)
    return pl.pallas_call(
        _ada_kernel,
        grid=(n // tn,),
        in_specs=[const(c_p), const(c_s),
                  pl.BlockSpec((D_MODEL, tn), lambda j: (0, j)),
                  pl.BlockSpec((1, tn), lambda j: (0, j)),
                  const(q_g), const(k_g), const(o_g)],
        out_specs=[pl.BlockSpec((mp, tn), lambda j: (0, j)),
                   pl.BlockSpec((ms, tn), lambda j: (0, j)),
                   pl.BlockSpec((1, G_END), lambda j: (0, 0))],
        out_shape=[jax.ShapeDtypeStruct((mp, n), F32), jax.ShapeDtypeStruct((ms, n), F32),
                   jax.ShapeDtypeStruct((1, G_END), F32)],
        name="ada",
    )(c_p, c_s, w_ada, b_ada, q_g, k_g, o_g)


STAGE_B_COST = 3840


def _round_robin(*gens):
    results = [None] * len(gens)
    live = list(range(len(gens)))
    while live:
        for n in list(live):
            try:
                yield next(gens[n])
            except StopIteration as stop:
                results[n] = stop.value
                live.remove(n)
    return results


def _attn_block(q_blk, kcat, kcat_sw, vcat, vcat_sw, bias, sink_a, sink_b):
    lane = lax.broadcasted_iota(jnp.int32, (BLK, LANES), 1)
    lo = lane < HEAD_DIM_A
    chunks = [q_blk[:, c * LANES:(c + 1) * LANES] for c in range(4)]
    zero = jnp.zeros((BLK, LANES), F32)
    q_lo = [jnp.where(lo, c, zero).astype(BF) for c in chunks]
    q_hi = [jnp.where(lo, zero, c).astype(BF) for c in chunks]
    qa = jnp.concatenate([q_lo[0], q_lo[1], q_hi[2], q_hi[3]], axis=0)
    qb = jnp.concatenate([q_hi[0], q_hi[1], q_lo[2], q_lo[3]], axis=0)

    def probs(qs, kc, sink):
        s = _dot_nt(qs, kc) + bias
        m = jnp.maximum(jnp.max(s, axis=-1, keepdims=True), sink)
        p = jnp.exp(s - m)
        den = jnp.sum(p, axis=-1, keepdims=True) + jnp.exp(sink - m)
        return p.astype(BF), 1.0 / den

    pa, ra = probs(qa, kcat, sink_a)
    yield 200
    pb, rb = probs(qb, kcat_sw, sink_b)
    yield 200
    oa = _dot(pa, vcat) * ra
    yield 40
    ob = _dot(pb, vcat_sw) * rb
    yield 40
    r = lambda o, i: o[i * BLK:(i + 1) * BLK, :]
    out = jnp.concatenate([
        jnp.where(lo, r(oa, 0), r(ob, 0)),
        jnp.where(lo, r(oa, 1), r(ob, 1)),
        jnp.where(lo, r(ob, 2), r(oa, 2)),
        jnp.where(lo, r(ob, 3), r(oa, 3)),
    ], axis=1)
    yield 30
    return out


def _hgrn_gates(fb, lb, one_m_lb, tri_bf):
    sig = _sigmoid(fb)
    kin = one_m_lb * (1.0 - sig)
    f = lb + one_m_lb * sig
    cum = _dot(tri_bf, jnp.log(f).astype(BF))
    return kin, f, cum


def _hgrn_span_decay(cum):
    q = BLK // 4
    ends = [cum[(n + 1) * q - 1:(n + 1) * q, :] for n in range(4)]
    d = -ends[0]
    for n in range(1, 4):
        d = jnp.maximum(d, ends[n - 1] - ends[n])
    return d


def _hgrn_state_step(qb, kin, cum, ib, st_ref):
    q_dec = (qb * jnp.exp(cum)).astype(BF)
    last = cum[BLK - 1:BLK, :]
    k_dec = (kin * jnp.exp(last - cum)).astype(BF)
    v_bf = ib.astype(BF)
    outs = []
    for h in range(N_HEADS_B):
        sl = slice(h * LANES, (h + 1) * LANES)
        st = st_ref[h]
        outs.append(_dot_nt(q_dec[:, sl], st.astype(BF)))
        st_ref[h] = st * jnp.exp(last[:, sl]) + _dot_tn(v_bf[:, sl], k_dec[:, sl])
    return jnp.concatenate(outs, axis=1)


def _hgrn_apply(amats, ib):
    v_bf = ib.astype(BF)
    return jnp.concatenate(
        [_dot(amats[h].astype(BF), v_bf[:, h * LANES:(h + 1) * LANES])
         for h in range(N_HEADS_B)], axis=1)


def _recur_block_fast(qb, kin, cum, ib, st_ref):
    base = _hgrn_state_step(qb, kin, cum, ib, st_ref)
    yield 130
    half = BLK // 2
    row = lax.broadcasted_iota(jnp.int32, (BLK, 1), 0)
    upper = row >= half
    piv = cum[half - 1:half, :]
    w_lvl = jnp.exp(jnp.concatenate([piv - cum[:half, :], cum[half:, :] - piv], axis=0))
    p_lvl = (jnp.where(upper, qb, kin) * w_lvl).astype(BF)
    mid = jnp.where(upper, cum[half + half // 2 - 1:half + half // 2, :],
                    cum[half // 2 - 1:half // 2, :])
    e_mid = cum - mid
    q_mid = (qb * jnp.exp(e_mid)).astype(BF)
    k_mid = (kin * jnp.exp(-e_mid)).astype(BF)
    yield 230

    ri = lax.broadcasted_iota(jnp.int32, (BLK, BLK), 0)
    ci = lax.broadcasted_iota(jnp.int32, (BLK, BLK), 1)
    same_half_causal = ((ri >= half) == (ci >= half)) & (ci <= ri)
    cross = (ri >= half) & (ci < half)
    amats = []
    for h in range(N_HEADS_B):
        sl = slice(h * LANES, (h + 1) * LANES)
        a_mid = _dot_nt(q_mid[:, sl], k_mid[:, sl])
        a_lvl = _dot_nt(p_lvl[:, sl], p_lvl[:, sl])
        amats.append(jnp.where(same_half_causal, a_mid, jnp.where(cross, a_lvl, 0.0)))
    yield 60
    full = base + _hgrn_apply(amats, ib)
    yield 40
    return full


def _hgrn_intra_robust(qb, kin, f, cum, ib, lvl_mask_ref):
    row = lax.broadcasted_iota(jnp.int32, (BLK, 1), 0)
    lvl_ops = []
    for b in LEVELS:
        pieces = []
        for r0 in range(0, BLK, 2 * b):
            piv = cum[r0 + b - 1:r0 + b, :]
            pieces.append(piv - cum[r0:r0 + b, :])
            pieces.append(cum[r0 + b:r0 + 2 * b, :] - piv)
        w = jnp.exp(jnp.concatenate(pieces, axis=0))
        second = (row & b) != 0
        lvl_ops.append((jnp.where(second, qb, kin) * w).astype(BF))

    n8 = BLK // SUB
    q3 = qb.reshape(n8, SUB, WIDTH_B)
    k3 = kin.reshape(n8, SUB, WIDTH_B)
    f3 = f.reshape(n8, SUB, WIDTH_B)
    v3 = ib.reshape(n8, SUB, WIDTH_B)
    subl = lax.broadcasted_iota(jnp.int32, (n8, SUB, 1), 1)

    def head(x, h):
        return x[..., h * LANES:(h + 1) * LANES]

    g = q3 * k3
    acc = [jnp.sum(head(g, h), axis=-1, keepdims=True) * head(v3, h) for h in range(N_HEADS_B)]
    dec = jnp.ones_like(f3)
    kd = k3
    vd = v3
    for d in range(1, SUB):
        dec = f3 * pltpu.roll(dec, 1, 1)
        kd = pltpu.roll(kd, 1, 1)
        vd = pltpu.roll(vd, 1, 1)
        g = q3 * kd * dec
        ok = subl >= d
        for h in range(N_HEADS_B):
            a = jnp.where(ok, jnp.sum(head(g, h), axis=-1, keepdims=True), 0.0)
            acc[h] = acc[h] + a * head(vd, h)

    amats = []
    for h in range(N_HEADS_B):
        sl = slice(h * LANES, (h + 1) * LANES)
        amat = jnp.zeros((BLK, BLK), F32)
        for li in range(len(LEVELS)):
            p = lvl_ops[li][:, sl]
            amat = amat + lvl_mask_ref[li] * _dot_nt(p, p)
        amats.append(amat)
    diag = jnp.concatenate([acc[h].reshape(BLK, LANES) for h in range(N_HEADS_B)], axis=1)
    return _hgrn_apply(amats, ib) + diag


def _branch_b_out(o, zb, og):
    outs = []
    for h in range(N_HEADS_B):
        sl = slice(h * LANES, (h + 1) * LANES)
        oh = o[:, sl]
        ms = jnp.mean(oh * oh, axis=-1, keepdims=True)
        outs.append(oh * lax.rsqrt(ms + EPS))
    return jnp.concatenate(outs, axis=1) * og * _silu(zb)


def _prompt_kernel(sinks_ref, *refs, tile, n_steps, tps, robust):
    x_refs = refs[:tps + 1]
    (mod_ref, ng_ref,
     w_in_ref, w_mg_ref, b_mg_ref, w_pa_ref, w_pb_ref, w_out_ref,
     gains_ref, lbl_ref,
     cb_ref, sb_ref, cl_ref, sl_ref, sgn_ref,
     seg_ref, tri_ref, bias_ref, lmask_ref,
     y_ref, kwin_ref, vwin_ref, state_ref, span_ref,
     st_ref, kprev_ref, kprev_sw_ref, vprev_ref, vprev_sw_ref,
     p0_ref, p1_ref, g_ref, h0_ref, h1_ref, ob_ref) = refs[tps + 1:]
    p_refs, h_refs = (p0_ref, p1_ref), (h0_ref, h1_ref)
    s = pl.program_id(0)
    nblk = tile // BLK

    def stage_a(x_ref, h_ref, p_ref):
        def prep():
            h_ref[...] = _norm_modulate(x_ref[...], ng_ref[...], mod_ref)

        def proj_chunk(c):
            def run():
                cs = slice(c * MXU_N, (c + 1) * MXU_N)
                p_ref[:, cs] = _dot(h_ref[...], w_in_ref[:, cs])
            return run

        return [prep] + [proj_chunk(c) for c in range(D_IN // MXU_N)]

    def gate_chunks(h_ref):
        def gate_chunk(c):
            def run():
                cs = slice(c * MXU_N, (c + 1) * MXU_N)
                g_ref[:, cs] = _sigmoid(_dot(h_ref[...], w_mg_ref[:, cs]) + b_mg_ref[:, cs])
            return run

        return [gate_chunk(c) for c in range(2 * D_MODEL // MXU_N)]

    def phase(h_cur_ref, a_next, b_parts):
        gc = gate_chunks(h_cur_ref)
        interleave(gc[:2] + a_next[:1] + gc[2:] + a_next[1:], b_parts, lead=10)

    def stage_b(x_ref, p_ref, t_idx, y_rows):
        blocks = [slice(blk * BLK, (blk + 1) * BLK) for blk in range(nblk)]

        def qkv_stream():
            seg = seg_ref[...]
            rc, ru, rd = _rope_tables(cb_ref[pl.ds(t_idx, 1), :], sb_ref[pl.ds(t_idx, 1), :],
                                      cl_ref[...], sl_ref[...], sgn_ref[0:1, :], sgn_ref[1:2, :])
            qa = p_ref[:, OFF_QA:OFF_QA + WIDTH_A]
            qa = qa * _head_norm_scale(qa, seg) * (gains_ref[:, G_Q:G_K] * (HEAD_DIM_A ** -0.5))
            yield 160
            qa = jnp.concatenate(
                [_rope(qa[:, c * LANES:(c + 1) * LANES], rc, ru, rd) for c in range(4)], axis=1)
            yield 160
            ka = p_ref[:, OFF_KA:OFF_KA + KV_WIDTH]
            ka = ka * _head_norm_scale(ka, seg[:KV_WIDTH, :KV_WIDTH]) * gains_ref[:, G_K:G_O]
            ka = _rope(ka, rc, ru, rd)
            va = p_ref[:, OFF_VA:OFF_VA + KV_WIDTH]
            ka_sw = pltpu.roll(ka, HEAD_DIM_A, 1)
            va_sw = pltpu.roll(va, HEAD_DIM_A, 1)
            rows4 = lax.broadcasted_iota(jnp.int32, (4 * BLK, 1), 0) // BLK

            def sink_col(heads):
                col = jnp.zeros((4 * BLK, 1), F32)
                for n, hd in enumerate(heads):
                    col = jnp.where(rows4 == n, sinks_ref[hd], col)
                return col
            kwin_ref[...] = ka[tile - WINDOW:, :]
            vwin_ref[...] = va[tile - WINDOW:, :]
            res = (qa, ka, va, ka_sw, va_sw, sink_col((0, 2, 5, 7)), sink_col((1, 3, 4, 6)))
            yield 120
            return res

        def gates_stream():
            lb = _lower_bound(lbl_ref[...])
            tri = tri_ref[...]
            gates = []
            half_w = WIDTH_B // 2
            for rs in blocks:
                halves = []
                for c0 in (0, half_w):
                    cs = slice(c0, c0 + half_w)
                    halves.append(_hgrn_gates(p_ref[rs, OFF_FB + c0:OFF_FB + c0 + half_w],
                                              lb[:, cs], 1.0 - lb[:, cs], tri))
                    yield 240
                gates.append(tuple(jnp.concatenate(pair, axis=1) for pair in zip(*halves)))
            span = span_ref[...]
            for g in gates:
                span = jnp.maximum(span, _hgrn_span_decay(g[2]))
            span_ref[...] = span
            yield 10
            return gates

        (qa, ka, va, ka_sw, va_sw, sink_a, sink_b), gates = (
            yield from _round_robin(qkv_stream(), gates_stream()))

        kv_refs = (kprev_ref, kprev_sw_ref, vprev_ref, vprev_sw_ref)
        kv_new = [a.astype(BF) for a in (ka, ka_sw, va, va_sw)]
        kv_old = [r[...] for r in kv_refs]
        for r, a in zip(kv_refs, kv_new):
            r[...] = a[blocks[-1]]

        def attend(blk, rs):
            cats = [jnp.concatenate([old if blk == 0 else new[blocks[blk - 1]], new[rs]], axis=0)
                    for old, new in zip(kv_old, kv_new)]
            bias = bias_ref[jnp.where(t_idx == 0, 0, 1)] if blk == 0 else bias_ref[1]
            return (yield from _attn_block(qa[rs], *cats, bias, sink_a, sink_b))

        def recur(blk, rs):
            kin, f, cum = gates[blk]
            qb, ib = p_ref[rs, OFF_QB:OFF_QB + WIDTH_B], p_ref[rs, OFF_IB:OFF_IB + WIDTH_B]
            if robust:
                base = _hgrn_state_step(qb, kin, cum, ib, st_ref)
                yield 130
                full = base + _hgrn_intra_robust(qb, kin, f, cum, ib, lmask_ref)
                yield 230 + 60 + 40
            else:
                full = yield from _recur_block_fast(qb, kin, cum, ib, st_ref)
            ob_ref[rs, :] = full

        ya_parts = []
        for blk, rs in enumerate(blocks):
            res = yield from _round_robin(attend(blk, rs), recur(blk, rs))
            ya_parts.append(res[0])

        ya_bf = (jnp.concatenate(ya_parts, axis=0)
                 * _silu(p_ref[:, OFF_ZA:OFF_ZA + WIDTH_A])).astype(BF)
        yield 160
        yb_bf = _branch_b_out(ob_ref[...], p_ref[:, OFF_ZB:OFF_ZB + WIDTH_B],
                              gains_ref[:, G_O:G_END]).astype(BF)
        yield 330

        y = (g_ref[:, :D_MODEL] * _dot(ya_bf, w_pa_ref[...])
             + g_ref[:, D_MODEL:] * _dot(yb_bf, w_pb_ref[...]))
        y_ref[y_rows, :] = x_ref[...] + _modulation(mod_ref)[2] * _dot(y.astype(BF), w_out_ref[...])
        yield 0

    def interleave(a_thunks, b_parts, lead):
        for th in a_thunks[:lead]:
            th()
        acc, done = 0, lead
        for cost in b_parts:
            acc += cost
            upto = lead + int(round((len(a_thunks) - lead) * min(acc, STAGE_B_COST) / STAGE_B_COST))
            for th in a_thunks[done:upto]:
                th()
            done = upto
        assert done == len(a_thunks) and acc == STAGE_B_COST, (done, acc)

    @pl.when(s == 0)
    def _():
        st_ref[...] = jnp.zeros_like(st_ref)
        span_ref[...] = jnp.zeros_like(span_ref)
        for r in (kprev_ref, kprev_sw_ref, vprev_ref, vprev_sw_ref):
            r[...] = jnp.zeros_like(r)
        for th in stage_a(x_refs[0], h0_ref, p0_ref):
            th()

    for k in range(tps):
        cur, nxt = k % 2, (k + 1) % 2
        phase(h_refs[cur], stage_a(x_refs[k + 1], h_refs[nxt], p_refs[nxt]),
              stage_b(x_refs[k], p_refs[cur], tps * s + k, slice(k * tile, (k + 1) * tile)))

    @pl.when(s == n_steps - 1)
    def _():
        for hd in range(N_HEADS_B):
            state_ref[hd] = st_ref[hd].T
        kwin_ref[...] = kwin_ref[...].T
        vwin_ref[...] = vwin_ref[...].T


def _const_spec(shape):
    nd = len(shape)
    return pl.BlockSpec(shape, lambda i, *_: (0,) * nd, pipeline_mode=pl.Buffered(1))


def _prompt_call(x, mod, consts, w, tile, tps, robust):
    t = x.shape[0]
    n_tiles = t // tile
    n_steps = n_tiles // tps
    assert n_steps * tps * tile == t and tps % 2 == 0
    row = lambda n: _const_spec((1, n))
    x_spec = lambda k: pl.BlockSpec(
        (tile, D_MODEL), lambda i, *_: (jnp.minimum(tps * i + k, n_tiles - 1), 0))
    in_specs = [x_spec(k) for k in range(tps + 1)] + [
        row(3 * D_MODEL), row(D_MODEL),
        _const_spec((D_MODEL, D_IN)), _const_spec((D_MODEL, 2 * D_MODEL)), row(2 * D_MODEL),
        _const_spec((WIDTH_A, D_MODEL)), _const_spec((WIDTH_B, D_MODEL)),
        _const_spec((D_MODEL, D_MODEL)),
        row(G_END), _const_spec((2, WIDTH_B)),
        _const_spec((n_tiles, LANES)), _const_spec((n_tiles, LANES)),
        _const_spec((tile, LANES)), _const_spec((tile, LANES)), _const_spec((2, LANES)),
        _const_spec((MXU_N, MXU_N)), _const_spec((BLK, BLK)),
        _const_spec((2, 4 * BLK, 2 * BLK)), _const_spec((len(LEVELS), BLK, BLK)),
    ]
    out_specs = [
        pl.BlockSpec((tps * tile, D_MODEL), lambda i, *_: (i, 0)),
        pl.BlockSpec((WINDOW, KV_WIDTH), lambda i, *_: (0, 0)),
        pl.BlockSpec((WINDOW, KV_WIDTH), lambda i, *_: (0, 0)),
        pl.BlockSpec((N_HEADS_B, HEAD_DIM_B, HEAD_DIM_B), lambda i, *_: (0, 0, 0)),
        pl.BlockSpec((1, WIDTH_B), lambda i, *_: (0, 0)),
    ]
    out_shape = [
        jax.ShapeDtypeStruct((t, D_MODEL), F32),
        jax.ShapeDtypeStruct((WINDOW, KV_WIDTH), F32),
        jax.ShapeDtypeStruct((WINDOW, KV_WIDTH), F32),
        jax.ShapeDtypeStruct((N_HEADS_B, HEAD_DIM_B, HEAD_DIM_B), F32),
        jax.ShapeDtypeStruct((1, WIDTH_B), F32),
    ]
    scratch = [
        pltpu.VMEM((N_HEADS_B, HEAD_DIM_B, HEAD_DIM_B), F32),
        pltpu.VMEM((BLK, KV_WIDTH), BF), pltpu.VMEM((BLK, KV_WIDTH), BF),
        pltpu.VMEM((BLK, KV_WIDTH), BF), pltpu.VMEM((BLK, KV_WIDTH), BF),
        pltpu.VMEM((tile, D_IN), F32), pltpu.VMEM((tile, D_IN), F32),
        pltpu.VMEM((tile, 2 * D_MODEL), F32),
        pltpu.VMEM((tile, D_MODEL), BF), pltpu.VMEM((tile, D_MODEL), BF),
        pltpu.VMEM((tile, WIDTH_B), F32),
    ]
    return pl.pallas_call(
        functools.partial(_prompt_kernel, tile=tile, n_steps=n_steps, tps=tps, robust=robust),
        grid_spec=pltpu.PrefetchScalarGridSpec(
            num_scalar_prefetch=1, grid=(n_steps,),
            in_specs=in_specs, out_specs=out_specs, scratch_shapes=scratch),
        out_shape=out_shape,
        compiler_params=pltpu.CompilerParams(
            dimension_semantics=("arbitrary",), vmem_limit_bytes=VMEM_LIMIT),
        name="prompt_layer_robust" if robust else "prompt_layer",
    )(w["sinks"], *([x] * (tps + 1)), mod, w["norm_g"],
      w["w_in"], w["w_merge"], w["b_merge"], w["w_proj_a"], w["w_proj_b"], w["w_out"],
      w["gains"], w["lb_logits"],
      consts["rope_cb"], consts["rope_sb"], consts["rope_cl"], consts["rope_sl"],
      consts["rope_sgn"],
      consts["seg"], consts["tri"], consts["bias"], consts["lmask"])


def _dec_in_kernel(x_ref, mod_ref, ng_ref, w_in_ref, w_mg_ref, b_mg_ref, gains_ref, lbl_ref,
                   rc_ref, ru_ref, rd_ref, seg_ref, perm_ref,
                   w_in_bf_ref, w_mg_bf_ref, g_ref,
                   qm_ref, kn_ref, vn_ref, za_ref, qb_ref, kin_ref, ib_ref, zb_ref,
                   h_ref, p_ref):
    j = pl.program_id(0)
    half_in = D_IN // 2

    @pl.when(j == 0)
    def _():
        h_ref[...] = _norm_modulate(x_ref[:, 0, :], ng_ref[...], mod_ref)

    w_in_bf = w_in_ref[...].astype(BF)
    w_mg_bf = w_mg_ref[...].astype(BF)
    w_in_bf_ref[...] = w_in_bf
    w_mg_bf_ref[...] = w_mg_bf
    h_bf = h_ref[...]
    g_ref[...] = _sigmoid(_dot(h_bf, w_mg_bf) + b_mg_ref[...])
    part = _dot(h_bf, w_in_bf)

    @pl.when(j == 0)
    def _():
        p_ref[:, :half_in] = part

    @pl.when(j == 1)
    def _():
        p_ref[:, half_in:] = part
        seg = seg_ref[...]
        rc, ru, rd = rc_ref[...], ru_ref[...], rd_ref[...]
        qa = p_ref[:, OFF_QA:OFF_QA + WIDTH_A]
        qa = qa * _head_norm_scale(qa, seg) * (gains_ref[:, G_Q:G_K] * (HEAD_DIM_A ** -0.5))
        qa = jnp.concatenate(
            [_rope(qa[:, c * LANES:(c + 1) * LANES], rc, ru, rd) for c in range(4)], axis=1)
        qm = _dot(qa.astype(BF), perm_ref[...])
        for hd in range(N_HEADS_A):
            qm_ref[:, hd, :] = qm[:, hd * LANES:(hd + 1) * LANES]
        ka = p_ref[:, OFF_KA:OFF_KA + KV_WIDTH]
        ka = ka * _head_norm_scale(ka, seg[:KV_WIDTH, :KV_WIDTH]) * gains_ref[:, G_K:G_O]
        kn_ref[...] = _rope(ka, rc, ru, rd)
        vn_ref[...] = p_ref[:, OFF_VA:OFF_VA + KV_WIDTH]
        za_ref[...] = p_ref[:, OFF_ZA:OFF_ZA + WIDTH_A]
        qb_ref[...] = p_ref[:, OFF_QB:OFF_QB + WIDTH_B]
        lb = _lower_bound(lbl_ref[...])
        kin_ref[...] = (1.0 - lb) * (1.0 - _sigmoid(p_ref[:, OFF_FB:OFF_FB + WIDTH_B]))
        ib_ref[...] = p_ref[:, OFF_IB:OFF_IB + WIDTH_B]
        zb_ref[...] = p_ref[:, OFF_ZB:OFF_ZB + WIDTH_B]


def _dec_in_call(x, mod, consts, w):
    b = x.shape[0]
    n_steps = 2
    half_in, half_mg = D_IN // n_steps, 2 * D_MODEL // n_steps
    assert half_in % LANES == 0
    const = lambda a: pl.BlockSpec(a.shape, lambda j: (0,) * a.ndim)
    cols = lambda rows, n: pl.BlockSpec((rows, n), lambda j: (0, j))
    row_out = lambda n: pl.BlockSpec((b, n), lambda j: (0, 0))
    f = lambda n: jax.ShapeDtypeStruct((b, n), F32)
    widths = [KV_WIDTH, KV_WIDTH, WIDTH_A, WIDTH_B, WIDTH_B, WIDTH_B, WIDTH_B]
    qm_shape = (b, N_HEADS_A, LANES)
    small = [w["norm_g"]]
    tail = [w["gains"], w["lb_logits"], consts["rope_c1"], consts["rope_u1"], consts["rope_d1"],
            consts["seg"], consts["perm"]]
    return pl.pallas_call(
        _dec_in_kernel,
        grid=(n_steps,),
        in_specs=[const(x), const(mod)] + [const(a) for a in small]
        + [cols(D_MODEL, half_in), cols(D_MODEL, half_mg), cols(1, half_mg)]
        + [const(a) for a in tail],
        out_specs=[cols(D_MODEL, half_in), cols(D_MODEL, half_mg), cols(b, half_mg),
                   pl.BlockSpec(qm_shape, lambda j: (0, 0, 0))]
        + [row_out(n) for n in widths],
        out_shape=[jax.ShapeDtypeStruct((D_MODEL, D_IN), BF),
                   jax.ShapeDtypeStruct((D_MODEL, 2 * D_MODEL), BF), f(2 * D_MODEL),
                   jax.ShapeDtypeStruct(qm_shape, F32)]
        + [f(n) for n in widths],
        scratch_shapes=[pltpu.VMEM((b, D_MODEL), BF), pltpu.VMEM((b, D_IN), F32)],
        compiler_params=pltpu.CompilerParams(
            dimension_semantics=("arbitrary",), vmem_limit_bytes=VMEM_LIMIT),
        name="decode_in",
    )(x, mod, *small, w["w_in_f32"], w["w_merge_f32"], w["b_merge"], *tail)


def _dec_mix_kernel(sinks_ref, qm_ref, kn_ref, vn_ref, qb_ref, kin_ref, ib_ref,
                    ck_ref, cv_ref, st_ref,
                    att_ref, ob_ref, nk_ref, nv_ref, nst_ref, *, bt):
    nh = N_HEADS_A
    s = jnp.concatenate([_dot(qm_ref[j].astype(BF), ck_ref[j].astype(BF))
                         for j in range(bt)], axis=0)
    key = lax.broadcasted_iota(jnp.int32, (bt * nh, WINDOW), 1)
    s = jnp.where(key == 0, NEG, s)
    rep = lambda r: jnp.broadcast_to(r[...][:, None, :], (bt, nh, r.shape[-1])).reshape(
        bt * nh, r.shape[-1])
    kn_rows, vn_rows = rep(kn_ref), rep(vn_ref)
    head = lax.broadcasted_iota(jnp.int32, (bt * nh, 1), 0) % nh
    sink = jnp.zeros((bt * nh, 1), F32)
    for hd in range(nh):
        sink = jnp.where(head == hd, sinks_ref[hd], sink)
    s_new = jnp.sum(qm_ref[...].reshape(bt * nh, LANES) * kn_rows, axis=-1, keepdims=True)
    m = jnp.maximum(jnp.maximum(jnp.max(s, axis=-1, keepdims=True), s_new), sink)
    p = jnp.exp(s - m)
    p_new = jnp.exp(s_new - m)
    den = jnp.sum(p, axis=-1, keepdims=True) + p_new + jnp.exp(sink - m)
    p_bf = p.astype(BF)
    pv = jnp.concatenate([_dot_nt(p_bf[j * nh:(j + 1) * nh, :], cv_ref[j].astype(BF))
                          for j in range(bt)], axis=0)
    att_ref[...] = ((pv + p_new * vn_rows) * (1.0 / den)).reshape(bt, nh, LANES)
    kn_t, vn_t = kn_ref[...].T, vn_ref[...].T
    newest = lax.broadcasted_iota(jnp.int32, (KV_WIDTH, WINDOW), 1) == WINDOW - 1
    for j in range(bt):
        nk_ref[j] = jnp.where(newest, kn_t[:, j:j + 1], pltpu.roll(ck_ref[j], WINDOW - 1, 1))
        nv_ref[j] = jnp.where(newest, vn_t[:, j:j + 1], pltpu.roll(cv_ref[j], WINDOW - 1, 1))

    for hd in range(N_HEADS_B):
        sl = slice(hd * LANES, (hd + 1) * LANES)
        kin_t = kin_ref[:, sl].T
        q_bf = qb_ref[:, sl].astype(BF)
        for j in range(bt):
            kcol = kin_t[:, j:j + 1]
            st = st_ref[j, hd]
            new = st - kcol * (st - ib_ref[j:j + 1, sl])
            nst_ref[j, hd] = new
            ob_ref[j:j + 1, sl] = _dot(q_bf, new.astype(BF))[j:j + 1, :]


def _dec_mix_call(sinks, qm, kn, vn, qb, kin, ib, cache_k, cache_v, state, bt):
    b = kn.shape[0]
    rows = lambda n: pl.BlockSpec((bt, n), lambda i, *_: (i, 0))
    cache_spec = pl.BlockSpec((bt, WINDOW, KV_WIDTH), lambda i, *_: (i, 0, 0))
    st_spec = pl.BlockSpec((bt, N_HEADS_B, HEAD_DIM_B, HEAD_DIM_B), lambda i, *_: (i, 0, 0, 0))
    qm_spec = pl.BlockSpec((bt, N_HEADS_A, LANES), lambda i, *_: (i, 0, 0))
    return pl.pallas_call(
        functools.partial(_dec_mix_kernel, bt=bt),
        grid_spec=pltpu.PrefetchScalarGridSpec(
            num_scalar_prefetch=1, grid=(b // bt,),
            in_specs=[qm_spec, rows(KV_WIDTH), rows(KV_WIDTH), rows(WIDTH_B), rows(WIDTH_B),
                      rows(WIDTH_B), cache_spec, cache_spec, st_spec],
            out_specs=[qm_spec, rows(WIDTH_B), cache_spec, cache_spec, st_spec]),
        out_shape=[jax.ShapeDtypeStruct((b, N_HEADS_A, LANES), F32),
                   jax.ShapeDtypeStruct((b, WIDTH_B), F32),
                   jax.ShapeDtypeStruct(cache_k.shape, F32),
                   jax.ShapeDtypeStruct(cache_v.shape, F32),
                   jax.ShapeDtypeStruct(state.shape, F32)],
        compiler_params=pltpu.CompilerParams(
            dimension_semantics=("arbitrary",), vmem_limit_bytes=VMEM_LIMIT),
        name="decode_mix",
    )(sinks, qm, kn, vn, qb, kin, ib, cache_k, cache_v, state)


def _dec_out_kernel(x_ref, mod_ref, att_ref, za_ref, ob_ref, zb_ref, g_ref, gains_ref,
                    permt_ref, w_pa_ref, w_pb_ref, w_out_ref,
                    y_ref, w_pa_bf_ref, w_pb_bf_ref, w_out_bf_ref):
    w_pa, w_pb, w_out = (r[...].astype(BF) for r in (w_pa_ref, w_pb_ref, w_out_ref))
    w_pa_bf_ref[...] = w_pa
    w_pb_bf_ref[...] = w_pb
    w_out_bf_ref[...] = w_out
    att = sum(_split_dot(att_ref[:, hd, :], permt_ref[hd * LANES:(hd + 1) * LANES, :])
              for hd in range(N_HEADS_A))
    ya = att * _silu(za_ref[...])
    yb = _branch_b_out(ob_ref[...], zb_ref[...], gains_ref[:, G_O:G_END])
    g = g_ref[...]
    y = (g[:, :D_MODEL] * _dot(ya.astype(BF), w_pa) + g[:, D_MODEL:] * _dot(yb.astype(BF), w_pb))
    y_ref[:, 0, :] = x_ref[:, 0, :] + _modulation(mod_ref)[2] * _dot(y.astype(BF), w_out)


def _dec_out_call(x, mod, att, za, ob, zb, g, consts, w):
    bf = lambda a: jax.ShapeDtypeStruct(a.shape, BF)
    ws = (w["w_proj_a_f32"], w["w_proj_b_f32"], w["w_out_f32"])
    return pl.pallas_call(
        _dec_out_kernel,
        out_shape=[jax.ShapeDtypeStruct(x.shape, F32)] + [bf(a) for a in ws],
        compiler_params=pltpu.CompilerParams(vmem_limit_bytes=VMEM_LIMIT),
        name="decode_out",
    )(x, mod, att, za, ob, zb, g, w["gains"], consts["permt"], *ws)


def _rope_consts(n_tiles, tile, past_len):
    half = ROT_DIM // 2
    inv = ROPE_THETA ** (-np.arange(0, ROT_DIM, 2, dtype=np.float64) / ROT_DIM)
    e = np.arange(LANES) % HEAD_DIM_A
    rot = e < ROT_DIM

    def tables(pos):
        ang = np.asarray(pos, np.float64)[:, None] * inv[e % half][None, :]
        return np.where(rot, np.cos(ang), 1.0), np.where(rot, np.sin(ang), 0.0)

    sgn = np.stack([np.where(e < half, -1.0, 0.0),
                    np.where(rot & (e >= half), 1.0, 0.0)])
    cb, sb = tables(np.arange(n_tiles) * tile)
    cl, sl = tables(np.arange(tile))
    c1, s1 = tables([past_len])
    f = lambda a: jnp.asarray(a, F32)
    return {"rope_cb": f(cb), "rope_sb": f(sb), "rope_cl": f(cl), "rope_sl": f(sl),
            "rope_sgn": f(sgn), "rope_c1": f(c1), "rope_u1": f(s1 * sgn[0:1]),
            "rope_d1": f(s1 * sgn[1:2])}


def _static_consts():
    seg = np.kron(np.eye(MXU_N // HEAD_DIM_A),
                  np.full((HEAD_DIM_A, HEAD_DIM_A), 1.0 / HEAD_DIM_A))
    tri = np.tril(np.ones((BLK, BLK)))
    r = np.arange(4 * BLK)[:, None] % BLK
    c = np.arange(2 * BLK)[None, :]
    ok_prev = (c < BLK) & (c > r)
    ok_cur = (c >= BLK) & (c - BLK <= r)
    bias = np.stack([np.where(ok_cur, 0.0, NEG), np.where(ok_prev | ok_cur, 0.0, NEG)])
    i = np.arange(BLK)[:, None]
    j = np.arange(BLK)[None, :]
    lmask = np.stack([((i // (2 * b)) == (j // (2 * b))) & ((i & b) != 0) & ((j & b) == 0)
                      for b in LEVELS]).astype(np.float32)
    perm = np.zeros((WIDTH_A, N_HEADS_A * LANES), np.float32)
    for hd in range(N_HEADS_A):
        kvh = hd // (N_HEADS_A // N_KV_A)
        for d in range(HEAD_DIM_A):
            perm[hd * HEAD_DIM_A + d, hd * LANES + kvh * HEAD_DIM_A + d] = 1.0
    return {
        "seg": jnp.asarray(seg, BF), "tri": jnp.asarray(tri, BF),
        "bias": jnp.asarray(bias, F32), "lmask": jnp.asarray(lmask, F32),
        "perm": jnp.asarray(perm, BF), "permt": jnp.asarray(perm.T, BF),
    }


def kernel(x_prompt, x_sample, cache_win_k, cache_win_v, state_hgrn, c_prompt, c_sample,
           w_ada, b_ada, norm_g, w_in, q_norm_g, k_norm_g, sinks, lb_logits, o_norm_g,
           w_merge, b_merge, w_proj_a, w_proj_b, w_out):
    depth = w_in.shape[0]
    assert depth == 1 and x_prompt.shape[0] == 1 and x_sample.shape[1] == 1
    t = x_prompt.shape[1]
    nb = x_sample.shape[0]
    past_len = t
    tile = PROMPT_TILE
    bt = DECODE_BATCH_TILE

    consts = _static_consts()
    consts.update(_rope_consts(t // tile, tile, past_len))

    w = {
        "sinks": sinks[0], "norm_g": norm_g,
        "w_in_f32": w_in[0], "w_merge_f32": w_merge[0], "b_merge": b_merge,
        "w_proj_a_f32": w_proj_a[0], "w_proj_b_f32": w_proj_b[0], "w_out_f32": w_out[0],
        "lb_logits": lb_logits,
    }

    mod_p, mod_s, w["gains"] = _ada_call(c_prompt, c_sample, w_ada[0], b_ada,
                                         q_norm_g, k_norm_g, o_norm_g)

    xs = x_sample
    (w["w_in"], w["w_merge"], g,
     qm, kn, vn, za, qb, kin, ib, zb) = _dec_in_call(xs, mod_s, consts, w)
    to_t = lambda c: jnp.transpose(c[0], (0, 2, 3, 1)).reshape(nb, KV_WIDTH, WINDOW)
    from_t = lambda c: jnp.transpose(
        c.reshape(-1, N_KV_A, HEAD_DIM_A, WINDOW), (0, 3, 1, 2))[None]
    att, ob, nk, nv, nst = _dec_mix_call(
        w["sinks"], qm, kn, vn, qb, kin, ib,
        to_t(cache_win_k), to_t(cache_win_v), state_hgrn[0], bt)
    y_s, w["w_proj_a"], w["w_proj_b"], w["w_out"] = _dec_out_call(
        xs, mod_s, att, za, ob, zb, g, consts, w)

    run = functools.partial(_prompt_call, x_prompt[0], mod_p, consts, w, tile,
                            PROMPT_TILES_PER_STEP)
    *fast, span = run(robust=False)
    y_p, kwin, vwin, st_p = lax.cond(jnp.max(span) < HGRN_FAST_MAX,
                                     lambda: tuple(fast), lambda: tuple(run(robust=True)[:4]))

    return (y_p[None], y_s,
            from_t(kwin[None]), from_t(vwin[None]), st_p[None, None],
            from_t(nk), from_t(nv), nst[None])
```

```python
import functools

import numpy as np
import jax
import jax.numpy as jnp
from jax import lax
from jax.experimental import pallas as pl
from jax.experimental.pallas import tpu as pltpu

D_MODEL = 1024
HEAD_DIM_A = 64
N_HEADS_A = 8
N_KV_A = 2
WIDTH_A = N_HEADS_A * HEAD_DIM_A
KV_WIDTH = N_KV_A * HEAD_DIM_A
WINDOW = 128
ROT_DIM = HEAD_DIM_A // 4
ROPE_THETA = 500000.0
HEAD_DIM_B = 128
N_HEADS_B = 4
WIDTH_B = N_HEADS_B * HEAD_DIM_B
EPS = 1e-6
OFF_QA = 0
OFF_KA = OFF_QA + WIDTH_A
OFF_VA = OFF_KA + KV_WIDTH
OFF_ZA = OFF_VA + KV_WIDTH
OFF_QB = OFF_ZA + WIDTH_A
OFF_FB = OFF_QB + WIDTH_B
OFF_IB = OFF_FB + WIDTH_B
OFF_ZB = OFF_IB + WIDTH_B
D_IN = OFF_ZB + WIDTH_B

LANES = 128
BLK = 128
SUB = 8
LEVELS = (64, 32, 16, 8)
HGRN_FAST_MAX = 80.0
MXU_N = 256
NEG = -1e30
VMEM_LIMIT = 56 * 1024 * 1024
PROMPT_TILE = 256
PROMPT_TILES_PER_STEP = 2
DECODE_BATCH_TILE = 16
G_Q, G_K, G_O = 0, WIDTH_A, WIDTH_A + KV_WIDTH
G_END = G_O + WIDTH_B

BF = jnp.bfloat16
F32 = jnp.float32


def _dot(a, b):
    return jnp.dot(a, b, preferred_element_type=F32)


def _dot_nt(a, b):
    return lax.dot_general(a, b, (((1,), (1,)), ((), ())), preferred_element_type=F32)


def _dot_tn(a, b):
    return lax.dot_general(a, b, (((0,), (0,)), ((), ())), preferred_element_type=F32)


def _split_dot(a_f32, b_bf):
    hi = a_f32.astype(BF)
    lo = (a_f32 - hi.astype(F32)).astype(BF)
    return _dot(hi, b_bf) + _dot(lo, b_bf)


def _sigmoid(x):
    return 1.0 / (1.0 + jnp.exp(-x))


def _silu(x):
    return x * _sigmoid(x)


def _lower_bound(lb_logits):
    l0 = lb_logits[0:1, :]
    l1 = lb_logits[1:2, :]
    m = jnp.maximum(l0, l1)
    e0 = jnp.exp(l0 - m)
    e1 = jnp.exp(l1 - m)
    return e0 / (e0 + e1)


def _rope_tables(cb, sb, cl, sl, sgn_up, sgn_dn):
    c = cb * cl - sb * sl
    s = sb * cl + cb * sl
    return c, s * sgn_up, s * sgn_dn


def _rope(x, c, s_up, s_dn):
    return x * c + pltpu.roll(x, LANES - ROT_DIM // 2, 1) * s_up + pltpu.roll(x, ROT_DIM // 2, 1) * s_dn


def _modulation(mod_ref):
    return (mod_ref[:, 0:D_MODEL], mod_ref[:, D_MODEL:2 * D_MODEL],
            mod_ref[:, 2 * D_MODEL:3 * D_MODEL])


def _norm_modulate(x, ng, mod_ref):
    shift, scale, _ = _modulation(mod_ref)
    ms = jnp.mean(x * x, axis=-1, keepdims=True)
    return (x * lax.rsqrt(ms + EPS) * (ng * (1.0 + scale)) + shift).astype(BF)


def _head_norm_scale(x, seg_mean_bf):
    sq = (x * x).astype(BF)
    w = min(x.shape[1], MXU_N)
    ms = jnp.concatenate([_dot(sq[:, c:c + w], seg_mean_bf[:w, :w])
                          for c in range(0, x.shape[1], w)], axis=1)
    return lax.rsqrt(ms + EPS)


def _ada_kernel(cp_ref, cs_ref, w_ref, b_ref, qg_ref, kg_ref, og_ref, op_ref, os_ref, gains_ref):
    gains_ref[...] = jnp.concatenate(
        [qg_ref[...]] * N_HEADS_A + [kg_ref[...]] * N_KV_A + [og_ref[...]] * N_HEADS_B, axis=1)
    w = w_ref[...]
    w_hi = w.astype(BF)
    w_lo = (w - w_hi.astype(F32)).astype(BF)
    b = b_ref[...]

    ns, n_p = cs_ref.shape[0], cp_ref.shape[0]
    c = jnp.concatenate([cs_ref[...], cp_ref[...],
                         jnp.zeros(((-n_p) % SUB, D_MODEL), F32)], axis=0)
    c_hi = c.astype(BF)
    c_lo = (c - c_hi.astype(F32)).astype(BF)
    out = _dot(c_hi, w_hi) + (_dot(c_hi, w_lo) + _dot(c_lo, w_hi)) + b
    os_ref[...] = out[:ns, :]
    op_ref[...] = out[ns:ns + n_p, :]


def _ada_call(c_p, c_s, w_ada, b_ada, q_g, k_g, o_g):
    mp, ms = c_p.shape[0], c_s.shape[0]
    n = w_ada.shape[1]
    tn = 512
    const = lambda a: pl.BlockSpec(a.shape, lambda j: (0,) * a.ndim)
    return pl.pallas_call(
        _ada_kernel,
        grid=(n // tn,),
        in_specs=[const(c_p), const(c_s),
                  pl.BlockSpec((D_MODEL, tn), lambda j: (0, j)),
                  pl.BlockSpec((1, tn), lambda j: (0, j)),
                  const(q_g), const(k_g), const(o_g)],
        out_specs=[pl.BlockSpec((mp, tn), lambda j: (0, j)),
                   pl.BlockSpec((ms, tn), lambda j: (0, j)),
                   pl.BlockSpec((1, G_END), lambda j: (0, 0))],
        out_shape=[jax.ShapeDtypeStruct((mp, n), F32), jax.ShapeDtypeStruct((ms, n), F32),
                   jax.ShapeDtypeStruct((1, G_END), F32)],
        name="ada",
    )(c_p, c_s, w_ada, b_ada, q_g, k_g, o_g)


STAGE_B_COST = 3840


def _round_robin(*gens):
    results = [None] * len(gens)
    live = list(range(len(gens)))
    while live:
        for n in list(live):
            try:
                yield next(gens[n])
            except StopIteration as stop:
                results[n] = stop.value
                live.remove(n)
    return results


def _attn_block(q_blk, kcat, kcat_sw, vcat, vcat_sw, bias, sink_a, sink_b):
    lane = lax.broadcasted_iota(jnp.int32, (BLK, LANES), 1)
    lo = lane < HEAD_DIM_A
    chunks = [q_blk[:, c * LANES:(c + 1) * LANES] for c in range(4)]
    zero = jnp.zeros((BLK, LANES), F32)
    q_lo = [jnp.where(lo, c, zero).astype(BF) for c in chunks]
    q_hi = [jnp.where(lo, zero, c).astype(BF) for c in chunks]
    qa = jnp.concatenate([q_lo[0], q_lo[1], q_hi[2], q_hi[3]], axis=0)
    qb = jnp.concatenate([q_hi[0], q_hi[1], q_lo[2], q_lo[3]], axis=0)

    def probs(qs, kc, sink):
        s_all = _dot_nt(qs, kc)
        ps, rs = [], []
        for r0 in range(0, 4 * BLK, 2 * BLK):
            rows = slice(r0, r0 + 2 * BLK)
            s = s_all[rows] + bias[rows]
            m = jnp.maximum(jnp.max(s, axis=-1, keepdims=True), sink[rows])
            p = jnp.exp(s - m)
            den = jnp.sum(p, axis=-1, keepdims=True) + jnp.exp(sink[rows] - m)
            ps.append(p.astype(BF))
            rs.append(1.0 / den)
        return jnp.concatenate(ps, axis=0), jnp.concatenate(rs, axis=0)

    pa, ra = probs(qa, kcat, sink_a)
    yield 200
    pb, rb = probs(qb, kcat_sw, sink_b)
    yield 200
    oa = _dot(pa, vcat) * ra
    yield 40
    ob = _dot(pb, vcat_sw) * rb
    yield 40
    r = lambda o, i: o[i * BLK:(i + 1) * BLK, :]
    out = jnp.concatenate([
        jnp.where(lo, r(oa, 0), r(ob, 0)),
        jnp.where(lo, r(oa, 1), r(ob, 1)),
        jnp.where(lo, r(ob, 2), r(oa, 2)),
        jnp.where(lo, r(ob, 3), r(oa, 3)),
    ], axis=1)
    yield 30
    return out


def _hgrn_gates(fb, lb, one_m_lb, tri_bf):
    sig = _sigmoid(fb)
    kin = one_m_lb * (1.0 - sig)
    f = lb + one_m_lb * sig
    cum = _dot(tri_bf, jnp.log(f).astype(BF))
    return kin, f, cum


def _hgrn_span_decay(cum):
    q = BLK // 4
    ends = [cum[(n + 1) * q - 1:(n + 1) * q, :] for n in range(4)]
    d = -ends[0]
    for n in range(1, 4):
        d = jnp.maximum(d, ends[n - 1] - ends[n])
    return d


def _hgrn_state_step(qb, kin, cum, ib, st_ref):
    q_dec = (qb * jnp.exp(cum)).astype(BF)
    last = cum[BLK - 1:BLK, :]
    k_dec = (kin * jnp.exp(last - cum)).astype(BF)
    v_bf = ib.astype(BF)
    outs = []
    for h in range(N_HEADS_B):
        sl = slice(h * LANES, (h + 1) * LANES)
        st = st_ref[h]
        outs.append(_dot_nt(q_dec[:, sl], st.astype(BF)))
        st_ref[h] = st * jnp.exp(last[:, sl]) + _dot_tn(v_bf[:, sl], k_dec[:, sl])
    return jnp.concatenate(outs, axis=1)


def _hgrn_apply(amats, ib):
    v_bf = ib.astype(BF)
    return jnp.concatenate(
        [_dot(amats[h].astype(BF), v_bf[:, h * LANES:(h + 1) * LANES])
         for h in range(N_HEADS_B)], axis=1)


def _recur_block_fast(qb, kin, cum, ib, st_ref):
    base = _hgrn_state_step(qb, kin, cum, ib, st_ref)
    yield 130
    half = BLK // 2
    row = lax.broadcasted_iota(jnp.int32, (BLK, 1), 0)
    upper = row >= half
    piv = cum[half - 1:half, :]
    w_lvl = jnp.exp(jnp.concatenate([piv - cum[:half, :], cum[half:, :] - piv], axis=0))
    p_lvl = (jnp.where(upper, qb, kin) * w_lvl).astype(BF)
    mid = jnp.where(upper, cum[half + half // 2 - 1:half + half // 2, :],
                    cum[half // 2 - 1:half // 2, :])
    e_mid = cum - mid
    q_mid = (qb * jnp.exp(e_mid)).astype(BF)
    k_mid = (kin * jnp.exp(-e_mid)).astype(BF)
    yield 230

    ri = lax.broadcasted_iota(jnp.int32, (BLK, BLK), 0)
    ci = lax.broadcasted_iota(jnp.int32, (BLK, BLK), 1)
    same_half_causal = ((ri >= half) == (ci >= half)) & (ci <= ri)
    cross = (ri >= half) & (ci < half)
    amats = []
    for h in range(N_HEADS_B):
        sl = slice(h * LANES, (h + 1) * LANES)
        a_mid = _dot_nt(q_mid[:, sl], k_mid[:, sl])
        a_lvl = _dot_nt(p_lvl[:, sl], p_lvl[:, sl])
        amats.append(jnp.where(same_half_causal, a_mid, jnp.where(cross, a_lvl, 0.0)))
    yield 60
    full = base + _hgrn_apply(amats, ib)
    yield 40
    return full


def _hgrn_intra_robust(qb, kin, f, cum, ib, lvl_mask_ref):
    row = lax.broadcasted_iota(jnp.int32, (BLK, 1), 0)
    lvl_ops = []
    for b in LEVELS:
        pieces = []
        for r0 in range(0, BLK, 2 * b):
            piv = cum[r0 + b - 1:r0 + b, :]
            pieces.append(piv - cum[r0:r0 + b, :])
            pieces.append(cum[r0 + b:r0 + 2 * b, :] - piv)
        w = jnp.exp(jnp.concatenate(pieces, axis=0))
        second = (row & b) != 0
        lvl_ops.append((jnp.where(second, qb, kin) * w).astype(BF))

    n8 = BLK // SUB
    q3 = qb.reshape(n8, SUB, WIDTH_B)
    k3 = kin.reshape(n8, SUB, WIDTH_B)
    f3 = f.reshape(n8, SUB, WIDTH_B)
    v3 = ib.reshape(n8, SUB, WIDTH_B)
    subl = lax.broadcasted_iota(jnp.int32, (n8, SUB, 1), 1)

    def head(x, h):
        return x[..., h * LANES:(h + 1) * LANES]

    g = q3 * k3
    acc = [jnp.sum(head(g, h), axis=-1, keepdims=True) * head(v3, h) for h in range(N_HEADS_B)]
    dec = jnp.ones_like(f3)
    kd = k3
    vd = v3
    for d in range(1, SUB):
        dec = f3 * pltpu.roll(dec, 1, 1)
        kd = pltpu.roll(kd, 1, 1)
        vd = pltpu.roll(vd, 1, 1)
        g = q3 * kd * dec
        ok = subl >= d
        for h in range(N_HEADS_B):
            a = jnp.where(ok, jnp.sum(head(g, h), axis=-1, keepdims=True), 0.0)
            acc[h] = acc[h] + a * head(vd, h)

    amats = []
    for h in range(N_HEADS_B):
        sl = slice(h * LANES, (h + 1) * LANES)
        amat = jnp.zeros((BLK, BLK), F32)
        for li in range(len(LEVELS)):
            p = lvl_ops[li][:, sl]
            amat = amat + lvl_mask_ref[li] * _dot_nt(p, p)
        amats.append(amat)
    diag = jnp.concatenate([acc[h].reshape(BLK, LANES) for h in range(N_HEADS_B)], axis=1)
    return _hgrn_apply(amats, ib) + diag


def _branch_b_out(o, zb, og):
    outs = []
    for h in range(N_HEADS_B):
        sl = slice(h * LANES, (h + 1) * LANES)
        oh = o[:, sl]
        ms = jnp.mean(oh * oh, axis=-1, keepdims=True)
        outs.append(oh * lax.rsqrt(ms + EPS))
    return jnp.concatenate(outs, axis=1) * og * _silu(zb)


def _prompt_kernel(sinks_ref, *refs, tile, n_steps, tps, robust):
    x_refs = refs[:tps + 1]
    (mod_ref, ng_ref,
     w_in_ref, w_mg_ref, b_mg_ref, w_pa_ref, w_pb_ref, w_out_ref,
     gains_ref, lbl_ref,
     cb_ref, sb_ref, cl_ref, sl_ref, sgn_ref,
     seg_ref, tri_ref, bias_ref, lmask_ref,
     y_ref, kwin_ref, vwin_ref, state_ref, span_ref,
     st_ref, kprev_ref, kprev_sw_ref, vprev_ref, vprev_sw_ref,
     p0_ref, p1_ref, g_ref, h0_ref, h1_ref, ob_ref) = refs[tps + 1:]
    p_refs, h_refs = (p0_ref, p1_ref), (h0_ref, h1_ref)
    s = pl.program_id(0)
    nblk = tile // BLK

    def stage_a(x_ref, h_ref, p_ref):
        def prep():
            h_ref[...] = _norm_modulate(x_ref[...], ng_ref[...], mod_ref)

        def proj_chunk(c):
            def run():
                cs = slice(c * MXU_N, (c + 1) * MXU_N)
                p_ref[:, cs] = _dot(h_ref[...], w_in_ref[:, cs])
            return run

        return [prep] + [proj_chunk(c) for c in range(D_IN // MXU_N)]

    def gate_chunks(h_ref):
        def gate_chunk(c):
            def run():
                cs = slice(c * MXU_N, (c + 1) * MXU_N)
                g_ref[:, cs] = _sigmoid(_dot(h_ref[...], w_mg_ref[:, cs]) + b_mg_ref[:, cs])
            return run

        return [gate_chunk(c) for c in range(2 * D_MODEL // MXU_N)]

    def phase(h_cur_ref, a_next, b_parts):
        gc = gate_chunks(h_cur_ref)
        interleave(gc[:2] + a_next[:1] + gc[2:] + a_next[1:], b_parts, lead=10)

    def stage_b(x_ref, p_ref, t_idx, y_rows):
        blocks = [slice(blk * BLK, (blk + 1) * BLK) for blk in range(nblk)]

        def qkv_stream():
            seg = seg_ref[...]
            rc, ru, rd = _rope_tables(cb_ref[pl.ds(t_idx, 1), :], sb_ref[pl.ds(t_idx, 1), :],
                                      cl_ref[...], sl_ref[...], sgn_ref[0:1, :], sgn_ref[1:2, :])
            qa = p_ref[:, OFF_QA:OFF_QA + WIDTH_A]
            qa = qa * _head_norm_scale(qa, seg) * (gains_ref[:, G_Q:G_K] * (HEAD_DIM_A ** -0.5))
            yield 160
            qa = jnp.concatenate(
                [_rope(qa[:, c * LANES:(c + 1) * LANES], rc, ru, rd) for c in range(4)], axis=1)
            yield 160
            ka = p_ref[:, OFF_KA:OFF_KA + KV_WIDTH]
            ka = ka * _head_norm_scale(ka, seg[:KV_WIDTH, :KV_WIDTH]) * gains_ref[:, G_K:G_O]
            ka = _rope(ka, rc, ru, rd)
            va = p_ref[:, OFF_VA:OFF_VA + KV_WIDTH]
            ka_sw = pltpu.roll(ka, HEAD_DIM_A, 1)
            va_sw = pltpu.roll(va, HEAD_DIM_A, 1)
            rows4 = lax.broadcasted_iota(jnp.int32, (4 * BLK, 1), 0) // BLK

            def sink_col(heads):
                col = jnp.zeros((4 * BLK, 1), F32)
                for n, hd in enumerate(heads):
                    col = jnp.where(rows4 == n, sinks_ref[hd], col)
                return col
            kwin_ref[...] = ka[tile - WINDOW:, :]
            vwin_ref[...] = va[tile - WINDOW:, :]
            res = (qa, ka, va, ka_sw, va_sw, sink_col((0, 2, 5, 7)), sink_col((1, 3, 4, 6)))
            yield 120
            return res

        def gates_stream():
            lb = _lower_bound(lbl_ref[...])
            tri = tri_ref[...]
            gates = []
            half_w = WIDTH_B // 2
            for rs in blocks:
                halves = []
                for c0 in (0, half_w):
                    cs = slice(c0, c0 + half_w)
                    halves.append(_hgrn_gates(p_ref[rs, OFF_FB + c0:OFF_FB + c0 + half_w],
                                              lb[:, cs], 1.0 - lb[:, cs], tri))
                    yield 240
                gates.append(tuple(jnp.concatenate(pair, axis=1) for pair in zip(*halves)))
            span = span_ref[...]
            for g in gates:
                span = jnp.maximum(span, _hgrn_span_decay(g[2]))
            span_ref[...] = span
            yield 10
            return gates

        (qa, ka, va, ka_sw, va_sw, sink_a, sink_b), gates = (
            yield from _round_robin(qkv_stream(), gates_stream()))

        kv_refs = (kprev_ref, kprev_sw_ref, vprev_ref, vprev_sw_ref)
        kv_new = [a.astype(BF) for a in (ka, ka_sw, va, va_sw)]
        kv_old = [r[...] for r in kv_refs]
        for r, a in zip(kv_refs, kv_new):
            r[...] = a[blocks[-1]]

        def attend(blk, rs):
            cats = [jnp.concatenate([old if blk == 0 else new[blocks[blk - 1]], new[rs]], axis=0)
                    for old, new in zip(kv_old, kv_new)]
            bias = bias_ref[jnp.where(t_idx == 0, 0, 1)] if blk == 0 else bias_ref[1]
            return (yield from _attn_block(qa[rs], *cats, bias, sink_a, sink_b))

        def recur(blk, rs):
            kin, f, cum = gates[blk]
            qb, ib = p_ref[rs, OFF_QB:OFF_QB + WIDTH_B], p_ref[rs, OFF_IB:OFF_IB + WIDTH_B]
            if robust:
                base = _hgrn_state_step(qb, kin, cum, ib, st_ref)
                yield 130
                full = base + _hgrn_intra_robust(qb, kin, f, cum, ib, lmask_ref)
                yield 230 + 60 + 40
            else:
                full = yield from _recur_block_fast(qb, kin, cum, ib, st_ref)
            ob_ref[rs, :] = full

        ya_parts = []
        for blk, rs in enumerate(blocks):
            res = yield from _round_robin(attend(blk, rs), recur(blk, rs))
            ya_parts.append(res[0])

        ya_bf = (jnp.concatenate(ya_parts, axis=0)
                 * _silu(p_ref[:, OFF_ZA:OFF_ZA + WIDTH_A])).astype(BF)
        yield 160
        yb_bf = _branch_b_out(ob_ref[...], p_ref[:, OFF_ZB:OFF_ZB + WIDTH_B],
                              gains_ref[:, G_O:G_END]).astype(BF)
        yield 330

        y = (g_ref[:, :D_MODEL] * _dot(ya_bf, w_pa_ref[...])
             + g_ref[:, D_MODEL:] * _dot(yb_bf, w_pb_ref[...]))
        y_ref[y_rows, :] = x_ref[...] + _modulation(mod_ref)[2] * _dot(y.astype(BF), w_out_ref[...])
        yield 0

    def interleave(a_thunks, b_parts, lead):
        for th in a_thunks[:lead]:
            th()
        acc, done = 0, lead
        for cost in b_parts:
            acc += cost
            upto = lead + int(round((len(a_thunks) - lead) * min(acc, STAGE_B_COST) / STAGE_B_COST))
            for th in a_thunks[done:upto]:
                th()
            done = upto
        assert done == len(a_thunks) and acc == STAGE_B_COST, (done, acc)

    @pl.when(s == 0)
    def _():
        st_ref[...] = jnp.zeros_like(st_ref)
        span_ref[...] = jnp.zeros_like(span_ref)
        for r in (kprev_ref, kprev_sw_ref, vprev_ref, vprev_sw_ref):
            r[...] = jnp.zeros_like(r)
        for th in stage_a(x_refs[0], h0_ref, p0_ref):
            th()

    for k in range(tps):
        cur, nxt = k % 2, (k + 1) % 2
        phase(h_refs[cur], stage_a(x_refs[k + 1], h_refs[nxt], p_refs[nxt]),
              stage_b(x_refs[k], p_refs[cur], tps * s + k, slice(k * tile, (k + 1) * tile)))

    @pl.when(s == n_steps - 1)
    def _():
        for hd in range(N_HEADS_B):
            state_ref[hd] = st_ref[hd].T
        kwin_ref[...] = kwin_ref[...].T
        vwin_ref[...] = vwin_ref[...].T


def _const_spec(shape):
    nd = len(shape)
    return pl.BlockSpec(shape, lambda i, *_: (0,) * nd, pipeline_mode=pl.Buffered(1))


def _prompt_call(x, mod, consts, w, tile, tps, robust):
    t = x.shape[0]
    n_tiles = t // tile
    n_steps = n_tiles // tps
    assert n_steps * tps * tile == t and tps % 2 == 0
    row = lambda n: _const_spec((1, n))
    x_spec = lambda k: pl.BlockSpec(
        (tile, D_MODEL), lambda i, *_: (jnp.minimum(tps * i + k, n_tiles - 1), 0))
    in_specs = [x_spec(k) for k in range(tps + 1)] + [
        row(3 * D_MODEL), row(D_MODEL),
        _const_spec((D_MODEL, D_IN)), _const_spec((D_MODEL, 2 * D_MODEL)), row(2 * D_MODEL),
        _const_spec((WIDTH_A, D_MODEL)), _const_spec((WIDTH_B, D_MODEL)),
        _const_spec((D_MODEL, D_MODEL)),
        row(G_END), _const_spec((2, WIDTH_B)),
        _const_spec((n_tiles, LANES)), _const_spec((n_tiles, LANES)),
        _const_spec((tile, LANES)), _const_spec((tile, LANES)), _const_spec((2, LANES)),
        _const_spec((MXU_N, MXU_N)), _const_spec((BLK, BLK)),
        _const_spec((2, 4 * BLK, 2 * BLK)), _const_spec((len(LEVELS), BLK, BLK)),
    ]
    out_specs = [
        pl.BlockSpec((tps * tile, D_MODEL), lambda i, *_: (i, 0)),
        pl.BlockSpec((WINDOW, KV_WIDTH), lambda i, *_: (0, 0)),
        pl.BlockSpec((WINDOW, KV_WIDTH), lambda i, *_: (0, 0)),
        pl.BlockSpec((N_HEADS_B, HEAD_DIM_B, HEAD_DIM_B), lambda i, *_: (0, 0, 0)),
        pl.BlockSpec((1, WIDTH_B), lambda i, *_: (0, 0)),
    ]
    out_shape = [
        jax.ShapeDtypeStruct((t, D_MODEL), F32),
        jax.ShapeDtypeStruct((WINDOW, KV_WIDTH), F32),
        jax.ShapeDtypeStruct((WINDOW, KV_WIDTH), F32),
        jax.ShapeDtypeStruct((N_HEADS_B, HEAD_DIM_B, HEAD_DIM_B), F32),
        jax.ShapeDtypeStruct((1, WIDTH_B), F32),
    ]
    scratch = [
        pltpu.VMEM((N_HEADS_B, HEAD_DIM_B, HEAD_DIM_B), F32),
        pltpu.VMEM((BLK, KV_WIDTH), BF), pltpu.VMEM((BLK, KV_WIDTH), BF),
        pltpu.VMEM((BLK, KV_WIDTH), BF), pltpu.VMEM((BLK, KV_WIDTH), BF),
        pltpu.VMEM((tile, D_IN), F32), pltpu.VMEM((tile, D_IN), F32),
        pltpu.VMEM((tile, 2 * D_MODEL), F32),
        pltpu.VMEM((tile, D_MODEL), BF), pltpu.VMEM((tile, D_MODEL), BF),
        pltpu.VMEM((tile, WIDTH_B), F32),
    ]
    return pl.pallas_call(
        functools.partial(_prompt_kernel, tile=tile, n_steps=n_steps, tps=tps, robust=robust),
        grid_spec=pltpu.PrefetchScalarGridSpec(
            num_scalar_prefetch=1, grid=(n_steps,),
            in_specs=in_specs, out_specs=out_specs, scratch_shapes=scratch),
        out_shape=out_shape,
        compiler_params=pltpu.CompilerParams(
            dimension_semantics=("arbitrary",), vmem_limit_bytes=VMEM_LIMIT),
        name="prompt_layer_robust" if robust else "prompt_layer",
    )(w["sinks"], *([x] * (tps + 1)), mod, w["norm_g"],
      w["w_in"], w["w_merge"], w["b_merge"], w["w_proj_a"], w["w_proj_b"], w["w_out"],
      w["gains"], w["lb_logits"],
      consts["rope_cb"], consts["rope_sb"], consts["rope_cl"], consts["rope_sl"],
      consts["rope_sgn"],
      consts["seg"], consts["tri"], consts["bias"], consts["lmask"])


def _dec_in_kernel(x_ref, mod_ref, ng_ref, w_in_ref, w_mg_ref, b_mg_ref, gains_ref, lbl_ref,
                   rc_ref, ru_ref, rd_ref, seg_ref, perm_ref,
                   w_in_bf_ref, w_mg_bf_ref, g_ref,
                   qm_ref, kn_ref, vn_ref, za_ref, qb_ref, kin_ref, ib_ref, zb_ref,
                   h_ref, p_ref):
    j = pl.program_id(0)
    half_in = D_IN // 2

    @pl.when(j == 0)
    def _():
        h_ref[...] = _norm_modulate(x_ref[:, 0, :], ng_ref[...], mod_ref)

    w_in_bf = w_in_ref[...].astype(BF)
    w_mg_bf = w_mg_ref[...].astype(BF)
    w_in_bf_ref[...] = w_in_bf
    w_mg_bf_ref[...] = w_mg_bf
    h_bf = h_ref[...]
    g_ref[...] = _sigmoid(_dot(h_bf, w_mg_bf) + b_mg_ref[...])
    part = _dot(h_bf, w_in_bf)

    @pl.when(j == 0)
    def _():
        p_ref[:, :half_in] = part

    @pl.when(j == 1)
    def _():
        p_ref[:, half_in:] = part
        seg = seg_ref[...]
        rc, ru, rd = rc_ref[...], ru_ref[...], rd_ref[...]
        qa = p_ref[:, OFF_QA:OFF_QA + WIDTH_A]
        qa = qa * _head_norm_scale(qa, seg) * (gains_ref[:, G_Q:G_K] * (HEAD_DIM_A ** -0.5))
        qa = jnp.concatenate(
            [_rope(qa[:, c * LANES:(c + 1) * LANES], rc, ru, rd) for c in range(4)], axis=1)
        qm = _dot(qa.astype(BF), perm_ref[...])
        for hd in range(N_HEADS_A):
            qm_ref[:, hd, :] = qm[:, hd * LANES:(hd + 1) * LANES]
        ka = p_ref[:, OFF_KA:OFF_KA + KV_WIDTH]
        ka = ka * _head_norm_scale(ka, seg[:KV_WIDTH, :KV_WIDTH]) * gains_ref[:, G_K:G_O]
        kn_ref[...] = _rope(ka, rc, ru, rd)
        vn_ref[...] = p_ref[:, OFF_VA:OFF_VA + KV_WIDTH]
        za_ref[...] = p_ref[:, OFF_ZA:OFF_ZA + WIDTH_A]
        qb_ref[...] = p_ref[:, OFF_QB:OFF_QB + WIDTH_B]
        lb = _lower_bound(lbl_ref[...])
        kin_ref[...] = (1.0 - lb) * (1.0 - _sigmoid(p_ref[:, OFF_FB:OFF_FB + WIDTH_B]))
        ib_ref[...] = p_ref[:, OFF_IB:OFF_IB + WIDTH_B]
        zb_ref[...] = p_ref[:, OFF_ZB:OFF_ZB + WIDTH_B]


def _dec_in_call(x, mod, consts, w):
    b = x.shape[0]
    n_steps = 2
    half_in, half_mg = D_IN // n_steps, 2 * D_MODEL // n_steps
    assert half_in % LANES == 0
    const = lambda a: pl.BlockSpec(a.shape, lambda j: (0,) * a.ndim)
    cols = lambda rows, n: pl.BlockSpec((rows, n), lambda j: (0, j))
    row_out = lambda n: pl.BlockSpec((b, n), lambda j: (0, 0))
    f = lambda n: jax.ShapeDtypeStruct((b, n), F32)
    widths = [KV_WIDTH, KV_WIDTH, WIDTH_A, WIDTH_B, WIDTH_B, WIDTH_B, WIDTH_B]
    qm_shape = (b, N_HEADS_A, LANES)
    small = [w["norm_g"]]
    tail = [w["gains"], w["lb_logits"], consts["rope_c1"], consts["rope_u1"], consts["rope_d1"],
            consts["seg"], consts["perm"]]
    return pl.pallas_call(
        _dec_in_kernel,
        grid=(n_steps,),
        in_specs=[const(x), const(mod)] + [const(a) for a in small]
        + [cols(D_MODEL, half_in), cols(D_MODEL, half_mg), cols(1, half_mg)]
        + [const(a) for a in tail],
        out_specs=[cols(D_MODEL, half_in), cols(D_MODEL, half_mg), cols(b, half_mg),
                   pl.BlockSpec(qm_shape, lambda j: (0, 0, 0))]
        + [row_out(n) for n in widths],
        out_shape=[jax.ShapeDtypeStruct((D_MODEL, D_IN), BF),
                   jax.ShapeDtypeStruct((D_MODEL, 2 * D_MODEL), BF), f(2 * D_MODEL),
                   jax.ShapeDtypeStruct(qm_shape, F32)]
        + [f(n) for n in widths],
        scratch_shapes=[pltpu.VMEM((b, D_MODEL), BF), pltpu.VMEM((b, D_IN), F32)],
        compiler_params=pltpu.CompilerParams(
            dimension_semantics=("arbitrary",), vmem_limit_bytes=VMEM_LIMIT),
        name="decode_in",
    )(x, mod, *small, w["w_in_f32"], w["w_merge_f32"], w["b_merge"], *tail)


def _dec_mix_kernel(sinks_ref, qm_ref, kn_ref, vn_ref, qb_ref, kin_ref, ib_ref,
                    ck_ref, cv_ref, st_ref,
                    att_ref, ob_ref, nk_ref, nv_ref, nst_ref, *, bt):
    nh = N_HEADS_A
    s = jnp.concatenate([_dot(qm_ref[j].astype(BF), ck_ref[j].astype(BF))
                         for j in range(bt)], axis=0)
    key = lax.broadcasted_iota(jnp.int32, (bt * nh, WINDOW), 1)
    s = jnp.where(key == 0, NEG, s)
    rep = lambda r: jnp.broadcast_to(r[...][:, None, :], (bt, nh, r.shape[-1])).reshape(
        bt * nh, r.shape[-1])
    kn_rows, vn_rows = rep(kn_ref), rep(vn_ref)
    head = lax.broadcasted_iota(jnp.int32, (bt * nh, 1), 0) % nh
    sink = jnp.zeros((bt * nh, 1), F32)
    for hd in range(nh):
        sink = jnp.where(head == hd, sinks_ref[hd], sink)
    s_new = jnp.sum(qm_ref[...].reshape(bt * nh, LANES) * kn_rows, axis=-1, keepdims=True)
    m = jnp.maximum(jnp.maximum(jnp.max(s, axis=-1, keepdims=True), s_new), sink)
    p = jnp.exp(s - m)
    p_new = jnp.exp(s_new - m)
    den = jnp.sum(p, axis=-1, keepdims=True) + p_new + jnp.exp(sink - m)
    p_bf = p.astype(BF)
    pv = jnp.concatenate([_dot_nt(p_bf[j * nh:(j + 1) * nh, :], cv_ref[j].astype(BF))
                          for j in range(bt)], axis=0)
    att_ref[...] = ((pv + p_new * vn_rows) * (1.0 / den)).reshape(bt, nh, LANES)
    kn_t, vn_t = kn_ref[...].T, vn_ref[...].T
    newest = lax.broadcasted_iota(jnp.int32, (KV_WIDTH, WINDOW), 1) == WINDOW - 1
    for j in range(bt):
        nk_ref[j] = jnp.where(newest, kn_t[:, j:j + 1], pltpu.roll(ck_ref[j], WINDOW - 1, 1))
        nv_ref[j] = jnp.where(newest, vn_t[:, j:j + 1], pltpu.roll(cv_ref[j], WINDOW - 1, 1))

    for hd in range(N_HEADS_B):
        sl = slice(hd * LANES, (hd + 1) * LANES)
        kin_t = kin_ref[:, sl].T
        q_bf = qb_ref[:, sl].astype(BF)
        for j in range(bt):
            kcol = kin_t[:, j:j + 1]
            st = st_ref[j, hd]
            new = st - kcol * (st - ib_ref[j:j + 1, sl])
            nst_ref[j, hd] = new
            ob_ref[j:j + 1, sl] = _dot(q_bf, new.astype(BF))[j:j + 1, :]


def _dec_mix_call(sinks, qm, kn, vn, qb, kin, ib, cache_k, cache_v, state, bt):
    b = kn.shape[0]
    rows = lambda n: pl.BlockSpec((bt, n), lambda i, *_: (i, 0))
    cache_spec = pl.BlockSpec((bt, WINDOW, KV_WIDTH), lambda i, *_: (i, 0, 0))
    st_spec = pl.BlockSpec((bt, N_HEADS_B, HEAD_DIM_B, HEAD_DIM_B), lambda i, *_: (i, 0, 0, 0))
    qm_spec = pl.BlockSpec((bt, N_HEADS_A, LANES), lambda i, *_: (i, 0, 0))
    return pl.pallas_call(
        functools.partial(_dec_mix_kernel, bt=bt),
        grid_spec=pltpu.PrefetchScalarGridSpec(
            num_scalar_prefetch=1, grid=(b // bt,),
            in_specs=[qm_spec, rows(KV_WIDTH), rows(KV_WIDTH), rows(WIDTH_B), rows(WIDTH_B),
                      rows(WIDTH_B), cache_spec, cache_spec, st_spec],
            out_specs=[qm_spec, rows(WIDTH_B), cache_spec, cache_spec, st_spec]),
        out_shape=[jax.ShapeDtypeStruct((b, N_HEADS_A, LANES), F32),
                   jax.ShapeDtypeStruct((b, WIDTH_B), F32),
                   jax.ShapeDtypeStruct(cache_k.shape, F32),
                   jax.ShapeDtypeStruct(cache_v.shape, F32),
                   jax.ShapeDtypeStruct(state.shape, F32)],
        compiler_params=pltpu.CompilerParams(
            dimension_semantics=("arbitrary",), vmem_limit_bytes=VMEM_LIMIT),
        name="decode_mix",
    )(sinks, qm, kn, vn, qb, kin, ib, cache_k, cache_v, state)


def _dec_out_kernel(x_ref, mod_ref, att_ref, za_ref, ob_ref, zb_ref, g_ref, gains_ref,
                    permt_ref, w_pa_ref, w_pb_ref, w_out_ref,
                    y_ref, w_pa_bf_ref, w_pb_bf_ref, w_out_bf_ref):
    w_pa, w_pb, w_out = (r[...].astype(BF) for r in (w_pa_ref, w_pb_ref, w_out_ref))
    w_pa_bf_ref[...] = w_pa
    w_pb_bf_ref[...] = w_pb
    w_out_bf_ref[...] = w_out
    att = sum(_split_dot(att_ref[:, hd, :], permt_ref[hd * LANES:(hd + 1) * LANES, :])
              for hd in range(N_HEADS_A))
    ya = att * _silu(za_ref[...])
    yb = _branch_b_out(ob_ref[...], zb_ref[...], gains_ref[:, G_O:G_END])
    g = g_ref[...]
    y = (g[:, :D_MODEL] * _dot(ya.astype(BF), w_pa) + g[:, D_MODEL:] * _dot(yb.astype(BF), w_pb))
    y_ref[:, 0, :] = x_ref[:, 0, :] + _modulation(mod_ref)[2] * _dot(y.astype(BF), w_out)


def _dec_out_call(x, mod, att, za, ob, zb, g, consts, w):
    bf = lambda a: jax.ShapeDtypeStruct(a.shape, BF)
    ws = (w["w_proj_a_f32"], w["w_proj_b_f32"], w["w_out_f32"])
    return pl.pallas_call(
        _dec_out_kernel,
        out_shape=[jax.ShapeDtypeStruct(x.shape, F32)] + [bf(a) for a in ws],
        compiler_params=pltpu.CompilerParams(vmem_limit_bytes=VMEM_LIMIT),
        name="decode_out",
    )(x, mod, att, za, ob, zb, g, w["gains"], consts["permt"], *ws)


def _rope_consts(n_tiles, tile, past_len):
    half = ROT_DIM // 2
    inv = ROPE_THETA ** (-np.arange(0, ROT_DIM, 2, dtype=np.float64) / ROT_DIM)
    e = np.arange(LANES) % HEAD_DIM_A
    rot = e < ROT_DIM

    def tables(pos):
        ang = np.asarray(pos, np.float64)[:, None] * inv[e % half][None, :]
        return np.where(rot, np.cos(ang), 1.0), np.where(rot, np.sin(ang), 0.0)

    sgn = np.stack([np.where(e < half, -1.0, 0.0),
                    np.where(rot & (e >= half), 1.0, 0.0)])
    cb, sb = tables(np.arange(n_tiles) * tile)
    cl, sl = tables(np.arange(tile))
    c1, s1 = tables([past_len])
    f = lambda a: jnp.asarray(a, F32)
    return {"rope_cb": f(cb), "rope_sb": f(sb), "rope_cl": f(cl), "rope_sl": f(sl),
            "rope_sgn": f(sgn), "rope_c1": f(c1), "rope_u1": f(s1 * sgn[0:1]),
            "rope_d1": f(s1 * sgn[1:2])}


def _static_consts():
    seg = np.kron(np.eye(MXU_N // HEAD_DIM_A),
                  np.full((HEAD_DIM_A, HEAD_DIM_A), 1.0 / HEAD_DIM_A))
    tri = np.tril(np.ones((BLK, BLK)))
    r = np.arange(4 * BLK)[:, None] % BLK
    c = np.arange(2 * BLK)[None, :]
    ok_prev = (c < BLK) & (c > r)
    ok_cur = (c >= BLK) & (c - BLK <= r)
    bias = np.stack([np.where(ok_cur, 0.0, NEG), np.where(ok_prev | ok_cur, 0.0, NEG)])
    i = np.arange(BLK)[:, None]
    j = np.arange(BLK)[None, :]
    lmask = np.stack([((i // (2 * b)) == (j // (2 * b))) & ((i & b) != 0) & ((j & b) == 0)
                      for b in LEVELS]).astype(np.float32)
    perm = np.zeros((WIDTH_A, N_HEADS_A * LANES), np.float32)
    for hd in range(N_HEADS_A):
        kvh = hd // (N_HEADS_A // N_KV_A)
        for d in range(HEAD_DIM_A):
            perm[hd * HEAD_DIM_A + d, hd * LANES + kvh * HEAD_DIM_A + d] = 1.0
    return {
        "seg": jnp.asarray(seg, BF), "tri": jnp.asarray(tri, BF),
        "bias": jnp.asarray(bias, F32), "lmask": jnp.asarray(lmask, F32),
        "perm": jnp.asarray(perm, BF), "permt": jnp.asarray(perm.T, BF),
    }


def kernel(x_prompt, x_sample, cache_win_k, cache_win_v, state_hgrn, c_prompt, c_sample,
           w_ada, b_ada, norm_g, w_in, q_norm_g, k_norm_g, sinks, lb_logits, o_norm_g,
           w_merge, b_merge, w_proj_a, w_proj_b, w_out):
    depth = w_in.shape[0]
    assert depth == 1 and x_prompt.shape[0] == 1 and x_sample.shape[1] == 1
    t = x_prompt.shape[1]
    nb = x_sample.shape[0]
    past_len = t
    tile = PROMPT_TILE
    bt = DECODE_BATCH_TILE

    consts = _static_consts()
    consts.update(_rope_consts(t // tile, tile, past_len))

    w = {
        "sinks": sinks[0], "norm_g": norm_g,
        "w_in_f32": w_in[0], "w_merge_f32": w_merge[0], "b_merge": b_merge,
        "w_proj_a_f32": w_proj_a[0], "w_proj_b_f32": w_proj_b[0], "w_out_f32": w_out[0],
        "lb_logits": lb_logits,
    }

    mod_p, mod_s, w["gains"] = _ada_call(c_prompt, c_sample, w_ada[0], b_ada,
                                         q_norm_g, k_norm_g, o_norm_g)

    xs = x_sample
    (w["w_in"], w["w_merge"], g,
     qm, kn, vn, za, qb, kin, ib, zb) = _dec_in_call(xs, mod_s, consts, w)
    to_t = lambda c: jnp.transpose(c[0], (0, 2, 3, 1)).reshape(nb, KV_WIDTH, WINDOW)
    from_t = lambda c: jnp.transpose(
        c.reshape(-1, N_KV_A, HEAD_DIM_A, WINDOW), (0, 3, 1, 2))[None]
    att, ob, nk, nv, nst = _dec_mix_call(
        w["sinks"], qm, kn, vn, qb, kin, ib,
        to_t(cache_win_k), to_t(cache_win_v), state_hgrn[0], bt)
    y_s, w["w_proj_a"], w["w_proj_b"], w["w_out"] = _dec_out_call(
        xs, mod_s, att, za, ob, zb, g, consts, w)

    run = functools.partial(_prompt_call, x_prompt[0], mod_p, consts, w, tile,
                            PROMPT_TILES_PER_STEP)
    *fast, span = run(robust=False)
    y_p, kwin, vwin, st_p = lax.cond(jnp.max(span) < HGRN_FAST_MAX,
                                     lambda: tuple(fast), lambda: tuple(run(robust=True)[:4]))

    return (y_p[None], y_s,
            from_t(kwin[None]), from_t(vwin[None]), st_p[None, None],
            from_t(nk), from_t(nv), nst[None])
```
